```python
import math
import jax, jax.numpy as jnp
from jax import lax
import numpy as np

D_MODEL = 2048
BATCH = 4
SEQ = 2048
DEPTH = 2
DEC_BATCH = 128
DEC_SEQ = 1
PAST_LEN = 16384
PAGE_SIZE = 128

H_A = 4
DK_A = D_MODEL // 8
DV_A = D_MODEL // 4
W_A = H_A * DV_A
GROUP_B = 16
G_B = D_MODEL // 32
P_B = 64
W_B = G_B * GROUP_B
H_C = 8
DK_C = D_MODEL // 8
DV_C = D_MODEL // 4
W_C = H_C * DV_C

CHUNK = 128
NORM_EPS = 1e-6
ROPE_BASE = 10000.0
COLS0 = 2 * H_A * DK_A + 3 * W_A + 2 * H_A + 2 * W_B
COLS1 = 2 * H_C * DK_C + 2 * W_C

kernel_name = "hybrid_mlstm_s5_retention_step"

F32 = jnp.float32


def _split_cols(x, sizes):
    out, start = [], 0
    for s in sizes:
        out.append(x[..., start:start + s])
        start += s
    return out


def _chunk_len(L):
    return CHUNK if L % CHUNK == 0 else L


def rms_norm(x, g):
    xf = x.astype(F32)
    return xf * lax.rsqrt(jnp.mean(xf * xf, axis=-1, keepdims=True) + NORM_EPS) * g.astype(F32)


def rope(x, pos):
    half = x.shape[-1] // 2
    inv = ROPE_BASE ** (-jnp.arange(half, dtype=F32) / half)
    ang = pos[:, None] * inv[None, :]
    cos = jnp.cos(ang)[None, :, None, :]
    sin = jnp.sin(ang)[None, :, None, :]
    x1, x2 = x[..., :half], x[..., half:]
    return jnp.concatenate([x1 * cos - x2 * sin, x1 * sin + x2 * cos], axis=-1)


def mlstm_chunkwise(q, k, v, i_pre, log_f, C0, n0, m0):
    bsz, L, H = q.shape[0], q.shape[1], q.shape[2]
    T = _chunk_len(L)
    nc = L // T

    def chunks(a):
        return jnp.moveaxis(a.reshape((bsz, nc, T) + a.shape[2:]), 1, 0)

    causal = jnp.tril(jnp.ones((T, T), dtype=bool))

    def step(carry, xs):
        C, n, m = carry
        qc, kc, vc, ic, fc = xs
        b = jnp.cumsum(fc, axis=1).transpose(0, 2, 1)
        ic = ic.transpose(0, 2, 1)
        d = jnp.where(causal, b[..., :, None] - b[..., None, :] + ic[..., None, :], -jnp.inf)
        inter = b + m[..., None]
        m_t = jnp.maximum(inter, jnp.max(d, axis=-1))
        w_intra = jnp.exp(d - m_t[..., None])
        w_inter = jnp.exp(inter - m_t)
        s = jnp.einsum('bthk,bshk->bhts', qc, kc) * w_intra
        num = jnp.einsum('bhts,bshv->bhtv', s, vc) + w_inter[..., None] * jnp.einsum('bthk,bhkv->bhtv', qc, C)
        den = jnp.sum(s, axis=-1) + w_inter * jnp.einsum('bthk,bhk->bht', qc, n)
        h = num / jnp.maximum(jnp.abs(den), jnp.exp(-m_t))[..., None]
        b_last = b[..., -1]
        g = b_last[..., None] - b + ic
        m_new = jnp.maximum(b_last + m, jnp.max(g, axis=-1))
        e = jnp.exp(g - m_new[..., None])
        decay = jnp.exp(b_last + m - m_new)
        ke = kc * e.transpose(0, 2, 1)[..., None]
        C_new = decay[..., None, None] * C + jnp.einsum('bshk,bshv->bhkv', ke, vc)
        n_new = decay[..., None] * n + jnp.sum(ke, axis=1)
        return (C_new, n_new, m_new), h.transpose(0, 2, 1, 3)

    init = (C0.astype(F32), n0.astype(F32), m0.astype(F32))
    xs = (chunks(q), chunks(k), chunks(v), chunks(i_pre), chunks(log_f))
    (C1, n1, m1), hs = lax.scan(step, init, xs)
    h = jnp.moveaxis(hs, 0, 1).reshape(bsz, L, H, v.shape[-1])
    return h, C1, n1, m1


def s5_scan(u, lam_re, lam_im, log_dt, b_re, b_im, c_re, c_im, d_skip, x0_re, x0_im):
    lam_re = lam_re.astype(F32)
    lam_im = lam_im.astype(F32)
    dt = jnp.exp(log_dt.astype(F32))[:, None]
    mag = jnp.exp(lam_re * dt)
    ang = lam_im * dt
    a_re, a_im = mag * jnp.cos(ang), mag * jnp.sin(ang)
    den = lam_re * lam_re + lam_im * lam_im
    f_re = ((a_re - 1.0) * lam_re + a_im * lam_im) / den
    f_im = (a_im * lam_re - (a_re - 1.0) * lam_im) / den
    b_re, b_im = b_re.astype(F32), b_im.astype(F32)
    bb_re = f_re[..., None] * b_re - f_im[..., None] * b_im
    bb_im = f_re[..., None] * b_im + f_im[..., None] * b_re
    bu_re = jnp.einsum('blgj,gpj->blgp', u, bb_re)
    bu_im = jnp.einsum('blgj,gpj->blgp', u, bb_im)
    x0_re, x0_im = x0_re.astype(F32), x0_im.astype(F32)
    bu_re = bu_re.at[:, 0].add(a_re * x0_re - a_im * x0_im)
    bu_im = bu_im.at[:, 0].add(a_re * x0_im + a_im * x0_re)
    ar = jnp.broadcast_to(a_re, bu_re.shape)
    ai = jnp.broadcast_to(a_im, bu_re.shape)

    def combine(e1, e2):
        a1r, a1i, b1r, b1i = e1
        a2r, a2i, b2r, b2i = e2
        return (a2r * a1r - a2i * a1i, a2r * a1i + a2i * a1r,
                a2r * b1r - a2i * b1i + b2r, a2r * b1i + a2i * b1r + b2i)

    _, _, xr, xi = lax.associative_scan(combine, (ar, ai, bu_re, bu_im), axis=1)
    y = (jnp.einsum('blgp,gjp->blgj', xr, c_re.astype(F32))
         - jnp.einsum('blgp,gjp->blgj', xi, c_im.astype(F32))
         + d_skip.astype(F32) * u)
    return y, xr[:, -1], xi[:, -1]


def retention_chunkwise(q, k, v, log_gamma, S0):
    bsz, L, H = q.shape[0], q.shape[1], q.shape[2]
    T = _chunk_len(L)
    nc = L // T
    pos = jnp.arange(T, dtype=F32)
    causal = jnp.tril(jnp.ones((T, T), dtype=bool))
    decay_mask = jnp.exp(jnp.where(causal[None], (pos[:, None] - pos[None, :])[None] * log_gamma[:, None, None], -jnp.inf))
    inner = jnp.exp((pos + 1.0)[:, None] * log_gamma[None, :])
    tail = jnp.exp((T - 1.0 - pos)[:, None] * log_gamma[None, :])
    chunk_decay = jnp.exp(T * log_gamma)

    def chunks(a):
        return jnp.moveaxis(a.reshape((bsz, nc, T) + a.shape[2:]), 1, 0)

    def step(S, xs):
        qc, kc, vc = xs
        s = jnp.einsum('bthk,bshk->bhts', qc, kc) * decay_mask
        o = (jnp.einsum('bhts,bshv->bthv', s, vc)
             + jnp.einsum('bthk,bhkv->bthv', qc, S) * inner[None, :, :, None])
        S_new = chunk_decay[None, :, None, None] * S + jnp.einsum('bshk,bshv->bhkv', kc * tail[None, :, :, None], vc)
        return S_new, o

    S1, os_ = lax.scan(step, S0.astype(F32), (chunks(q), chunks(k), chunks(v)))
    o = jnp.moveaxis(os_, 0, 1).reshape(bsz, L, H, v.shape[-1])
    return o, S1


def mixer_ab(h, C0, n0, m0, s5r0, s5i0, w_in0, b_gates0, g_head_a, lam_re, lam_im, log_dt,
             b_re, b_im, c_re, c_im, d_skip, w_glu, b_glu, w_out0):
    bsz, L, _ = h.shape
    proj = jnp.einsum('bld,de->ble', h, w_in0)
    q, k, v, o, z_a, i_pre, f_pre, u, z_b = _split_cols(
        proj, (H_A * DK_A, H_A * DK_A, W_A, W_A, W_A, H_A, H_A, W_B, W_B))
    q = q.reshape(bsz, L, H_A, DK_A).astype(F32) * (DK_A ** -0.5)
    k = k.reshape(bsz, L, H_A, DK_A).astype(F32)
    v = v.reshape(bsz, L, H_A, DV_A).astype(F32)
    i_pre = i_pre.astype(F32) + b_gates0[:H_A].astype(F32)
    log_f = jax.nn.log_sigmoid(f_pre.astype(F32) + b_gates0[H_A:].astype(F32))
    h_a, C1, n1, m1 = mlstm_chunkwise(q, k, v, i_pre, log_f, C0, n0, m0)
    h_a = rms_norm(h_a, g_head_a).reshape(bsz, L, W_A)
    y_a = h_a * jax.nn.sigmoid(o) * jax.nn.silu(z_a)
    u = u.reshape(bsz, L, G_B, GROUP_B).astype(F32)
    y_b, s5r1, s5i1 = s5_scan(u, lam_re, lam_im, log_dt, b_re, b_im, c_re, c_im, d_skip, s5r0, s5i0)
    y_b = jax.nn.gelu(y_b.reshape(bsz, L, W_B))
    y_b = y_b * jax.nn.sigmoid(y_b @ w_glu + b_glu) * jax.nn.silu(z_b)
    out = jnp.concatenate([y_a, y_b], axis=-1) @ w_out0
    return out, C1, n1, m1, s5r1, s5i1


def mixer_c(h, pos, S0, w_in1, g_head_c, w_out1):
    bsz, L, _ = h.shape
    proj = jnp.einsum('bld,de->ble', h, w_in1)
    q, k, v, z_c = _split_cols(proj, (H_C * DK_C, H_C * DK_C, W_C, W_C))
    q = rope(q.reshape(bsz, L, H_C, DK_C).astype(F32), pos)
    k = rope(k.reshape(bsz, L, H_C, DK_C).astype(F32), pos) * (DK_C ** -0.5)
    v = v.reshape(bsz, L, H_C, DV_C).astype(F32)
    log_gamma = jnp.log1p(-jnp.exp2(-5.0 - jnp.arange(H_C, dtype=F32)))
    o, S1 = retention_chunkwise(q, k, v, log_gamma, S0)
    o = rms_norm(o, g_head_c).reshape(bsz, L, W_C) * jax.nn.silu(z_c)
    return o @ w_out1, S1


def trunk(x, pos, C0, n0, m0, s5r0, s5i0, S0, g_pre, g_post, w_in0, b_gates0, g_head_a,
          lam_re, lam_im, log_dt, b_re, b_im, c_re, c_im, d_skip, w_glu, b_glu, w_out0,
          w_in1, g_head_c, w_out1):
    h = x.astype(F32)
    for layer in range(DEPTH):
        a = rms_norm(h, g_pre[layer])
        if layer % 2 == 0:
            mix, C1, n1, m1, s5r1, s5i1 = mixer_ab(
                a, C0, n0, m0, s5r0, s5i0, w_in0, b_gates0, g_head_a, lam_re, lam_im, log_dt,
                b_re, b_im, c_re, c_im, d_skip, w_glu, b_glu, w_out0)
        else:
            mix, S1 = mixer_c(a, pos, S0, w_in1, g_head_c, w_out1)
        h = h + rms_norm(mix, g_post[layer])
    return h, C1, n1, m1, s5r1, s5i1, S1


def setup_inputs(seed: int = 0) -> dict:
    key = jax.random.key(seed)
    ks = jax.random.split(key, 32)

    def nrm(k, shape, s):
        return jax.random.normal(k, shape, F32) * s

    lam_im = jnp.pi * jnp.broadcast_to(jnp.arange(P_B, dtype=F32), (G_B, P_B)) + nrm(ks[13], (G_B, P_B), 0.01)
    return {
        'x_prompt': nrm(ks[0], (BATCH, SEQ, D_MODEL), 1.0),
        'x_sample': nrm(ks[1], (DEC_BATCH, DEC_SEQ, D_MODEL), 1.0),
        'state_mlstm_C': nrm(ks[2], (DEC_BATCH, H_A, DK_A, DV_A), 0.05),
        'state_mlstm_n': nrm(ks[3], (DEC_BATCH, H_A, DK_A), 0.5),
        'state_mlstm_m': nrm(ks[4], (DEC_BATCH, H_A), 1.0),
        'state_s5_re': nrm(ks[5], (DEC_BATCH, G_B, P_B), 0.5),
        'state_s5_im': nrm(ks[6], (DEC_BATCH, G_B, P_B), 0.5),
        'state_ret': nrm(ks[7], (DEC_BATCH, H_C, DK_C, DV_C), 0.05),
        'g_pre': 1.0 + nrm(ks[8], (DEPTH, D_MODEL), 0.02),
        'g_post': 1.0 + nrm(ks[9], (DEPTH, D_MODEL), 0.02),
        'w_in0': nrm(ks[10], (D_MODEL, COLS0), D_MODEL ** -0.5),
        'b_gates0': jnp.concatenate([nrm(ks[11], (H_A,), 0.1), 3.0 + nrm(ks[31], (H_A,), 0.1)]),
        'g_head_a': 1.0 + nrm(ks[12], (H_A, DV_A), 0.02),
        'lam_re': -0.5 + nrm(ks[14], (G_B, P_B), 0.01),
        'lam_im': lam_im,
        'log_dt': jax.random.uniform(ks[15], (G_B,), F32, math.log(0.001), math.log(0.1)),
        'b_re': nrm(ks[16], (G_B, P_B, GROUP_B), (2 * GROUP_B) ** -0.5),
        'b_im': nrm(ks[17], (G_B, P_B, GROUP_B), (2 * GROUP_B) ** -0.5),
        'c_re': nrm(ks[18], (G_B, GROUP_B, P_B), (2 * P_B) ** -0.5),
        'c_im': nrm(ks[19], (G_B, GROUP_B, P_B), (2 * P_B) ** -0.5),
        'd_skip': nrm(ks[20], (G_B, GROUP_B), 0.5),
        'w_glu': nrm(ks[21], (W_B, W_B), W_B ** -0.5),
        'b_glu': nrm(ks[22], (W_B,), 0.02),
        'w_out0': nrm(ks[23], (W_A + W_B, D_MODEL), (W_A + W_B) ** -0.5),
        'w_in1': nrm(ks[24], (D_MODEL, COLS1), D_MODEL ** -0.5),
        'g_head_c': 1.0 + nrm(ks[25], (H_C, DV_C), 0.02),
        'w_out1': nrm(ks[26], (W_C, D_MODEL), W_C ** -0.5),
    }


def reference(x_prompt, x_sample, state_mlstm_C, state_mlstm_n, state_mlstm_m, state_s5_re,
              state_s5_im, state_ret, g_pre, g_post, w_in0, b_gates0, g_head_a, lam_re, lam_im,
              log_dt, b_re, b_im, c_re, c_im, d_skip, w_glu, b_glu, w_out0, w_in1, g_head_c, w_out1):
    pos_p = jnp.arange(SEQ, dtype=F32)
    y_prompt, C_p, n_p, m_p, s5r_p, s5i_p, S_p = trunk(
        x_prompt, pos_p,
        jnp.zeros((BATCH, H_A, DK_A, DV_A), F32), jnp.zeros((BATCH, H_A, DK_A), F32),
        jnp.zeros((BATCH, H_A), F32), jnp.zeros((BATCH, G_B, P_B), F32),
        jnp.zeros((BATCH, G_B, P_B), F32), jnp.zeros((BATCH, H_C, DK_C, DV_C), F32),
        g_pre, g_post, w_in0, b_gates0, g_head_a, lam_re, lam_im, log_dt, b_re, b_im,
        c_re, c_im, d_skip, w_glu, b_glu, w_out0, w_in1, g_head_c, w_out1)
    pos_s = PAST_LEN + jnp.arange(DEC_SEQ, dtype=F32)
    y_sample, C_s, n_s, m_s, s5r_s, s5i_s, S_s = trunk(
        x_sample, pos_s, state_mlstm_C, state_mlstm_n, state_mlstm_m, state_s5_re, state_s5_im,
        state_ret, g_pre, g_post, w_in0, b_gates0, g_head_a, lam_re, lam_im, log_dt, b_re, b_im,
        c_re, c_im, d_skip, w_glu, b_glu, w_out0, w_in1, g_head_c, w_out1)
    return (y_prompt, y_sample, C_p, n_p, m_p, s5r_p, s5i_p, S_p, C_s, n_s, m_s, s5r_s, s5i_s, S_s)
```

```python
import functools
import math

import jax
import jax.numpy as jnp
from jax import lax
from jax.experimental import pallas as pl
from jax.experimental.pallas import tpu as pltpu

F32 = jnp.float32
BF16 = jnp.bfloat16

D_MODEL = 2048
BATCH = 4
SEQ = 2048
DEC_BATCH = 128
PAST_LEN = 16384
H_A = 4
DK_A = 256
DV_A = 512
W_A = H_A * DV_A
GROUP_B = 16
G_B = 64
P_B = 64
W_B = G_B * GROUP_B
H_C = 8
DK_C = 256
DV_C = 512
W_C = H_C * DV_C
CHUNK = 128
NORM_EPS = 1e-6
ROPE_BASE = 10000.0
QKV0 = 2 * H_A * DK_A + 3 * W_A
GATE0 = QKV0 + 2 * H_A
SCALE_A = DK_A ** -0.5
SCALE_C = DK_C ** -0.5
LOG_GAMMA = tuple(math.log1p(-(2.0 ** (-5.0 - h))) for h in range(H_C))
S5_T = 16
S5_W = S5_T * GROUP_B
LANES = 128
VMEM_LIMIT = 48 * 1024 * 1024


def _cparams(*sem):
    return pltpu.CompilerParams(dimension_semantics=sem, vmem_limit_bytes=VMEM_LIMIT)


def _dot(a, b):
    return jnp.dot(a, b, preferred_element_type=F32)


def _dot_nt(a, b):
    return lax.dot_general(a, b, (((1,), (1,)), ((), ())), preferred_element_type=F32)


def _dot_tn(a, b):
    return lax.dot_general(a, b, (((0,), (0,)), ((), ())), preferred_element_type=F32)


def _split(x):
    hi = x.astype(BF16)
    return hi, (x - hi.astype(F32)).astype(BF16)


def _dot3(a, b):
    a_hi, a_lo = _split(a)
    b_hi, b_lo = _split(b)
    return _dot(a_hi, b_hi) + _dot(a_hi, b_lo) + _dot(a_lo, b_hi)


def _log_sigmoid(x):
    return jnp.minimum(x, 0.0) - jnp.log1p(jnp.exp(-jnp.abs(x)))


def _silu(x):
    return x * jax.nn.sigmoid(x)


def _gelu_tanh(x):
    return 0.5 * x * (1.0 + jnp.tanh(math.sqrt(2.0 / math.pi) * (x + 0.044715 * (x * x * x))))


def _rms(x, g):
    return x * lax.rsqrt(jnp.mean(x * x, axis=-1, keepdims=True) + NORM_EPS) * g


def _norm_gates_kernel(x_ref, g_ref, wg_ref, a_ref, gates_ref):
    a = _rms(x_ref[...], g_ref[...])
    a_ref[...] = a.astype(BF16)
    gates_ref[...] = _dot3(a, wg_ref[...])


def _norm_kernel(x_ref, g_ref, a_ref):
    a_ref[...] = _rms(x_ref[...], g_ref[...]).astype(BF16)


def _norm_gates(x, g, wg, tm):
    m = x.shape[0]
    return pl.pallas_call(
        _norm_gates_kernel,
        grid=(m // tm,),
        in_specs=[pl.BlockSpec((tm, D_MODEL), lambda i: (i, 0)),
                  pl.BlockSpec((1, D_MODEL), lambda i: (0, 0)),
                  pl.BlockSpec((D_MODEL, LANES), lambda i: (0, 0))],
        out_specs=[pl.BlockSpec((tm, D_MODEL), lambda i: (i, 0)),
                   pl.BlockSpec((tm, LANES), lambda i: (i, 0))],
        out_shape=[jax.ShapeDtypeStruct((m, D_MODEL), BF16),
                   jax.ShapeDtypeStruct((m, LANES), F32)],
        compiler_params=_cparams("parallel"),
        name="norm_gates",
    )(x, g, wg)


def _norm(x, g, tm):
    m = x.shape[0]
    return pl.pallas_call(
        _norm_kernel,
        grid=(m // tm,),
        in_specs=[pl.BlockSpec((tm, D_MODEL), lambda i: (i, 0)),
                  pl.BlockSpec((1, D_MODEL), lambda i: (0, 0))],
        out_specs=pl.BlockSpec((tm, D_MODEL), lambda i: (i, 0)),
        out_shape=jax.ShapeDtypeStruct((m, D_MODEL), BF16),
        compiler_params=_cparams("parallel"),
        name="norm",
    )(x, g)


def _mm_kernel(x_ref, w_ref, o_ref):
    o_ref[...] = _dot(x_ref[...], w_ref[...]).astype(o_ref.dtype)


def _matmul(x, w, out_dtype, tm, tn):
    m, k = x.shape
    n = w.shape[1]
    return pl.pallas_call(
        _mm_kernel,
        grid=(n // tn, m // tm),
        in_specs=[pl.BlockSpec((tm, k), lambda j, i: (i, 0)),
                  pl.BlockSpec((k, tn), lambda j, i: (0, j))],
        out_specs=pl.BlockSpec((tm, tn), lambda j, i: (i, j)),
        out_shape=jax.ShapeDtypeStruct((m, n), out_dtype),
        compiler_params=_cparams("parallel", "parallel"),
        name="proj",
    )(x, w)


def _outproj_kernel(*refs, nka, nkb):
    if nkb:
        ya_ref, yb_ref, w_ref, h_ref, g_ref, o_ref, acc_ref = refs
    else:
        ya_ref, w_ref, h_ref, g_ref, o_ref, acc_ref = refs
    k = pl.program_id(1)

    @pl.when(k == 0)
    def _():
        acc_ref[...] = jnp.zeros_like(acc_ref)

    if nkb:
        @pl.when(k < nka)
        def _():
            acc_ref[...] += _dot(ya_ref[...].astype(BF16), w_ref[...])

        @pl.when(k >= nka)
        def _():
            acc_ref[...] += _dot(yb_ref[...].astype(BF16), w_ref[...])
    else:
        acc_ref[...] += _dot(ya_ref[...].astype(BF16), w_ref[...])

    @pl.when(k == nka + nkb - 1)
    def _():
        o_ref[...] = h_ref[...] + _rms(acc_ref[...], g_ref[...])


def _outproj(ya, yb, w, h, g, tm, tk):
    m = h.shape[0]
    nka = ya.shape[1] // tk
    nkb = 0 if yb is None else yb.shape[1] // tk
    in_specs = [pl.BlockSpec((tm, tk), lambda i, k: (i, jnp.minimum(k, nka - 1)))]
    args = [ya]
    if nkb:
        in_specs.append(pl.BlockSpec((tm, tk), lambda i, k: (i, jnp.maximum(k - nka, 0))))
        args.append(yb)
    in_specs += [pl.BlockSpec((tk, D_MODEL), lambda i, k: (k, 0)),
                 pl.BlockSpec((tm, D_MODEL), lambda i, k: (i, 0)),
                 pl.BlockSpec((1, D_MODEL), lambda i, k: (0, 0))]
    args += [w, h, g]
    return pl.pallas_call(
        functools.partial(_outproj_kernel, nka=nka, nkb=nkb),
        grid=(m // tm, nka + nkb),
        in_specs=in_specs,
        out_specs=pl.BlockSpec((tm, D_MODEL), lambda i, k: (i, 0)),
        out_shape=jax.ShapeDtypeStruct((m, D_MODEL), F32),
        scratch_shapes=[pltpu.VMEM((tm, D_MODEL), F32)],
        compiler_params=_cparams("parallel", "arbitrary"),
        name="outproj",
    )(*args)


def _mlstm_prompt_kernel(bg_ref, q_ref, k_ref, v_ref, o_ref, z_ref, gc_ref, gr_ref, gh_ref,
                         y_ref, c_out, n_out, m_out, c_s, n_s, m_s):
    c = pl.program_id(1)
    t = CHUNK

    @pl.when(c == 0)
    def _():
        c_s[...] = jnp.zeros_like(c_s)
        n_s[...] = jnp.zeros_like(n_s)
        m_s[...] = jnp.zeros_like(m_s)

    row = lax.broadcasted_iota(jnp.int32, (t, t), 0)
    col = lax.broadcasted_iota(jnp.int32, (t, t), 1)
    tril = col <= row
    triu = row <= col
    gc = gc_ref[...]
    gr = gr_ref[...]
    for h in range(H_A):
        b_i = bg_ref[h]
        b_f = bg_ref[H_A + h]
        i_col = gc[:, h:h + 1] + b_i
        i_row = gr[h:h + 1, :] + b_i
        lf_col = _log_sigmoid(gc[:, H_A + h:H_A + h + 1] + b_f)
        lf_row = _log_sigmoid(gr[H_A + h:H_A + h + 1, :] + b_f)
        b_col = jnp.sum(jnp.where(tril, lf_row, 0.0), axis=1, keepdims=True)
        b_row = jnp.sum(jnp.where(triu, lf_col, 0.0), axis=0, keepdims=True)
        m_prev = m_s[h][:, 0:1]
        d = jnp.where(tril, b_col - b_row + i_row, -jnp.inf)
        inter = b_col + m_prev
        m_t = jnp.maximum(inter, jnp.max(d, axis=1, keepdims=True))
        w_intra = jnp.exp(d - m_t)
        w_inter = jnp.exp(inter - m_t) * SCALE_A
        q = q_ref[:, h * DK_A:(h + 1) * DK_A]
        k = k_ref[:, h * DK_A:(h + 1) * DK_A]
        v = v_ref[:, h * DV_A:(h + 1) * DV_A]
        s = _dot_nt(q, k) * (w_intra * SCALE_A)
        c_old = c_s[h]
        n_old = n_s[h]
        num = _dot(s.astype(BF16), v) + w_inter * _dot(q, c_old.astype(BF16))
        qn = jnp.sum(q.astype(F32) * n_old, axis=1, keepdims=True)
        den = jnp.sum(s, axis=1, keepdims=True) + w_inter * qn
        hh = num / jnp.maximum(jnp.abs(den), jnp.exp(-m_t))
        hn = _rms(hh, gh_ref[h:h + 1, :])
        o = o_ref[:, h * DV_A:(h + 1) * DV_A].astype(F32)
        z = z_ref[:, h * DV_A:(h + 1) * DV_A].astype(F32)
        y_ref[:, h * DV_A:(h + 1) * DV_A] = (hn * jax.nn.sigmoid(o) * _silu(z)).astype(BF16)
        b_last = b_col[t - 1:t, :]
        g_col = b_last - b_col + i_col
        m_new = jnp.maximum(b_last + m_prev, jnp.max(g_col, axis=0, keepdims=True))
        e_col = jnp.exp(g_col - m_new)
        decay = jnp.exp(b_last + m_prev - m_new)
        ke = k.astype(F32) * e_col
        c_s[h] = decay * c_old + _dot_tn(ke.astype(BF16), v)
        n_s[h] = decay * n_old + jnp.sum(ke, axis=0, keepdims=True)
        m_s[h] = jnp.broadcast_to(m_new, (1, LANES))

    @pl.when(c == pl.num_programs(1) - 1)
    def _():
        c_out[...] = c_s[...]
        n_out[...] = n_s[...]
        m_out[...] = m_s[...]


def _mlstm_prompt(bg, proj, gates_col, gates_row, g_head):
    nc = SEQ // CHUNK
    t = CHUNK
    qk_w = H_A * DK_A
    return pl.pallas_call(
        _mlstm_prompt_kernel,
        grid=(BATCH, nc),
        in_specs=[pl.BlockSpec(memory_space=pltpu.SMEM),
                  pl.BlockSpec((None, t, qk_w), lambda b, c: (b, c, 0)),
                  pl.BlockSpec((None, t, qk_w), lambda b, c: (b, c, 1)),
                  pl.BlockSpec((None, t, W_A), lambda b, c: (b, c, 1)),
                  pl.BlockSpec((None, t, W_A), lambda b, c: (b, c, 2)),
                  pl.BlockSpec((None, t, W_A), lambda b, c: (b, c, 3)),
                  pl.BlockSpec((None, t, LANES), lambda b, c: (b, c, 0)),
                  pl.BlockSpec((None, 2 * H_A, t), lambda b, c: (b, 0, c)),
                  pl.BlockSpec((H_A, DV_A), lambda b, c: (0, 0))],
        out_specs=[pl.BlockSpec((None, t, W_A), lambda b, c: (b, c, 0)),
                   pl.BlockSpec((None, H_A, DK_A, DV_A), lambda b, c: (b, 0, 0, 0)),
                   pl.BlockSpec((None, H_A, 1, DK_A), lambda b, c: (b, 0, 0, 0)),
                   pl.BlockSpec((None, H_A, 1, LANES), lambda b, c: (b, 0, 0, 0))],
        out_shape=[jax.ShapeDtypeStruct((BATCH, SEQ, W_A), BF16),
                   jax.ShapeDtypeStruct((BATCH, H_A, DK_A, DV_A), F32),
                   jax.ShapeDtypeStruct((BATCH, H_A, 1, DK_A), F32),
                   jax.ShapeDtypeStruct((BATCH, H_A, 1, LANES), F32)],
        scratch_shapes=[pltpu.VMEM((H_A, DK_A, DV_A), F32),
                        pltpu.VMEM((H_A, 1, DK_A), F32),
                        pltpu.VMEM((H_A, 1, LANES), F32)],
        compiler_params=_cparams("parallel", "arbitrary"),
        name="mlstm_prompt",
    )(bg, proj, proj, proj, proj, proj, gates_col, gates_row, g_head)


def _mlstm_sample_kernel(gi_ref, gf_ref, bi_ref, bf_ref, m0_ref, q_ref, k_ref, v_ref, o_ref, z_ref,
                         qt_ref, kt_ref, n0_ref, gh_ref, c0_ref,
                         y_ref, c1_ref, n1_ref, m1_ref):
    r = pl.program_id(0)
    rows = pl.ds(r, 1)
    shift = lax.rem(DEC_BATCH - r, DEC_BATCH)
    i_v = gi_ref[rows, :] + bi_ref[...]
    lf_v = _log_sigmoid(gf_ref[rows, :] + bf_ref[...])
    m0_v = m0_ref[rows, :]
    m_t = jnp.maximum(lf_v + m0_v, i_v)
    w_in = jnp.exp(i_v - m_t)
    w_st = jnp.exp(lf_v + m0_v - m_t)
    floor = jnp.exp(-m_t)
    m1_ref[rows, :] = m_t
    for h in range(H_A):
        ks = slice(h * DK_A, (h + 1) * DK_A)
        vs = slice(h * DV_A, (h + 1) * DV_A)
        wi = w_in[:, h:h + 1]
        ws = w_st[:, h:h + 1]
        q_col = pltpu.roll(qt_ref[h], shift, axis=1)[:, 0:1]
        k_col = pltpu.roll(kt_ref[h], shift, axis=1)[:, 0:1]
        q_row = q_ref[rows, ks]
        k_row = k_ref[rows, ks]
        v_row = v_ref[rows, vs]
        n_row = n0_ref[rows, ks]
        c_old = c0_ref[0, h]
        qk = jnp.sum(q_row * k_row, axis=1, keepdims=True) * SCALE_A
        s = qk * wi
        q_c = jnp.sum(c_old * q_col, axis=0, keepdims=True) * SCALE_A
        qn = jnp.sum(q_row * n_row, axis=1, keepdims=True) * SCALE_A
        num = s * v_row + ws * q_c
        den = s + ws * qn
        hh = num / jnp.maximum(jnp.abs(den), floor[:, h:h + 1])
        hn = _rms(hh, gh_ref[h:h + 1, :])
        y_ref[rows, vs] = hn * jax.nn.sigmoid(o_ref[rows, vs]) * _silu(z_ref[rows, vs])
        c1_ref[0, h] = ws * c_old + (wi * k_col) * v_row
        n1_ref[rows, ks] = ws * n_row + wi * k_row


def _mlstm_sample(gi, gf, bi, bf, m0, proj, qt, kt, n0, g_head, c0):
    qk_w = H_A * DK_A
    full = lambda shape: pl.BlockSpec(shape, lambda r: (0,) * len(shape))
    return pl.pallas_call(
        _mlstm_sample_kernel,
        grid=(DEC_BATCH,),
        in_specs=[full((DEC_BATCH, H_A)), full((DEC_BATCH, H_A)), full((1, H_A)), full((1, H_A)),
                  full((DEC_BATCH, H_A)),
                  pl.BlockSpec((DEC_BATCH, qk_w), lambda r: (0, 0)),
                  pl.BlockSpec((DEC_BATCH, qk_w), lambda r: (0, 1)),
                  pl.BlockSpec((DEC_BATCH, W_A), lambda r: (0, 1)),
                  pl.BlockSpec((DEC_BATCH, W_A), lambda r: (0, 2)),
                  pl.BlockSpec((DEC_BATCH, W_A), lambda r: (0, 3)),
                  full((H_A, DK_A, DEC_BATCH)), full((H_A, DK_A, DEC_BATCH)),
                  full((DEC_BATCH, qk_w)), full((H_A, DV_A)),
                  pl.BlockSpec((1, H_A, DK_A, DV_A), lambda r: (r, 0, 0, 0))],
        out_specs=[full((DEC_BATCH, W_A)),
                   pl.BlockSpec((1, H_A, DK_A, DV_A), lambda r: (r, 0, 0, 0)),
                   full((DEC_BATCH, qk_w)), full((DEC_BATCH, H_A))],
        out_shape=[jax.ShapeDtypeStruct((DEC_BATCH, W_A), F32),
                   jax.ShapeDtypeStruct((DEC_BATCH, H_A, DK_A, DV_A), F32),
                   jax.ShapeDtypeStruct((DEC_BATCH, qk_w), F32),
                   jax.ShapeDtypeStruct((DEC_BATCH, H_A), F32)],
        compiler_params=_cparams("arbitrary"),
        name="mlstm_sample",
    )(gi, gf, bi, bf, m0, proj, proj, proj, proj, proj, qt, kt, n0, g_head, c0)


def _s5_build_kernel(ldt_ref, lamr_ref, lami_ref, lamrc_ref, lamic_ref, b1_ref, cc_ref, dsk_ref,
                     m_ref, w_ref, v_ref, p1_ref, p2_ref):
    dt = jnp.exp(ldt_ref[...])

    def disc(lam_re, lam_im):
        mag = jnp.exp(lam_re * dt)
        ang = lam_im * dt
        a_re, a_im = mag * jnp.cos(ang), mag * jnp.sin(ang)
        den = lam_re * lam_re + lam_im * lam_im
        f_re = ((a_re - 1.0) * lam_re + a_im * lam_im) / den
        f_im = (a_im * lam_re - (a_re - 1.0) * lam_im) / den
        return a_re, a_im, f_re, f_im

    def power(lam_re, lam_im, tau):
        mag = jnp.exp(tau * (lam_re * dt))
        ang = tau * (lam_im * dt)
        return mag * jnp.cos(ang), mag * jnp.sin(ang)

    lam_re, lam_im = lamr_ref[...], lami_ref[...]
    a_re, a_im, f_re, f_im = disc(lam_re, lam_im)
    srow = lax.broadcasted_iota(jnp.int32, (S5_W, LANES), 0) // GROUP_B
    p_re, p_im = power(lam_re, lam_im, (S5_T - 1 - srow).astype(F32))
    af_re = p_re * f_re - p_im * f_im
    af_im = p_re * f_im + p_im * f_re
    lane = lax.broadcasted_iota(jnp.int32, (1, LANES), 1)
    sgn_lane = jnp.where(lane < P_B, -1.0, 1.0)
    b1 = b1_ref[...]
    b2 = sgn_lane * pltpu.roll(b1, P_B, axis=1)
    w_ref[...] = (af_re * b1 + af_im * b2).astype(BF16)

    lam_re_c, lam_im_c = lamrc_ref[...], lamic_ref[...]
    ac_re, ac_im, fc_re, fc_im = disc(lam_re_c, lam_im_c)
    tlane = lax.broadcasted_iota(jnp.int32, (LANES, S5_W), 1) // GROUP_B
    q_re, q_im = power(lam_re_c, lam_im_c, tlane.astype(F32))
    cc = cc_ref[...]
    cs = pltpu.roll(cc, P_B, axis=0)
    rowi = lax.broadcasted_iota(jnp.int32, (LANES, 1), 0)
    sgn_row = jnp.where(rowi < P_B, 1.0, -1.0)

    def readout(r_re, r_im):
        return sgn_row * (r_re * cc) - r_im * cs

    v_ref[...] = readout(q_re * ac_re - q_im * ac_im, q_re * ac_im + q_im * ac_re).astype(BF16)
    vf = readout(q_re * fc_re - q_im * fc_im, q_re * fc_im + q_im * fc_re)
    kw = _dot3(b1[0:GROUP_B, :], vf)
    lane_w = lax.broadcasted_iota(jnp.int32, (GROUP_B, S5_W), 1)
    blocks = [kw]
    for s in range(1, S5_T):
        blocks.append(jnp.where(lane_w >= s * GROUP_B, pltpu.roll(kw, s * GROUP_B, axis=1), 0.0))
    toep = jnp.concatenate(blocks, axis=0)
    ri = lax.broadcasted_iota(jnp.int32, (S5_W, S5_W), 0)
    ci = lax.broadcasted_iota(jnp.int32, (S5_W, S5_W), 1)
    m_ref[...] = (toep + jnp.where(ri == ci, dsk_ref[...], 0.0)).astype(BF16)

    r_re, r_im = power(lam_re, lam_im, float(S5_T))
    rid = lax.broadcasted_iota(jnp.int32, (8, LANES), 0)
    p1 = jnp.where(rid == 7, a_re, 0.0)
    p2 = jnp.where(rid == 7, sgn_lane * a_im, 0.0)
    for kk in range(7):
        p1 = jnp.where(rid == kk, r_re, p1)
        p2 = jnp.where(rid == kk, sgn_lane * r_im, p2)
        r_re, r_im = r_re * r_re - r_im * r_im, 2.0 * (r_re * r_im)
    p1_ref[...] = p1
    p2_ref[...] = p2


def _s5_build(ldt, lamr, lami, lamrc, lamic, b1, cc, dsk):
    g3 = lambda a, b: pl.BlockSpec((None, a, b), lambda g: (g, 0, 0))
    return pl.pallas_call(
        _s5_build_kernel,
        grid=(G_B,),
        in_specs=[g3(1, 1), g3(1, LANES), g3(1, LANES), g3(LANES, 1), g3(LANES, 1),
                  g3(S5_W, LANES), g3(LANES, S5_W), g3(1, S5_W)],
        out_specs=[g3(S5_W, S5_W), g3(S5_W, LANES), g3(LANES, S5_W), g3(8, LANES), g3(8, LANES)],
        out_shape=[jax.ShapeDtypeStruct((G_B, S5_W, S5_W), BF16),
                   jax.ShapeDtypeStruct((G_B, S5_W, LANES), BF16),
                   jax.ShapeDtypeStruct((G_B, LANES, S5_W), BF16),
                   jax.ShapeDtypeStruct((G_B, 8, LANES), F32),
                   jax.ShapeDtypeStruct((G_B, 8, LANES), F32)],
        compiler_params=_cparams("parallel"),
        name="s5_build",
    )(ldt, lamr, lami, lamrc, lamic, b1, cc, dsk)


def _s5_prompt_kernel(u_ref, m_ref, w_ref, v_ref, p1_ref, p2_ref, y_ref, xf_ref):
    nblk = SEQ // S5_T
    rows = BATCH * nblk
    u = u_ref[...]
    s = _dot(u, w_ref[...])
    bidx = lax.broadcasted_iota(jnp.int32, (rows, LANES), 0) & (nblk - 1)
    for kk in range(7):
        sh = 1 << kk
        r = jnp.where(bidx >= sh, pltpu.roll(s, sh, axis=0), 0.0)
        s = s + p1_ref[kk:kk + 1, :] * r + p2_ref[kk:kk + 1, :] * pltpu.roll(r, P_B, axis=1)
    x_prev = jnp.where(bidx >= 1, pltpu.roll(s, 1, axis=0), 0.0)
    y = _dot(u, m_ref[...]) + _dot(x_prev.astype(BF16), v_ref[...])
    y_ref[...] = _gelu_tanh(y).astype(BF16)
    for b in range(BATCH):
        xf_ref[b:b + 1, :] = s[(b + 1) * nblk - 1:(b + 1) * nblk, :]


def _s5_prompt(u, m, w, v, p1, p2):
    rows = BATCH * (SEQ // S5_T)
    g3 = lambda a, b: pl.BlockSpec((None, a, b), lambda g: (g, 0, 0))
    return pl.pallas_call(
        _s5_prompt_kernel,
        grid=(G_B,),
        in_specs=[g3(rows, S5_W), g3(S5_W, S5_W), g3(S5_W, LANES), g3(LANES, S5_W),
                  g3(8, LANES), g3(8, LANES)],
        out_specs=[g3(rows, S5_W), g3(BATCH, LANES)],
        out_shape=[jax.ShapeDtypeStruct((G_B, rows, S5_W), BF16),
                   jax.ShapeDtypeStruct((G_B, BATCH, LANES), F32)],
        compiler_params=_cparams("parallel"),
        name="s5_prompt",
    )(u, m, w, v, p1, p2)


def _s5_sample_kernel(u_ref, x0_ref, m_ref, w_ref, v_ref, p1_ref, p2_ref, y_ref, x1_ref):
    u = u_ref[...]
    x0 = x0_ref[...]
    half = S5_W // 2
    x1_ref[...] = (p1_ref[7:8, :] * x0 + p2_ref[7:8, :] * pltpu.roll(x0, P_B, axis=1)
                   + _dot(u, w_ref[half:, :]))
    y = (_dot(x0.astype(BF16), v_ref[...])[:, 0:GROUP_B]
         + _dot(u, m_ref[half:, :])[:, S5_W - GROUP_B:])
    y_ref[...] = _gelu_tanh(y)


def _s5_sample(u, x0, m, w, v, p1, p2):
    g3 = lambda a, b: pl.BlockSpec((None, a, b), lambda g: (g, 0, 0))
    return pl.pallas_call(
        _s5_sample_kernel,
        grid=(G_B,),
        in_specs=[g3(DEC_BATCH, LANES), g3(DEC_BATCH, LANES), g3(S5_W, S5_W), g3(S5_W, LANES),
                  g3(LANES, S5_W), g3(8, LANES), g3(8, LANES)],
        out_specs=[g3(DEC_BATCH, GROUP_B), g3(DEC_BATCH, LANES)],
        out_shape=[jax.ShapeDtypeStruct((G_B, DEC_BATCH, GROUP_B), F32),
                   jax.ShapeDtypeStruct((G_B, DEC_BATCH, LANES), F32)],
        compiler_params=_cparams("parallel"),
        name="s5_sample",
    )(u, x0, m, w, v, p1, p2)


def _glu_kernel(y_ref, w_ref, b_ref, z_ref, o_ref):
    y = y_ref[...]
    gate = jax.nn.sigmoid(_dot(y.astype(BF16), w_ref[...]) + b_ref[...])
    o_ref[...] = (y.astype(F32) * gate * _silu(z_ref[...].astype(F32))).astype(o_ref.dtype)


def _glu(y, w, b, proj_b, tm):
    m = y.shape[0]
    return pl.pallas_call(
        _glu_kernel,
        grid=(m // tm,),
        in_specs=[pl.BlockSpec((tm, W_B), lambda i: (i, 0)),
                  pl.BlockSpec((W_B, W_B), lambda i: (0, 0)),
                  pl.BlockSpec((1, W_B), lambda i: (0, 0)),
                  pl.BlockSpec((tm, W_B), lambda i: (i, 1))],
        out_specs=pl.BlockSpec((tm, W_B), lambda i: (i, 0)),
        out_shape=jax.ShapeDtypeStruct((m, W_B), BF16),
        compiler_params=_cparams("parallel"),
        name="glu",
    )(y, w, b, proj_b)


def _rope_table_kernel(cos_ref, sin_ref, *, pos0):
    shape = cos_ref.shape
    pos = lax.broadcasted_iota(jnp.int32, shape, 0).astype(F32) + pos0
    j = lax.broadcasted_iota(jnp.int32, shape, 1).astype(F32)
    ang = pos * jnp.exp(j * (-math.log(ROPE_BASE) / (DK_C // 2)))
    cos_ref[...] = jnp.cos(ang)
    sin_ref[...] = jnp.sin(ang)


def _rope_table(rows, pos0):
    shape = jax.ShapeDtypeStruct((rows, DK_C // 2), F32)
    return pl.pallas_call(functools.partial(_rope_table_kernel, pos0=float(pos0)),
                          out_shape=[shape, shape], name="rope_table")()


def _rope(x, cos, sin):
    half = DK_C // 2
    x1, x2 = x[:, :half], x[:, half:]
    return jnp.concatenate([x1 * cos - x2 * sin, x1 * sin + x2 * cos], axis=-1)


def _ret_prompt_kernel(q_ref, k_ref, v_ref, z_ref, cos_ref, sin_ref, gh_ref, y_ref, s_out, s_s):
    c = pl.program_id(1)
    t = CHUNK

    @pl.when(c == 0)
    def _():
        s_s[...] = jnp.zeros_like(s_s)

    row = lax.broadcasted_iota(jnp.int32, (t, t), 0)
    col = lax.broadcasted_iota(jnp.int32, (t, t), 1)
    tril = col <= row
    diff = (row - col).astype(F32)
    tpos = lax.broadcasted_iota(jnp.int32, (t, 1), 0).astype(F32)
    cos, sin = cos_ref[...], sin_ref[...]
    for h in range(H_C):
        lg = LOG_GAMMA[h]
        ks = slice(h * DK_C, (h + 1) * DK_C)
        vs = slice(h * DV_C, (h + 1) * DV_C)
        q = _rope(q_ref[:, ks].astype(F32), cos, sin).astype(BF16)
        k32 = _rope(k_ref[:, ks].astype(F32), cos, sin)
        v = v_ref[:, vs]
        mask = jnp.where(tril, jnp.exp(diff * lg), 0.0) * SCALE_C
        s = _dot_nt(q, k32.astype(BF16)) * mask
        s_old = s_s[h]
        o = _dot(s.astype(BF16), v) + _dot(q, s_old.astype(BF16)) * jnp.exp((tpos + 1.0) * lg)
        y_ref[:, vs] = (_rms(o, gh_ref[h:h + 1, :]) * _silu(z_ref[:, vs].astype(F32))).astype(BF16)
        k_tail = k32 * (jnp.exp((t - 1.0 - tpos) * lg) * SCALE_C)
        s_s[h] = math.exp(t * lg) * s_old + _dot_tn(k_tail.astype(BF16), v)

    @pl.when(c == pl.num_programs(1) - 1)
    def _():
        s_out[...] = s_s[...]


def _ret_prompt(proj, cos, sin, g_head):
    nc = SEQ // CHUNK
    t = CHUNK
    qk_w = H_C * DK_C
    return pl.pallas_call(
        _ret_prompt_kernel,
        grid=(BATCH, nc),
        in_specs=[pl.BlockSpec((None, t, qk_w), lambda b, c: (b, c, 0)),
                  pl.BlockSpec((None, t, qk_w), lambda b, c: (b, c, 1)),
                  pl.BlockSpec((None, t, W_C), lambda b, c: (b, c, 1)),
                  pl.BlockSpec((None, t, W_C), lambda b, c: (b, c, 2)),
                  pl.BlockSpec((t, DK_C // 2), lambda b, c: (c, 0)),
                  pl.BlockSpec((t, DK_C // 2), lambda b, c: (c, 0)),
                  pl.BlockSpec((H_C, DV_C), lambda b, c: (0, 0))],
        out_specs=[pl.BlockSpec((None, t, W_C), lambda b, c: (b, c, 0)),
                   pl.BlockSpec((None, H_C, DK_C, DV_C), lambda b, c: (b, 0, 0, 0))],
        out_shape=[jax.ShapeDtypeStruct((BATCH, SEQ, W_C), BF16),
                   jax.ShapeDtypeStruct((BATCH, H_C, DK_C, DV_C), F32)],
        scratch_shapes=[pltpu.VMEM((H_C, DK_C, DV_C), F32)],
        compiler_params=_cparams("parallel", "arbitrary"),
        name="ret_prompt",
    )(proj, proj, proj, proj, cos, sin, g_head)


def _rope_sample_kernel(q_ref, k_ref, cos_ref, sin_ref, qo_ref, ko_ref):
    cos, sin = cos_ref[0:1, :], sin_ref[0:1, :]
    for h in range(H_C):
        ks = slice(h * DK_C, (h + 1) * DK_C)
        qo_ref[:, ks] = _rope(q_ref[:, ks], cos, sin)
        ko_ref[:, ks] = _rope(k_ref[:, ks], cos, sin)


def _rope_sample(proj, cos, sin):
    qk_w = H_C * DK_C
    shape = jax.ShapeDtypeStruct((DEC_BATCH, qk_w), F32)
    return pl.pallas_call(
        _rope_sample_kernel,
        grid=(1,),
        in_specs=[pl.BlockSpec((DEC_BATCH, qk_w), lambda i: (0, 0)),
                  pl.BlockSpec((DEC_BATCH, qk_w), lambda i: (0, 1)),
                  pl.BlockSpec((8, DK_C // 2), lambda i: (0, 0)),
                  pl.BlockSpec((8, DK_C // 2), lambda i: (0, 0))],
        out_specs=[pl.BlockSpec((DEC_BATCH, qk_w), lambda i: (0, 0)),
                   pl.BlockSpec((DEC_BATCH, qk_w), lambda i: (0, 0))],
        out_shape=[shape, shape],
        compiler_params=_cparams("arbitrary"),
        name="rope_sample",
    )(proj, proj, cos, sin)


def _ret_sample_kernel(q_ref, k_ref, v_ref, z_ref, qt_ref, kt_ref, gh_ref, s0_ref, y_ref, s1_ref):
    r = pl.program_id(0)
    rows = pl.ds(r, 1)
    shift = lax.rem(DEC_BATCH - r, DEC_BATCH)
    for h in range(H_C):
        gamma = math.exp(LOG_GAMMA[h])
        ks = slice(h * DK_C, (h + 1) * DK_C)
        vs = slice(h * DV_C, (h + 1) * DV_C)
        q_col = pltpu.roll(qt_ref[h], shift, axis=1)[:, 0:1]
        k_col = pltpu.roll(kt_ref[h], shift, axis=1)[:, 0:1]
        v_row = v_ref[rows, vs]
        s_old = s0_ref[0, h]
        qk = jnp.sum(q_ref[rows, ks] * k_ref[rows, ks], axis=1, keepdims=True) * SCALE_C
        o = qk * v_row + jnp.sum(s_old * q_col, axis=0, keepdims=True) * gamma
        y_ref[rows, vs] = _rms(o, gh_ref[h:h + 1, :]) * _silu(z_ref[rows, vs])
        s1_ref[0, h] = gamma * s_old + (k_col * SCALE_C) * v_row


def _ret_sample(q, k, proj, qt, kt, g_head, s0):
    qk_w = H_C * DK_C
    full = lambda shape: pl.BlockSpec(shape, lambda r: (0,) * len(shape))
    return pl.pallas_call(
        _ret_sample_kernel,
        grid=(DEC_BATCH,),
        in_specs=[full((DEC_BATCH, qk_w)), full((DEC_BATCH, qk_w)),
                  pl.BlockSpec((DEC_BATCH, W_C), lambda r: (0, 1)),
                  pl.BlockSpec((DEC_BATCH, W_C), lambda r: (0, 2)),
                  full((H_C, DK_C, DEC_BATCH)), full((H_C, DK_C, DEC_BATCH)),
                  full((H_C, DV_C)),
                  pl.BlockSpec((1, H_C, DK_C, DV_C), lambda r: (r, 0, 0, 0))],
        out_specs=[full((DEC_BATCH, W_C)),
                   pl.BlockSpec((1, H_C, DK_C, DV_C), lambda r: (r, 0, 0, 0))],
        out_shape=[jax.ShapeDtypeStruct((DEC_BATCH, W_C), F32),
                   jax.ShapeDtypeStruct((DEC_BATCH, H_C, DK_C, DV_C), F32)],
        compiler_params=_cparams("arbitrary"),
        name="ret_sample",
    )(q, k, proj, proj, qt, kt, g_head, s0)


def kernel(x_prompt, x_sample, state_mlstm_C, state_mlstm_n, state_mlstm_m, state_s5_re, state_s5_im, state_ret, g_pre, g_post, w_in0, b_gates0, g_head_a, lam_re, lam_im, log_dt, b_re, b_im, c_re, c_im, d_skip, w_glu, b_glu, w_out0, w_in1, g_head_c, w_out1):
    mp = BATCH * SEQ
    xp = x_prompt.reshape(mp, D_MODEL).astype(F32)
    xs = x_sample.reshape(DEC_BATCH, D_MODEL).astype(F32)

    w0a = w_in0[:, :QKV0].astype(BF16)
    w0g = jnp.pad(w_in0[:, QKV0:GATE0].astype(F32), ((0, 0), (0, LANES - 2 * H_A)))
    w0b = w_in0[:, GATE0:].astype(BF16)
    w1 = w_in1.astype(BF16)
    wo0 = w_out0.astype(BF16)
    wo1 = w_out1.astype(BF16)
    wg = w_glu.astype(BF16)
    bglu = b_glu.reshape(1, W_B).astype(F32)
    g_pre = g_pre.astype(F32)
    g_post = g_post.astype(F32)
    bg = b_gates0.astype(F32)

    dup = lambda a: jnp.concatenate([a, a], axis=-1).astype(F32)
    lamr, lami = dup(lam_re), dup(lam_im)
    b1 = jnp.concatenate([b_re.transpose(0, 2, 1), b_im.transpose(0, 2, 1)], axis=-1)
    cc = jnp.concatenate([c_re.transpose(0, 2, 1), c_im.transpose(0, 2, 1)], axis=1)
    s5m, s5w, s5v, s5p1, s5p2 = _s5_build(
        log_dt.reshape(G_B, 1, 1).astype(F32), lamr[:, None, :], lami[:, None, :],
        lamr[:, :, None], lami[:, :, None],
        jnp.tile(b1.astype(F32), (1, S5_T, 1)), jnp.tile(cc.astype(F32), (1, 1, S5_T)),
        jnp.tile(d_skip.astype(F32), (1, S5_T))[:, None, :])

    a0, gates = _norm_gates(xp, g_pre[0:1], w0g, 256)
    pa = _matmul(a0, w0a, BF16, 512, 1024)
    pb = _matmul(a0, w0b, BF16, 512, 1024)
    gates3 = gates.reshape(BATCH, SEQ, LANES)
    gates_row = gates3[:, :, :2 * H_A].transpose(0, 2, 1)
    ya, c_p, n_p, m_p = _mlstm_prompt(bg, pa.reshape(BATCH, SEQ, QKV0), gates3, gates_row,
                                      g_head_a.astype(F32))
    nblk = SEQ // S5_T
    u = pb[:, :W_B].reshape(BATCH, nblk, S5_T, G_B, GROUP_B).transpose(3, 0, 1, 2, 4)
    yb, xf = _s5_prompt(u.reshape(G_B, BATCH * nblk, S5_W), s5m, s5w, s5v, s5p1, s5p2)
    yb = yb.reshape(G_B, BATCH, nblk, S5_T, GROUP_B).transpose(1, 2, 3, 0, 4).reshape(mp, W_B)
    yb = _glu(yb, wg, bglu, pb, 512)
    h1 = _outproj(ya.reshape(mp, W_A), yb, wo0, xp, g_post[0:1], 512, 1024)
    s5r_p = xf[:, :, :P_B].transpose(1, 0, 2)
    s5i_p = xf[:, :, P_B:].transpose(1, 0, 2)

    a1 = _norm(h1, g_pre[1:2], 256)
    p1 = _matmul(a1, w1, BF16, 512, 1024)
    cos_p, sin_p = _rope_table(SEQ, 0)
    yc, s_p = _ret_prompt(p1.reshape(BATCH, SEQ, -1), cos_p, sin_p, g_head_c.astype(F32))
    y_p = _outproj(yc.reshape(mp, W_C), None, wo1, h1, g_post[1:2], 512, 1024)

    a0s, gates_s = _norm_gates(xs, g_pre[0:1], w0g, DEC_BATCH)
    pas = _matmul(a0s, w0a, F32, DEC_BATCH, 1024)
    pbs = _matmul(a0s, w0b, F32, DEC_BATCH, 1024)
    qk_a = H_A * DK_A
    to_cols = lambda a, nh, dk: a.reshape(DEC_BATCH, nh, dk).transpose(1, 2, 0)
    yas, c_s, n_s, m_s = _mlstm_sample(
        gates_s[:, :H_A], gates_s[:, H_A:2 * H_A], bg[None, :H_A], bg[None, H_A:],
        state_mlstm_m.astype(F32), pas,
        to_cols(pas[:, :qk_a], H_A, DK_A), to_cols(pas[:, qk_a:2 * qk_a], H_A, DK_A),
        state_mlstm_n.reshape(DEC_BATCH, qk_a).astype(F32), g_head_a.astype(F32),
        state_mlstm_C.astype(F32))
    us = pbs[:, :W_B].reshape(DEC_BATCH, G_B, GROUP_B).transpose(1, 0, 2)
    us = jnp.pad(us, ((0, 0), (0, 0), (LANES - GROUP_B, 0))).astype(BF16)
    x0 = jnp.concatenate([state_s5_re, state_s5_im], axis=-1).astype(F32).transpose(1, 0, 2)
    ybs, x1 = _s5_sample(us, x0, s5m, s5w, s5v, s5p1, s5p2)
    ybs = _glu(ybs.transpose(1, 0, 2).reshape(DEC_BATCH, W_B), wg, bglu, pbs, DEC_BATCH)
    h1s = _outproj(yas, ybs, wo0, xs, g_post[0:1], DEC_BATCH, 1024)
    s5r_s = x1[:, :, :P_B].transpose(1, 0, 2)
    s5i_s = x1[:, :, P_B:].transpose(1, 0, 2)

    a1s = _norm(h1s, g_pre[1:2], DEC_BATCH)
    p1s = _matmul(a1s, w1, F32, DEC_BATCH, 1024)
    cos_s, sin_s = _rope_table(8, PAST_LEN)
    qs, ks = _rope_sample(p1s, cos_s, sin_s)
    ycs, s_s = _ret_sample(qs, ks, p1s, to_cols(qs, H_C, DK_C), to_cols(ks, H_C, DK_C),
                           g_head_c.astype(F32), state_ret.astype(F32))
    y_s = _outproj(ycs, None, wo1, h1s, g_post[1:2], DEC_BATCH, 1024)

    return (y_p.reshape(BATCH, SEQ, D_MODEL), y_s.reshape(DEC_BATCH, 1, D_MODEL),
            c_p, n_p.reshape(BATCH, H_A, DK_A), m_p[:, :, 0, 0],
            s5r_p, s5i_p, s_p,
            c_s, n_s.reshape(DEC_BATCH, H_A, DK_A), m_s,
            s5r_s, s5i_s, s_s)
```

```python
import functools
import math

import jax
import jax.numpy as jnp
from jax import lax
from jax.experimental import pallas as pl
from jax.experimental.pallas import tpu as pltpu

F32 = jnp.float32
BF16 = jnp.bfloat16

D_MODEL = 2048
BATCH = 4
SEQ = 2048
DEC_BATCH = 128
PAST_LEN = 16384
H_A = 4
DK_A = 256
DV_A = 512
W_A = H_A * DV_A
GROUP_B = 16
G_B = 64
P_B = 64
W_B = G_B * GROUP_B
H_C = 8
DK_C = 256
DV_C = 512
W_C = H_C * DV_C
CHUNK = 128
NORM_EPS = 1e-6
ROPE_BASE = 10000.0
QKV0 = 2 * H_A * DK_A + 3 * W_A
GATE0 = QKV0 + 2 * H_A
SCALE_A = DK_A ** -0.5
SCALE_C = DK_C ** -0.5
LOG_GAMMA = tuple(math.log1p(-(2.0 ** (-5.0 - h))) for h in range(H_C))
S5_T = 16
S5_W = S5_T * GROUP_B
LANES = 128
OCT = LANES // GROUP_B
S5_LEVELS = int(math.log2(SEQ // S5_T))
S5_A1_ROW = S5_LEVELS
VMEM_LIMIT = 48 * 1024 * 1024


def _cparams(*sem):
    return pltpu.CompilerParams(dimension_semantics=sem, vmem_limit_bytes=VMEM_LIMIT)


def _dot(a, b):
    return jnp.dot(a, b, preferred_element_type=F32)


def _dot_nt(a, b):
    return lax.dot_general(a, b, (((1,), (1,)), ((), ())), preferred_element_type=F32)


def _dot_tn(a, b):
    return lax.dot_general(a, b, (((0,), (0,)), ((), ())), preferred_element_type=F32)


def _split(x):
    hi = x.astype(BF16)
    return hi, (x - hi.astype(F32)).astype(BF16)


def _dot3(a, b):
    a_hi, a_lo = _split(a)
    b_hi, b_lo = _split(b)
    return _dot(a_hi, b_hi) + _dot(a_hi, b_lo) + _dot(a_lo, b_hi)


def _log_sigmoid(x):
    return jnp.minimum(x, 0.0) - jnp.log1p(jnp.exp(-jnp.abs(x)))


def _silu(x):
    return x * jax.nn.sigmoid(x)


def _gelu_tanh(x):
    return 0.5 * x * (1.0 + jnp.tanh(math.sqrt(2.0 / math.pi) * (x + 0.044715 * (x * x * x))))


def _rms(x, g):
    return x * lax.rsqrt(jnp.mean(x * x, axis=-1, keepdims=True) + NORM_EPS) * g


def _norm_gates_kernel(x_ref, g_ref, wg_ref, a_ref, gates_ref):
    a = _rms(x_ref[...], g_ref[...])
    a_ref[...] = a.astype(BF16)
    gates_ref[...] = _dot3(a, wg_ref[...])


def _norm_kernel(x_ref, g_ref, a_ref):
    a_ref[...] = _rms(x_ref[...], g_ref[...]).astype(BF16)


def _norm_gates(x, g, wg, tm):
    m = x.shape[0]
    return pl.pallas_call(
        _norm_gates_kernel,
        grid=(m // tm,),
        in_specs=[pl.BlockSpec((tm, D_MODEL), lambda i: (i, 0)),
                  pl.BlockSpec((1, D_MODEL), lambda i: (0, 0)),
                  pl.BlockSpec((D_MODEL, LANES), lambda i: (0, 0))],
        out_specs=[pl.BlockSpec((tm, D_MODEL), lambda i: (i, 0)),
                   pl.BlockSpec((tm, LANES), lambda i: (i, 0))],
        out_shape=[jax.ShapeDtypeStruct((m, D_MODEL), BF16),
                   jax.ShapeDtypeStruct((m, LANES), F32)],
        compiler_params=_cparams("parallel"),
        name="norm_gates",
    )(x, g, wg)


def _norm(x, g, tm):
    m = x.shape[0]
    return pl.pallas_call(
        _norm_kernel,
        grid=(m // tm,),
        in_specs=[pl.BlockSpec((tm, D_MODEL), lambda i: (i, 0)),
                  pl.BlockSpec((1, D_MODEL), lambda i: (0, 0))],
        out_specs=pl.BlockSpec((tm, D_MODEL), lambda i: (i, 0)),
        out_shape=jax.ShapeDtypeStruct((m, D_MODEL), BF16),
        compiler_params=_cparams("parallel"),
        name="norm",
    )(x, g)


def _mm_kernel(x_ref, w_ref, o_ref):
    o_ref[...] = _dot(x_ref[...], w_ref[...]).astype(o_ref.dtype)


def _matmul(x, w, out_dtype, tm, tn):
    m, k = x.shape
    n = w.shape[1]
    return pl.pallas_call(
        _mm_kernel,
        grid=(n // tn, m // tm),
        in_specs=[pl.BlockSpec((tm, k), lambda j, i: (i, 0)),
                  pl.BlockSpec((k, tn), lambda j, i: (0, j))],
        out_specs=pl.BlockSpec((tm, tn), lambda j, i: (i, j)),
        out_shape=jax.ShapeDtypeStruct((m, n), out_dtype),
        compiler_params=_cparams("parallel", "parallel"),
        name="proj",
    )(x, w)


def _outproj_kernel(*refs, nka, nkb):
    if nkb:
        ya_ref, yb_ref, w_ref, h_ref, g_ref, o_ref, acc_ref = refs
    else:
        ya_ref, w_ref, h_ref, g_ref, o_ref, acc_ref = refs
    k = pl.program_id(1)

    @pl.when(k == 0)
    def _():
        acc_ref[...] = jnp.zeros_like(acc_ref)

    if nkb:
        @pl.when(k < nka)
        def _():
            acc_ref[...] += _dot(ya_ref[...].astype(BF16), w_ref[...])

        @pl.when(k >= nka)
        def _():
            acc_ref[...] += _dot(yb_ref[...].astype(BF16), w_ref[...])
    else:
        acc_ref[...] += _dot(ya_ref[...].astype(BF16), w_ref[...])

    @pl.when(k == nka + nkb - 1)
    def _():
        o_ref[...] = h_ref[...] + _rms(acc_ref[...], g_ref[...])


def _outproj(ya, yb, w, h, g, tm, tk):
    m = h.shape[0]
    nka = ya.shape[1] // tk
    nkb = 0 if yb is None else yb.shape[1] // tk
    in_specs = [pl.BlockSpec((tm, tk), lambda i, k: (i, jnp.minimum(k, nka - 1)))]
    args = [ya]
    if nkb:
        in_specs.append(pl.BlockSpec((tm, tk), lambda i, k: (i, jnp.maximum(k - nka, 0))))
        args.append(yb)
    in_specs += [pl.BlockSpec((tk, D_MODEL), lambda i, k: (k, 0)),
                 pl.BlockSpec((tm, D_MODEL), lambda i, k: (i, 0)),
                 pl.BlockSpec((1, D_MODEL), lambda i, k: (0, 0))]
    args += [w, h, g]
    return pl.pallas_call(
        functools.partial(_outproj_kernel, nka=nka, nkb=nkb),
        grid=(m // tm, nka + nkb),
        in_specs=in_specs,
        out_specs=pl.BlockSpec((tm, D_MODEL), lambda i, k: (i, 0)),
        out_shape=jax.ShapeDtypeStruct((m, D_MODEL), F32),
        scratch_shapes=[pltpu.VMEM((tm, D_MODEL), F32)],
        compiler_params=_cparams("parallel", "arbitrary"),
        name="outproj",
    )(*args)


def _mlstm_prompt_kernel(bg_ref, q_ref, k_ref, v_ref, o_ref, z_ref, gc_ref, gr_ref, gh_ref,
                         y_ref, c_out, n_out, m_out, c_s, n_s, m_s):
    c = pl.program_id(1)
    t = CHUNK

    @pl.when(c == 0)
    def _():
        c_s[...] = jnp.zeros_like(c_s)
        n_s[...] = jnp.zeros_like(n_s)
        m_s[...] = jnp.zeros_like(m_s)

    row = lax.broadcasted_iota(jnp.int32, (t, t), 0)
    col = lax.broadcasted_iota(jnp.int32, (t, t), 1)
    tril = col <= row
    triu = row <= col
    gc = gc_ref[...]
    gr = gr_ref[...]
    for h in range(H_A):
        b_i = bg_ref[h]
        b_f = bg_ref[H_A + h]
        i_col = gc[:, h:h + 1] + b_i
        i_row = gr[h:h + 1, :] + b_i
        lf_col = _log_sigmoid(gc[:, H_A + h:H_A + h + 1] + b_f)
        lf_row = _log_sigmoid(gr[H_A + h:H_A + h + 1, :] + b_f)
        b_col = jnp.sum(jnp.where(tril, lf_row, 0.0), axis=1, keepdims=True)
        b_row = jnp.sum(jnp.where(triu, lf_col, 0.0), axis=0, keepdims=True)
        m_prev = m_s[h][:, 0:1]
        d = jnp.where(tril, b_col - b_row + i_row, -jnp.inf)
        inter = b_col + m_prev
        m_t = jnp.maximum(inter, jnp.max(d, axis=1, keepdims=True))
        w_intra = jnp.exp(d - m_t)
        w_inter = jnp.exp(inter - m_t) * SCALE_A
        q = q_ref[:, h * DK_A:(h + 1) * DK_A]
        k = k_ref[:, h * DK_A:(h + 1) * DK_A]
        v = v_ref[:, h * DV_A:(h + 1) * DV_A]
        s = _dot_nt(q, k) * (w_intra * SCALE_A)
        c_old = c_s[h]
        n_old = n_s[h]
        num = _dot(s.astype(BF16), v) + w_inter * _dot(q, c_old.astype(BF16))
        qn = jnp.sum(q.astype(F32) * n_old, axis=1, keepdims=True)
        den = jnp.sum(s, axis=1, keepdims=True) + w_inter * qn
        hh = num / jnp.maximum(jnp.abs(den), jnp.exp(-m_t))
        hn = _rms(hh, gh_ref[h:h + 1, :])
        o = o_ref[:, h * DV_A:(h + 1) * DV_A].astype(F32)
        z = z_ref[:, h * DV_A:(h + 1) * DV_A].astype(F32)
        y_ref[:, h * DV_A:(h + 1) * DV_A] = (hn * jax.nn.sigmoid(o) * _silu(z)).astype(BF16)
        b_last = b_col[t - 1:t, :]
        g_col = b_last - b_col + i_col
        m_new = jnp.maximum(b_last + m_prev, jnp.max(g_col, axis=0, keepdims=True))
        e_col = jnp.exp(g_col - m_new)
        decay = jnp.exp(b_last + m_prev - m_new)
        ke = k.astype(F32) * e_col
        c_s[h] = decay * c_old + _dot_tn(ke.astype(BF16), v)
        n_s[h] = decay * n_old + jnp.sum(ke, axis=0, keepdims=True)
        m_s[h] = jnp.broadcast_to(m_new, (1, LANES))

    @pl.when(c == pl.num_programs(1) - 1)
    def _():
        c_out[...] = c_s[...]
        n_out[...] = n_s[...]
        m_out[...] = m_s[...]


def _mlstm_prompt(bg, proj, gates_col, gates_row, g_head):
    nc = SEQ // CHUNK
    t = CHUNK
    qk_w = H_A * DK_A
    return pl.pallas_call(
        _mlstm_prompt_kernel,
        grid=(BATCH, nc),
        in_specs=[pl.BlockSpec(memory_space=pltpu.SMEM),
                  pl.BlockSpec((None, t, qk_w), lambda b, c: (b, c, 0)),
                  pl.BlockSpec((None, t, qk_w), lambda b, c: (b, c, 1)),
                  pl.BlockSpec((None, t, W_A), lambda b, c: (b, c, 1)),
                  pl.BlockSpec((None, t, W_A), lambda b, c: (b, c, 2)),
                  pl.BlockSpec((None, t, W_A), lambda b, c: (b, c, 3)),
                  pl.BlockSpec((None, t, LANES), lambda b, c: (b, c, 0)),
                  pl.BlockSpec((None, 2 * H_A, t), lambda b, c: (b, 0, c)),
                  pl.BlockSpec((H_A, DV_A), lambda b, c: (0, 0))],
        out_specs=[pl.BlockSpec((None, t, W_A), lambda b, c: (b, c, 0)),
                   pl.BlockSpec((None, H_A, DK_A, DV_A), lambda b, c: (b, 0, 0, 0)),
                   pl.BlockSpec((None, H_A, 1, DK_A), lambda b, c: (b, 0, 0, 0)),
                   pl.BlockSpec((None, H_A, 1, LANES), lambda b, c: (b, 0, 0, 0))],
        out_shape=[jax.ShapeDtypeStruct((BATCH, SEQ, W_A), BF16),
                   jax.ShapeDtypeStruct((BATCH, H_A, DK_A, DV_A), F32),
                   jax.ShapeDtypeStruct((BATCH, H_A, 1, DK_A), F32),
                   jax.ShapeDtypeStruct((BATCH, H_A, 1, LANES), F32)],
        scratch_shapes=[pltpu.VMEM((H_A, DK_A, DV_A), F32),
                        pltpu.VMEM((H_A, 1, DK_A), F32),
                        pltpu.VMEM((H_A, 1, LANES), F32)],
        compiler_params=_cparams("parallel", "arbitrary"),
        name="mlstm_prompt",
    )(bg, proj, proj, proj, proj, proj, gates_col, gates_row, g_head)


def _mlstm_sample_kernel(gi_ref, gf_ref, bi_ref, bf_ref, m0_ref, q_ref, k_ref, v_ref, o_ref, z_ref,
                         qt_ref, kt_ref, n0_ref, gh_ref, c0_ref,
                         y_ref, c1_ref, n1_ref, m1_ref):
    r = pl.program_id(0)
    rows = pl.ds(r, 1)
    shift = lax.rem(DEC_BATCH - r, DEC_BATCH)
    i_v = gi_ref[rows, :] + bi_ref[...]
    lf_v = _log_sigmoid(gf_ref[rows, :] + bf_ref[...])
    m0_v = m0_ref[rows, :]
    m_t = jnp.maximum(lf_v + m0_v, i_v)
    w_in = jnp.exp(i_v - m_t)
    w_st = jnp.exp(lf_v + m0_v - m_t)
    floor = jnp.exp(-m_t)
    m1_ref[rows, :] = m_t
    for h in range(H_A):
        ks = slice(h * DK_A, (h + 1) * DK_A)
        vs = slice(h * DV_A, (h + 1) * DV_A)
        wi = w_in[:, h:h + 1]
        ws = w_st[:, h:h + 1]
        q_col = pltpu.roll(qt_ref[h], shift, axis=1)[:, 0:1]
        k_col = pltpu.roll(kt_ref[h], shift, axis=1)[:, 0:1]
        q_row = q_ref[rows, ks]
        k_row = k_ref[rows, ks]
        v_row = v_ref[rows, vs]
        n_row = n0_ref[rows, ks]
        c_old = c0_ref[0, h]
        qk = jnp.sum(q_row * k_row, axis=1, keepdims=True) * SCALE_A
        s = qk * wi
        q_c = jnp.sum(c_old * q_col, axis=0, keepdims=True) * SCALE_A
        qn = jnp.sum(q_row * n_row, axis=1, keepdims=True) * SCALE_A
        num = s * v_row + ws * q_c
        den = s + ws * qn
        hh = num / jnp.maximum(jnp.abs(den), floor[:, h:h + 1])
        hn = _rms(hh, gh_ref[h:h + 1, :])
        y_ref[rows, vs] = hn * jax.nn.sigmoid(o_ref[rows, vs]) * _silu(z_ref[rows, vs])
        c1_ref[0, h] = ws * c_old + (wi * k_col) * v_row
        n1_ref[rows, ks] = ws * n_row + wi * k_row


def _mlstm_sample(gi, gf, bi, bf, m0, proj, qt, kt, n0, g_head, c0):
    qk_w = H_A * DK_A
    full = lambda shape: pl.BlockSpec(shape, lambda r: (0,) * len(shape))
    return pl.pallas_call(
        _mlstm_sample_kernel,
        grid=(DEC_BATCH,),
        in_specs=[full((DEC_BATCH, H_A)), full((DEC_BATCH, H_A)), full((1, H_A)), full((1, H_A)),
                  full((DEC_BATCH, H_A)),
                  pl.BlockSpec((DEC_BATCH, qk_w), lambda r: (0, 0)),
                  pl.BlockSpec((DEC_BATCH, qk_w), lambda r: (0, 1)),
                  pl.BlockSpec((DEC_BATCH, W_A), lambda r: (0, 1)),
                  pl.BlockSpec((DEC_BATCH, W_A), lambda r: (0, 2)),
                  pl.BlockSpec((DEC_BATCH, W_A), lambda r: (0, 3)),
                  full((H_A, DK_A, DEC_BATCH)), full((H_A, DK_A, DEC_BATCH)),
                  full((DEC_BATCH, qk_w)), full((H_A, DV_A)),
                  pl.BlockSpec((1, H_A, DK_A, DV_A), lambda r: (r, 0, 0, 0))],
        out_specs=[full((DEC_BATCH, W_A)),
                   pl.BlockSpec((1, H_A, DK_A, DV_A), lambda r: (r, 0, 0, 0)),
                   full((DEC_BATCH, qk_w)), full((DEC_BATCH, H_A))],
        out_shape=[jax.ShapeDtypeStruct((DEC_BATCH, W_A), F32),
                   jax.ShapeDtypeStruct((DEC_BATCH, H_A, DK_A, DV_A), F32),
                   jax.ShapeDtypeStruct((DEC_BATCH, qk_w), F32),
                   jax.ShapeDtypeStruct((DEC_BATCH, H_A), F32)],
        compiler_params=_cparams("arbitrary"),
        name="mlstm_sample",
    )(gi, gf, bi, bf, m0, proj, proj, proj, proj, proj, qt, kt, n0, g_head, c0)


def _s5_build_kernel(ldt_ref, lamr_ref, lami_ref, lamrc_ref, lamic_ref, b1_ref, cc_ref, dsk_ref,
                     m_ref, w_ref, v_ref, p1_ref, p2_ref):
    dt = jnp.exp(ldt_ref[...])

    def disc(lam_re, lam_im):
        mag = jnp.exp(lam_re * dt)
        ang = lam_im * dt
        a_re, a_im = mag * jnp.cos(ang), mag * jnp.sin(ang)
        den = lam_re * lam_re + lam_im * lam_im
        f_re = ((a_re - 1.0) * lam_re + a_im * lam_im) / den
        f_im = (a_im * lam_re - (a_re - 1.0) * lam_im) / den
        return a_re, a_im, f_re, f_im

    def power(lam_re, lam_im, tau):
        mag = jnp.exp(tau * (lam_re * dt))
        ang = tau * (lam_im * dt)
        return mag * jnp.cos(ang), mag * jnp.sin(ang)

    lam_re, lam_im = lamr_ref[...], lami_ref[...]
    a_re, a_im, f_re, f_im = disc(lam_re, lam_im)
    srow = lax.broadcasted_iota(jnp.int32, (S5_W, LANES), 0) // GROUP_B
    p_re, p_im = power(lam_re, lam_im, (S5_T - 1 - srow).astype(F32))
    af_re = p_re * f_re - p_im * f_im
    af_im = p_re * f_im + p_im * f_re
    lane = lax.broadcasted_iota(jnp.int32, (1, LANES), 1)
    sgn_lane = jnp.where(lane < P_B, -1.0, 1.0)
    b1 = b1_ref[...]
    b2 = sgn_lane * pltpu.roll(b1, P_B, axis=1)
    w_ref[...] = (af_re * b1 + af_im * b2).astype(BF16)

    lam_re_c, lam_im_c = lamrc_ref[...], lamic_ref[...]
    ac_re, ac_im, fc_re, fc_im = disc(lam_re_c, lam_im_c)
    tlane = lax.broadcasted_iota(jnp.int32, (LANES, S5_W), 1) // GROUP_B
    q_re, q_im = power(lam_re_c, lam_im_c, tlane.astype(F32))
    cc = cc_ref[...]
    cs = pltpu.roll(cc, P_B, axis=0)
    rowi = lax.broadcasted_iota(jnp.int32, (LANES, 1), 0)
    sgn_row = jnp.where(rowi < P_B, 1.0, -1.0)

    def readout(r_re, r_im):
        return sgn_row * (r_re * cc) - r_im * cs

    v_ref[...] = readout(q_re * ac_re - q_im * ac_im, q_re * ac_im + q_im * ac_re).astype(BF16)
    vf = readout(q_re * fc_re - q_im * fc_im, q_re * fc_im + q_im * fc_re)
    kw = _dot3(b1[0:GROUP_B, :], vf)
    lane_w = lax.broadcasted_iota(jnp.int32, (GROUP_B, S5_W), 1)
    blocks = [kw]
    for s in range(1, S5_T):
        blocks.append(jnp.where(lane_w >= s * GROUP_B, pltpu.roll(kw, s * GROUP_B, axis=1), 0.0))
    toep = jnp.concatenate(blocks, axis=0)
    ri = lax.broadcasted_iota(jnp.int32, (S5_W, S5_W), 0)
    ci = lax.broadcasted_iota(jnp.int32, (S5_W, S5_W), 1)
    m_ref[...] = (toep + jnp.where(ri == ci, dsk_ref[...], 0.0)).astype(BF16)

    r_re, r_im = power(lam_re, lam_im, float(S5_T))
    rid = lax.broadcasted_iota(jnp.int32, (8, LANES), 0)
    p1 = jnp.where(rid == S5_A1_ROW, a_re, 0.0)
    p2 = jnp.where(rid == S5_A1_ROW, sgn_lane * a_im, 0.0)
    for kk in range(S5_LEVELS):
        p1 = jnp.where(rid == kk, r_re, p1)
        p2 = jnp.where(rid == kk, sgn_lane * r_im, p2)
        r_re, r_im = r_re * r_re - r_im * r_im, 2.0 * (r_re * r_im)
    p1_ref[...] = p1
    p2_ref[...] = p2


def _s5_build(ldt, lamr, lami, lamrc, lamic, b1, cc, dsk):
    g3 = lambda a, b: pl.BlockSpec((None, a, b), lambda g: (g, 0, 0))
    return pl.pallas_call(
        _s5_build_kernel,
        grid=(G_B,),
        in_specs=[g3(1, 1), g3(1, LANES), g3(1, LANES), g3(LANES, 1), g3(LANES, 1),
                  g3(S5_W, LANES), g3(LANES, S5_W), g3(1, S5_W)],
        out_specs=[g3(S5_W, S5_W), g3(S5_W, LANES), g3(LANES, S5_W), g3(8, LANES), g3(8, LANES)],
        out_shape=[jax.ShapeDtypeStruct((G_B, S5_W, S5_W), BF16),
                   jax.ShapeDtypeStruct((G_B, S5_W, LANES), BF16),
                   jax.ShapeDtypeStruct((G_B, LANES, S5_W), BF16),
                   jax.ShapeDtypeStruct((G_B, 8, LANES), F32),
                   jax.ShapeDtypeStruct((G_B, 8, LANES), F32)],
        compiler_params=_cparams("parallel"),
        name="s5_build",
    )(ldt, lamr, lami, lamrc, lamic, b1, cc, dsk)


def _s5_prompt_kernel(u_ref, m_ref, w_ref, v_ref, p1_ref, p2_ref, y_ref, xf_ref, y_s):
    nblk = SEQ // S5_T
    rows = BATCH * nblk
    bidx = lax.broadcasted_iota(jnp.int32, (rows, LANES), 0) & (nblk - 1)
    steps = [u_ref[pl.ds(s, rows, stride=S5_T), :] for s in range(S5_T)]
    for g in range(OCT):
        gl = slice(g * GROUP_B, (g + 1) * GROUP_B)
        u = jnp.concatenate([x[:, gl] for x in steps], axis=-1).astype(BF16)
        s = _dot(u, w_ref[g])
        for kk in range(S5_LEVELS):
            sh = 1 << kk
            r = jnp.where(bidx >= sh, pltpu.roll(s, sh, axis=0), 0.0)
            s = s + p1_ref[g, kk:kk + 1, :] * r + p2_ref[g, kk:kk + 1, :] * pltpu.roll(r, P_B, axis=1)
        x_prev = jnp.where(bidx >= 1, pltpu.roll(s, 1, axis=0), 0.0)
        y = _dot(u, m_ref[g]) + _dot(x_prev.astype(BF16), v_ref[g])
        y_s[g] = _gelu_tanh(y)
        for b in range(BATCH):
            xf_ref[g, b:b + 1, :] = s[(b + 1) * nblk - 1:(b + 1) * nblk, :]
    for t in range(S5_T):
        half = slice((t // OCT) * LANES, (t // OCT + 1) * LANES)
        tl = slice((t % OCT) * GROUP_B, (t % OCT + 1) * GROUP_B)
        y_ref[pl.ds(t, rows, stride=S5_T), :] = jnp.concatenate(
            [y_s[g, :, half][:, tl] for g in range(OCT)], axis=-1)


def _s5_prompt(u, m, w, v, p1, p2):
    mp = BATCH * SEQ
    rows = BATCH * (SEQ // S5_T)
    o3 = lambda a, b: pl.BlockSpec((OCT, a, b), lambda g: (g, 0, 0))
    return pl.pallas_call(
        _s5_prompt_kernel,
        grid=(G_B // OCT,),
        in_specs=[pl.BlockSpec((mp, LANES), lambda g: (0, g)),
                  o3(S5_W, S5_W), o3(S5_W, LANES), o3(LANES, S5_W), o3(8, LANES), o3(8, LANES)],
        out_specs=[pl.BlockSpec((mp, LANES), lambda g: (0, g)), o3(BATCH, LANES)],
        out_shape=[jax.ShapeDtypeStruct((mp, W_B), F32),
                   jax.ShapeDtypeStruct((G_B, BATCH, LANES), F32)],
        scratch_shapes=[pltpu.VMEM((OCT, rows, S5_W), F32)],
        compiler_params=_cparams("parallel"),
        name="s5_prompt",
    )(u, m, w, v, p1, p2)


def _s5_sample_kernel(u_ref, x0_ref, m_ref, w_ref, v_ref, p1_ref, p2_ref, y_ref, x1_ref):
    u = u_ref[...]
    x0 = x0_ref[...]
    half = S5_W // 2
    a1 = slice(S5_A1_ROW, S5_A1_ROW + 1)
    x1_ref[...] = (p1_ref[a1, :] * x0 + p2_ref[a1, :] * pltpu.roll(x0, P_B, axis=1)
                   + _dot(u, w_ref[half:, :]))
    y = (_dot(x0.astype(BF16), v_ref[...])[:, 0:GROUP_B]
         + _dot(u, m_ref[half:, :])[:, S5_W - GROUP_B:])
    y_ref[...] = _gelu_tanh(y)


def _s5_sample(u, x0, m, w, v, p1, p2):
    g3 = lambda a, b: pl.BlockSpec((None, a, b), lambda g: (g, 0, 0))
    return pl.pallas_call(
        _s5_sample_kernel,
        grid=(G_B,),
        in_specs=[g3(DEC_BATCH, LANES), g3(DEC_BATCH, LANES), g3(S5_W, S5_W), g3(S5_W, LANES),
                  g3(LANES, S5_W), g3(8, LANES), g3(8, LANES)],
        out_specs=[g3(DEC_BATCH, GROUP_B), g3(DEC_BATCH, LANES)],
        out_shape=[jax.ShapeDtypeStruct((G_B, DEC_BATCH, GROUP_B), F32),
                   jax.ShapeDtypeStruct((G_B, DEC_BATCH, LANES), F32)],
        compiler_params=_cparams("parallel"),
        name="s5_sample",
    )(u, x0, m, w, v, p1, p2)


def _glu_kernel(y_ref, w_ref, b_ref, z_ref, o_ref):
    y = y_ref[...]
    gate = jax.nn.sigmoid(_dot(y.astype(BF16), w_ref[...]) + b_ref[...])
    o_ref[...] = (y.astype(F32) * gate * _silu(z_ref[...].astype(F32))).astype(o_ref.dtype)


def _glu(y, w, b, proj_b, tm):
    m = y.shape[0]
    return pl.pallas_call(
        _glu_kernel,
        grid=(m // tm,),
        in_specs=[pl.BlockSpec((tm, W_B), lambda i: (i, 0)),
                  pl.BlockSpec((W_B, W_B), lambda i: (0, 0)),
                  pl.BlockSpec((1, W_B), lambda i: (0, 0)),
                  pl.BlockSpec((tm, W_B), lambda i: (i, 1))],
        out_specs=pl.BlockSpec((tm, W_B), lambda i: (i, 0)),
        out_shape=jax.ShapeDtypeStruct((m, W_B), BF16),
        compiler_params=_cparams("parallel"),
        name="glu",
    )(y, w, b, proj_b)


def _rope_table_kernel(cos_ref, sin_ref, *, pos0):
    shape = cos_ref.shape
    pos = lax.broadcasted_iota(jnp.int32, shape, 0).astype(F32) + pos0
    j = lax.broadcasted_iota(jnp.int32, shape, 1).astype(F32)
    ang = pos * jnp.power(ROPE_BASE, -(j / (DK_C // 2)))
    cos_ref[...] = jnp.cos(ang)
    sin_ref[...] = jnp.sin(ang)


def _rope_table(rows, pos0):
    shape = jax.ShapeDtypeStruct((rows, DK_C // 2), F32)
    return pl.pallas_call(functools.partial(_rope_table_kernel, pos0=float(pos0)),
                          out_shape=[shape, shape], name="rope_table")()


def _rope(x, cos, sin):
    half = DK_C // 2
    x1, x2 = x[:, :half], x[:, half:]
    return jnp.concatenate([x1 * cos - x2 * sin, x1 * sin + x2 * cos], axis=-1)


def _ret_prompt_kernel(q_ref, k_ref, v_ref, z_ref, cos_ref, sin_ref, gh_ref, y_ref, s_out, s_s):
    c = pl.program_id(1)
    t = CHUNK

    @pl.when(c == 0)
    def _():
        s_s[...] = jnp.zeros_like(s_s)

    row = lax.broadcasted_iota(jnp.int32, (t, t), 0)
    col = lax.broadcasted_iota(jnp.int32, (t, t), 1)
    tril = col <= row
    diff = (row - col).astype(F32)
    tpos = lax.broadcasted_iota(jnp.int32, (t, 1), 0).astype(F32)
    cos, sin = cos_ref[...], sin_ref[...]
    for h in range(H_C):
        lg = LOG_GAMMA[h]
        ks = slice(h * DK_C, (h + 1) * DK_C)
        vs = slice(h * DV_C, (h + 1) * DV_C)
        q = _rope(q_ref[:, ks].astype(F32), cos, sin).astype(BF16)
        k32 = _rope(k_ref[:, ks].astype(F32), cos, sin)
        v = v_ref[:, vs]
        mask = jnp.where(tril, jnp.exp(diff * lg), 0.0) * SCALE_C
        s = _dot_nt(q, k32.astype(BF16)) * mask
        s_old = s_s[h]
        o = _dot(s.astype(BF16), v) + _dot(q, s_old.astype(BF16)) * jnp.exp((tpos + 1.0) * lg)
        y_ref[:, vs] = (_rms(o, gh_ref[h:h + 1, :]) * _silu(z_ref[:, vs].astype(F32))).astype(BF16)
        k_tail = k32 * (jnp.exp((t - 1.0 - tpos) * lg) * SCALE_C)
        s_s[h] = math.exp(t * lg) * s_old + _dot_tn(k_tail.astype(BF16), v)

    @pl.when(c == pl.num_programs(1) - 1)
    def _():
        s_out[...] = s_s[...]


def _ret_prompt(proj, cos, sin, g_head):
    nc = SEQ // CHUNK
    t = CHUNK
    qk_w = H_C * DK_C
    return pl.pallas_call(
        _ret_prompt_kernel,
        grid=(BATCH, nc),
        in_specs=[pl.BlockSpec((None, t, qk_w), lambda b, c: (b, c, 0)),
                  pl.BlockSpec((None, t, qk_w), lambda b, c: (b, c, 1)),
                  pl.BlockSpec((None, t, W_C), lambda b, c: (b, c, 1)),
                  pl.BlockSpec((None, t, W_C), lambda b, c: (b, c, 2)),
                  pl.BlockSpec((t, DK_C // 2), lambda b, c: (c, 0)),
                  pl.BlockSpec((t, DK_C // 2), lambda b, c: (c, 0)),
                  pl.BlockSpec((H_C, DV_C), lambda b, c: (0, 0))],
        out_specs=[pl.BlockSpec((None, t, W_C), lambda b, c: (b, c, 0)),
                   pl.BlockSpec((None, H_C, DK_C, DV_C), lambda b, c: (b, 0, 0, 0))],
        out_shape=[jax.ShapeDtypeStruct((BATCH, SEQ, W_C), BF16),
                   jax.ShapeDtypeStruct((BATCH, H_C, DK_C, DV_C), F32)],
        scratch_shapes=[pltpu.VMEM((H_C, DK_C, DV_C), F32)],
        compiler_params=_cparams("parallel", "arbitrary"),
        name="ret_prompt",
    )(proj, proj, proj, proj, cos, sin, g_head)


def _rope_sample_kernel(q_ref, k_ref, cos_ref, sin_ref, qo_ref, ko_ref):
    cos, sin = cos_ref[0:1, :], sin_ref[0:1, :]
    for h in range(H_C):
        ks = slice(h * DK_C, (h + 1) * DK_C)
        qo_ref[:, ks] = _rope(q_ref[:, ks], cos, sin)
        ko_ref[:, ks] = _rope(k_ref[:, ks], cos, sin)


def _rope_sample(proj, cos, sin):
    qk_w = H_C * DK_C
    shape = jax.ShapeDtypeStruct((DEC_BATCH, qk_w), F32)
    return pl.pallas_call(
        _rope_sample_kernel,
        grid=(1,),
        in_specs=[pl.BlockSpec((DEC_BATCH, qk_w), lambda i: (0, 0)),
                  pl.BlockSpec((DEC_BATCH, qk_w), lambda i: (0, 1)),
                  pl.BlockSpec((8, DK_C // 2), lambda i: (0, 0)),
                  pl.BlockSpec((8, DK_C // 2), lambda i: (0, 0))],
        out_specs=[pl.BlockSpec((DEC_BATCH, qk_w), lambda i: (0, 0)),
                   pl.BlockSpec((DEC_BATCH, qk_w), lambda i: (0, 0))],
        out_shape=[shape, shape],
        compiler_params=_cparams("arbitrary"),
        name="rope_sample",
    )(proj, proj, cos, sin)


def _ret_sample_kernel(q_ref, k_ref, v_ref, z_ref, qt_ref, kt_ref, gh_ref, s0_ref, y_ref, s1_ref):
    r = pl.program_id(0)
    rows = pl.ds(r, 1)
    shift = lax.rem(DEC_BATCH - r, DEC_BATCH)
    for h in range(H_C):
        gamma = math.exp(LOG_GAMMA[h])
        ks = slice(h * DK_C, (h + 1) * DK_C)
        vs = slice(h * DV_C, (h + 1) * DV_C)
        q_col = pltpu.roll(qt_ref[h], shift, axis=1)[:, 0:1]
        k_col = pltpu.roll(kt_ref[h], shift, axis=1)[:, 0:1]
        v_row = v_ref[rows, vs]
        s_old = s0_ref[0, h]
        qk = jnp.sum(q_ref[rows, ks] * k_ref[rows, ks], axis=1, keepdims=True) * SCALE_C
        o = qk * v_row + jnp.sum(s_old * q_col, axis=0, keepdims=True) * gamma
        y_ref[rows, vs] = _rms(o, gh_ref[h:h + 1, :]) * _silu(z_ref[rows, vs])
        s1_ref[0, h] = gamma * s_old + (k_col * SCALE_C) * v_row


def _ret_sample(q, k, proj, qt, kt, g_head, s0):
    qk_w = H_C * DK_C
    full = lambda shape: pl.BlockSpec(shape, lambda r: (0,) * len(shape))
    return pl.pallas_call(
        _ret_sample_kernel,
        grid=(DEC_BATCH,),
        in_specs=[full((DEC_BATCH, qk_w)), full((DEC_BATCH, qk_w)),
                  pl.BlockSpec((DEC_BATCH, W_C), lambda r: (0, 1)),
                  pl.BlockSpec((DEC_BATCH, W_C), lambda r: (0, 2)),
                  full((H_C, DK_C, DEC_BATCH)), full((H_C, DK_C, DEC_BATCH)),
                  full((H_C, DV_C)),
                  pl.BlockSpec((1, H_C, DK_C, DV_C), lambda r: (r, 0, 0, 0))],
        out_specs=[full((DEC_BATCH, W_C)),
                   pl.BlockSpec((1, H_C, DK_C, DV_C), lambda r: (r, 0, 0, 0))],
        out_shape=[jax.ShapeDtypeStruct((DEC_BATCH, W_C), F32),
                   jax.ShapeDtypeStruct((DEC_BATCH, H_C, DK_C, DV_C), F32)],
        compiler_params=_cparams("arbitrary"),
        name="ret_sample",
    )(q, k, proj, proj, qt, kt, g_head, s0)


def kernel(x_prompt, x_sample, state_mlstm_C, state_mlstm_n, state_mlstm_m, state_s5_re, state_s5_im, state_ret, g_pre, g_post, w_in0, b_gates0, g_head_a, lam_re, lam_im, log_dt, b_re, b_im, c_re, c_im, d_skip, w_glu, b_glu, w_out0, w_in1, g_head_c, w_out1):
    mp = BATCH * SEQ
    xp = x_prompt.reshape(mp, D_MODEL).astype(F32)
    xs = x_sample.reshape(DEC_BATCH, D_MODEL).astype(F32)

    w0a = w_in0[:, :QKV0].astype(BF16)
    w0g = jnp.pad(w_in0[:, QKV0:GATE0].astype(F32), ((0, 0), (0, LANES - 2 * H_A)))
    w0b = w_in0[:, GATE0:].astype(BF16)
    w1 = w_in1.astype(BF16)
    wo0 = w_out0.astype(BF16)
    wo1 = w_out1.astype(BF16)
    wg = w_glu.astype(BF16)
    bglu = b_glu.reshape(1, W_B).astype(F32)
    g_pre = g_pre.astype(F32)
    g_post = g_post.astype(F32)
    bg = b_gates0.astype(F32)

    dup = lambda a: jnp.concatenate([a, a], axis=-1).astype(F32)
    lamr, lami = dup(lam_re), dup(lam_im)
    b1 = jnp.concatenate([b_re.transpose(0, 2, 1), b_im.transpose(0, 2, 1)], axis=-1)
    cc = jnp.concatenate([c_re.transpose(0, 2, 1), c_im.transpose(0, 2, 1)], axis=1)
    s5m, s5w, s5v, s5p1, s5p2 = _s5_build(
        log_dt.reshape(G_B, 1, 1).astype(F32), lamr[:, None, :], lami[:, None, :],
        lamr[:, :, None], lami[:, :, None],
        jnp.tile(b1.astype(F32), (1, S5_T, 1)), jnp.tile(cc.astype(F32), (1, 1, S5_T)),
        jnp.tile(d_skip.astype(F32), (1, S5_T))[:, None, :])

    a0, gates = _norm_gates(xp, g_pre[0:1], w0g, 256)
    pa = _matmul(a0, w0a, BF16, 512, 1024)
    pb = _matmul(a0, w0b, F32, 512, 1024)
    gates3 = gates.reshape(BATCH, SEQ, LANES)
    gates_row = gates3[:, :, :2 * H_A].transpose(0, 2, 1)
    ya, c_p, n_p, m_p = _mlstm_prompt(bg, pa.reshape(BATCH, SEQ, QKV0), gates3, gates_row,
                                      g_head_a.astype(F32))
    yb, xf = _s5_prompt(pb, s5m, s5w, s5v, s5p1, s5p2)
    yb = _glu(yb, wg, bglu, pb, 512)
    h1 = _outproj(ya.reshape(mp, W_A), yb, wo0, xp, g_post[0:1], 512, 1024)
    s5r_p = xf[:, :, :P_B].transpose(1, 0, 2)
    s5i_p = xf[:, :, P_B:].transpose(1, 0, 2)

    a1 = _norm(h1, g_pre[1:2], 256)
    p1 = _matmul(a1, w1, BF16, 512, 1024)
    cos_p, sin_p = _rope_table(SEQ, 0)
    yc, s_p = _ret_prompt(p1.reshape(BATCH, SEQ, -1), cos_p, sin_p, g_head_c.astype(F32))
    y_p = _outproj(yc.reshape(mp, W_C), None, wo1, h1, g_post[1:2], 512, 1024)

    a0s, gates_s = _norm_gates(xs, g_pre[0:1], w0g, DEC_BATCH)
    pas = _matmul(a0s, w0a, F32, DEC_BATCH, 1024)
    pbs = _matmul(a0s, w0b, F32, DEC_BATCH, 1024)
    qk_a = H_A * DK_A
    to_cols = lambda a, nh, dk: a.reshape(DEC_BATCH, nh, dk).transpose(1, 2, 0)
    yas, c_s, n_s, m_s = _mlstm_sample(
        gates_s[:, :H_A], gates_s[:, H_A:2 * H_A], bg[None, :H_A], bg[None, H_A:],
        state_mlstm_m.astype(F32), pas,
        to_cols(pas[:, :qk_a], H_A, DK_A), to_cols(pas[:, qk_a:2 * qk_a], H_A, DK_A),
        state_mlstm_n.reshape(DEC_BATCH, qk_a).astype(F32), g_head_a.astype(F32),
        state_mlstm_C.astype(F32))
    us = pbs[:, :W_B].reshape(DEC_BATCH, G_B, GROUP_B).transpose(1, 0, 2)
    us = jnp.pad(us, ((0, 0), (0, 0), (LANES - GROUP_B, 0))).astype(BF16)
    x0 = jnp.concatenate([state_s5_re, state_s5_im], axis=-1).astype(F32).transpose(1, 0, 2)
    ybs, x1 = _s5_sample(us, x0, s5m, s5w, s5v, s5p1, s5p2)
    ybs = _glu(ybs.transpose(1, 0, 2).reshape(DEC_BATCH, W_B), wg, bglu, pbs, DEC_BATCH)
    h1s = _outproj(yas, ybs, wo0, xs, g_post[0:1], DEC_BATCH, 1024)
    s5r_s = x1[:, :, :P_B].transpose(1, 0, 2)
    s5i_s = x1[:, :, P_B:].transpose(1, 0, 2)

    a1s = _norm(h1s, g_pre[1:2], DEC_BATCH)
    p1s = _matmul(a1s, w1, F32, DEC_BATCH, 1024)
    cos_s, sin_s = _rope_table(8, PAST_LEN)
    qs, ks = _rope_sample(p1s, cos_s, sin_s)
    ycs, s_s = _ret_sample(qs, ks, p1s, to_cols(qs, H_C, DK_C), to_cols(ks, H_C, DK_C),
                           g_head_c.astype(F32), state_ret.astype(F32))
    y_s = _outproj(ycs, None, wo1, h1s, g_post[1:2], DEC_BATCH, 1024)

    return (y_p.reshape(BATCH, SEQ, D_MODEL), y_s.reshape(DEC_BATCH, 1, D_MODEL),
            c_p, n_p.reshape(BATCH, H_A, DK_A), m_p[:, :, 0, 0],
            s5r_p, s5i_p, s_p,
            c_s, n_s.reshape(DEC_BATCH, H_A, DK_A), m_s,
            s5r_s, s5i_s, s_s)
```

```python
import functools
import math

import jax
import jax.numpy as jnp
from jax import lax
from jax.experimental import pallas as pl
from jax.experimental.pallas import tpu as pltpu

F32 = jnp.float32
BF16 = jnp.bfloat16

D_MODEL = 2048
BATCH = 4
SEQ = 2048
DEC_BATCH = 128
PAST_LEN = 16384
H_A = 4
DK_A = 256
DV_A = 512
W_A = H_A * DV_A
GROUP_B = 16
G_B = 64
P_B = 64
W_B = G_B * GROUP_B
H_C = 8
DK_C = 256
DV_C = 512
W_C = H_C * DV_C
CHUNK = 128
NORM_EPS = 1e-6
ROPE_BASE = 10000.0
QKV0 = 2 * H_A * DK_A + 3 * W_A
GATE0 = QKV0 + 2 * H_A
SCALE_A = DK_A ** -0.5
SCALE_C = DK_C ** -0.5
LOG_GAMMA = tuple(math.log1p(-(2.0 ** (-5.0 - h))) for h in range(H_C))
S5_T = 16
S5_W = S5_T * GROUP_B
LANES = 128
OCT = LANES // GROUP_B
S5_LEVELS = int(math.log2(SEQ // S5_T))
S5_A1_ROW = S5_LEVELS
VMEM_LIMIT = 48 * 1024 * 1024


def _cparams(*sem):
    return pltpu.CompilerParams(dimension_semantics=sem, vmem_limit_bytes=VMEM_LIMIT)


def _dot(a, b):
    return jnp.dot(a, b, preferred_element_type=F32)


def _dot_nt(a, b):
    return lax.dot_general(a, b, (((1,), (1,)), ((), ())), preferred_element_type=F32)


def _dot_tn(a, b):
    return lax.dot_general(a, b, (((0,), (0,)), ((), ())), preferred_element_type=F32)


def _split(x):
    hi = x.astype(BF16)
    return hi, (x - hi.astype(F32)).astype(BF16)


def _dot3(a, b):
    a_hi, a_lo = _split(a)
    b_hi, b_lo = _split(b)
    return _dot(a_hi, b_hi) + _dot(a_hi, b_lo) + _dot(a_lo, b_hi)


def _log_sigmoid(x):
    return jnp.minimum(x, 0.0) - jnp.log1p(jnp.exp(-jnp.abs(x)))


def _silu(x):
    return x * jax.nn.sigmoid(x)


def _gelu_tanh(x):
    return 0.5 * x * (1.0 + jnp.tanh(math.sqrt(2.0 / math.pi) * (x + 0.044715 * (x * x * x))))


def _rms(x, g):
    return x * lax.rsqrt(jnp.mean(x * x, axis=-1, keepdims=True) + NORM_EPS) * g


def _norm_gates_kernel(x_ref, g_ref, wg_ref, a_ref, gates_ref):
    a = _rms(x_ref[...], g_ref[...])
    a_ref[...] = a.astype(BF16)
    a_hi, a_lo = _split(a)
    w_hi, w_lo = _split(wg_ref[...])
    gates_ref[...] = _dot_nt(a_hi, w_hi) + _dot_nt(a_hi, w_lo) + _dot_nt(a_lo, w_hi)


def _norm_kernel(x_ref, g_ref, a_ref):
    a_ref[...] = _rms(x_ref[...], g_ref[...]).astype(BF16)


def _norm_gates(x, g, wg, tm):
    m = x.shape[0]
    return pl.pallas_call(
        _norm_gates_kernel,
        grid=(m // tm,),
        in_specs=[pl.BlockSpec((tm, D_MODEL), lambda i: (i, 0)),
                  pl.BlockSpec((1, D_MODEL), lambda i: (0, 0)),
                  pl.BlockSpec((LANES, D_MODEL), lambda i: (0, 0))],
        out_specs=[pl.BlockSpec((tm, D_MODEL), lambda i: (i, 0)),
                   pl.BlockSpec((tm, LANES), lambda i: (i, 0))],
        out_shape=[jax.ShapeDtypeStruct((m, D_MODEL), BF16),
                   jax.ShapeDtypeStruct((m, LANES), F32)],
        compiler_params=_cparams("parallel"),
        name="norm_gates",
    )(x, g, wg)


def _norm(x, g, tm):
    m = x.shape[0]
    return pl.pallas_call(
        _norm_kernel,
        grid=(m // tm,),
        in_specs=[pl.BlockSpec((tm, D_MODEL), lambda i: (i, 0)),
                  pl.BlockSpec((1, D_MODEL), lambda i: (0, 0))],
        out_specs=pl.BlockSpec((tm, D_MODEL), lambda i: (i, 0)),
        out_shape=jax.ShapeDtypeStruct((m, D_MODEL), BF16),
        compiler_params=_cparams("parallel"),
        name="norm",
    )(x, g)


def _proj_kernel(xp_ref, xs_ref, w_ref, op_ref, os_ref, wb_ref, *, w_transposed):
    mm = _dot_nt if w_transposed else _dot

    @pl.when(pl.program_id(1) == 0)
    def _():
        wb_ref[...] = w_ref[...].astype(BF16)
        os_ref[...] = mm(xs_ref[...], wb_ref[...])

    op_ref[...] = mm(xp_ref[...], wb_ref[...]).astype(op_ref.dtype)


def _proj(xp, xs, w, n_cols, out_dtype, tm, tn, w_transposed):
    m, k = xp.shape
    ms = xs.shape[0]
    if w_transposed:
        w_spec = pl.BlockSpec((tn, k), lambda j, i: (j, 0))
        wb_shape = (tn, k)
    else:
        w_spec = pl.BlockSpec((k, tn), lambda j, i: (0, j))
        wb_shape = (k, tn)
    return pl.pallas_call(
        functools.partial(_proj_kernel, w_transposed=w_transposed),
        grid=(n_cols // tn, m // tm),
        in_specs=[pl.BlockSpec((tm, k), lambda j, i: (i, 0)),
                  pl.BlockSpec((ms, k), lambda j, i: (0, 0)),
                  w_spec],
        out_specs=[pl.BlockSpec((tm, tn), lambda j, i: (i, j)),
                   pl.BlockSpec((ms, tn), lambda j, i: (0, j))],
        out_shape=[jax.ShapeDtypeStruct((m, n_cols), out_dtype),
                   jax.ShapeDtypeStruct((ms, n_cols), F32)],
        scratch_shapes=[pltpu.VMEM(wb_shape, BF16)],
        compiler_params=_cparams("parallel", "arbitrary"),
        name="proj",
    )(xp, xs, w)


def _outproj_kernel(*refs, nka, nkb):
    if nkb:
        ya_ref, yb_ref, w_ref, h_ref, g_ref, o_ref, acc_ref = refs
    else:
        ya_ref, w_ref, h_ref, g_ref, o_ref, acc_ref = refs
    k = pl.program_id(1)

    @pl.when(k == 0)
    def _():
        acc_ref[...] = jnp.zeros_like(acc_ref)

    if nkb:
        @pl.when(k < nka)
        def _():
            acc_ref[...] += _dot(ya_ref[...].astype(BF16), w_ref[...])

        @pl.when(k >= nka)
        def _():
            acc_ref[...] += _dot(yb_ref[...].astype(BF16), w_ref[...])
    else:
        acc_ref[...] += _dot(ya_ref[...].astype(BF16), w_ref[...])

    @pl.when(k == nka + nkb - 1)
    def _():
        o_ref[...] = h_ref[...] + _rms(acc_ref[...], g_ref[...])


def _outproj(ya, yb, w, h, g, tm, tk):
    m = h.shape[0]
    nka = ya.shape[1] // tk
    nkb = 0 if yb is None else yb.shape[1] // tk
    in_specs = [pl.BlockSpec((tm, tk), lambda i, k: (i, jnp.minimum(k, nka - 1)))]
    args = [ya]
    if nkb:
        in_specs.append(pl.BlockSpec((tm, tk), lambda i, k: (i, jnp.maximum(k - nka, 0))))
        args.append(yb)
    in_specs += [pl.BlockSpec((tk, D_MODEL), lambda i, k: (k, 0)),
                 pl.BlockSpec((tm, D_MODEL), lambda i, k: (i, 0)),
                 pl.BlockSpec((1, D_MODEL), lambda i, k: (0, 0))]
    args += [w, h, g]
    return pl.pallas_call(
        functools.partial(_outproj_kernel, nka=nka, nkb=nkb),
        grid=(m // tm, nka + nkb),
        in_specs=in_specs,
        out_specs=pl.BlockSpec((tm, D_MODEL), lambda i, k: (i, 0)),
        out_shape=jax.ShapeDtypeStruct((m, D_MODEL), F32),
        scratch_shapes=[pltpu.VMEM((tm, D_MODEL), F32)],
        compiler_params=_cparams("parallel", "arbitrary"),
        name="outproj",
    )(*args)


def _mlstm_prompt_kernel(bg_ref, q_ref, k_ref, v_ref, o_ref, z_ref, gc_ref, gr_ref, gh_ref,
                         y_ref, c_out, n_out, m_out, c_s, n_s, m_s):
    c = pl.program_id(1)
    t = CHUNK

    @pl.when(c == 0)
    def _():
        c_s[...] = jnp.zeros_like(c_s)
        n_s[...] = jnp.zeros_like(n_s)
        m_s[...] = jnp.zeros_like(m_s)

    row = lax.broadcasted_iota(jnp.int32, (t, t), 0)
    col = lax.broadcasted_iota(jnp.int32, (t, t), 1)
    tril = col <= row
    triu = row <= col
    gc = gc_ref[...]
    gr = gr_ref[...]
    for h in range(H_A):
        b_i = bg_ref[h]
        b_f = bg_ref[H_A + h]
        i_col = gc[:, h:h + 1] + b_i
        i_row = gr[h:h + 1, :] + b_i
        lf_col = _log_sigmoid(gc[:, H_A + h:H_A + h + 1] + b_f)
        lf_row = _log_sigmoid(gr[H_A + h:H_A + h + 1, :] + b_f)
        b_col = jnp.sum(jnp.where(tril, lf_row, 0.0), axis=1, keepdims=True)
        b_row = jnp.sum(jnp.where(triu, lf_col, 0.0), axis=0, keepdims=True)
        m_prev = m_s[h][:, 0:1]
        d = jnp.where(tril, b_col - b_row + i_row, -jnp.inf)
        inter = b_col + m_prev
        m_t = jnp.maximum(inter, jnp.max(d, axis=1, keepdims=True))
        w_intra = jnp.exp(d - m_t)
        w_inter = jnp.exp(inter - m_t) * SCALE_A
        q = q_ref[:, h * DK_A:(h + 1) * DK_A]
        k = k_ref[:, h * DK_A:(h + 1) * DK_A]
        v = v_ref[:, h * DV_A:(h + 1) * DV_A]
        s = _dot_nt(q, k) * (w_intra * SCALE_A)
        c_old = c_s[h]
        n_old = n_s[h]
        num = _dot(s.astype(BF16), v) + w_inter * _dot(q, c_old.astype(BF16))
        qn = jnp.sum(q.astype(F32) * n_old, axis=1, keepdims=True)
        den = jnp.sum(s, axis=1, keepdims=True) + w_inter * qn
        hh = num / jnp.maximum(jnp.abs(den), jnp.exp(-m_t))
        hn = _rms(hh, gh_ref[h:h + 1, :])
        o = o_ref[:, h * DV_A:(h + 1) * DV_A].astype(F32)
        z = z_ref[:, h * DV_A:(h + 1) * DV_A].astype(F32)
        y_ref[:, h * DV_A:(h + 1) * DV_A] = (hn * jax.nn.sigmoid(o) * _silu(z)).astype(BF16)
        b_last = b_col[t - 1:t, :]
        g_col = b_last - b_col + i_col
        m_new = jnp.maximum(b_last + m_prev, jnp.max(g_col, axis=0, keepdims=True))
        e_col = jnp.exp(g_col - m_new)
        decay = jnp.exp(b_last + m_prev - m_new)
        ke = k.astype(F32) * e_col
        c_s[h] = decay * c_old + _dot_tn(ke.astype(BF16), v)
        n_s[h] = decay * n_old + jnp.sum(ke, axis=0, keepdims=True)
        m_s[h] = jnp.broadcast_to(m_new, (1, LANES))

    @pl.when(c == pl.num_programs(1) - 1)
    def _():
        c_out[...] = c_s[...]
        n_out[...] = n_s[...]
        m_out[...] = m_s[...]


def _mlstm_prompt(bg, proj, gates_col, gates_row, g_head):
    nc = SEQ // CHUNK
    t = CHUNK
    qk_w = H_A * DK_A
    return pl.pallas_call(
        _mlstm_prompt_kernel,
        grid=(BATCH, nc),
        in_specs=[pl.BlockSpec(memory_space=pltpu.SMEM),
                  pl.BlockSpec((None, t, qk_w), lambda b, c: (b, c, 0)),
                  pl.BlockSpec((None, t, qk_w), lambda b, c: (b, c, 1)),
                  pl.BlockSpec((None, t, W_A), lambda b, c: (b, c, 1)),
                  pl.BlockSpec((None, t, W_A), lambda b, c: (b, c, 2)),
                  pl.BlockSpec((None, t, W_A), lambda b, c: (b, c, 3)),
                  pl.BlockSpec((None, t, LANES), lambda b, c: (b, c, 0)),
                  pl.BlockSpec((None, 2 * H_A, t), lambda b, c: (b, 0, c)),
                  pl.BlockSpec((H_A, DV_A), lambda b, c: (0, 0))],
        out_specs=[pl.BlockSpec((None, t, W_A), lambda b, c: (b, c, 0)),
                   pl.BlockSpec((None, H_A, DK_A, DV_A), lambda b, c: (b, 0, 0, 0)),
                   pl.BlockSpec((None, H_A, 1, DK_A), lambda b, c: (b, 0, 0, 0)),
                   pl.BlockSpec((None, H_A, 1, LANES), lambda b, c: (b, 0, 0, 0))],
        out_shape=[jax.ShapeDtypeStruct((BATCH, SEQ, W_A), BF16),
                   jax.ShapeDtypeStruct((BATCH, H_A, DK_A, DV_A), F32),
                   jax.ShapeDtypeStruct((BATCH, H_A, 1, DK_A), F32),
                   jax.ShapeDtypeStruct((BATCH, H_A, 1, LANES), F32)],
        scratch_shapes=[pltpu.VMEM((H_A, DK_A, DV_A), F32),
                        pltpu.VMEM((H_A, 1, DK_A), F32),
                        pltpu.VMEM((H_A, 1, LANES), F32)],
        compiler_params=_cparams("parallel", "arbitrary"),
        name="mlstm_prompt",
    )(bg, proj, proj, proj, proj, proj, gates_col, gates_row, g_head)


def _mlstm_sample_kernel(gi_ref, gf_ref, bi_ref, bf_ref, m0_ref, q_ref, k_ref, v_ref, o_ref, z_ref,
                         qt_ref, kt_ref, n0_ref, gh_ref, c0_ref,
                         y_ref, c1_ref, n1_ref, m1_ref):
    r = pl.program_id(0)
    rows = pl.ds(r, 1)
    shift = lax.rem(DEC_BATCH - r, DEC_BATCH)
    i_v = gi_ref[rows, :] + bi_ref[...]
    lf_v = _log_sigmoid(gf_ref[rows, :] + bf_ref[...])
    m0_v = m0_ref[rows, :]
    m_t = jnp.maximum(lf_v + m0_v, i_v)
    w_in = jnp.exp(i_v - m_t)
    w_st = jnp.exp(lf_v + m0_v - m_t)
    floor = jnp.exp(-m_t)
    m1_ref[rows, :] = m_t
    for h in range(H_A):
        ks = slice(h * DK_A, (h + 1) * DK_A)
        vs = slice(h * DV_A, (h + 1) * DV_A)
        wi = w_in[:, h:h + 1]
        ws = w_st[:, h:h + 1]
        q_col = pltpu.roll(qt_ref[h], shift, axis=1)[:, 0:1]
        k_col = pltpu.roll(kt_ref[h], shift, axis=1)[:, 0:1]
        q_row = q_ref[rows, ks]
        k_row = k_ref[rows, ks]
        v_row = v_ref[rows, vs]
        n_row = n0_ref[rows, ks]
        c_old = c0_ref[0, h]
        qk = jnp.sum(q_row * k_row, axis=1, keepdims=True) * SCALE_A
        s = qk * wi
        q_c = jnp.sum(c_old * q_col, axis=0, keepdims=True) * SCALE_A
        qn = jnp.sum(q_row * n_row, axis=1, keepdims=True) * SCALE_A
        num = s * v_row + ws * q_c
        den = s + ws * qn
        hh = num / jnp.maximum(jnp.abs(den), floor[:, h:h + 1])
        hn = _rms(hh, gh_ref[h:h + 1, :])
        y_ref[rows, vs] = hn * jax.nn.sigmoid(o_ref[rows, vs]) * _silu(z_ref[rows, vs])
        c1_ref[0, h] = ws * c_old + (wi * k_col) * v_row
        n1_ref[rows, ks] = ws * n_row + wi * k_row


def _mlstm_sample(gi, gf, bi, bf, m0, proj, qt, kt, n0, g_head, c0):
    qk_w = H_A * DK_A
    full = lambda shape: pl.BlockSpec(shape, lambda r: (0,) * len(shape))
    return pl.pallas_call(
        _mlstm_sample_kernel,
        grid=(DEC_BATCH,),
        in_specs=[full((DEC_BATCH, H_A)), full((DEC_BATCH, H_A)), full((1, H_A)), full((1, H_A)),
                  full((DEC_BATCH, H_A)),
                  pl.BlockSpec((DEC_BATCH, qk_w), lambda r: (0, 0)),
                  pl.BlockSpec((DEC_BATCH, qk_w), lambda r: (0, 1)),
                  pl.BlockSpec((DEC_BATCH, W_A), lambda r: (0, 1)),
                  pl.BlockSpec((DEC_BATCH, W_A), lambda r: (0, 2)),
                  pl.BlockSpec((DEC_BATCH, W_A), lambda r: (0, 3)),
                  full((H_A, DK_A, DEC_BATCH)), full((H_A, DK_A, DEC_BATCH)),
                  full((DEC_BATCH, qk_w)), full((H_A, DV_A)),
                  pl.BlockSpec((1, H_A, DK_A, DV_A), lambda r: (r, 0, 0, 0))],
        out_specs=[full((DEC_BATCH, W_A)),
                   pl.BlockSpec((1, H_A, DK_A, DV_A), lambda r: (r, 0, 0, 0)),
                   full((DEC_BATCH, qk_w)), full((DEC_BATCH, H_A))],
        out_shape=[jax.ShapeDtypeStruct((DEC_BATCH, W_A), F32),
                   jax.ShapeDtypeStruct((DEC_BATCH, H_A, DK_A, DV_A), F32),
                   jax.ShapeDtypeStruct((DEC_BATCH, qk_w), F32),
                   jax.ShapeDtypeStruct((DEC_BATCH, H_A), F32)],
        compiler_params=_cparams("arbitrary"),
        name="mlstm_sample",
    )(gi, gf, bi, bf, m0, proj, proj, proj, proj, proj, qt, kt, n0, g_head, c0)


def _s5_build_kernel(ldt_ref, lamr_ref, lami_ref, lamrc_ref, lamic_ref, b1_ref, cc_ref, dsk_ref,
                     m_ref, w_ref, v_ref, p1_ref, p2_ref):
    dt = jnp.exp(ldt_ref[...])

    def disc(lam_re, lam_im):
        mag = jnp.exp(lam_re * dt)
        ang = lam_im * dt
        a_re, a_im = mag * jnp.cos(ang), mag * jnp.sin(ang)
        den = lam_re * lam_re + lam_im * lam_im
        f_re = ((a_re - 1.0) * lam_re + a_im * lam_im) / den
        f_im = (a_im * lam_re - (a_re - 1.0) * lam_im) / den
        return a_re, a_im, f_re, f_im

    def power(lam_re, lam_im, tau):
        mag = jnp.exp(tau * (lam_re * dt))
        ang = tau * (lam_im * dt)
        return mag * jnp.cos(ang), mag * jnp.sin(ang)

    lam_re, lam_im = lamr_ref[...], lami_ref[...]
    a_re, a_im, f_re, f_im = disc(lam_re, lam_im)
    srow = lax.broadcasted_iota(jnp.int32, (S5_W, LANES), 0) // GROUP_B
    p_re, p_im = power(lam_re, lam_im, (S5_T - 1 - srow).astype(F32))
    af_re = p_re * f_re - p_im * f_im
    af_im = p_re * f_im + p_im * f_re
    lane = lax.broadcasted_iota(jnp.int32, (1, LANES), 1)
    sgn_lane = jnp.where(lane < P_B, -1.0, 1.0)
    b1 = b1_ref[...]
    b2 = sgn_lane * pltpu.roll(b1, P_B, axis=1)
    w_ref[...] = (af_re * b1 + af_im * b2).astype(BF16)

    lam_re_c, lam_im_c = lamrc_ref[...], lamic_ref[...]
    ac_re, ac_im, fc_re, fc_im = disc(lam_re_c, lam_im_c)
    tlane = lax.broadcasted_iota(jnp.int32, (LANES, S5_W), 1) // GROUP_B
    q_re, q_im = power(lam_re_c, lam_im_c, tlane.astype(F32))
    cc = cc_ref[...]
    cs = pltpu.roll(cc, P_B, axis=0)
    rowi = lax.broadcasted_iota(jnp.int32, (LANES, 1), 0)
    sgn_row = jnp.where(rowi < P_B, 1.0, -1.0)

    def readout(r_re, r_im):
        return sgn_row * (r_re * cc) - r_im * cs

    v_ref[...] = readout(q_re * ac_re - q_im * ac_im, q_re * ac_im + q_im * ac_re).astype(BF16)
    vf = readout(q_re * fc_re - q_im * fc_im, q_re * fc_im + q_im * fc_re)
    kw = _dot3(b1[0:GROUP_B, :], vf)
    lane_w = lax.broadcasted_iota(jnp.int32, (GROUP_B, S5_W), 1)
    blocks = [kw]
    for s in range(1, S5_T):
        blocks.append(jnp.where(lane_w >= s * GROUP_B, pltpu.roll(kw, s * GROUP_B, axis=1), 0.0))
    toep = jnp.concatenate(blocks, axis=0)
    ri = lax.broadcasted_iota(jnp.int32, (S5_W, S5_W), 0)
    ci = lax.broadcasted_iota(jnp.int32, (S5_W, S5_W), 1)
    m_ref[...] = (toep + jnp.where(ri == ci, dsk_ref[...], 0.0)).astype(BF16)

    r_re, r_im = power(lam_re, lam_im, float(S5_T))
    rid = lax.broadcasted_iota(jnp.int32, (8, LANES), 0)
    p1 = jnp.where(rid == S5_A1_ROW, a_re, 0.0)
    p2 = jnp.where(rid == S5_A1_ROW, sgn_lane * a_im, 0.0)
    for kk in range(S5_LEVELS):
        p1 = jnp.where(rid == kk, r_re, p1)
        p2 = jnp.where(rid == kk, sgn_lane * r_im, p2)
        r_re, r_im = r_re * r_re - r_im * r_im, 2.0 * (r_re * r_im)
    p1_ref[...] = p1
    p2_ref[...] = p2


def _s5_build(ldt, lamr, lami, lamrc, lamic, b1, cc, dsk):
    g3 = lambda a, b: pl.BlockSpec((None, a, b), lambda g: (g, 0, 0))
    return pl.pallas_call(
        _s5_build_kernel,
        grid=(G_B,),
        in_specs=[g3(1, 1), g3(1, LANES), g3(1, LANES), g3(LANES, 1), g3(LANES, 1),
                  g3(S5_W, LANES), g3(LANES, S5_W), g3(1, S5_W)],
        out_specs=[g3(S5_W, S5_W), g3(S5_W, LANES), g3(LANES, S5_W), g3(8, LANES), g3(8, LANES)],
        out_shape=[jax.ShapeDtypeStruct((G_B, S5_W, S5_W), BF16),
                   jax.ShapeDtypeStruct((G_B, S5_W, LANES), BF16),
                   jax.ShapeDtypeStruct((G_B, LANES, S5_W), BF16),
                   jax.ShapeDtypeStruct((G_B, 8, LANES), F32),
                   jax.ShapeDtypeStruct((G_B, 8, LANES), F32)],
        compiler_params=_cparams("parallel"),
        name="s5_build",
    )(ldt, lamr, lami, lamrc, lamic, b1, cc, dsk)


def _s5_prompt_kernel(u_ref, m_ref, w_ref, v_ref, p1_ref, p2_ref, y_ref, xf_ref, y_s):
    nblk = SEQ // S5_T
    rows = BATCH * nblk
    bidx = lax.broadcasted_iota(jnp.int32, (rows, LANES), 0) & (nblk - 1)
    steps = [u_ref[pl.ds(s, rows, stride=S5_T), :] for s in range(S5_T)]
    for g in range(OCT):
        gl = slice(g * GROUP_B, (g + 1) * GROUP_B)
        u = jnp.concatenate([x[:, gl] for x in steps], axis=-1).astype(BF16)
        s = _dot(u, w_ref[g])
        for kk in range(S5_LEVELS):
            sh = 1 << kk
            r = jnp.where(bidx >= sh, pltpu.roll(s, sh, axis=0), 0.0)
            s = s + p1_ref[g, kk:kk + 1, :] * r + p2_ref[g, kk:kk + 1, :] * pltpu.roll(r, P_B, axis=1)
        x_prev = jnp.where(bidx >= 1, pltpu.roll(s, 1, axis=0), 0.0)
        y = _dot(u, m_ref[g]) + _dot(x_prev.astype(BF16), v_ref[g])
        y_s[g] = _gelu_tanh(y)
        for b in range(BATCH):
            xf_ref[g, b:b + 1, :] = s[(b + 1) * nblk - 1:(b + 1) * nblk, :]
    for t in range(S5_T):
        half = slice((t // OCT) * LANES, (t // OCT + 1) * LANES)
        tl = slice((t % OCT) * GROUP_B, (t % OCT + 1) * GROUP_B)
        y_ref[pl.ds(t, rows, stride=S5_T), :] = jnp.concatenate(
            [y_s[g, :, half][:, tl] for g in range(OCT)], axis=-1)


def _s5_prompt(u, m, w, v, p1, p2):
    mp = BATCH * SEQ
    rows = BATCH * (SEQ // S5_T)
    o3 = lambda a, b: pl.BlockSpec((OCT, a, b), lambda g: (g, 0, 0))
    return pl.pallas_call(
        _s5_prompt_kernel,
        grid=(G_B // OCT,),
        in_specs=[pl.BlockSpec((mp, LANES), lambda g: (0, g)),
                  o3(S5_W, S5_W), o3(S5_W, LANES), o3(LANES, S5_W), o3(8, LANES), o3(8, LANES)],
        out_specs=[pl.BlockSpec((mp, LANES), lambda g: (0, g)), o3(BATCH, LANES)],
        out_shape=[jax.ShapeDtypeStruct((mp, W_B), F32),
                   jax.ShapeDtypeStruct((G_B, BATCH, LANES), F32)],
        scratch_shapes=[pltpu.VMEM((OCT, rows, S5_W), F32)],
        compiler_params=_cparams("parallel"),
        name="s5_prompt",
    )(u, m, w, v, p1, p2)


def _s5_sample_kernel(u_ref, x0_ref, m_ref, w_ref, v_ref, p1_ref, p2_ref, y_ref, x1_ref):
    u = u_ref[...]
    x0 = x0_ref[...]
    half = S5_W // 2
    a1 = slice(S5_A1_ROW, S5_A1_ROW + 1)
    x1_ref[...] = (p1_ref[a1, :] * x0 + p2_ref[a1, :] * pltpu.roll(x0, P_B, axis=1)
                   + _dot(u, w_ref[half:, :]))
    y = (_dot(x0.astype(BF16), v_ref[...])[:, 0:GROUP_B]
         + _dot(u, m_ref[half:, :])[:, S5_W - GROUP_B:])
    y_ref[...] = _gelu_tanh(y)


def _s5_sample(u, x0, m, w, v, p1, p2):
    g3 = lambda a, b: pl.BlockSpec((None, a, b), lambda g: (g, 0, 0))
    return pl.pallas_call(
        _s5_sample_kernel,
        grid=(G_B,),
        in_specs=[g3(DEC_BATCH, LANES), g3(DEC_BATCH, LANES), g3(S5_W, S5_W), g3(S5_W, LANES),
                  g3(LANES, S5_W), g3(8, LANES), g3(8, LANES)],
        out_specs=[g3(DEC_BATCH, GROUP_B), g3(DEC_BATCH, LANES)],
        out_shape=[jax.ShapeDtypeStruct((G_B, DEC_BATCH, GROUP_B), F32),
                   jax.ShapeDtypeStruct((G_B, DEC_BATCH, LANES), F32)],
        compiler_params=_cparams("parallel"),
        name="s5_sample",
    )(u, x0, m, w, v, p1, p2)


def _glu_kernel(y_ref, w_ref, b_ref, z_ref, o_ref):
    y = y_ref[...]
    gate = jax.nn.sigmoid(_dot(y.astype(BF16), w_ref[...]) + b_ref[...])
    o_ref[...] = (y.astype(F32) * gate * _silu(z_ref[...].astype(F32))).astype(o_ref.dtype)


def _glu(y, w, b, proj_b, tm):
    m = y.shape[0]
    return pl.pallas_call(
        _glu_kernel,
        grid=(m // tm,),
        in_specs=[pl.BlockSpec((tm, W_B), lambda i: (i, 0)),
                  pl.BlockSpec((W_B, W_B), lambda i: (0, 0)),
                  pl.BlockSpec((1, W_B), lambda i: (0, 0)),
                  pl.BlockSpec((tm, W_B), lambda i: (i, 1))],
        out_specs=pl.BlockSpec((tm, W_B), lambda i: (i, 0)),
        out_shape=jax.ShapeDtypeStruct((m, W_B), BF16),
        compiler_params=_cparams("parallel"),
        name="glu",
    )(y, w, b, proj_b)


def _rope_table_kernel(cos_ref, sin_ref, *, pos0):
    shape = cos_ref.shape
    pos = lax.broadcasted_iota(jnp.int32, shape, 0).astype(F32) + pos0
    j = lax.broadcasted_iota(jnp.int32, shape, 1).astype(F32)
    ang = pos * jnp.power(ROPE_BASE, -(j / (DK_C // 2)))
    cos_ref[...] = jnp.cos(ang)
    sin_ref[...] = jnp.sin(ang)


def _rope_table(rows, pos0):
    shape = jax.ShapeDtypeStruct((rows, DK_C // 2), F32)
    return pl.pallas_call(functools.partial(_rope_table_kernel, pos0=float(pos0)),
                          out_shape=[shape, shape], name="rope_table")()


def _rope(x, cos, sin):
    half = DK_C // 2
    x1, x2 = x[:, :half], x[:, half:]
    return jnp.concatenate([x1 * cos - x2 * sin, x1 * sin + x2 * cos], axis=-1)


def _ret_prompt_kernel(q_ref, k_ref, v_ref, z_ref, cos_ref, sin_ref, gh_ref, y_ref, s_out, s_s):
    c = pl.program_id(1)
    t = CHUNK

    @pl.when(c == 0)
    def _():
        s_s[...] = jnp.zeros_like(s_s)

    row = lax.broadcasted_iota(jnp.int32, (t, t), 0)
    col = lax.broadcasted_iota(jnp.int32, (t, t), 1)
    tril = col <= row
    diff = (row - col).astype(F32)
    tpos = lax.broadcasted_iota(jnp.int32, (t, 1), 0).astype(F32)
    cos, sin = cos_ref[...], sin_ref[...]
    for h in range(H_C):
        lg = LOG_GAMMA[h]
        ks = slice(h * DK_C, (h + 1) * DK_C)
        vs = slice(h * DV_C, (h + 1) * DV_C)
        q = _rope(q_ref[:, ks].astype(F32), cos, sin).astype(BF16)
        k32 = _rope(k_ref[:, ks].astype(F32), cos, sin)
        v = v_ref[:, vs]
        mask = jnp.where(tril, jnp.exp(diff * lg), 0.0) * SCALE_C
        s = _dot_nt(q, k32.astype(BF16)) * mask
        s_old = s_s[h]
        o = _dot(s.astype(BF16), v) + _dot(q, s_old.astype(BF16)) * jnp.exp((tpos + 1.0) * lg)
        y_ref[:, vs] = (_rms(o, gh_ref[h:h + 1, :]) * _silu(z_ref[:, vs].astype(F32))).astype(BF16)
        k_tail = k32 * (jnp.exp((t - 1.0 - tpos) * lg) * SCALE_C)
        s_s[h] = math.exp(t * lg) * s_old + _dot_tn(k_tail.astype(BF16), v)

    @pl.when(c == pl.num_programs(1) - 1)
    def _():
        s_out[...] = s_s[...]


def _ret_prompt(proj, cos, sin, g_head):
    nc = SEQ // CHUNK
    t = CHUNK
    qk_w = H_C * DK_C
    return pl.pallas_call(
        _ret_prompt_kernel,
        grid=(BATCH, nc),
        in_specs=[pl.BlockSpec((None, t, qk_w), lambda b, c: (b, c, 0)),
                  pl.BlockSpec((None, t, qk_w), lambda b, c: (b, c, 1)),
                  pl.BlockSpec((None, t, W_C), lambda b, c: (b, c, 1)),
                  pl.BlockSpec((None, t, W_C), lambda b, c: (b, c, 2)),
                  pl.BlockSpec((t, DK_C // 2), lambda b, c: (c, 0)),
                  pl.BlockSpec((t, DK_C // 2), lambda b, c: (c, 0)),
                  pl.BlockSpec((H_C, DV_C), lambda b, c: (0, 0))],
        out_specs=[pl.BlockSpec((None, t, W_C), lambda b, c: (b, c, 0)),
                   pl.BlockSpec((None, H_C, DK_C, DV_C), lambda b, c: (b, 0, 0, 0))],
        out_shape=[jax.ShapeDtypeStruct((BATCH, SEQ, W_C), BF16),
                   jax.ShapeDtypeStruct((BATCH, H_C, DK_C, DV_C), F32)],
        scratch_shapes=[pltpu.VMEM((H_C, DK_C, DV_C), F32)],
        compiler_params=_cparams("parallel", "arbitrary"),
        name="ret_prompt",
    )(proj, proj, proj, proj, cos, sin, g_head)


def _rope_sample_kernel(q_ref, k_ref, cos_ref, sin_ref, qo_ref, ko_ref):
    cos, sin = cos_ref[0:1, :], sin_ref[0:1, :]
    for h in range(H_C):
        ks = slice(h * DK_C, (h + 1) * DK_C)
        qo_ref[:, ks] = _rope(q_ref[:, ks], cos, sin)
        ko_ref[:, ks] = _rope(k_ref[:, ks], cos, sin)


def _rope_sample(proj, cos, sin):
    qk_w = H_C * DK_C
    shape = jax.ShapeDtypeStruct((DEC_BATCH, qk_w), F32)
    return pl.pallas_call(
        _rope_sample_kernel,
        grid=(1,),
        in_specs=[pl.BlockSpec((DEC_BATCH, qk_w), lambda i: (0, 0)),
                  pl.BlockSpec((DEC_BATCH, qk_w), lambda i: (0, 1)),
                  pl.BlockSpec((8, DK_C // 2), lambda i: (0, 0)),
                  pl.BlockSpec((8, DK_C // 2), lambda i: (0, 0))],
        out_specs=[pl.BlockSpec((DEC_BATCH, qk_w), lambda i: (0, 0)),
                   pl.BlockSpec((DEC_BATCH, qk_w), lambda i: (0, 0))],
        out_shape=[shape, shape],
        compiler_params=_cparams("arbitrary"),
        name="rope_sample",
    )(proj, proj, cos, sin)


def _ret_sample_kernel(q_ref, k_ref, v_ref, z_ref, qt_ref, kt_ref, gh_ref, s0_ref, y_ref, s1_ref):
    r = pl.program_id(0)
    rows = pl.ds(r, 1)
    shift = lax.rem(DEC_BATCH - r, DEC_BATCH)
    for h in range(H_C):
        gamma = math.exp(LOG_GAMMA[h])
        ks = slice(h * DK_C, (h + 1) * DK_C)
        vs = slice(h * DV_C, (h + 1) * DV_C)
        q_col = pltpu.roll(qt_ref[h], shift, axis=1)[:, 0:1]
        k_col = pltpu.roll(kt_ref[h], shift, axis=1)[:, 0:1]
        v_row = v_ref[rows, vs]
        s_old = s0_ref[0, h]
        qk = jnp.sum(q_ref[rows, ks] * k_ref[rows, ks], axis=1, keepdims=True) * SCALE_C
        o = qk * v_row + jnp.sum(s_old * q_col, axis=0, keepdims=True) * gamma
        y_ref[rows, vs] = _rms(o, gh_ref[h:h + 1, :]) * _silu(z_ref[rows, vs])
        s1_ref[0, h] = gamma * s_old + (k_col * SCALE_C) * v_row


def _ret_sample(q, k, proj, qt, kt, g_head, s0):
    qk_w = H_C * DK_C
    full = lambda shape: pl.BlockSpec(shape, lambda r: (0,) * len(shape))
    return pl.pallas_call(
        _ret_sample_kernel,
        grid=(DEC_BATCH,),
        in_specs=[full((DEC_BATCH, qk_w)), full((DEC_BATCH, qk_w)),
                  pl.BlockSpec((DEC_BATCH, W_C), lambda r: (0, 1)),
                  pl.BlockSpec((DEC_BATCH, W_C), lambda r: (0, 2)),
                  full((H_C, DK_C, DEC_BATCH)), full((H_C, DK_C, DEC_BATCH)),
                  full((H_C, DV_C)),
                  pl.BlockSpec((1, H_C, DK_C, DV_C), lambda r: (r, 0, 0, 0))],
        out_specs=[full((DEC_BATCH, W_C)),
                   pl.BlockSpec((1, H_C, DK_C, DV_C), lambda r: (r, 0, 0, 0))],
        out_shape=[jax.ShapeDtypeStruct((DEC_BATCH, W_C), F32),
                   jax.ShapeDtypeStruct((DEC_BATCH, H_C, DK_C, DV_C), F32)],
        compiler_params=_cparams("arbitrary"),
        name="ret_sample",
    )(q, k, proj, proj, qt, kt, g_head, s0)


def kernel(x_prompt, x_sample, state_mlstm_C, state_mlstm_n, state_mlstm_m, state_s5_re, state_s5_im, state_ret, g_pre, g_post, w_in0, b_gates0, g_head_a, lam_re, lam_im, log_dt, b_re, b_im, c_re, c_im, d_skip, w_glu, b_glu, w_out0, w_in1, g_head_c, w_out1):
    mp = BATCH * SEQ
    xp = x_prompt.reshape(mp, D_MODEL).astype(F32)
    xs = x_sample.reshape(DEC_BATCH, D_MODEL).astype(F32)

    w0t = w_in0.T.astype(F32)
    w0g = jnp.pad(w0t[QKV0:GATE0], ((0, LANES - 2 * H_A), (0, 0)))
    w0b = w0t[GATE0:]
    w1 = w_in1.astype(F32)
    wo0 = w_out0.astype(BF16)
    wo1 = w_out1.astype(BF16)
    wg = w_glu.astype(BF16)
    bglu = b_glu.reshape(1, W_B).astype(F32)
    g_pre = g_pre.astype(F32)
    g_post = g_post.astype(F32)
    bg = b_gates0.astype(F32)
    gh_a = g_head_a.astype(F32)
    gh_c = g_head_c.astype(F32)

    dup = lambda a: jnp.concatenate([a, a], axis=-1).astype(F32)
    lamr, lami = dup(lam_re), dup(lam_im)
    b1 = jnp.concatenate([b_re.transpose(0, 2, 1), b_im.transpose(0, 2, 1)], axis=-1)
    cc = jnp.concatenate([c_re.transpose(0, 2, 1), c_im.transpose(0, 2, 1)], axis=1)
    s5m, s5w, s5v, s5p1, s5p2 = _s5_build(
        log_dt.reshape(G_B, 1, 1).astype(F32), lamr[:, None, :], lami[:, None, :],
        lamr[:, :, None], lami[:, :, None],
        jnp.tile(b1.astype(F32), (1, S5_T, 1)), jnp.tile(cc.astype(F32), (1, 1, S5_T)),
        jnp.tile(d_skip.astype(F32), (1, S5_T))[:, None, :])

    a0, gates = _norm_gates(xp, g_pre[0:1], w0g, 256)
    a0s, gates_s = _norm_gates(xs, g_pre[0:1], w0g, DEC_BATCH)
    pa, pas = _proj(a0, a0s, w0t, QKV0, BF16, 512, 1024, True)
    pb, pbs = _proj(a0, a0s, w0b, 2 * W_B, F32, 512, 1024, True)
    gates3 = gates.reshape(BATCH, SEQ, LANES)
    gates_row = gates3[:, :, :2 * H_A].transpose(0, 2, 1)
    ya, c_p, n_p, m_p = _mlstm_prompt(bg, pa.reshape(BATCH, SEQ, QKV0), gates3, gates_row, gh_a)
    yb, xf = _s5_prompt(pb, s5m, s5w, s5v, s5p1, s5p2)
    yb = _glu(yb, wg, bglu, pb, 512)
    h1 = _outproj(ya.reshape(mp, W_A), yb, wo0, xp, g_post[0:1], 512, 1024)
    s5r_p = xf[:, :, :P_B].transpose(1, 0, 2)
    s5i_p = xf[:, :, P_B:].transpose(1, 0, 2)
    qk_a = H_A * DK_A
    to_cols = lambda a, nh, dk: a.reshape(DEC_BATCH, nh, dk).transpose(1, 2, 0)
    yas, c_s, n_s, m_s = _mlstm_sample(
        gates_s[:, :H_A], gates_s[:, H_A:2 * H_A], bg[None, :H_A], bg[None, H_A:],
        state_mlstm_m.astype(F32), pas,
        to_cols(pas[:, :qk_a], H_A, DK_A), to_cols(pas[:, qk_a:2 * qk_a], H_A, DK_A),
        state_mlstm_n.reshape(DEC_BATCH, qk_a).astype(F32), gh_a, state_mlstm_C.astype(F32))
    us = pbs[:, :W_B].reshape(DEC_BATCH, G_B, GROUP_B).transpose(1, 0, 2)
    us = jnp.pad(us, ((0, 0), (0, 0), (LANES - GROUP_B, 0))).astype(BF16)
    x0 = jnp.concatenate([state_s5_re, state_s5_im], axis=-1).astype(F32).transpose(1, 0, 2)
    ybs, x1 = _s5_sample(us, x0, s5m, s5w, s5v, s5p1, s5p2)
    ybs = _glu(ybs.transpose(1, 0, 2).reshape(DEC_BATCH, W_B), wg, bglu, pbs, DEC_BATCH)
    h1s = _outproj(yas, ybs, wo0, xs, g_post[0:1], DEC_BATCH, 1024)
    s5r_s = x1[:, :, :P_B].transpose(1, 0, 2)
    s5i_s = x1[:, :, P_B:].transpose(1, 0, 2)

    a1 = _norm(h1, g_pre[1:2], 256)
    a1s = _norm(h1s, g_pre[1:2], DEC_BATCH)
    p1, p1s = _proj(a1, a1s, w1, 2 * H_C * DK_C + 2 * W_C, BF16, 512, 1024, False)
    cos_p, sin_p = _rope_table(SEQ, 0)
    yc, s_p = _ret_prompt(p1.reshape(BATCH, SEQ, -1), cos_p, sin_p, gh_c)
    y_p = _outproj(yc.reshape(mp, W_C), None, wo1, h1, g_post[1:2], 512, 1024)
    cos_s, sin_s = _rope_table(8, PAST_LEN)
    qs, ks = _rope_sample(p1s, cos_s, sin_s)
    ycs, s_s = _ret_sample(qs, ks, p1s, to_cols(qs, H_C, DK_C), to_cols(ks, H_C, DK_C),
                           gh_c, state_ret.astype(F32))
    y_s = _outproj(ycs, None, wo1, h1s, g_post[1:2], DEC_BATCH, 1024)

    return (y_p.reshape(BATCH, SEQ, D_MODEL), y_s.reshape(DEC_BATCH, 1, D_MODEL),
            c_p, n_p.reshape(BATCH, H_A, DK_A), m_p[:, :, 0, 0],
            s5r_p, s5i_p, s_p,
            c_s, n_s.reshape(DEC_BATCH, H_A, DK_A), m_s,
            s5r_s, s5i_s, s_s)
```

```python
import functools
import math

import jax
import jax.numpy as jnp
from jax import lax
from jax.experimental import pallas as pl
from jax.experimental.pallas import tpu as pltpu

F32 = jnp.float32
BF16 = jnp.bfloat16

D_MODEL = 2048
BATCH = 4
SEQ = 2048
DEC_BATCH = 128
PAST_LEN = 16384
H_A = 4
DK_A = 256
DV_A = 512
W_A = H_A * DV_A
GROUP_B = 16
G_B = 64
P_B = 64
W_B = G_B * GROUP_B
H_C = 8
DK_C = 256
DV_C = 512
W_C = H_C * DV_C
CHUNK = 128
NORM_EPS = 1e-6
ROPE_BASE = 10000.0
QKV0 = 2 * H_A * DK_A + 3 * W_A
GATE0 = QKV0 + 2 * H_A
SCALE_A = DK_A ** -0.5
SCALE_C = DK_C ** -0.5
LOG_GAMMA = tuple(math.log1p(-(2.0 ** (-5.0 - h))) for h in range(H_C))
S5_T = 16
S5_W = S5_T * GROUP_B
LANES = 128
OCT = LANES // GROUP_B
S5_LEVELS = int(math.log2(SEQ // S5_T))
S5_A1_ROW = S5_LEVELS
VMEM_LIMIT = 48 * 1024 * 1024
OUTPROJ_VMEM_LIMIT = 56 * 1024 * 1024


def _cparams(*sem):
    return pltpu.CompilerParams(dimension_semantics=sem, vmem_limit_bytes=VMEM_LIMIT)


def _dot(a, b):
    return jnp.dot(a, b, preferred_element_type=F32)


def _dot_nt(a, b):
    return lax.dot_general(a, b, (((1,), (1,)), ((), ())), preferred_element_type=F32)


def _dot_tn(a, b):
    return lax.dot_general(a, b, (((0,), (0,)), ((), ())), preferred_element_type=F32)


def _split(x):
    hi = x.astype(BF16)
    return hi, (x - hi.astype(F32)).astype(BF16)


def _dot3(a, b):
    a_hi, a_lo = _split(a)
    b_hi, b_lo = _split(b)
    return _dot(a_hi, b_hi) + _dot(a_hi, b_lo) + _dot(a_lo, b_hi)


def _log_sigmoid(x):
    return jnp.minimum(x, 0.0) - jnp.log1p(jnp.exp(-jnp.abs(x)))


def _silu(x):
    return x * jax.nn.sigmoid(x)


def _gelu_tanh(x):
    return 0.5 * x * (1.0 + jnp.tanh(math.sqrt(2.0 / math.pi) * (x + 0.044715 * (x * x * x))))


def _rms(x, g):
    return x * lax.rsqrt(jnp.mean(x * x, axis=-1, keepdims=True) + NORM_EPS) * g


def _norm_gates_kernel(x_ref, g_ref, wg_ref, a_ref, gates_ref):
    a = _rms(x_ref[...], g_ref[...])
    a_ref[...] = a.astype(BF16)
    a_hi, a_lo = _split(a)
    w_hi, w_lo = _split(wg_ref[...])
    gates_ref[...] = _dot_nt(a_hi, w_hi) + _dot_nt(a_hi, w_lo) + _dot_nt(a_lo, w_hi)


def _norm_kernel(x_ref, g_ref, a_ref):
    a_ref[...] = _rms(x_ref[...], g_ref[...]).astype(BF16)


def _norm_gates(x, g, wg, tm):
    m = x.shape[0]
    return pl.pallas_call(
        _norm_gates_kernel,
        grid=(m // tm,),
        in_specs=[pl.BlockSpec((tm, D_MODEL), lambda i: (i, 0)),
                  pl.BlockSpec((1, D_MODEL), lambda i: (0, 0)),
                  pl.BlockSpec((LANES, D_MODEL), lambda i: (0, 0))],
        out_specs=[pl.BlockSpec((tm, D_MODEL), lambda i: (i, 0)),
                   pl.BlockSpec((tm, LANES), lambda i: (i, 0))],
        out_shape=[jax.ShapeDtypeStruct((m, D_MODEL), BF16),
                   jax.ShapeDtypeStruct((m, LANES), F32)],
        compiler_params=_cparams("parallel"),
        name="norm_gates",
    )(x, g, wg)


def _norm(x, g, tm):
    m = x.shape[0]
    return pl.pallas_call(
        _norm_kernel,
        grid=(m // tm,),
        in_specs=[pl.BlockSpec((tm, D_MODEL), lambda i: (i, 0)),
                  pl.BlockSpec((1, D_MODEL), lambda i: (0, 0))],
        out_specs=pl.BlockSpec((tm, D_MODEL), lambda i: (i, 0)),
        out_shape=jax.ShapeDtypeStruct((m, D_MODEL), BF16),
        compiler_params=_cparams("parallel"),
        name="norm",
    )(x, g)


def _proj_kernel(xp_ref, xs_ref, w_ref, op_ref, os_ref, wb_ref, *, w_transposed):
    mm = _dot_nt if w_transposed else _dot

    @pl.when(pl.program_id(1) == 0)
    def _():
        wb_ref[...] = w_ref[...].astype(BF16)
        os_ref[...] = mm(xs_ref[...], wb_ref[...])

    op_ref[...] = mm(xp_ref[...], wb_ref[...]).astype(op_ref.dtype)


def _proj(xp, xs, w, n_cols, out_dtype, tm, tn, w_transposed):
    m, k = xp.shape
    ms = xs.shape[0]
    if w_transposed:
        w_spec = pl.BlockSpec((tn, k), lambda j, i: (j, 0))
        wb_shape = (tn, k)
    else:
        w_spec = pl.BlockSpec((k, tn), lambda j, i: (0, j))
        wb_shape = (k, tn)
    return pl.pallas_call(
        functools.partial(_proj_kernel, w_transposed=w_transposed),
        grid=(n_cols // tn, m // tm),
        in_specs=[pl.BlockSpec((tm, k), lambda j, i: (i, 0)),
                  pl.BlockSpec((ms, k), lambda j, i: (0, 0)),
                  w_spec],
        out_specs=[pl.BlockSpec((tm, tn), lambda j, i: (i, j)),
                   pl.BlockSpec((ms, tn), lambda j, i: (0, j))],
        out_shape=[jax.ShapeDtypeStruct((m, n_cols), out_dtype),
                   jax.ShapeDtypeStruct((ms, n_cols), F32)],
        scratch_shapes=[pltpu.VMEM(wb_shape, BF16)],
        compiler_params=_cparams("parallel", "arbitrary"),
        name="proj",
    )(xp, xs, w)


def _outproj_kernel(*refs, two):
    if two:
        ya_ref, yb_ref, yas_ref, ybs_ref, wa_ref, wb_ref, h_ref, hs_ref, g_ref, o_ref, os_ref = refs
    else:
        ya_ref, yas_ref, wa_ref, h_ref, hs_ref, g_ref, o_ref, os_ref = refs
        yb_ref = ybs_ref = wb_ref = None

    def run(a_ref, b_ref, res_ref):
        mix = _dot(a_ref[...].astype(BF16), wa_ref[...])
        if two:
            mix = mix + _dot(b_ref[...].astype(BF16), wb_ref[...])
        return res_ref[...] + _rms(mix, g_ref[...])

    @pl.when(pl.program_id(0) == 0)
    def _():
        os_ref[...] = run(yas_ref, ybs_ref, hs_ref)

    o_ref[...] = run(ya_ref, yb_ref, h_ref)


def _outproj(ya, yb, yas, ybs, w, h, hs, g, tm):
    m = h.shape[0]
    ms = hs.shape[0]
    ka = ya.shape[1]
    two = yb is not None
    once = dict(pipeline_mode=pl.Buffered(1))
    row = lambda width: pl.BlockSpec((tm, width), lambda i: (i, 0))
    fixed = lambda rows, width: pl.BlockSpec((rows, width), lambda i: (0, 0), **once)
    if two:
        kb = yb.shape[1]
        in_specs = [row(ka), row(kb), fixed(ms, ka), fixed(ms, kb), fixed(ka, D_MODEL),
                    pl.BlockSpec((kb, D_MODEL), lambda i: (ka // kb, 0), **once)]
        args = [ya, yb, yas, ybs, w, w]
    else:
        in_specs = [row(ka), fixed(ms, ka), fixed(ka, D_MODEL)]
        args = [ya, yas, w]
    in_specs += [row(D_MODEL), fixed(ms, D_MODEL), fixed(1, D_MODEL)]
    args += [h, hs, g]
    return pl.pallas_call(
        functools.partial(_outproj_kernel, two=two),
        grid=(m // tm,),
        in_specs=in_specs,
        out_specs=[row(D_MODEL), pl.BlockSpec((ms, D_MODEL), lambda i: (0, 0))],
        out_shape=[jax.ShapeDtypeStruct((m, D_MODEL), F32), jax.ShapeDtypeStruct((ms, D_MODEL), F32)],
        compiler_params=pltpu.CompilerParams(dimension_semantics=("arbitrary",),
                                             vmem_limit_bytes=OUTPROJ_VMEM_LIMIT),
        name="outproj",
    )(*args)


def _mlstm_prompt_kernel(bg_ref, q_ref, k_ref, v_ref, o_ref, z_ref, gc_ref, gr_ref, gh_ref,
                         y_ref, c_out, n_out, m_out, c_s, n_s, m_s):
    c = pl.program_id(1)
    t = CHUNK

    @pl.when(c == 0)
    def _():
        c_s[...] = jnp.zeros_like(c_s)
        n_s[...] = jnp.zeros_like(n_s)
        m_s[...] = jnp.zeros_like(m_s)

    row = lax.broadcasted_iota(jnp.int32, (t, t), 0)
    col = lax.broadcasted_iota(jnp.int32, (t, t), 1)
    tril = col <= row
    triu = row <= col
    gc = gc_ref[...]
    gr = gr_ref[...]
    for h in range(H_A):
        b_i = bg_ref[h]
        b_f = bg_ref[H_A + h]
        i_col = gc[:, h:h + 1] + b_i
        i_row = gr[h:h + 1, :] + b_i
        lf_col = _log_sigmoid(gc[:, H_A + h:H_A + h + 1] + b_f)
        lf_row = _log_sigmoid(gr[H_A + h:H_A + h + 1, :] + b_f)
        b_col = jnp.sum(jnp.where(tril, lf_row, 0.0), axis=1, keepdims=True)
        b_row = jnp.sum(jnp.where(triu, lf_col, 0.0), axis=0, keepdims=True)
        m_prev = m_s[h][:, 0:1]
        d = jnp.where(tril, b_col - b_row + i_row, -jnp.inf)
        inter = b_col + m_prev
        m_t = jnp.maximum(inter, jnp.max(d, axis=1, keepdims=True))
        w_intra = jnp.exp(d - m_t)
        w_inter = jnp.exp(inter - m_t) * SCALE_A
        q = q_ref[:, h * DK_A:(h + 1) * DK_A]
        k = k_ref[:, h * DK_A:(h + 1) * DK_A]
        v = v_ref[:, h * DV_A:(h + 1) * DV_A]
        s = _dot_nt(q, k) * (w_intra * SCALE_A)
        c_old = c_s[h]
        n_old = n_s[h]
        num = _dot(s.astype(BF16), v) + w_inter * _dot(q, c_old.astype(BF16))
        qn = jnp.sum(q.astype(F32) * n_old, axis=1, keepdims=True)
        den = jnp.sum(s, axis=1, keepdims=True) + w_inter * qn
        hh = num / jnp.maximum(jnp.abs(den), jnp.exp(-m_t))
        hn = _rms(hh, gh_ref[h:h + 1, :])
        o = o_ref[:, h * DV_A:(h + 1) * DV_A].astype(F32)
        z = z_ref[:, h * DV_A:(h + 1) * DV_A].astype(F32)
        y_ref[:, h * DV_A:(h + 1) * DV_A] = (hn * jax.nn.sigmoid(o) * _silu(z)).astype(BF16)
        b_last = b_col[t - 1:t, :]
        g_col = b_last - b_col + i_col
        m_new = jnp.maximum(b_last + m_prev, jnp.max(g_col, axis=0, keepdims=True))
        e_col = jnp.exp(g_col - m_new)
        decay = jnp.exp(b_last + m_prev - m_new)
        ke = k.astype(F32) * e_col
        c_s[h] = decay * c_old + _dot_tn(ke.astype(BF16), v)
        n_s[h] = decay * n_old + jnp.sum(ke, axis=0, keepdims=True)
        m_s[h] = jnp.broadcast_to(m_new, (1, LANES))

    @pl.when(c == pl.num_programs(1) - 1)
    def _():
        c_out[...] = c_s[...]
        n_out[...] = n_s[...]
        m_out[...] = m_s[...]


def _mlstm_prompt(bg, proj, gates_col, gates_row, g_head):
    nc = SEQ // CHUNK
    t = CHUNK
    qk_w = H_A * DK_A
    return pl.pallas_call(
        _mlstm_prompt_kernel,
        grid=(BATCH, nc),
        in_specs=[pl.BlockSpec(memory_space=pltpu.SMEM),
                  pl.BlockSpec((None, t, qk_w), lambda b, c: (b, c, 0)),
                  pl.BlockSpec((None, t, qk_w), lambda b, c: (b, c, 1)),
                  pl.BlockSpec((None, t, W_A), lambda b, c: (b, c, 1)),
                  pl.BlockSpec((None, t, W_A), lambda b, c: (b, c, 2)),
                  pl.BlockSpec((None, t, W_A), lambda b, c: (b, c, 3)),
                  pl.BlockSpec((None, t, LANES), lambda b, c: (b, c, 0)),
                  pl.BlockSpec((None, 2 * H_A, t), lambda b, c: (b, 0, c)),
                  pl.BlockSpec((H_A, DV_A), lambda b, c: (0, 0))],
        out_specs=[pl.BlockSpec((None, t, W_A), lambda b, c: (b, c, 0)),
                   pl.BlockSpec((None, H_A, DK_A, DV_A), lambda b, c: (b, 0, 0, 0)),
                   pl.BlockSpec((None, H_A, 1, DK_A), lambda b, c: (b, 0, 0, 0)),
                   pl.BlockSpec((None, H_A, 1, LANES), lambda b, c: (b, 0, 0, 0))],
        out_shape=[jax.ShapeDtypeStruct((BATCH, SEQ, W_A), BF16),
                   jax.ShapeDtypeStruct((BATCH, H_A, DK_A, DV_A), F32),
                   jax.ShapeDtypeStruct((BATCH, H_A, 1, DK_A), F32),
                   jax.ShapeDtypeStruct((BATCH, H_A, 1, LANES), F32)],
        scratch_shapes=[pltpu.VMEM((H_A, DK_A, DV_A), F32),
                        pltpu.VMEM((H_A, 1, DK_A), F32),
                        pltpu.VMEM((H_A, 1, LANES), F32)],
        compiler_params=_cparams("parallel", "arbitrary"),
        name="mlstm_prompt",
    )(bg, proj, proj, proj, proj, proj, gates_col, gates_row, g_head)


def _mlstm_sample_kernel(gi_ref, gf_ref, bi_ref, bf_ref, m0_ref, q_ref, k_ref, v_ref, o_ref, z_ref,
                         qt_ref, kt_ref, n0_ref, gh_ref, c0_ref,
                         y_ref, c1_ref, n1_ref, m1_ref):
    r = pl.program_id(0)
    rows = pl.ds(r, 1)
    shift = lax.rem(DEC_BATCH - r, DEC_BATCH)
    i_v = gi_ref[rows, :] + bi_ref[...]
    lf_v = _log_sigmoid(gf_ref[rows, :] + bf_ref[...])
    m0_v = m0_ref[rows, :]
    m_t = jnp.maximum(lf_v + m0_v, i_v)
    w_in = jnp.exp(i_v - m_t)
    w_st = jnp.exp(lf_v + m0_v - m_t)
    floor = jnp.exp(-m_t)
    m1_ref[rows, :] = m_t
    for h in range(H_A):
        ks = slice(h * DK_A, (h + 1) * DK_A)
        vs = slice(h * DV_A, (h + 1) * DV_A)
        wi = w_in[:, h:h + 1]
        ws = w_st[:, h:h + 1]
        q_col = pltpu.roll(qt_ref[h], shift, axis=1)[:, 0:1]
        k_col = pltpu.roll(kt_ref[h], shift, axis=1)[:, 0:1]
        q_row = q_ref[rows, ks]
        k_row = k_ref[rows, ks]
        v_row = v_ref[rows, vs]
        n_row = n0_ref[rows, ks]
        c_old = c0_ref[0, h]
        qk = jnp.sum(q_row * k_row, axis=1, keepdims=True) * SCALE_A
        s = qk * wi
        q_c = jnp.sum(c_old * q_col, axis=0, keepdims=True) * SCALE_A
        qn = jnp.sum(q_row * n_row, axis=1, keepdims=True) * SCALE_A
        num = s * v_row + ws * q_c
        den = s + ws * qn
        hh = num / jnp.maximum(jnp.abs(den), floor[:, h:h + 1])
        hn = _rms(hh, gh_ref[h:h + 1, :])
        y_ref[rows, vs] = hn * jax.nn.sigmoid(o_ref[rows, vs]) * _silu(z_ref[rows, vs])
        c1_ref[0, h] = ws * c_old + (wi * k_col) * v_row
        n1_ref[rows, ks] = ws * n_row + wi * k_row


def _mlstm_sample(gi, gf, bi, bf, m0, proj, qt, kt, n0, g_head, c0):
    qk_w = H_A * DK_A
    full = lambda shape: pl.BlockSpec(shape, lambda r: (0,) * len(shape))
    return pl.pallas_call(
        _mlstm_sample_kernel,
        grid=(DEC_BATCH,),
        in_specs=[full((DEC_BATCH, H_A)), full((DEC_BATCH, H_A)), full((1, H_A)), full((1, H_A)),
                  full((DEC_BATCH, H_A)),
                  pl.BlockSpec((DEC_BATCH, qk_w), lambda r: (0, 0)),
                  pl.BlockSpec((DEC_BATCH, qk_w), lambda r: (0, 1)),
                  pl.BlockSpec((DEC_BATCH, W_A), lambda r: (0, 1)),
                  pl.BlockSpec((DEC_BATCH, W_A), lambda r: (0, 2)),
                  pl.BlockSpec((DEC_BATCH, W_A), lambda r: (0, 3)),
                  full((H_A, DK_A, DEC_BATCH)), full((H_A, DK_A, DEC_BATCH)),
                  full((DEC_BATCH, qk_w)), full((H_A, DV_A)),
                  pl.BlockSpec((1, H_A, DK_A, DV_A), lambda r: (r, 0, 0, 0))],
        out_specs=[full((DEC_BATCH, W_A)),
                   pl.BlockSpec((1, H_A, DK_A, DV_A), lambda r: (r, 0, 0, 0)),
                   full((DEC_BATCH, qk_w)), full((DEC_BATCH, H_A))],
        out_shape=[jax.ShapeDtypeStruct((DEC_BATCH, W_A), F32),
                   jax.ShapeDtypeStruct((DEC_BATCH, H_A, DK_A, DV_A), F32),
                   jax.ShapeDtypeStruct((DEC_BATCH, qk_w), F32),
                   jax.ShapeDtypeStruct((DEC_BATCH, H_A), F32)],
        compiler_params=_cparams("arbitrary"),
        name="mlstm_sample",
    )(gi, gf, bi, bf, m0, proj, proj, proj, proj, proj, qt, kt, n0, g_head, c0)


def _s5_build_kernel(ldt_ref, lamr_ref, lami_ref, b1_ref, cc_ref, dsk_ref,
                     m_ref, w_ref, v_ref, p1_ref, p2_ref):
    dt = jnp.exp(ldt_ref[...])
    lam_re, lam_im = lamr_ref[...], lami_ref[...]
    mag = jnp.exp(lam_re * dt)
    ang = lam_im * dt
    a_re, a_im = mag * jnp.cos(ang), mag * jnp.sin(ang)
    den = lam_re * lam_re + lam_im * lam_im
    f_re = ((a_re - 1.0) * lam_re + a_im * lam_im) / den
    f_im = (a_im * lam_re - (a_re - 1.0) * lam_im) / den
    pows = [(jnp.ones_like(a_re), jnp.zeros_like(a_im))]
    for _ in range(S5_T):
        r_re, r_im = pows[-1]
        pows.append((r_re * a_re - r_im * a_im, r_re * a_im + r_im * a_re))

    def tall(vals):
        return jnp.concatenate([jnp.broadcast_to(x, (GROUP_B, LANES)) for x in vals], axis=0)

    af_re = tall([pows[S5_T - 1 - s][0] * f_re - pows[S5_T - 1 - s][1] * f_im for s in range(S5_T)])
    af_im = tall([pows[S5_T - 1 - s][0] * f_im + pows[S5_T - 1 - s][1] * f_re for s in range(S5_T)])
    lane = lax.broadcasted_iota(jnp.int32, (1, LANES), 1)
    sgn_lane = jnp.where(lane < P_B, -1.0, 1.0)
    b1 = b1_ref[...]
    b2 = sgn_lane * pltpu.roll(b1, P_B, axis=1)
    w_ref[...] = (af_re * b1 + af_im * b2).astype(BF16)

    eye = (lax.broadcasted_iota(jnp.int32, (LANES, LANES), 0)
           == lax.broadcasted_iota(jnp.int32, (LANES, LANES), 1))

    def column(row):
        return jnp.sum(jnp.where(eye, row, 0.0), axis=1, keepdims=True)

    ac_re, ac_im, fc_re, fc_im = column(a_re), column(a_im), column(f_re), column(f_im)
    kid = lax.broadcasted_iota(jnp.int32, (S5_T, LANES), 0)
    expand = (lax.broadcasted_iota(jnp.int32, (S5_T, S5_W), 1) // GROUP_B
              == lax.broadcasted_iota(jnp.int32, (S5_T, S5_W), 0)).astype(BF16)

    def lane_blocks(part):
        stacked = jnp.zeros((S5_T, LANES), F32)
        for k in range(S5_T):
            stacked = jnp.where(kid == k, pows[k][part], stacked)
        hi = stacked.astype(BF16)
        rest = stacked - hi.astype(F32)
        mid = rest.astype(BF16)
        lo = (rest - mid.astype(F32)).astype(BF16)
        return _dot_tn(hi, expand) + _dot_tn(mid, expand) + _dot_tn(lo, expand)

    q_re, q_im = lane_blocks(0), lane_blocks(1)
    cc = cc_ref[...]
    cs = pltpu.roll(cc, P_B, axis=0)
    rowi = lax.broadcasted_iota(jnp.int32, (LANES, 1), 0)
    sgn_row = jnp.where(rowi < P_B, 1.0, -1.0)

    def readout(r_re, r_im):
        return sgn_row * (r_re * cc) - r_im * cs

    v_ref[...] = readout(q_re * ac_re - q_im * ac_im, q_re * ac_im + q_im * ac_re).astype(BF16)
    vf = readout(q_re * fc_re - q_im * fc_im, q_re * fc_im + q_im * fc_re)
    kw = _dot3(b1[0:GROUP_B, :], vf)
    lane_w = lax.broadcasted_iota(jnp.int32, (GROUP_B, S5_W), 1)
    blocks = [kw]
    for s in range(1, S5_T):
        blocks.append(jnp.where(lane_w >= s * GROUP_B, pltpu.roll(kw, s * GROUP_B, axis=1), 0.0))
    toep = jnp.concatenate(blocks, axis=0)
    ri = lax.broadcasted_iota(jnp.int32, (S5_W, S5_W), 0)
    ci = lax.broadcasted_iota(jnp.int32, (S5_W, S5_W), 1)
    m_ref[...] = (toep + jnp.where(ri == ci, dsk_ref[...], 0.0)).astype(BF16)

    r_re, r_im = pows[S5_T]
    rid = lax.broadcasted_iota(jnp.int32, (8, LANES), 0)
    p1 = jnp.where(rid == S5_A1_ROW, a_re, 0.0)
    p2 = jnp.where(rid == S5_A1_ROW, sgn_lane * a_im, 0.0)
    for kk in range(S5_LEVELS):
        p1 = jnp.where(rid == kk, r_re, p1)
        p2 = jnp.where(rid == kk, sgn_lane * r_im, p2)
        r_re, r_im = r_re * r_re - r_im * r_im, 2.0 * (r_re * r_im)
    p1_ref[...] = p1
    p2_ref[...] = p2


def _s5_build(ldt, lamr, lami, b1, cc, dsk):
    g3 = lambda a, b: pl.BlockSpec((None, a, b), lambda g: (g, 0, 0))
    return pl.pallas_call(
        _s5_build_kernel,
        grid=(G_B,),
        in_specs=[g3(1, 1), g3(1, LANES), g3(1, LANES),
                  g3(S5_W, LANES), g3(LANES, S5_W), g3(1, S5_W)],
        out_specs=[g3(S5_W, S5_W), g3(S5_W, LANES), g3(LANES, S5_W), g3(8, LANES), g3(8, LANES)],
        out_shape=[jax.ShapeDtypeStruct((G_B, S5_W, S5_W), BF16),
                   jax.ShapeDtypeStruct((G_B, S5_W, LANES), BF16),
                   jax.ShapeDtypeStruct((G_B, LANES, S5_W), BF16),
                   jax.ShapeDtypeStruct((G_B, 8, LANES), F32),
                   jax.ShapeDtypeStruct((G_B, 8, LANES), F32)],
        compiler_params=_cparams("parallel"),
        name="s5_build",
    )(ldt, lamr, lami, b1, cc, dsk)


def _s5_prompt_kernel(u_ref, m_ref, w_ref, v_ref, p1_ref, p2_ref, y_ref, xf_ref, y_s):
    nblk = SEQ // S5_T
    rows = BATCH * nblk
    bidx = lax.broadcasted_iota(jnp.int32, (rows, LANES), 0) & (nblk - 1)
    steps = [u_ref[pl.ds(s, rows, stride=S5_T), :] for s in range(S5_T)]
    for g in range(OCT):
        gl = slice(g * GROUP_B, (g + 1) * GROUP_B)
        u = jnp.concatenate([x[:, gl] for x in steps], axis=-1).astype(BF16)
        s = _dot(u, w_ref[g])
        for kk in range(S5_LEVELS):
            sh = 1 << kk
            r = jnp.where(bidx >= sh, pltpu.roll(s, sh, axis=0), 0.0)
            s = s + p1_ref[g, kk:kk + 1, :] * r + p2_ref[g, kk:kk + 1, :] * pltpu.roll(r, P_B, axis=1)
        x_prev = jnp.where(bidx >= 1, pltpu.roll(s, 1, axis=0), 0.0)
        y = _dot(u, m_ref[g]) + _dot(x_prev.astype(BF16), v_ref[g])
        y_s[g] = _gelu_tanh(y)
        for b in range(BATCH):
            xf_ref[g, b:b + 1, :] = s[(b + 1) * nblk - 1:(b + 1) * nblk, :]
    for t in range(S5_T):
        half = slice((t // OCT) * LANES, (t // OCT + 1) * LANES)
        tl = slice((t % OCT) * GROUP_B, (t % OCT + 1) * GROUP_B)
        y_ref[pl.ds(t, rows, stride=S5_T), :] = jnp.concatenate(
            [y_s[g, :, half][:, tl] for g in range(OCT)], axis=-1)


def _s5_prompt(u, m, w, v, p1, p2):
    mp = BATCH * SEQ
    rows = BATCH * (SEQ // S5_T)
    o3 = lambda a, b: pl.BlockSpec((OCT, a, b), lambda g: (g, 0, 0))
    return pl.pallas_call(
        _s5_prompt_kernel,
        grid=(G_B // OCT,),
        in_specs=[pl.BlockSpec((mp, LANES), lambda g: (0, g)),
                  o3(S5_W, S5_W), o3(S5_W, LANES), o3(LANES, S5_W), o3(8, LANES), o3(8, LANES)],
        out_specs=[pl.BlockSpec((mp, LANES), lambda g: (0, g)), o3(BATCH, LANES)],
        out_shape=[jax.ShapeDtypeStruct((mp, W_B), F32),
                   jax.ShapeDtypeStruct((G_B, BATCH, LANES), F32)],
        scratch_shapes=[pltpu.VMEM((OCT, rows, S5_W), F32)],
        compiler_params=_cparams("parallel"),
        name="s5_prompt",
    )(u, m, w, v, p1, p2)


def _s5_sample_kernel(u_ref, xr_ref, xi_ref, m_ref, w_ref, v_ref, p1_ref, p2_ref,
                      y_ref, x1r_ref, x1i_ref):
    lane = lax.broadcasted_iota(jnp.int32, (DEC_BATCH, LANES), 1)
    last = LANES - GROUP_B
    half = S5_W // 2
    a1 = slice(S5_A1_ROW, S5_A1_ROW + 1)
    u_all = u_ref[...]
    ys = []
    for g in range(OCT):
        ps = slice(g * P_B, (g + 1) * P_B)
        u = jnp.where(lane >= last, pltpu.roll(u_all, (last - g * GROUP_B) % LANES, axis=1), 0.0)
        u = u.astype(BF16)
        x0 = jnp.concatenate([xr_ref[:, ps], xi_ref[:, ps]], axis=-1)
        x1 = (p1_ref[g, a1, :] * x0 + p2_ref[g, a1, :] * pltpu.roll(x0, P_B, axis=1)
              + _dot(u, w_ref[g, half:, :]))
        x1r_ref[:, ps] = x1[:, :P_B]
        x1i_ref[:, ps] = x1[:, P_B:]
        y = (_dot(x0.astype(BF16), v_ref[g])[:, 0:GROUP_B]
             + _dot(u, m_ref[g, half:, :])[:, S5_W - GROUP_B:])
        ys.append(_gelu_tanh(y))
    y_ref[...] = jnp.concatenate(ys, axis=-1)


def _s5_sample(u, xr, xi, m, w, v, p1, p2):
    o3 = lambda a, b: pl.BlockSpec((OCT, a, b), lambda g: (g, 0, 0))
    tile = pl.BlockSpec((DEC_BATCH, LANES), lambda g: (0, g))
    st = pl.BlockSpec((DEC_BATCH, OCT * P_B), lambda g: (0, g))
    st_shape = jax.ShapeDtypeStruct((DEC_BATCH, G_B * P_B), F32)
    return pl.pallas_call(
        _s5_sample_kernel,
        grid=(G_B // OCT,),
        in_specs=[tile, st, st, o3(S5_W, S5_W), o3(S5_W, LANES), o3(LANES, S5_W), o3(8, LANES), o3(8, LANES)],
        out_specs=[tile, st, st],
        out_shape=[jax.ShapeDtypeStruct((DEC_BATCH, W_B), F32), st_shape, st_shape],
        compiler_params=_cparams("parallel"),
        name="s5_sample",
    )(u, xr, xi, m, w, v, p1, p2)


def _glu_kernel(y_ref, w_ref, b_ref, z_ref, o_ref):
    y = y_ref[...]
    gate = jax.nn.sigmoid(_dot(y.astype(BF16), w_ref[...]) + b_ref[...])
    o_ref[...] = (y.astype(F32) * gate * _silu(z_ref[...].astype(F32))).astype(o_ref.dtype)


def _glu(y, w, b, proj_b, tm):
    m = y.shape[0]
    return pl.pallas_call(
        _glu_kernel,
        grid=(m // tm,),
        in_specs=[pl.BlockSpec((tm, W_B), lambda i: (i, 0)),
                  pl.BlockSpec((W_B, W_B), lambda i: (0, 0)),
                  pl.BlockSpec((1, W_B), lambda i: (0, 0)),
                  pl.BlockSpec((tm, W_B), lambda i: (i, 1))],
        out_specs=pl.BlockSpec((tm, W_B), lambda i: (i, 0)),
        out_shape=jax.ShapeDtypeStruct((m, W_B), BF16),
        compiler_params=_cparams("parallel"),
        name="glu",
    )(y, w, b, proj_b)


def _rope_table_kernel(cos_ref, sin_ref, *, pos0):
    shape = cos_ref.shape
    pos = lax.broadcasted_iota(jnp.int32, shape, 0).astype(F32) + pos0
    j = lax.broadcasted_iota(jnp.int32, shape, 1).astype(F32)
    ang = pos * jnp.power(ROPE_BASE, -(j / (DK_C // 2)))
    cos_ref[...] = jnp.cos(ang)
    sin_ref[...] = jnp.sin(ang)


def _rope_table(rows, pos0):
    shape = jax.ShapeDtypeStruct((rows, DK_C // 2), F32)
    return pl.pallas_call(functools.partial(_rope_table_kernel, pos0=float(pos0)),
                          out_shape=[shape, shape], name="rope_table")()


def _rope(x, cos, sin):
    half = DK_C // 2
    x1, x2 = x[:, :half], x[:, half:]
    return jnp.concatenate([x1 * cos - x2 * sin, x1 * sin + x2 * cos], axis=-1)


def _ret_prompt_kernel(q_ref, k_ref, v_ref, z_ref, cos_ref, sin_ref, gh_ref, y_ref, s_out, s_s):
    c = pl.program_id(1)
    t = CHUNK

    @pl.when(c == 0)
    def _():
        s_s[...] = jnp.zeros_like(s_s)

    row = lax.broadcasted_iota(jnp.int32, (t, t), 0)
    col = lax.broadcasted_iota(jnp.int32, (t, t), 1)
    tril = col <= row
    diff = (row - col).astype(F32)
    tpos = lax.broadcasted_iota(jnp.int32, (t, 1), 0).astype(F32)
    cos, sin = cos_ref[...], sin_ref[...]
    for h in range(H_C):
        lg = LOG_GAMMA[h]
        ks = slice(h * DK_C, (h + 1) * DK_C)
        vs = slice(h * DV_C, (h + 1) * DV_C)
        q = _rope(q_ref[:, ks].astype(F32), cos, sin).astype(BF16)
        k32 = _rope(k_ref[:, ks].astype(F32), cos, sin)
        v = v_ref[:, vs]
        mask = jnp.where(tril, jnp.exp(diff * lg), 0.0) * SCALE_C
        s = _dot_nt(q, k32.astype(BF16)) * mask
        s_old = s_s[h]
        o = _dot(s.astype(BF16), v) + _dot(q, s_old.astype(BF16)) * jnp.exp((tpos + 1.0) * lg)
        y_ref[:, vs] = (_rms(o, gh_ref[h:h + 1, :]) * _silu(z_ref[:, vs].astype(F32))).astype(BF16)
        k_tail = k32 * (jnp.exp((t - 1.0 - tpos) * lg) * SCALE_C)
        s_s[h] = math.exp(t * lg) * s_old + _dot_tn(k_tail.astype(BF16), v)

    @pl.when(c == pl.num_programs(1) - 1)
    def _():
        s_out[...] = s_s[...]


def _ret_prompt(proj, cos, sin, g_head):
    nc = SEQ // CHUNK
    t = CHUNK
    qk_w = H_C * DK_C
    return pl.pallas_call(
        _ret_prompt_kernel,
        grid=(BATCH, nc),
        in_specs=[pl.BlockSpec((None, t, qk_w), lambda b, c: (b, c, 0)),
                  pl.BlockSpec((None, t, qk_w), lambda b, c: (b, c, 1)),
                  pl.BlockSpec((None, t, W_C), lambda b, c: (b, c, 1)),
                  pl.BlockSpec((None, t, W_C), lambda b, c: (b, c, 2)),
                  pl.BlockSpec((t, DK_C // 2), lambda b, c: (c, 0)),
                  pl.BlockSpec((t, DK_C // 2), lambda b, c: (c, 0)),
                  pl.BlockSpec((H_C, DV_C), lambda b, c: (0, 0))],
        out_specs=[pl.BlockSpec((None, t, W_C), lambda b, c: (b, c, 0)),
                   pl.BlockSpec((None, H_C, DK_C, DV_C), lambda b, c: (b, 0, 0, 0))],
        out_shape=[jax.ShapeDtypeStruct((BATCH, SEQ, W_C), BF16),
                   jax.ShapeDtypeStruct((BATCH, H_C, DK_C, DV_C), F32)],
        scratch_shapes=[pltpu.VMEM((H_C, DK_C, DV_C), F32)],
        compiler_params=_cparams("parallel", "arbitrary"),
        name="ret_prompt",
    )(proj, proj, proj, proj, cos, sin, g_head)


def _rope_sample_kernel(q_ref, k_ref, cos_ref, sin_ref, qo_ref, ko_ref):
    cos, sin = cos_ref[0:1, :], sin_ref[0:1, :]
    for h in range(H_C):
        ks = slice(h * DK_C, (h + 1) * DK_C)
        qo_ref[:, ks] = _rope(q_ref[:, ks], cos, sin)
        ko_ref[:, ks] = _rope(k_ref[:, ks], cos, sin)


def _rope_sample(proj, cos, sin):
    qk_w = H_C * DK_C
    shape = jax.ShapeDtypeStruct((DEC_BATCH, qk_w), F32)
    return pl.pallas_call(
        _rope_sample_kernel,
        grid=(1,),
        in_specs=[pl.BlockSpec((DEC_BATCH, qk_w), lambda i: (0, 0)),
                  pl.BlockSpec((DEC_BATCH, qk_w), lambda i: (0, 1)),
                  pl.BlockSpec((8, DK_C // 2), lambda i: (0, 0)),
                  pl.BlockSpec((8, DK_C // 2), lambda i: (0, 0))],
        out_specs=[pl.BlockSpec((DEC_BATCH, qk_w), lambda i: (0, 0)),
                   pl.BlockSpec((DEC_BATCH, qk_w), lambda i: (0, 0))],
        out_shape=[shape, shape],
        compiler_params=_cparams("arbitrary"),
        name="rope_sample",
    )(proj, proj, cos, sin)


def _ret_sample_kernel(q_ref, k_ref, v_ref, z_ref, qt_ref, kt_ref, gh_ref, s0_ref, y_ref, s1_ref):
    r = pl.program_id(0)
    rows = pl.ds(r, 1)
    shift = lax.rem(DEC_BATCH - r, DEC_BATCH)
    for h in range(H_C):
        gamma = math.exp(LOG_GAMMA[h])
        ks = slice(h * DK_C, (h + 1) * DK_C)
        vs = slice(h * DV_C, (h + 1) * DV_C)
        q_col = pltpu.roll(qt_ref[h], shift, axis=1)[:, 0:1]
        k_col = pltpu.roll(kt_ref[h], shift, axis=1)[:, 0:1]
        v_row = v_ref[rows, vs]
        s_old = s0_ref[0, h]
        qk = jnp.sum(q_ref[rows, ks] * k_ref[rows, ks], axis=1, keepdims=True) * SCALE_C
        o = qk * v_row + jnp.sum(s_old * q_col, axis=0, keepdims=True) * gamma
        y_ref[rows, vs] = _rms(o, gh_ref[h:h + 1, :]) * _silu(z_ref[rows, vs])
        s1_ref[0, h] = gamma * s_old + (k_col * SCALE_C) * v_row


def _ret_sample(q, k, proj, qt, kt, g_head, s0):
    qk_w = H_C * DK_C
    full = lambda shape: pl.BlockSpec(shape, lambda r: (0,) * len(shape))
    return pl.pallas_call(
        _ret_sample_kernel,
        grid=(DEC_BATCH,),
        in_specs=[full((DEC_BATCH, qk_w)), full((DEC_BATCH, qk_w)),
                  pl.BlockSpec((DEC_BATCH, W_C), lambda r: (0, 1)),
                  pl.BlockSpec((DEC_BATCH, W_C), lambda r: (0, 2)),
                  full((H_C, DK_C, DEC_BATCH)), full((H_C, DK_C, DEC_BATCH)),
                  full((H_C, DV_C)),
                  pl.BlockSpec((1, H_C, DK_C, DV_C), lambda r: (r, 0, 0, 0))],
        out_specs=[full((DEC_BATCH, W_C)),
                   pl.BlockSpec((1, H_C, DK_C, DV_C), lambda r: (r, 0, 0, 0))],
        out_shape=[jax.ShapeDtypeStruct((DEC_BATCH, W_C), F32),
                   jax.ShapeDtypeStruct((DEC_BATCH, H_C, DK_C, DV_C), F32)],
        compiler_params=_cparams("arbitrary"),
        name="ret_sample",
    )(q, k, proj, proj, qt, kt, g_head, s0)


def kernel(x_prompt, x_sample, state_mlstm_C, state_mlstm_n, state_mlstm_m, state_s5_re, state_s5_im, state_ret, g_pre, g_post, w_in0, b_gates0, g_head_a, lam_re, lam_im, log_dt, b_re, b_im, c_re, c_im, d_skip, w_glu, b_glu, w_out0, w_in1, g_head_c, w_out1):
    mp = BATCH * SEQ
    xp = x_prompt.reshape(mp, D_MODEL).astype(F32)
    xs = x_sample.reshape(DEC_BATCH, D_MODEL).astype(F32)

    w0t = w_in0.T.astype(F32)
    w0g = jnp.pad(w0t[QKV0:GATE0], ((0, LANES - 2 * H_A), (0, 0)))
    w0b = w0t[GATE0:]
    w1 = w_in1.astype(F32)
    wo0 = w_out0.astype(BF16)
    wo1 = w_out1.astype(BF16)
    wg = w_glu.astype(BF16)
    bglu = b_glu.reshape(1, W_B).astype(F32)
    g_pre = g_pre.astype(F32)
    g_post = g_post.astype(F32)
    bg = b_gates0.astype(F32)
    gh_a = g_head_a.astype(F32)
    gh_c = g_head_c.astype(F32)

    dup = lambda a: jnp.concatenate([a, a], axis=-1).astype(F32)
    lamr, lami = dup(lam_re), dup(lam_im)
    b1 = jnp.concatenate([b_re.transpose(0, 2, 1), b_im.transpose(0, 2, 1)], axis=-1)
    cc = jnp.concatenate([c_re.transpose(0, 2, 1), c_im.transpose(0, 2, 1)], axis=1)
    s5m, s5w, s5v, s5p1, s5p2 = _s5_build(
        log_dt.reshape(G_B, 1, 1).astype(F32), lamr[:, None, :], lami[:, None, :],
        jnp.tile(b1.astype(F32), (1, S5_T, 1)), jnp.tile(cc.astype(F32), (1, 1, S5_T)),
        jnp.tile(d_skip.astype(F32), (1, S5_T))[:, None, :])

    a0, gates = _norm_gates(xp, g_pre[0:1], w0g, 256)
    a0s, gates_s = _norm_gates(xs, g_pre[0:1], w0g, DEC_BATCH)
    pa, pas = _proj(a0, a0s, w0t, QKV0, BF16, 512, 1024, True)
    pb, pbs = _proj(a0, a0s, w0b, 2 * W_B, F32, 512, 1024, True)
    gates3 = gates.reshape(BATCH, SEQ, LANES)
    gates_row = gates3[:, :, :2 * H_A].transpose(0, 2, 1)
    ya, c_p, n_p, m_p = _mlstm_prompt(bg, pa.reshape(BATCH, SEQ, QKV0), gates3, gates_row, gh_a)
    yb, xf = _s5_prompt(pb, s5m, s5w, s5v, s5p1, s5p2)
    yb = _glu(yb, wg, bglu, pb, 512)
    s5r_p = xf[:, :, :P_B].transpose(1, 0, 2)
    s5i_p = xf[:, :, P_B:].transpose(1, 0, 2)
    qk_a = H_A * DK_A
    to_cols = lambda a, nh, dk: a.reshape(DEC_BATCH, nh, dk).transpose(1, 2, 0)
    yas, c_s, n_s, m_s = _mlstm_sample(
        gates_s[:, :H_A], gates_s[:, H_A:2 * H_A], bg[None, :H_A], bg[None, H_A:],
        state_mlstm_m.astype(F32), pas,
        to_cols(pas[:, :qk_a], H_A, DK_A), to_cols(pas[:, qk_a:2 * qk_a], H_A, DK_A),
        state_mlstm_n.reshape(DEC_BATCH, qk_a).astype(F32), gh_a, state_mlstm_C.astype(F32))
    ybs, s5r_s, s5i_s = _s5_sample(
        pbs, state_s5_re.reshape(DEC_BATCH, G_B * P_B).astype(F32),
        state_s5_im.reshape(DEC_BATCH, G_B * P_B).astype(F32), s5m, s5w, s5v, s5p1, s5p2)
    s5r_s = s5r_s.reshape(DEC_BATCH, G_B, P_B)
    s5i_s = s5i_s.reshape(DEC_BATCH, G_B, P_B)
    ybs = _glu(ybs, wg, bglu, pbs, DEC_BATCH)
    h1, h1s = _outproj(ya.reshape(mp, W_A), yb, yas, ybs, wo0, xp, xs, g_post[0:1], 512)

    a1 = _norm(h1, g_pre[1:2], 256)
    a1s = _norm(h1s, g_pre[1:2], DEC_BATCH)
    p1, p1s = _proj(a1, a1s, w1, 2 * H_C * DK_C + 2 * W_C, BF16, 512, 1024, False)
    cos_p, sin_p = _rope_table(SEQ, 0)
    yc, s_p = _ret_prompt(p1.reshape(BATCH, SEQ, -1), cos_p, sin_p, gh_c)
    cos_s, sin_s = _rope_table(8, PAST_LEN)
    qs, ks = _rope_sample(p1s, cos_s, sin_s)
    ycs, s_s = _ret_sample(qs, ks, p1s, to_cols(qs, H_C, DK_C), to_cols(ks, H_C, DK_C),
                           gh_c, state_ret.astype(F32))
    y_p, y_s = _outproj(yc.reshape(mp, W_C), None, ycs, None, wo1, h1, h1s, g_post[1:2], 512)

    return (y_p.reshape(BATCH, SEQ, D_MODEL), y_s.reshape(DEC_BATCH, 1, D_MODEL),
            c_p, n_p.reshape(BATCH, H_A, DK_A), m_p[:, :, 0, 0],
            s5r_p, s5i_p, s_p,
            c_s, n_s.reshape(DEC_BATCH, H_A, DK_A), m_s,
            s5r_s, s5i_s, s_s)
```

```python
import functools
import math

import jax
import jax.numpy as jnp
from jax import lax
from jax.experimental import pallas as pl
from jax.experimental.pallas import tpu as pltpu

F32 = jnp.float32
BF16 = jnp.bfloat16

D_MODEL = 2048
BATCH = 4
SEQ = 2048
DEC_BATCH = 128
PAST_LEN = 16384
H_A = 4
DK_A = 256
DV_A = 512
W_A = H_A * DV_A
GROUP_B = 16
G_B = 64
P_B = 64
W_B = G_B * GROUP_B
H_C = 8
DK_C = 256
DV_C = 512
W_C = H_C * DV_C
CHUNK = 128
NORM_EPS = 1e-6
ROPE_BASE = 10000.0
QKV0 = 2 * H_A * DK_A + 3 * W_A
GATE0 = QKV0 + 2 * H_A
SCALE_A = DK_A ** -0.5
SCALE_C = DK_C ** -0.5
LOG_GAMMA = tuple(math.log1p(-(2.0 ** (-5.0 - h))) for h in range(H_C))
S5_T = 16
S5_W = S5_T * GROUP_B
LANES = 128
OCT = LANES // GROUP_B
S5_LEVELS = int(math.log2(SEQ // S5_T))
S5_A1_ROW = S5_LEVELS
VMEM_LIMIT = 48 * 1024 * 1024
BIG_VMEM_LIMIT = 56 * 1024 * 1024
SAMPLE_TB = 2


def _cparams(*sem):
    return pltpu.CompilerParams(dimension_semantics=sem, vmem_limit_bytes=VMEM_LIMIT)


def _dot(a, b):
    return jnp.dot(a, b, preferred_element_type=F32)


def _dot_nt(a, b):
    return lax.dot_general(a, b, (((1,), (1,)), ((), ())), preferred_element_type=F32)


def _dot_tn(a, b):
    return lax.dot_general(a, b, (((0,), (0,)), ((), ())), preferred_element_type=F32)


def _split(x):
    hi = x.astype(BF16)
    return hi, (x - hi.astype(F32)).astype(BF16)


def _dot3(a, b):
    a_hi, a_lo = _split(a)
    b_hi, b_lo = _split(b)
    return _dot(a_hi, b_hi) + _dot(a_hi, b_lo) + _dot(a_lo, b_hi)


def _log_sigmoid(x):
    return jnp.minimum(x, 0.0) - jnp.log1p(jnp.exp(-jnp.abs(x)))


def _silu(x):
    return x * jax.nn.sigmoid(x)


def _gelu_tanh(x):
    return 0.5 * x * (1.0 + jnp.tanh(math.sqrt(2.0 / math.pi) * (x + 0.044715 * (x * x * x))))


def _rms(x, g):
    return x * lax.rsqrt(jnp.mean(x * x, axis=-1, keepdims=True) + NORM_EPS) * g


def _norm_gates_kernel(x_ref, g_ref, wg_ref, a_ref, gates_ref):
    a = _rms(x_ref[...], g_ref[...])
    a_ref[...] = a.astype(BF16)
    a_hi, a_lo = _split(a)
    w_hi, w_lo = _split(wg_ref[...])
    gates_ref[...] = _dot_nt(a_hi, w_hi) + _dot_nt(a_hi, w_lo) + _dot_nt(a_lo, w_hi)


def _norm_gates(x, g, wg, tm):
    m = x.shape[0]
    return pl.pallas_call(
        _norm_gates_kernel,
        grid=(m // tm,),
        in_specs=[pl.BlockSpec((tm, D_MODEL), lambda i: (i, 0)),
                  pl.BlockSpec((1, D_MODEL), lambda i: (0, 0)),
                  pl.BlockSpec((LANES, D_MODEL), lambda i: (0, 0))],
        out_specs=[pl.BlockSpec((tm, D_MODEL), lambda i: (i, 0)),
                   pl.BlockSpec((tm, LANES), lambda i: (i, 0))],
        out_shape=[jax.ShapeDtypeStruct((m, D_MODEL), BF16),
                   jax.ShapeDtypeStruct((m, LANES), F32)],
        compiler_params=_cparams("parallel"),
        name="norm_gates",
    )(x, g, wg)


def _proj_kernel(xp_ref, xs_ref, w_ref, op_ref, os_ref, wb_ref, *, w_transposed):
    mm = _dot_nt if w_transposed else _dot

    @pl.when(pl.program_id(1) == 0)
    def _():
        wb_ref[...] = w_ref[...].astype(BF16)
        os_ref[...] = mm(xs_ref[...], wb_ref[...])

    op_ref[...] = mm(xp_ref[...], wb_ref[...]).astype(op_ref.dtype)


def _proj(xp, xs, w, n_cols, out_dtype, tm, tn, w_transposed):
    m, k = xp.shape
    ms = xs.shape[0]
    if w_transposed:
        w_spec = pl.BlockSpec((tn, k), lambda j, i: (j, 0))
        wb_shape = (tn, k)
    else:
        w_spec = pl.BlockSpec((k, tn), lambda j, i: (0, j))
        wb_shape = (k, tn)
    return pl.pallas_call(
        functools.partial(_proj_kernel, w_transposed=w_transposed),
        grid=(n_cols // tn, m // tm),
        in_specs=[pl.BlockSpec((tm, k), lambda j, i: (i, 0)),
                  pl.BlockSpec((ms, k), lambda j, i: (0, 0)),
                  w_spec],
        out_specs=[pl.BlockSpec((tm, tn), lambda j, i: (i, j)),
                   pl.BlockSpec((ms, tn), lambda j, i: (0, j))],
        out_shape=[jax.ShapeDtypeStruct((m, n_cols), out_dtype),
                   jax.ShapeDtypeStruct((ms, n_cols), F32)],
        scratch_shapes=[pltpu.VMEM(wb_shape, BF16)],
        compiler_params=_cparams("parallel", "arbitrary"),
        name="proj",
    )(xp, xs, w)


def _outproj_kernel(*refs, two, next_norm):
    refs = list(refs)
    ya_ref = refs.pop(0)
    yb_ref = refs.pop(0) if two else None
    yas_ref = refs.pop(0)
    ybs_ref = refs.pop(0) if two else None
    wa_ref = refs.pop(0)
    wb_ref = refs.pop(0) if two else None
    h_ref, hs_ref, g_ref = refs.pop(0), refs.pop(0), refs.pop(0)
    gn_ref = refs.pop(0) if next_norm else None
    o_ref, os_ref = refs.pop(0), refs.pop(0)
    a_ref, as_ref = (refs.pop(0), refs.pop(0)) if next_norm else (None, None)

    def run(x_ref, b_ref, res_ref, out_ref, nxt_ref):
        mix = _dot(x_ref[...].astype(BF16), wa_ref[...])
        if two:
            mix = mix + _dot(b_ref[...].astype(BF16), wb_ref[...])
        new = res_ref[...] + _rms(mix, g_ref[...])
        out_ref[...] = new
        if next_norm:
            nxt_ref[...] = _rms(new, gn_ref[...]).astype(BF16)

    @pl.when(pl.program_id(0) == 0)
    def _():
        run(yas_ref, ybs_ref, hs_ref, os_ref, as_ref)

    run(ya_ref, yb_ref, h_ref, o_ref, a_ref)


def _outproj(ya, yb, yas, ybs, w, h, hs, g, g_next, tm):
    m = h.shape[0]
    ms = hs.shape[0]
    ka = ya.shape[1]
    two = yb is not None
    next_norm = g_next is not None
    once = dict(pipeline_mode=pl.Buffered(1))
    row = lambda width: pl.BlockSpec((tm, width), lambda i: (i, 0))
    fixed = lambda rows, width: pl.BlockSpec((rows, width), lambda i: (0, 0), **once)
    if two:
        kb = yb.shape[1]
        in_specs = [row(ka), row(kb), fixed(ms, ka), fixed(ms, kb), fixed(ka, D_MODEL),
                    pl.BlockSpec((kb, D_MODEL), lambda i: (ka // kb, 0), **once)]
        args = [ya, yb, yas, ybs, w, w]
    else:
        in_specs = [row(ka), fixed(ms, ka), fixed(ka, D_MODEL)]
        args = [ya, yas, w]
    in_specs += [row(D_MODEL), fixed(ms, D_MODEL), fixed(1, D_MODEL)]
    args += [h, hs, g]
    sample_out = pl.BlockSpec((ms, D_MODEL), lambda i: (0, 0))
    out_specs = [row(D_MODEL), sample_out]
    out_shape = [jax.ShapeDtypeStruct((m, D_MODEL), F32), jax.ShapeDtypeStruct((ms, D_MODEL), F32)]
    if next_norm:
        in_specs.append(fixed(1, D_MODEL))
        args.append(g_next)
        out_specs += [row(D_MODEL), sample_out]
        out_shape += [jax.ShapeDtypeStruct((m, D_MODEL), BF16), jax.ShapeDtypeStruct((ms, D_MODEL), BF16)]
    return pl.pallas_call(
        functools.partial(_outproj_kernel, two=two, next_norm=next_norm),
        grid=(m // tm,),
        in_specs=in_specs,
        out_specs=out_specs,
        out_shape=out_shape,
        compiler_params=pltpu.CompilerParams(dimension_semantics=("arbitrary",),
                                             vmem_limit_bytes=BIG_VMEM_LIMIT),
        name="outproj",
    )(*args)


def _mlstm_prompt_kernel(bg_ref, q_ref, k_ref, v_ref, o_ref, z_ref, gc_ref, gr_ref, gh_ref,
                         y_ref, c_out, n_out, m_out, c_s, n_s, m_s):
    c = pl.program_id(1)
    t = CHUNK

    @pl.when(c == 0)
    def _():
        c_s[...] = jnp.zeros_like(c_s)
        n_s[...] = jnp.zeros_like(n_s)
        m_s[...] = jnp.zeros_like(m_s)

    row = lax.broadcasted_iota(jnp.int32, (t, t), 0)
    col = lax.broadcasted_iota(jnp.int32, (t, t), 1)
    tril = col <= row
    triu = row <= col
    gc = gc_ref[...]
    gr = gr_ref[...]
    for h in range(H_A):
        b_i = bg_ref[h]
        b_f = bg_ref[H_A + h]
        i_col = gc[:, h:h + 1] + b_i
        i_row = gr[h:h + 1, :] + b_i
        lf_col = _log_sigmoid(gc[:, H_A + h:H_A + h + 1] + b_f)
        lf_row = _log_sigmoid(gr[H_A + h:H_A + h + 1, :] + b_f)
        b_col = jnp.sum(jnp.where(tril, lf_row, 0.0), axis=1, keepdims=True)
        b_row = jnp.sum(jnp.where(triu, lf_col, 0.0), axis=0, keepdims=True)
        m_prev = m_s[h][:, 0:1]
        d = jnp.where(tril, b_col - b_row + i_row, -jnp.inf)
        inter = b_col + m_prev
        m_t = jnp.maximum(inter, jnp.max(d, axis=1, keepdims=True))
        w_intra = jnp.exp(d - m_t)
        w_inter = jnp.exp(inter - m_t) * SCALE_A
        q = q_ref[:, h * DK_A:(h + 1) * DK_A]
        k = k_ref[:, h * DK_A:(h + 1) * DK_A]
        v = v_ref[:, h * DV_A:(h + 1) * DV_A]
        s = _dot_nt(q, k) * (w_intra * SCALE_A)
        c_old = c_s[h]
        n_old = n_s[h]
        num = _dot(s.astype(BF16), v) + w_inter * _dot(q, c_old.astype(BF16))
        qn = jnp.sum(q.astype(F32) * n_old, axis=1, keepdims=True)
        den = jnp.sum(s, axis=1, keepdims=True) + w_inter * qn
        hh = num / jnp.maximum(jnp.abs(den), jnp.exp(-m_t))
        hn = _rms(hh, gh_ref[h:h + 1, :])
        o = o_ref[:, h * DV_A:(h + 1) * DV_A].astype(F32)
        z = z_ref[:, h * DV_A:(h + 1) * DV_A].astype(F32)
        y_ref[:, h * DV_A:(h + 1) * DV_A] = (hn * jax.nn.sigmoid(o) * _silu(z)).astype(BF16)
        b_last = b_col[t - 1:t, :]
        g_col = b_last - b_col + i_col
        m_new = jnp.maximum(b_last + m_prev, jnp.max(g_col, axis=0, keepdims=True))
        e_col = jnp.exp(g_col - m_new)
        decay = jnp.exp(b_last + m_prev - m_new)
        ke = k.astype(F32) * e_col
        c_s[h] = decay * c_old + _dot_tn(ke.astype(BF16), v)
        n_s[h] = decay * n_old + jnp.sum(ke, axis=0, keepdims=True)
        m_s[h] = jnp.broadcast_to(m_new, (1, LANES))

    @pl.when(c == pl.num_programs(1) - 1)
    def _():
        c_out[...] = c_s[...]
        n_out[...] = n_s[...]
        m_out[...] = m_s[...]


def _mlstm_prompt(bg, proj, gates_col, gates_row, g_head):
    nc = SEQ // CHUNK
    t = CHUNK
    qk_w = H_A * DK_A
    return pl.pallas_call(
        _mlstm_prompt_kernel,
        grid=(BATCH, nc),
        in_specs=[pl.BlockSpec(memory_space=pltpu.SMEM),
                  pl.BlockSpec((None, t, qk_w), lambda b, c: (b, c, 0)),
                  pl.BlockSpec((None, t, qk_w), lambda b, c: (b, c, 1)),
                  pl.BlockSpec((None, t, W_A), lambda b, c: (b, c, 1)),
                  pl.BlockSpec((None, t, W_A), lambda b, c: (b, c, 2)),
                  pl.BlockSpec((None, t, W_A), lambda b, c: (b, c, 3)),
                  pl.BlockSpec((None, t, LANES), lambda b, c: (b, c, 0)),
                  pl.BlockSpec((None, 2 * H_A, t), lambda b, c: (b, 0, c)),
                  pl.BlockSpec((H_A, DV_A), lambda b, c: (0, 0))],
        out_specs=[pl.BlockSpec((None, t, W_A), lambda b, c: (b, c, 0)),
                   pl.BlockSpec((None, H_A, DK_A, DV_A), lambda b, c: (b, 0, 0, 0)),
                   pl.BlockSpec((None, H_A, 1, DK_A), lambda b, c: (b, 0, 0, 0)),
                   pl.BlockSpec((None, H_A, 1, LANES), lambda b, c: (b, 0, 0, 0))],
        out_shape=[jax.ShapeDtypeStruct((BATCH, SEQ, W_A), BF16),
                   jax.ShapeDtypeStruct((BATCH, H_A, DK_A, DV_A), F32),
                   jax.ShapeDtypeStruct((BATCH, H_A, 1, DK_A), F32),
                   jax.ShapeDtypeStruct((BATCH, H_A, 1, LANES), F32)],
        scratch_shapes=[pltpu.VMEM((H_A, DK_A, DV_A), F32),
                        pltpu.VMEM((H_A, 1, DK_A), F32),
                        pltpu.VMEM((H_A, 1, LANES), F32)],
        compiler_params=_cparams("parallel", "arbitrary"),
        name="mlstm_prompt",
    )(bg, proj, proj, proj, proj, proj, gates_col, gates_row, g_head)


def _mlstm_sample_kernel(gi_ref, gf_ref, bi_ref, bf_ref, m0_ref, q_ref, k_ref, v_ref, o_ref, z_ref,
                         qt_ref, kt_ref, n0_ref, gh_ref, c0_ref,
                         y_ref, c1_ref, n1_ref, m1_ref):
    r0 = pl.program_id(0) * SAMPLE_TB
    shift = lax.rem(DEC_BATCH - r0, DEC_BATCH)
    q_cols = [pltpu.roll(qt_ref[h], shift, axis=1) for h in range(H_A)]
    k_cols = [pltpu.roll(kt_ref[h], shift, axis=1) for h in range(H_A)]
    for j in range(SAMPLE_TB):
        rows = pl.ds(r0 + j, 1)
        i_v = gi_ref[rows, :] + bi_ref[...]
        lf_v = _log_sigmoid(gf_ref[rows, :] + bf_ref[...])
        m0_v = m0_ref[rows, :]
        m_t = jnp.maximum(lf_v + m0_v, i_v)
        w_in = jnp.exp(i_v - m_t)
        w_st = jnp.exp(lf_v + m0_v - m_t)
        floor = jnp.exp(-m_t)
        m1_ref[rows, :] = m_t
        for h in range(H_A):
            ks = slice(h * DK_A, (h + 1) * DK_A)
            vs = slice(h * DV_A, (h + 1) * DV_A)
            wi = w_in[:, h:h + 1]
            ws = w_st[:, h:h + 1]
            q_col = q_cols[h][:, j:j + 1]
            k_col = k_cols[h][:, j:j + 1]
            q_row = q_ref[rows, ks]
            k_row = k_ref[rows, ks]
            v_row = v_ref[rows, vs]
            n_row = n0_ref[rows, ks]
            c_old = c0_ref[j, h]
            qk = jnp.sum(q_row * k_row, axis=1, keepdims=True) * SCALE_A
            s = qk * wi
            q_c = jnp.sum(c_old * q_col, axis=0, keepdims=True) * SCALE_A
            qn = jnp.sum(q_row * n_row, axis=1, keepdims=True) * SCALE_A
            num = s * v_row + ws * q_c
            den = s + ws * qn
            hh = num / jnp.maximum(jnp.abs(den), floor[:, h:h + 1])
            hn = _rms(hh, gh_ref[h:h + 1, :])
            y_ref[rows, vs] = hn * jax.nn.sigmoid(o_ref[rows, vs]) * _silu(z_ref[rows, vs])
            c1_ref[j, h] = ws * c_old + (wi * k_col) * v_row
            n1_ref[rows, ks] = ws * n_row + wi * k_row


def _mlstm_sample(gi, gf, bi, bf, m0, proj, qt, kt, n0, g_head, c0):
    qk_w = H_A * DK_A
    once = dict(pipeline_mode=pl.Buffered(1))
    full = lambda shape: pl.BlockSpec(shape, lambda r: (0,) * len(shape))
    cols = lambda width, j: pl.BlockSpec((DEC_BATCH, width), lambda r: (0, j), **once)
    state = pl.BlockSpec((SAMPLE_TB, H_A, DK_A, DV_A), lambda r: (r, 0, 0, 0))
    return pl.pallas_call(
        _mlstm_sample_kernel,
        grid=(DEC_BATCH // SAMPLE_TB,),
        in_specs=[full((DEC_BATCH, H_A)), full((DEC_BATCH, H_A)), full((1, H_A)), full((1, H_A)),
                  full((DEC_BATCH, H_A)),
                  cols(qk_w, 0), cols(qk_w, 1), cols(W_A, 1), cols(W_A, 2), cols(W_A, 3),
                  pl.BlockSpec((H_A, DK_A, DEC_BATCH), lambda r: (0, 0, 0), **once),
                  pl.BlockSpec((H_A, DK_A, DEC_BATCH), lambda r: (0, 0, 0), **once),
                  cols(qk_w, 0), full((H_A, DV_A)), state],
        out_specs=[full((DEC_BATCH, W_A)), state,
                   full((DEC_BATCH, qk_w)), full((DEC_BATCH, H_A))],
        out_shape=[jax.ShapeDtypeStruct((DEC_BATCH, W_A), F32),
                   jax.ShapeDtypeStruct((DEC_BATCH, H_A, DK_A, DV_A), F32),
                   jax.ShapeDtypeStruct((DEC_BATCH, qk_w), F32),
                   jax.ShapeDtypeStruct((DEC_BATCH, H_A), F32)],
        compiler_params=_cparams("arbitrary"),
        name="mlstm_sample",
    )(gi, gf, bi, bf, m0, proj, proj, proj, proj, proj, qt, kt, n0, g_head, c0)


def _s5_build_kernel(ldt_ref, lamr_ref, lami_ref, b1_ref, cc_ref, dsk_ref,
                     m_ref, w_ref, v_ref, p1_ref, p2_ref):
    dt = jnp.exp(ldt_ref[...])
    lam_re, lam_im = lamr_ref[...], lami_ref[...]
    mag = jnp.exp(lam_re * dt)
    ang = lam_im * dt
    a_re, a_im = mag * jnp.cos(ang), mag * jnp.sin(ang)
    den = lam_re * lam_re + lam_im * lam_im
    f_re = ((a_re - 1.0) * lam_re + a_im * lam_im) / den
    f_im = (a_im * lam_re - (a_re - 1.0) * lam_im) / den
    pows = [(jnp.ones_like(a_re), jnp.zeros_like(a_im))]
    for _ in range(S5_T):
        r_re, r_im = pows[-1]
        pows.append((r_re * a_re - r_im * a_im, r_re * a_im + r_im * a_re))

    def tall(vals):
        return jnp.concatenate([jnp.broadcast_to(x, (GROUP_B, LANES)) for x in vals], axis=0)

    af_re = tall([pows[S5_T - 1 - s][0] * f_re - pows[S5_T - 1 - s][1] * f_im for s in range(S5_T)])
    af_im = tall([pows[S5_T - 1 - s][0] * f_im + pows[S5_T - 1 - s][1] * f_re for s in range(S5_T)])
    lane = lax.broadcasted_iota(jnp.int32, (1, LANES), 1)
    sgn_lane = jnp.where(lane < P_B, -1.0, 1.0)
    b1 = b1_ref[...]
    b2 = sgn_lane * pltpu.roll(b1, P_B, axis=1)
    w_ref[...] = (af_re * b1 + af_im * b2).astype(BF16)

    eye = (lax.broadcasted_iota(jnp.int32, (LANES, LANES), 0)
           == lax.broadcasted_iota(jnp.int32, (LANES, LANES), 1))

    def column(row):
        return jnp.sum(jnp.where(eye, row, 0.0), axis=1, keepdims=True)

    ac_re, ac_im, fc_re, fc_im = column(a_re), column(a_im), column(f_re), column(f_im)
    kid = lax.broadcasted_iota(jnp.int32, (S5_T, LANES), 0)
    expand = (lax.broadcasted_iota(jnp.int32, (S5_T, S5_W), 1) // GROUP_B
              == lax.broadcasted_iota(jnp.int32, (S5_T, S5_W), 0)).astype(BF16)

    def lane_blocks(part):
        stacked = jnp.zeros((S5_T, LANES), F32)
        for k in range(S5_T):
            stacked = jnp.where(kid == k, pows[k][part], stacked)
        hi = stacked.astype(BF16)
        rest = stacked - hi.astype(F32)
        mid = rest.astype(BF16)
        lo = (rest - mid.astype(F32)).astype(BF16)
        return _dot_tn(hi, expand) + _dot_tn(mid, expand) + _dot_tn(lo, expand)

    q_re, q_im = lane_blocks(0), lane_blocks(1)
    cc = cc_ref[...]
    cs = pltpu.roll(cc, P_B, axis=0)
    rowi = lax.broadcasted_iota(jnp.int32, (LANES, 1), 0)
    sgn_row = jnp.where(rowi < P_B, 1.0, -1.0)

    def readout(r_re, r_im):
        return sgn_row * (r_re * cc) - r_im * cs

    v_ref[...] = readout(q_re * ac_re - q_im * ac_im, q_re * ac_im + q_im * ac_re).astype(BF16)
    vf = readout(q_re * fc_re - q_im * fc_im, q_re * fc_im + q_im * fc_re)
    kw = _dot3(b1[0:GROUP_B, :], vf)
    lane_w = lax.broadcasted_iota(jnp.int32, (GROUP_B, S5_W), 1)
    blocks = [kw]
    for s in range(1, S5_T):
        blocks.append(jnp.where(lane_w >= s * GROUP_B, pltpu.roll(kw, s * GROUP_B, axis=1), 0.0))
    toep = jnp.concatenate(blocks, axis=0)
    ri = lax.broadcasted_iota(jnp.int32, (S5_W, S5_W), 0)
    ci = lax.broadcasted_iota(jnp.int32, (S5_W, S5_W), 1)
    m_ref[...] = (toep + jnp.where(ri == ci, dsk_ref[...], 0.0)).astype(BF16)

    r_re, r_im = pows[S5_T]
    rid = lax.broadcasted_iota(jnp.int32, (8, LANES), 0)
    p1 = jnp.where(rid == S5_A1_ROW, a_re, 0.0)
    p2 = jnp.where(rid == S5_A1_ROW, sgn_lane * a_im, 0.0)
    for kk in range(S5_LEVELS):
        p1 = jnp.where(rid == kk, r_re, p1)
        p2 = jnp.where(rid == kk, sgn_lane * r_im, p2)
        r_re, r_im = r_re * r_re - r_im * r_im, 2.0 * (r_re * r_im)
    p1_ref[...] = p1
    p2_ref[...] = p2


def _s5_build(ldt, lamr, lami, b1, cc, dsk):
    g3 = lambda a, b: pl.BlockSpec((None, a, b), lambda g: (g, 0, 0))
    return pl.pallas_call(
        _s5_build_kernel,
        grid=(G_B,),
        in_specs=[g3(1, 1), g3(1, LANES), g3(1, LANES),
                  g3(S5_W, LANES), g3(LANES, S5_W), g3(1, S5_W)],
        out_specs=[g3(S5_W, S5_W), g3(S5_W, LANES), g3(LANES, S5_W), g3(8, LANES), g3(8, LANES)],
        out_shape=[jax.ShapeDtypeStruct((G_B, S5_W, S5_W), BF16),
                   jax.ShapeDtypeStruct((G_B, S5_W, LANES), BF16),
                   jax.ShapeDtypeStruct((G_B, LANES, S5_W), BF16),
                   jax.ShapeDtypeStruct((G_B, 8, LANES), F32),
                   jax.ShapeDtypeStruct((G_B, 8, LANES), F32)],
        compiler_params=_cparams("parallel"),
        name="s5_build",
    )(ldt, lamr, lami, b1, cc, dsk)


def _s5_prompt_kernel(u_ref, m_ref, w_ref, v_ref, p1_ref, p2_ref, y_ref, xf_ref, y_s):
    nblk = SEQ // S5_T
    rows = BATCH * nblk
    bidx = lax.broadcasted_iota(jnp.int32, (rows, LANES), 0) & (nblk - 1)
    steps = [u_ref[pl.ds(s, rows, stride=S5_T), :] for s in range(S5_T)]
    for g in range(OCT):
        gl = slice(g * GROUP_B, (g + 1) * GROUP_B)
        u = jnp.concatenate([x[:, gl] for x in steps], axis=-1).astype(BF16)
        s = _dot(u, w_ref[g])
        for kk in range(S5_LEVELS):
            sh = 1 << kk
            r = jnp.where(bidx >= sh, pltpu.roll(s, sh, axis=0), 0.0)
            s = s + p1_ref[g, kk:kk + 1, :] * r + p2_ref[g, kk:kk + 1, :] * pltpu.roll(r, P_B, axis=1)
        x_prev = jnp.where(bidx >= 1, pltpu.roll(s, 1, axis=0), 0.0)
        y = _dot(u, m_ref[g]) + _dot(x_prev.astype(BF16), v_ref[g])
        y_s[g] = _gelu_tanh(y)
        for b in range(BATCH):
            xf_ref[g, b:b + 1, :] = s[(b + 1) * nblk - 1:(b + 1) * nblk, :]
    for t in range(S5_T):
        half = slice((t // OCT) * LANES, (t // OCT + 1) * LANES)
        tl = slice((t % OCT) * GROUP_B, (t % OCT + 1) * GROUP_B)
        y_ref[pl.ds(t, rows, stride=S5_T), :] = jnp.concatenate(
            [y_s[g, :, half][:, tl] for g in range(OCT)], axis=-1)


def _s5_prompt(u, m, w, v, p1, p2):
    mp = BATCH * SEQ
    rows = BATCH * (SEQ // S5_T)
    o3 = lambda a, b: pl.BlockSpec((OCT, a, b), lambda g: (g, 0, 0))
    return pl.pallas_call(
        _s5_prompt_kernel,
        grid=(G_B // OCT,),
        in_specs=[pl.BlockSpec((mp, LANES), lambda g: (0, g)),
                  o3(S5_W, S5_W), o3(S5_W, LANES), o3(LANES, S5_W), o3(8, LANES), o3(8, LANES)],
        out_specs=[pl.BlockSpec((mp, LANES), lambda g: (0, g)), o3(BATCH, LANES)],
        out_shape=[jax.ShapeDtypeStruct((mp, W_B), F32),
                   jax.ShapeDtypeStruct((G_B, BATCH, LANES), F32)],
        scratch_shapes=[pltpu.VMEM((OCT, rows, S5_W), F32)],
        compiler_params=_cparams("parallel"),
        name="s5_prompt",
    )(u, m, w, v, p1, p2)


def _s5_sample_kernel(u_ref, xr_ref, xi_ref, m_ref, w_ref, v_ref, p1_ref, p2_ref,
                      y_ref, x1r_ref, x1i_ref):
    lane = lax.broadcasted_iota(jnp.int32, (DEC_BATCH, LANES), 1)
    last = LANES - GROUP_B
    half = S5_W // 2
    a1 = slice(S5_A1_ROW, S5_A1_ROW + 1)
    u_all = u_ref[...]
    ys = []
    for g in range(OCT):
        ps = slice(g * P_B, (g + 1) * P_B)
        u = jnp.where(lane >= last, pltpu.roll(u_all, (last - g * GROUP_B) % LANES, axis=1), 0.0)
        u = u.astype(BF16)
        x0 = jnp.concatenate([xr_ref[:, ps], xi_ref[:, ps]], axis=-1)
        x1 = (p1_ref[g, a1, :] * x0 + p2_ref[g, a1, :] * pltpu.roll(x0, P_B, axis=1)
              + _dot(u, w_ref[g, half:, :]))
        x1r_ref[:, ps] = x1[:, :P_B]
        x1i_ref[:, ps] = x1[:, P_B:]
        y = (_dot(x0.astype(BF16), v_ref[g])[:, 0:GROUP_B]
             + _dot(u, m_ref[g, half:, :])[:, S5_W - GROUP_B:])
        ys.append(_gelu_tanh(y))
    y_ref[...] = jnp.concatenate(ys, axis=-1)


def _s5_sample(u, xr, xi, m, w, v, p1, p2):
    o3 = lambda a, b: pl.BlockSpec((OCT, a, b), lambda g: (g, 0, 0))
    tile = pl.BlockSpec((DEC_BATCH, LANES), lambda g: (0, g))
    st = pl.BlockSpec((DEC_BATCH, OCT * P_B), lambda g: (0, g))
    st_shape = jax.ShapeDtypeStruct((DEC_BATCH, G_B * P_B), F32)
    return pl.pallas_call(
        _s5_sample_kernel,
        grid=(G_B // OCT,),
        in_specs=[tile, st, st, o3(S5_W, S5_W), o3(S5_W, LANES), o3(LANES, S5_W), o3(8, LANES), o3(8, LANES)],
        out_specs=[tile, st, st],
        out_shape=[jax.ShapeDtypeStruct((DEC_BATCH, W_B), F32), st_shape, st_shape],
        compiler_params=_cparams("parallel"),
        name="s5_sample",
    )(u, xr, xi, m, w, v, p1, p2)


def _glu_kernel(y_ref, w_ref, b_ref, z_ref, o_ref):
    y = y_ref[...]
    gate = jax.nn.sigmoid(_dot(y.astype(BF16), w_ref[...]) + b_ref[...])
    o_ref[...] = (y.astype(F32) * gate * _silu(z_ref[...].astype(F32))).astype(o_ref.dtype)


def _glu(y, w, b, proj_b, tm):
    m = y.shape[0]
    return pl.pallas_call(
        _glu_kernel,
        grid=(m // tm,),
        in_specs=[pl.BlockSpec((tm, W_B), lambda i: (i, 0)),
                  pl.BlockSpec((W_B, W_B), lambda i: (0, 0)),
                  pl.BlockSpec((1, W_B), lambda i: (0, 0)),
                  pl.BlockSpec((tm, W_B), lambda i: (i, 1))],
        out_specs=pl.BlockSpec((tm, W_B), lambda i: (i, 0)),
        out_shape=jax.ShapeDtypeStruct((m, W_B), BF16),
        compiler_params=_cparams("parallel"),
        name="glu",
    )(y, w, b, proj_b)


def _rope_table_kernel(cos_ref, sin_ref, *, pos0):
    shape = cos_ref.shape
    pos = lax.broadcasted_iota(jnp.int32, shape, 0).astype(F32) + pos0
    j = lax.broadcasted_iota(jnp.int32, shape, 1).astype(F32)
    ang = pos * jnp.power(ROPE_BASE, -(j / (DK_C // 2)))
    cos_ref[...] = jnp.cos(ang)
    sin_ref[...] = jnp.sin(ang)


def _rope_table(rows, pos0):
    shape = jax.ShapeDtypeStruct((rows, DK_C // 2), F32)
    return pl.pallas_call(functools.partial(_rope_table_kernel, pos0=float(pos0)),
                          out_shape=[shape, shape], name="rope_table")()


def _rope(x, cos, sin):
    half = DK_C // 2
    x1, x2 = x[:, :half], x[:, half:]
    return jnp.concatenate([x1 * cos - x2 * sin, x1 * sin + x2 * cos], axis=-1)


def _ret_prompt_kernel(q_ref, k_ref, v_ref, z_ref, cos_ref, sin_ref, gh_ref, y_ref, s_out, s_s):
    c = pl.program_id(1)
    t = CHUNK

    @pl.when(c == 0)
    def _():
        s_s[...] = jnp.zeros_like(s_s)

    row = lax.broadcasted_iota(jnp.int32, (t, t), 0)
    col = lax.broadcasted_iota(jnp.int32, (t, t), 1)
    tril = col <= row
    diff = (row - col).astype(F32)
    tpos = lax.broadcasted_iota(jnp.int32, (t, 1), 0).astype(F32)
    cos, sin = cos_ref[...], sin_ref[...]
    for h in range(H_C):
        lg = LOG_GAMMA[h]
        ks = slice(h * DK_C, (h + 1) * DK_C)
        vs = slice(h * DV_C, (h + 1) * DV_C)
        q = _rope(q_ref[:, ks].astype(F32), cos, sin).astype(BF16)
        k32 = _rope(k_ref[:, ks].astype(F32), cos, sin)
        v = v_ref[:, vs]
        mask = jnp.where(tril, jnp.exp(diff * lg), 0.0) * SCALE_C
        s = _dot_nt(q, k32.astype(BF16)) * mask
        s_old = s_s[h]
        o = _dot(s.astype(BF16), v) + _dot(q, s_old.astype(BF16)) * jnp.exp((tpos + 1.0) * lg)
        y_ref[:, vs] = (_rms(o, gh_ref[h:h + 1, :]) * _silu(z_ref[:, vs].astype(F32))).astype(BF16)
        k_tail = k32 * (jnp.exp((t - 1.0 - tpos) * lg) * SCALE_C)
        s_s[h] = math.exp(t * lg) * s_old + _dot_tn(k_tail.astype(BF16), v)

    @pl.when(c == pl.num_programs(1) - 1)
    def _():
        s_out[...] = s_s[...]


def _ret_prompt(proj, cos, sin, g_head):
    nc = SEQ // CHUNK
    t = CHUNK
    qk_w = H_C * DK_C
    return pl.pallas_call(
        _ret_prompt_kernel,
        grid=(BATCH, nc),
        in_specs=[pl.BlockSpec((None, t, qk_w), lambda b, c: (b, c, 0)),
                  pl.BlockSpec((None, t, qk_w), lambda b, c: (b, c, 1)),
                  pl.BlockSpec((None, t, W_C), lambda b, c: (b, c, 1)),
                  pl.BlockSpec((None, t, W_C), lambda b, c: (b, c, 2)),
                  pl.BlockSpec((t, DK_C // 2), lambda b, c: (c, 0)),
                  pl.BlockSpec((t, DK_C // 2), lambda b, c: (c, 0)),
                  pl.BlockSpec((H_C, DV_C), lambda b, c: (0, 0))],
        out_specs=[pl.BlockSpec((None, t, W_C), lambda b, c: (b, c, 0)),
                   pl.BlockSpec((None, H_C, DK_C, DV_C), lambda b, c: (b, 0, 0, 0))],
        out_shape=[jax.ShapeDtypeStruct((BATCH, SEQ, W_C), BF16),
                   jax.ShapeDtypeStruct((BATCH, H_C, DK_C, DV_C), F32)],
        scratch_shapes=[pltpu.VMEM((H_C, DK_C, DV_C), F32)],
        compiler_params=_cparams("parallel", "arbitrary"),
        name="ret_prompt",
    )(proj, proj, proj, proj, cos, sin, g_head)


def _rope_sample_kernel(q_ref, k_ref, cos_ref, sin_ref, qo_ref, ko_ref):
    cos, sin = cos_ref[0:1, :], sin_ref[0:1, :]
    for h in range(H_C):
        ks = slice(h * DK_C, (h + 1) * DK_C)
        qo_ref[:, ks] = _rope(q_ref[:, ks], cos, sin)
        ko_ref[:, ks] = _rope(k_ref[:, ks], cos, sin)


def _rope_sample(proj, cos, sin):
    qk_w = H_C * DK_C
    shape = jax.ShapeDtypeStruct((DEC_BATCH, qk_w), F32)
    return pl.pallas_call(
        _rope_sample_kernel,
        grid=(1,),
        in_specs=[pl.BlockSpec((DEC_BATCH, qk_w), lambda i: (0, 0)),
                  pl.BlockSpec((DEC_BATCH, qk_w), lambda i: (0, 1)),
                  pl.BlockSpec((8, DK_C // 2), lambda i: (0, 0)),
                  pl.BlockSpec((8, DK_C // 2), lambda i: (0, 0))],
        out_specs=[pl.BlockSpec((DEC_BATCH, qk_w), lambda i: (0, 0)),
                   pl.BlockSpec((DEC_BATCH, qk_w), lambda i: (0, 0))],
        out_shape=[shape, shape],
        compiler_params=_cparams("arbitrary"),
        name="rope_sample",
    )(proj, proj, cos, sin)


def _ret_sample_kernel(q_ref, k_ref, v_ref, z_ref, qt_ref, kt_ref, gh_ref, s0_ref, y_ref, s1_ref):
    r0 = pl.program_id(0) * SAMPLE_TB
    shift = lax.rem(DEC_BATCH - r0, DEC_BATCH)
    for h in range(H_C):
        gamma = math.exp(LOG_GAMMA[h])
        ks = slice(h * DK_C, (h + 1) * DK_C)
        vs = slice(h * DV_C, (h + 1) * DV_C)
        q_cols = pltpu.roll(qt_ref[h], shift, axis=1)
        k_cols = pltpu.roll(kt_ref[h], shift, axis=1)
        for j in range(SAMPLE_TB):
            rows = pl.ds(r0 + j, 1)
            v_row = v_ref[rows, vs]
            s_old = s0_ref[j, h]
            qk = jnp.sum(q_ref[rows, ks] * k_ref[rows, ks], axis=1, keepdims=True) * SCALE_C
            o = qk * v_row + jnp.sum(s_old * q_cols[:, j:j + 1], axis=0, keepdims=True) * gamma
            y_ref[rows, vs] = _rms(o, gh_ref[h:h + 1, :]) * _silu(z_ref[rows, vs])
            s1_ref[j, h] = gamma * s_old + (k_cols[:, j:j + 1] * SCALE_C) * v_row


def _ret_sample(q, k, proj, qt, kt, g_head, s0):
    qk_w = H_C * DK_C
    once = dict(pipeline_mode=pl.Buffered(1))
    fixed = lambda shape, *idx: pl.BlockSpec(shape, lambda r: idx or (0,) * len(shape), **once)
    state = pl.BlockSpec((SAMPLE_TB, H_C, DK_C, DV_C), lambda r: (r, 0, 0, 0))
    return pl.pallas_call(
        _ret_sample_kernel,
        grid=(DEC_BATCH // SAMPLE_TB,),
        in_specs=[fixed((DEC_BATCH, qk_w)), fixed((DEC_BATCH, qk_w)),
                  fixed((DEC_BATCH, W_C), 0, 1), fixed((DEC_BATCH, W_C), 0, 2),
                  fixed((H_C, DK_C, DEC_BATCH)), fixed((H_C, DK_C, DEC_BATCH)),
                  fixed((H_C, DV_C)), state],
        out_specs=[pl.BlockSpec((DEC_BATCH, W_C), lambda r: (0, 0)), state],
        out_shape=[jax.ShapeDtypeStruct((DEC_BATCH, W_C), F32),
                   jax.ShapeDtypeStruct((DEC_BATCH, H_C, DK_C, DV_C), F32)],
        compiler_params=pltpu.CompilerParams(dimension_semantics=("arbitrary",),
                                             vmem_limit_bytes=BIG_VMEM_LIMIT),
        name="ret_sample",
    )(q, k, proj, proj, qt, kt, g_head, s0)


def kernel(x_prompt, x_sample, state_mlstm_C, state_mlstm_n, state_mlstm_m, state_s5_re, state_s5_im, state_ret, g_pre, g_post, w_in0, b_gates0, g_head_a, lam_re, lam_im, log_dt, b_re, b_im, c_re, c_im, d_skip, w_glu, b_glu, w_out0, w_in1, g_head_c, w_out1):
    mp = BATCH * SEQ
    xp = x_prompt.reshape(mp, D_MODEL).astype(F32)
    xs = x_sample.reshape(DEC_BATCH, D_MODEL).astype(F32)

    w0t = w_in0.T.astype(F32)
    w0g = jnp.pad(w0t[QKV0:GATE0], ((0, LANES - 2 * H_A), (0, 0)))
    w0b = w0t[GATE0:]
    w1 = w_in1.astype(F32)
    wo0 = w_out0.astype(BF16)
    wo1 = w_out1.astype(BF16)
    wg = w_glu.astype(BF16)
    bglu = b_glu.reshape(1, W_B).astype(F32)
    g_pre = g_pre.astype(F32)
    g_post = g_post.astype(F32)
    bg = b_gates0.astype(F32)
    gh_a = g_head_a.astype(F32)
    gh_c = g_head_c.astype(F32)

    dup = lambda a: jnp.concatenate([a, a], axis=-1).astype(F32)
    lamr, lami = dup(lam_re), dup(lam_im)
    b1 = jnp.concatenate([b_re.transpose(0, 2, 1), b_im.transpose(0, 2, 1)], axis=-1)
    cc = jnp.concatenate([c_re.transpose(0, 2, 1), c_im.transpose(0, 2, 1)], axis=1)
    s5m, s5w, s5v, s5p1, s5p2 = _s5_build(
        log_dt.reshape(G_B, 1, 1).astype(F32), lamr[:, None, :], lami[:, None, :],
        jnp.tile(b1.astype(F32), (1, S5_T, 1)), jnp.tile(cc.astype(F32), (1, 1, S5_T)),
        jnp.tile(d_skip.astype(F32), (1, S5_T))[:, None, :])

    a0, gates = _norm_gates(xp, g_pre[0:1], w0g, 256)
    a0s, gates_s = _norm_gates(xs, g_pre[0:1], w0g, DEC_BATCH)
    pa, pas = _proj(a0, a0s, w0t, QKV0, BF16, 1024, 1024, True)
    pb, pbs = _proj(a0, a0s, w0b, 2 * W_B, F32, 1024, 1024, True)
    gates3 = gates.reshape(BATCH, SEQ, LANES)
    gates_row = gates3[:, :, :2 * H_A].transpose(0, 2, 1)
    ya, c_p, n_p, m_p = _mlstm_prompt(bg, pa.reshape(BATCH, SEQ, QKV0), gates3, gates_row, gh_a)
    yb, xf = _s5_prompt(pb, s5m, s5w, s5v, s5p1, s5p2)
    yb = _glu(yb, wg, bglu, pb, 512)
    s5r_p = xf[:, :, :P_B].transpose(1, 0, 2)
    s5i_p = xf[:, :, P_B:].transpose(1, 0, 2)
    qk_a = H_A * DK_A
    to_cols = lambda a, nh, dk: a.reshape(DEC_BATCH, nh, dk).transpose(1, 2, 0)
    yas, c_s, n_s, m_s = _mlstm_sample(
        gates_s[:, :H_A], gates_s[:, H_A:2 * H_A], bg[None, :H_A], bg[None, H_A:],
        state_mlstm_m.astype(F32), pas,
        to_cols(pas[:, :qk_a], H_A, DK_A), to_cols(pas[:, qk_a:2 * qk_a], H_A, DK_A),
        state_mlstm_n.reshape(DEC_BATCH, qk_a).astype(F32), gh_a, state_mlstm_C.astype(F32))
    ybs, s5r_s, s5i_s = _s5_sample(
        pbs, state_s5_re.reshape(DEC_BATCH, G_B * P_B).astype(F32),
        state_s5_im.reshape(DEC_BATCH, G_B * P_B).astype(F32), s5m, s5w, s5v, s5p1, s5p2)
    s5r_s = s5r_s.reshape(DEC_BATCH, G_B, P_B)
    s5i_s = s5i_s.reshape(DEC_BATCH, G_B, P_B)
    ybs = _glu(ybs, wg, bglu, pbs, DEC_BATCH)
    h1, h1s, a1, a1s = _outproj(ya.reshape(mp, W_A), yb, yas, ybs, wo0, xp, xs, g_post[0:1],
                                g_pre[1:2], 512)

    p1, p1s = _proj(a1, a1s, w1, 2 * H_C * DK_C + 2 * W_C, BF16, 1024, 1024, False)
    cos_p, sin_p = _rope_table(SEQ, 0)
    yc, s_p = _ret_prompt(p1.reshape(BATCH, SEQ, -1), cos_p, sin_p, gh_c)
    cos_s, sin_s = _rope_table(8, PAST_LEN)
    qs, ks = _rope_sample(p1s, cos_s, sin_s)
    ycs, s_s = _ret_sample(qs, ks, p1s, to_cols(qs, H_C, DK_C), to_cols(ks, H_C, DK_C),
                           gh_c, state_ret.astype(F32))
    y_p, y_s = _outproj(yc.reshape(mp, W_C), None, ycs, None, wo1, h1, h1s, g_post[1:2], None, 512)

    return (y_p.reshape(BATCH, SEQ, D_MODEL), y_s.reshape(DEC_BATCH, 1, D_MODEL),
            c_p, n_p.reshape(BATCH, H_A, DK_A), m_p[:, :, 0, 0],
            s5r_p, s5i_p, s_p,
            c_s, n_s.reshape(DEC_BATCH, H_A, DK_A), m_s,
            s5r_s, s5i_s, s_s)
```

```python
import functools
import math

import jax
import jax.numpy as jnp
from jax import lax
from jax.experimental import pallas as pl
from jax.experimental.pallas import tpu as pltpu

F32 = jnp.float32
BF16 = jnp.bfloat16

D_MODEL = 2048
BATCH = 4
SEQ = 2048
DEC_BATCH = 128
PAST_LEN = 16384
H_A = 4
DK_A = 256
DV_A = 512
W_A = H_A * DV_A
GROUP_B = 16
G_B = 64
P_B = 64
W_B = G_B * GROUP_B
H_C = 8
DK_C = 256
DV_C = 512
W_C = H_C * DV_C
CHUNK = 256
NORM_EPS = 1e-6
ROPE_BASE = 10000.0
QKV0 = 2 * H_A * DK_A + 3 * W_A
GATE0 = QKV0 + 2 * H_A
SCALE_A = DK_A ** -0.5
SCALE_C = DK_C ** -0.5
LOG_GAMMA = tuple(math.log1p(-(2.0 ** (-5.0 - h))) for h in range(H_C))
S5_T = 16
S5_W = S5_T * GROUP_B
LANES = 128
OCT = LANES // GROUP_B
S5_LEVELS = int(math.log2(SEQ // S5_T))
S5_A1_ROW = S5_LEVELS
VMEM_LIMIT = 48 * 1024 * 1024
BIG_VMEM_LIMIT = 56 * 1024 * 1024
MLSTM_SAMPLE_TB = 4
SAMPLE_TB = 2


def _cparams(*sem):
    return pltpu.CompilerParams(dimension_semantics=sem, vmem_limit_bytes=VMEM_LIMIT)


def _dot(a, b):
    return jnp.dot(a, b, preferred_element_type=F32)


def _dot_nt(a, b):
    return lax.dot_general(a, b, (((1,), (1,)), ((), ())), preferred_element_type=F32)


def _dot_tn(a, b):
    return lax.dot_general(a, b, (((0,), (0,)), ((), ())), preferred_element_type=F32)


def _split(x):
    hi = x.astype(BF16)
    return hi, (x - hi.astype(F32)).astype(BF16)


def _dot3(a, b):
    a_hi, a_lo = _split(a)
    b_hi, b_lo = _split(b)
    return _dot(a_hi, b_hi) + _dot(a_hi, b_lo) + _dot(a_lo, b_hi)


def _log_sigmoid(x):
    return jnp.minimum(x, 0.0) - jnp.log1p(jnp.exp(-jnp.abs(x)))


def _silu(x):
    return x * jax.nn.sigmoid(x)


def _gelu_tanh(x):
    return 0.5 * x * (1.0 + jnp.tanh(math.sqrt(2.0 / math.pi) * (x + 0.044715 * (x * x * x))))


def _rms(x, g):
    return x * lax.rsqrt(jnp.mean(x * x, axis=-1, keepdims=True) + NORM_EPS) * g


def _norm_gates_kernel(x_ref, g_ref, wg_ref, a_ref, gates_ref):
    a = _rms(x_ref[...], g_ref[...])
    a_ref[...] = a.astype(BF16)
    a_hi, a_lo = _split(a)
    w_hi, w_lo = _split(wg_ref[...])
    gates_ref[...] = _dot_nt(a_hi, w_hi) + _dot_nt(a_hi, w_lo) + _dot_nt(a_lo, w_hi)


def _norm_gates(x, g, wg, tm):
    m = x.shape[0]
    return pl.pallas_call(
        _norm_gates_kernel,
        grid=(m // tm,),
        in_specs=[pl.BlockSpec((tm, D_MODEL), lambda i: (i, 0)),
                  pl.BlockSpec((1, D_MODEL), lambda i: (0, 0)),
                  pl.BlockSpec((LANES, D_MODEL), lambda i: (0, 0))],
        out_specs=[pl.BlockSpec((tm, D_MODEL), lambda i: (i, 0)),
                   pl.BlockSpec((tm, LANES), lambda i: (i, 0))],
        out_shape=[jax.ShapeDtypeStruct((m, D_MODEL), BF16),
                   jax.ShapeDtypeStruct((m, LANES), F32)],
        compiler_params=_cparams("parallel"),
        name="norm_gates",
    )(x, g, wg)


def _proj_kernel(xp_ref, xs_ref, w_ref, op_ref, os_ref, wb_ref, *, w_transposed):
    mm = _dot_nt if w_transposed else _dot

    @pl.when(pl.program_id(1) == 0)
    def _():
        wb_ref[...] = w_ref[...].astype(BF16)
        os_ref[...] = mm(xs_ref[...], wb_ref[...])

    op_ref[...] = mm(xp_ref[...], wb_ref[...]).astype(op_ref.dtype)


def _proj(xp, xs, w, n_cols, out_dtype, tm, tn, w_transposed):
    m, k = xp.shape
    ms = xs.shape[0]
    if w_transposed:
        w_spec = pl.BlockSpec((tn, k), lambda j, i: (j, 0))
        wb_shape = (tn, k)
    else:
        w_spec = pl.BlockSpec((k, tn), lambda j, i: (0, j))
        wb_shape = (k, tn)
    return pl.pallas_call(
        functools.partial(_proj_kernel, w_transposed=w_transposed),
        grid=(n_cols // tn, m // tm),
        in_specs=[pl.BlockSpec((tm, k), lambda j, i: (i, 0)),
                  pl.BlockSpec((ms, k), lambda j, i: (0, 0)),
                  w_spec],
        out_specs=[pl.BlockSpec((tm, tn), lambda j, i: (i, j)),
                   pl.BlockSpec((ms, tn), lambda j, i: (0, j))],
        out_shape=[jax.ShapeDtypeStruct((m, n_cols), out_dtype),
                   jax.ShapeDtypeStruct((ms, n_cols), F32)],
        scratch_shapes=[pltpu.VMEM(wb_shape, BF16)],
        compiler_params=_cparams("parallel", "arbitrary"),
        name="proj",
    )(xp, xs, w)


def _outproj_kernel(*refs, two, next_norm):
    refs = list(refs)
    ya_ref = refs.pop(0)
    yb_ref = refs.pop(0) if two else None
    yas_ref = refs.pop(0)
    ybs_ref = refs.pop(0) if two else None
    wa_ref = refs.pop(0)
    wb_ref = refs.pop(0) if two else None
    h_ref, hs_ref, g_ref = refs.pop(0), refs.pop(0), refs.pop(0)
    gn_ref = refs.pop(0) if next_norm else None
    o_ref, os_ref = refs.pop(0), refs.pop(0)
    a_ref, as_ref = (refs.pop(0), refs.pop(0)) if next_norm else (None, None)

    def run(x_ref, b_ref, res_ref, out_ref, nxt_ref):
        mix = _dot(x_ref[...].astype(BF16), wa_ref[...])
        if two:
            mix = mix + _dot(b_ref[...].astype(BF16), wb_ref[...])
        new = res_ref[...] + _rms(mix, g_ref[...])
        out_ref[...] = new
        if next_norm:
            nxt_ref[...] = _rms(new, gn_ref[...]).astype(BF16)

    @pl.when(pl.program_id(0) == 0)
    def _():
        run(yas_ref, ybs_ref, hs_ref, os_ref, as_ref)

    run(ya_ref, yb_ref, h_ref, o_ref, a_ref)


def _outproj(ya, yb, yas, ybs, w, h, hs, g, g_next, tm):
    m = h.shape[0]
    ms = hs.shape[0]
    ka = ya.shape[1]
    two = yb is not None
    next_norm = g_next is not None
    once = dict(pipeline_mode=pl.Buffered(1))
    row = lambda width: pl.BlockSpec((tm, width), lambda i: (i, 0))
    fixed = lambda rows, width: pl.BlockSpec((rows, width), lambda i: (0, 0), **once)
    if two:
        kb = yb.shape[1]
        in_specs = [row(ka), row(kb), fixed(ms, ka), fixed(ms, kb), fixed(ka, D_MODEL),
                    pl.BlockSpec((kb, D_MODEL), lambda i: (ka // kb, 0), **once)]
        args = [ya, yb, yas, ybs, w, w]
    else:
        in_specs = [row(ka), fixed(ms, ka), fixed(ka, D_MODEL)]
        args = [ya, yas, w]
    in_specs += [row(D_MODEL), fixed(ms, D_MODEL), fixed(1, D_MODEL)]
    args += [h, hs, g]
    sample_out = pl.BlockSpec((ms, D_MODEL), lambda i: (0, 0))
    out_specs = [row(D_MODEL), sample_out]
    out_shape = [jax.ShapeDtypeStruct((m, D_MODEL), F32), jax.ShapeDtypeStruct((ms, D_MODEL), F32)]
    if next_norm:
        in_specs.append(fixed(1, D_MODEL))
        args.append(g_next)
        out_specs += [row(D_MODEL), sample_out]
        out_shape += [jax.ShapeDtypeStruct((m, D_MODEL), BF16), jax.ShapeDtypeStruct((ms, D_MODEL), BF16)]
    return pl.pallas_call(
        functools.partial(_outproj_kernel, two=two, next_norm=next_norm),
        grid=(m // tm,),
        in_specs=in_specs,
        out_specs=out_specs,
        out_shape=out_shape,
        compiler_params=pltpu.CompilerParams(dimension_semantics=("arbitrary",),
                                             vmem_limit_bytes=BIG_VMEM_LIMIT),
        name="outproj",
    )(*args)


def _mlstm_prompt_kernel(bg_ref, q_ref, k_ref, v_ref, o_ref, z_ref, gc_ref, gr_ref, gh_ref,
                         y_ref, c_out, n_out, m_out, c_s, n_s, m_s):
    c = pl.program_id(1)
    t = CHUNK

    @pl.when(c == 0)
    def _():
        c_s[...] = jnp.zeros_like(c_s)
        n_s[...] = jnp.zeros_like(n_s)
        m_s[...] = jnp.zeros_like(m_s)

    row = lax.broadcasted_iota(jnp.int32, (t, t), 0)
    col = lax.broadcasted_iota(jnp.int32, (t, t), 1)
    tril = col <= row
    triu = row <= col
    gc = gc_ref[...]
    gr = gr_ref[...]
    for h in range(H_A):
        b_i = bg_ref[h]
        b_f = bg_ref[H_A + h]
        i_col = gc[:, h:h + 1] + b_i
        i_row = gr[h:h + 1, :] + b_i
        lf_col = _log_sigmoid(gc[:, H_A + h:H_A + h + 1] + b_f)
        lf_row = _log_sigmoid(gr[H_A + h:H_A + h + 1, :] + b_f)
        b_col = jnp.sum(jnp.where(tril, lf_row, 0.0), axis=1, keepdims=True)
        b_row = jnp.sum(jnp.where(triu, lf_col, 0.0), axis=0, keepdims=True)
        m_prev = m_s[h][:, 0:1]
        d = jnp.where(tril, b_col - b_row + i_row, -jnp.inf)
        inter = b_col + m_prev
        m_t = jnp.maximum(inter, jnp.max(d, axis=1, keepdims=True))
        w_intra = jnp.exp(d - m_t)
        w_inter = jnp.exp(inter - m_t) * SCALE_A
        q = q_ref[:, h * DK_A:(h + 1) * DK_A]
        k = k_ref[:, h * DK_A:(h + 1) * DK_A]
        v = v_ref[:, h * DV_A:(h + 1) * DV_A]
        s = _dot_nt(q, k) * (w_intra * SCALE_A)
        c_old = c_s[h]
        n_old = n_s[h]
        num = _dot(s.astype(BF16), v) + w_inter * _dot(q, c_old.astype(BF16))
        qn = jnp.sum(q.astype(F32) * n_old, axis=1, keepdims=True)
        den = jnp.sum(s, axis=1, keepdims=True) + w_inter * qn
        hh = num * (1.0 / jnp.maximum(jnp.abs(den), jnp.exp(-m_t)))
        hn = _rms(hh, gh_ref[h:h + 1, :])
        o = o_ref[:, h * DV_A:(h + 1) * DV_A].astype(F32)
        z = z_ref[:, h * DV_A:(h + 1) * DV_A].astype(F32)
        gate = z / ((1.0 + jnp.exp(-o)) * (1.0 + jnp.exp(-z)))
        y_ref[:, h * DV_A:(h + 1) * DV_A] = (hn * gate).astype(BF16)
        b_last = b_col[t - 1:t, :]
        g_col = b_last - b_col + i_col
        m_new = jnp.maximum(b_last + m_prev, jnp.max(g_col, axis=0, keepdims=True))
        e_col = jnp.exp(g_col - m_new)
        decay = jnp.exp(b_last + m_prev - m_new)
        ke = k.astype(F32) * e_col
        c_s[h] = decay * c_old + _dot_tn(ke.astype(BF16), v)
        n_s[h] = decay * n_old + jnp.sum(ke, axis=0, keepdims=True)
        m_s[h] = jnp.broadcast_to(m_new, (1, LANES))

    @pl.when(c == pl.num_programs(1) - 1)
    def _():
        c_out[...] = c_s[...]
        n_out[...] = n_s[...]
        m_out[...] = m_s[...]


def _mlstm_prompt(bg, proj, gates_col, gates_row, g_head):
    nc = SEQ // CHUNK
    t = CHUNK
    qk_w = H_A * DK_A
    return pl.pallas_call(
        _mlstm_prompt_kernel,
        grid=(BATCH, nc),
        in_specs=[pl.BlockSpec(memory_space=pltpu.SMEM),
                  pl.BlockSpec((None, t, qk_w), lambda b, c: (b, c, 0)),
                  pl.BlockSpec((None, t, qk_w), lambda b, c: (b, c, 1)),
                  pl.BlockSpec((None, t, W_A), lambda b, c: (b, c, 1)),
                  pl.BlockSpec((None, t, W_A), lambda b, c: (b, c, 2)),
                  pl.BlockSpec((None, t, W_A), lambda b, c: (b, c, 3)),
                  pl.BlockSpec((None, t, LANES), lambda b, c: (b, c, 0)),
                  pl.BlockSpec((None, 2 * H_A, t), lambda b, c: (b, 0, c)),
                  pl.BlockSpec((H_A, DV_A), lambda b, c: (0, 0))],
        out_specs=[pl.BlockSpec((None, t, W_A), lambda b, c: (b, c, 0)),
                   pl.BlockSpec((None, H_A, DK_A, DV_A), lambda b, c: (b, 0, 0, 0)),
                   pl.BlockSpec((None, H_A, 1, DK_A), lambda b, c: (b, 0, 0, 0)),
                   pl.BlockSpec((None, H_A, 1, LANES), lambda b, c: (b, 0, 0, 0))],
        out_shape=[jax.ShapeDtypeStruct((BATCH, SEQ, W_A), BF16),
                   jax.ShapeDtypeStruct((BATCH, H_A, DK_A, DV_A), F32),
                   jax.ShapeDtypeStruct((BATCH, H_A, 1, DK_A), F32),
                   jax.ShapeDtypeStruct((BATCH, H_A, 1, LANES), F32)],
        scratch_shapes=[pltpu.VMEM((H_A, DK_A, DV_A), F32),
                        pltpu.VMEM((H_A, 1, DK_A), F32),
                        pltpu.VMEM((H_A, 1, LANES), F32)],
        compiler_params=_cparams("parallel", "arbitrary"),
        name="mlstm_prompt",
    )(bg, proj, proj, proj, proj, proj, gates_col, gates_row, g_head)


def _mlstm_sample_kernel(gi_ref, gf_ref, bi_ref, bf_ref, m0_ref, q_ref, k_ref, v_ref, o_ref, z_ref,
                         qt_ref, kt_ref, n0_ref, gh_ref, c0_ref,
                         y_ref, c1_ref, n1_ref, m1_ref):
    r0 = pl.program_id(0) * MLSTM_SAMPLE_TB
    shift = lax.rem(DEC_BATCH - r0, DEC_BATCH)
    q_cols = [pltpu.roll(qt_ref[h], shift, axis=1) for h in range(H_A)]
    k_cols = [pltpu.roll(kt_ref[h], shift, axis=1) for h in range(H_A)]
    for j in range(MLSTM_SAMPLE_TB):
        rows = pl.ds(r0 + j, 1)
        i_v = gi_ref[rows, :] + bi_ref[...]
        lf_v = _log_sigmoid(gf_ref[rows, :] + bf_ref[...])
        m0_v = m0_ref[rows, :]
        m_t = jnp.maximum(lf_v + m0_v, i_v)
        w_in = jnp.exp(i_v - m_t)
        w_st = jnp.exp(lf_v + m0_v - m_t)
        floor = jnp.exp(-m_t)
        m1_ref[rows, :] = m_t
        for h in range(H_A):
            ks = slice(h * DK_A, (h + 1) * DK_A)
            vs = slice(h * DV_A, (h + 1) * DV_A)
            wi = w_in[:, h:h + 1]
            ws = w_st[:, h:h + 1]
            q_col = q_cols[h][:, j:j + 1]
            k_col = k_cols[h][:, j:j + 1]
            q_row = q_ref[rows, ks]
            k_row = k_ref[rows, ks]
            v_row = v_ref[rows, vs]
            n_row = n0_ref[rows, ks]
            c_old = c0_ref[j, h]
            qk = jnp.sum(q_row * k_row, axis=1, keepdims=True) * SCALE_A
            s = qk * wi
            q_c = jnp.sum(c_old * q_col, axis=0, keepdims=True) * SCALE_A
            qn = jnp.sum(q_row * n_row, axis=1, keepdims=True) * SCALE_A
            num = s * v_row + ws * q_c
            den = s + ws * qn
            hh = num / jnp.maximum(jnp.abs(den), floor[:, h:h + 1])
            hn = _rms(hh, gh_ref[h:h + 1, :])
            y_ref[rows, vs] = hn * jax.nn.sigmoid(o_ref[rows, vs]) * _silu(z_ref[rows, vs])
            c1_ref[j, h] = ws * c_old + (wi * k_col) * v_row
            n1_ref[rows, ks] = ws * n_row + wi * k_row


def _mlstm_sample(gi, gf, bi, bf, m0, proj, qt, kt, n0, g_head, c0):
    qk_w = H_A * DK_A
    once = dict(pipeline_mode=pl.Buffered(1))
    full = lambda shape: pl.BlockSpec(shape, lambda r: (0,) * len(shape))
    cols = lambda width, j: pl.BlockSpec((DEC_BATCH, width), lambda r: (0, j), **once)
    state = pl.BlockSpec((MLSTM_SAMPLE_TB, H_A, DK_A, DV_A), lambda r: (r, 0, 0, 0))
    return pl.pallas_call(
        _mlstm_sample_kernel,
        grid=(DEC_BATCH // MLSTM_SAMPLE_TB,),
        in_specs=[full((DEC_BATCH, H_A)), full((DEC_BATCH, H_A)), full((1, H_A)), full((1, H_A)),
                  full((DEC_BATCH, H_A)),
                  cols(qk_w, 0), cols(qk_w, 1), cols(W_A, 1), cols(W_A, 2), cols(W_A, 3),
                  pl.BlockSpec((H_A, DK_A, DEC_BATCH), lambda r: (0, 0, 0), **once),
                  pl.BlockSpec((H_A, DK_A, DEC_BATCH), lambda r: (0, 0, 0), **once),
                  cols(qk_w, 0), full((H_A, DV_A)), state],
        out_specs=[full((DEC_BATCH, W_A)), state,
                   full((DEC_BATCH, qk_w)), full((DEC_BATCH, H_A))],
        out_shape=[jax.ShapeDtypeStruct((DEC_BATCH, W_A), F32),
                   jax.ShapeDtypeStruct((DEC_BATCH, H_A, DK_A, DV_A), F32),
                   jax.ShapeDtypeStruct((DEC_BATCH, qk_w), F32),
                   jax.ShapeDtypeStruct((DEC_BATCH, H_A), F32)],
        compiler_params=pltpu.CompilerParams(dimension_semantics=("arbitrary",),
                                             vmem_limit_bytes=BIG_VMEM_LIMIT),
        name="mlstm_sample",
    )(gi, gf, bi, bf, m0, proj, proj, proj, proj, proj, qt, kt, n0, g_head, c0)


def _s5_build_kernel(ldt_ref, lamr_ref, lami_ref, b1_ref, cc_ref, dsk_ref,
                     m_ref, w_ref, v_ref, p1_ref, p2_ref):
    dt = jnp.exp(ldt_ref[...])
    lam_re, lam_im = lamr_ref[...], lami_ref[...]
    mag = jnp.exp(lam_re * dt)
    ang = lam_im * dt
    a_re, a_im = mag * jnp.cos(ang), mag * jnp.sin(ang)
    den = lam_re * lam_re + lam_im * lam_im
    f_re = ((a_re - 1.0) * lam_re + a_im * lam_im) / den
    f_im = (a_im * lam_re - (a_re - 1.0) * lam_im) / den
    pows = [(jnp.ones_like(a_re), jnp.zeros_like(a_im))]
    for _ in range(S5_T):
        r_re, r_im = pows[-1]
        pows.append((r_re * a_re - r_im * a_im, r_re * a_im + r_im * a_re))

    def tall(vals):
        return jnp.concatenate([jnp.broadcast_to(x, (GROUP_B, LANES)) for x in vals], axis=0)

    af_re = tall([pows[S5_T - 1 - s][0] * f_re - pows[S5_T - 1 - s][1] * f_im for s in range(S5_T)])
    af_im = tall([pows[S5_T - 1 - s][0] * f_im + pows[S5_T - 1 - s][1] * f_re for s in range(S5_T)])
    lane = lax.broadcasted_iota(jnp.int32, (1, LANES), 1)
    sgn_lane = jnp.where(lane < P_B, -1.0, 1.0)
    b1 = b1_ref[...]
    b2 = sgn_lane * pltpu.roll(b1, P_B, axis=1)
    w_ref[...] = (af_re * b1 + af_im * b2).astype(BF16)

    eye = (lax.broadcasted_iota(jnp.int32, (LANES, LANES), 0)
           == lax.broadcasted_iota(jnp.int32, (LANES, LANES), 1))

    def column(row):
        return jnp.sum(jnp.where(eye, row, 0.0), axis=1, keepdims=True)

    ac_re, ac_im, fc_re, fc_im = column(a_re), column(a_im), column(f_re), column(f_im)
    kid = lax.broadcasted_iota(jnp.int32, (S5_T, LANES), 0)
    expand = (lax.broadcasted_iota(jnp.int32, (S5_T, S5_W), 1) // GROUP_B
              == lax.broadcasted_iota(jnp.int32, (S5_T, S5_W), 0)).astype(BF16)

    def lane_blocks(part):
        stacked = jnp.zeros((S5_T, LANES), F32)
        for k in range(S5_T):
            stacked = jnp.where(kid == k, pows[k][part], stacked)
        hi = stacked.astype(BF16)
        rest = stacked - hi.astype(F32)
        mid = rest.astype(BF16)
        lo = (rest - mid.astype(F32)).astype(BF16)
        return _dot_tn(hi, expand) + _dot_tn(mid, expand) + _dot_tn(lo, expand)

    q_re, q_im = lane_blocks(0), lane_blocks(1)
    cc = cc_ref[...]
    cs = pltpu.roll(cc, P_B, axis=0)
    rowi = lax.broadcasted_iota(jnp.int32, (LANES, 1), 0)
    sgn_row = jnp.where(rowi < P_B, 1.0, -1.0)

    def readout(r_re, r_im):
        return sgn_row * (r_re * cc) - r_im * cs

    v_ref[...] = readout(q_re * ac_re - q_im * ac_im, q_re * ac_im + q_im * ac_re).astype(BF16)
    vf = readout(q_re * fc_re - q_im * fc_im, q_re * fc_im + q_im * fc_re)
    kw = _dot3(b1[0:GROUP_B, :], vf)
    lane_w = lax.broadcasted_iota(jnp.int32, (GROUP_B, S5_W), 1)
    blocks = [kw]
    for s in range(1, S5_T):
        blocks.append(jnp.where(lane_w >= s * GROUP_B, pltpu.roll(kw, s * GROUP_B, axis=1), 0.0))
    toep = jnp.concatenate(blocks, axis=0)
    ri = lax.broadcasted_iota(jnp.int32, (S5_W, S5_W), 0)
    ci = lax.broadcasted_iota(jnp.int32, (S5_W, S5_W), 1)
    m_ref[...] = (toep + jnp.where(ri == ci, dsk_ref[...], 0.0)).astype(BF16)

    r_re, r_im = pows[S5_T]
    rid = lax.broadcasted_iota(jnp.int32, (8, LANES), 0)
    p1 = jnp.where(rid == S5_A1_ROW, a_re, 0.0)
    p2 = jnp.where(rid == S5_A1_ROW, sgn_lane * a_im, 0.0)
    for kk in range(S5_LEVELS):
        p1 = jnp.where(rid == kk, r_re, p1)
        p2 = jnp.where(rid == kk, sgn_lane * r_im, p2)
        r_re, r_im = r_re * r_re - r_im * r_im, 2.0 * (r_re * r_im)
    p1_ref[...] = p1
    p2_ref[...] = p2


def _s5_build(ldt, lamr, lami, b1, cc, dsk):
    g3 = lambda a, b: pl.BlockSpec((None, a, b), lambda g: (g, 0, 0))
    return pl.pallas_call(
        _s5_build_kernel,
        grid=(G_B,),
        in_specs=[g3(1, 1), g3(1, LANES), g3(1, LANES),
                  g3(S5_W, LANES), g3(LANES, S5_W), g3(1, S5_W)],
        out_specs=[g3(S5_W, S5_W), g3(S5_W, LANES), g3(LANES, S5_W), g3(8, LANES), g3(8, LANES)],
        out_shape=[jax.ShapeDtypeStruct((G_B, S5_W, S5_W), BF16),
                   jax.ShapeDtypeStruct((G_B, S5_W, LANES), BF16),
                   jax.ShapeDtypeStruct((G_B, LANES, S5_W), BF16),
                   jax.ShapeDtypeStruct((G_B, 8, LANES), F32),
                   jax.ShapeDtypeStruct((G_B, 8, LANES), F32)],
        compiler_params=_cparams("parallel"),
        name="s5_build",
    )(ldt, lamr, lami, b1, cc, dsk)


def _s5_prompt_kernel(u_ref, m_ref, w_ref, v_ref, p1_ref, p2_ref, y_ref, xf_ref, y_s):
    nblk = SEQ // S5_T
    rows = BATCH * nblk
    bidx = lax.broadcasted_iota(jnp.int32, (rows, LANES), 0) & (nblk - 1)
    steps = [u_ref[pl.ds(s, rows, stride=S5_T), :] for s in range(S5_T)]
    for g in range(OCT):
        gl = slice(g * GROUP_B, (g + 1) * GROUP_B)
        u = jnp.concatenate([x[:, gl] for x in steps], axis=-1).astype(BF16)
        s = _dot(u, w_ref[g])
        for kk in range(S5_LEVELS):
            sh = 1 << kk
            r = jnp.where(bidx >= sh, pltpu.roll(s, sh, axis=0), 0.0)
            s = s + p1_ref[g, kk:kk + 1, :] * r + p2_ref[g, kk:kk + 1, :] * pltpu.roll(r, P_B, axis=1)
        x_prev = jnp.where(bidx >= 1, pltpu.roll(s, 1, axis=0), 0.0)
        y = _dot(u, m_ref[g]) + _dot(x_prev.astype(BF16), v_ref[g])
        y_s[g] = _gelu_tanh(y)
        for b in range(BATCH):
            xf_ref[g, b:b + 1, :] = s[(b + 1) * nblk - 1:(b + 1) * nblk, :]
    for t in range(S5_T):
        half = slice((t // OCT) * LANES, (t // OCT + 1) * LANES)
        tl = slice((t % OCT) * GROUP_B, (t % OCT + 1) * GROUP_B)
        y_ref[pl.ds(t, rows, stride=S5_T), :] = jnp.concatenate(
            [y_s[g, :, half][:, tl] for g in range(OCT)], axis=-1)


def _s5_prompt(u, m, w, v, p1, p2):
    mp = BATCH * SEQ
    rows = BATCH * (SEQ // S5_T)
    o3 = lambda a, b: pl.BlockSpec((OCT, a, b), lambda g: (g, 0, 0))
    return pl.pallas_call(
        _s5_prompt_kernel,
        grid=(G_B // OCT,),
        in_specs=[pl.BlockSpec((mp, LANES), lambda g: (0, g)),
                  o3(S5_W, S5_W), o3(S5_W, LANES), o3(LANES, S5_W), o3(8, LANES), o3(8, LANES)],
        out_specs=[pl.BlockSpec((mp, LANES), lambda g: (0, g)), o3(BATCH, LANES)],
        out_shape=[jax.ShapeDtypeStruct((mp, W_B), F32),
                   jax.ShapeDtypeStruct((G_B, BATCH, LANES), F32)],
        scratch_shapes=[pltpu.VMEM((OCT, rows, S5_W), F32)],
        compiler_params=_cparams("parallel"),
        name="s5_prompt",
    )(u, m, w, v, p1, p2)


def _s5_sample_kernel(u_ref, xr_ref, xi_ref, m_ref, w_ref, v_ref, p1_ref, p2_ref,
                      y_ref, x1r_ref, x1i_ref):
    lane = lax.broadcasted_iota(jnp.int32, (DEC_BATCH, LANES), 1)
    last = LANES - GROUP_B
    half = S5_W // 2
    a1 = slice(S5_A1_ROW, S5_A1_ROW + 1)
    u_all = u_ref[...]
    ys = []
    for g in range(OCT):
        ps = slice(g * P_B, (g + 1) * P_B)
        u = jnp.where(lane >= last, pltpu.roll(u_all, (last - g * GROUP_B) % LANES, axis=1), 0.0)
        u = u.astype(BF16)
        x0 = jnp.concatenate([xr_ref[:, ps], xi_ref[:, ps]], axis=-1)
        x1 = (p1_ref[g, a1, :] * x0 + p2_ref[g, a1, :] * pltpu.roll(x0, P_B, axis=1)
              + _dot(u, w_ref[g, half:, :]))
        x1r_ref[:, ps] = x1[:, :P_B]
        x1i_ref[:, ps] = x1[:, P_B:]
        y = (_dot(x0.astype(BF16), v_ref[g])[:, 0:GROUP_B]
             + _dot(u, m_ref[g, half:, :])[:, S5_W - GROUP_B:])
        ys.append(_gelu_tanh(y))
    y_ref[...] = jnp.concatenate(ys, axis=-1)


def _s5_sample(u, xr, xi, m, w, v, p1, p2):
    o3 = lambda a, b: pl.BlockSpec((OCT, a, b), lambda g: (g, 0, 0))
    tile = pl.BlockSpec((DEC_BATCH, LANES), lambda g: (0, g))
    st = pl.BlockSpec((DEC_BATCH, OCT * P_B), lambda g: (0, g))
    st_shape = jax.ShapeDtypeStruct((DEC_BATCH, G_B * P_B), F32)
    return pl.pallas_call(
        _s5_sample_kernel,
        grid=(G_B // OCT,),
        in_specs=[tile, st, st, o3(S5_W, S5_W), o3(S5_W, LANES), o3(LANES, S5_W), o3(8, LANES), o3(8, LANES)],
        out_specs=[tile, st, st],
        out_shape=[jax.ShapeDtypeStruct((DEC_BATCH, W_B), F32), st_shape, st_shape],
        compiler_params=_cparams("parallel"),
        name="s5_sample",
    )(u, xr, xi, m, w, v, p1, p2)


def _glu_kernel(y_ref, w_ref, b_ref, z_ref, o_ref):
    y = y_ref[...]
    gate = jax.nn.sigmoid(_dot(y.astype(BF16), w_ref[...]) + b_ref[...])
    o_ref[...] = (y.astype(F32) * gate * _silu(z_ref[...].astype(F32))).astype(o_ref.dtype)


def _glu(y, w, b, proj_b, tm):
    m = y.shape[0]
    return pl.pallas_call(
        _glu_kernel,
        grid=(m // tm,),
        in_specs=[pl.BlockSpec((tm, W_B), lambda i: (i, 0)),
                  pl.BlockSpec((W_B, W_B), lambda i: (0, 0)),
                  pl.BlockSpec((1, W_B), lambda i: (0, 0)),
                  pl.BlockSpec((tm, W_B), lambda i: (i, 1))],
        out_specs=pl.BlockSpec((tm, W_B), lambda i: (i, 0)),
        out_shape=jax.ShapeDtypeStruct((m, W_B), BF16),
        compiler_params=_cparams("parallel"),
        name="glu",
    )(y, w, b, proj_b)


def _rope_table_kernel(cos_ref, sin_ref, *, pos0):
    shape = cos_ref.shape
    pos = lax.broadcasted_iota(jnp.int32, shape, 0).astype(F32) + pos0
    j = lax.broadcasted_iota(jnp.int32, shape, 1).astype(F32)
    ang = pos * jnp.power(ROPE_BASE, -(j / (DK_C // 2)))
    cos_ref[...] = jnp.cos(ang)
    sin_ref[...] = jnp.sin(ang)


def _rope_table(rows, pos0):
    shape = jax.ShapeDtypeStruct((rows, DK_C // 2), F32)
    return pl.pallas_call(functools.partial(_rope_table_kernel, pos0=float(pos0)),
                          out_shape=[shape, shape], name="rope_table")()


def _rope(x, cos, sin):
    half = DK_C // 2
    x1, x2 = x[:, :half], x[:, half:]
    return jnp.concatenate([x1 * cos - x2 * sin, x1 * sin + x2 * cos], axis=-1)


def _ret_prompt_kernel(q_ref, k_ref, v_ref, z_ref, cos_ref, sin_ref, gh_ref, y_ref, s_out, s_s):
    c = pl.program_id(1)
    t = CHUNK

    @pl.when(c == 0)
    def _():
        s_s[...] = jnp.zeros_like(s_s)

    row = lax.broadcasted_iota(jnp.int32, (t, t), 0)
    col = lax.broadcasted_iota(jnp.int32, (t, t), 1)
    tril = col <= row
    diff = (row - col).astype(F32)
    tpos = lax.broadcasted_iota(jnp.int32, (t, 1), 0).astype(F32)
    cos, sin = cos_ref[...], sin_ref[...]
    for h in range(H_C):
        lg = LOG_GAMMA[h]
        ks = slice(h * DK_C, (h + 1) * DK_C)
        vs = slice(h * DV_C, (h + 1) * DV_C)
        q = _rope(q_ref[:, ks].astype(F32), cos, sin).astype(BF16)
        k32 = _rope(k_ref[:, ks].astype(F32), cos, sin)
        v = v_ref[:, vs]
        mask = jnp.where(tril, jnp.exp(diff * lg), 0.0) * SCALE_C
        s = _dot_nt(q, k32.astype(BF16)) * mask
        s_old = s_s[h]
        o = _dot(s.astype(BF16), v) + _dot(q, s_old.astype(BF16)) * jnp.exp((tpos + 1.0) * lg)
        y_ref[:, vs] = (_rms(o, gh_ref[h:h + 1, :]) * _silu(z_ref[:, vs].astype(F32))).astype(BF16)
        k_tail = k32 * (jnp.exp((t - 1.0 - tpos) * lg) * SCALE_C)
        s_s[h] = math.exp(t * lg) * s_old + _dot_tn(k_tail.astype(BF16), v)

    @pl.when(c == pl.num_programs(1) - 1)
    def _():
        s_out[...] = s_s[...]


def _ret_prompt(proj, cos, sin, g_head):
    nc = SEQ // CHUNK
    t = CHUNK
    qk_w = H_C * DK_C
    return pl.pallas_call(
        _ret_prompt_kernel,
        grid=(BATCH, nc),
        in_specs=[pl.BlockSpec((None, t, qk_w), lambda b, c: (b, c, 0)),
                  pl.BlockSpec((None, t, qk_w), lambda b, c: (b, c, 1)),
                  pl.BlockSpec((None, t, W_C), lambda b, c: (b, c, 1)),
                  pl.BlockSpec((None, t, W_C), lambda b, c: (b, c, 2)),
                  pl.BlockSpec((t, DK_C // 2), lambda b, c: (c, 0)),
                  pl.BlockSpec((t, DK_C // 2), lambda b, c: (c, 0)),
                  pl.BlockSpec((H_C, DV_C), lambda b, c: (0, 0))],
        out_specs=[pl.BlockSpec((None, t, W_C), lambda b, c: (b, c, 0)),
                   pl.BlockSpec((None, H_C, DK_C, DV_C), lambda b, c: (b, 0, 0, 0))],
        out_shape=[jax.ShapeDtypeStruct((BATCH, SEQ, W_C), BF16),
                   jax.ShapeDtypeStruct((BATCH, H_C, DK_C, DV_C), F32)],
        scratch_shapes=[pltpu.VMEM((H_C, DK_C, DV_C), F32)],
        compiler_params=_cparams("parallel", "arbitrary"),
        name="ret_prompt",
    )(proj, proj, proj, proj, cos, sin, g_head)


def _rope_sample_kernel(q_ref, k_ref, cos_ref, sin_ref, qo_ref, ko_ref):
    cos, sin = cos_ref[0:1, :], sin_ref[0:1, :]
    for h in range(H_C):
        ks = slice(h * DK_C, (h + 1) * DK_C)
        qo_ref[:, ks] = _rope(q_ref[:, ks], cos, sin)
        ko_ref[:, ks] = _rope(k_ref[:, ks], cos, sin)


def _rope_sample(proj, cos, sin):
    qk_w = H_C * DK_C
    shape = jax.ShapeDtypeStruct((DEC_BATCH, qk_w), F32)
    return pl.pallas_call(
        _rope_sample_kernel,
        grid=(1,),
        in_specs=[pl.BlockSpec((DEC_BATCH, qk_w), lambda i: (0, 0)),
                  pl.BlockSpec((DEC_BATCH, qk_w), lambda i: (0, 1)),
                  pl.BlockSpec((8, DK_C // 2), lambda i: (0, 0)),
                  pl.BlockSpec((8, DK_C // 2), lambda i: (0, 0))],
        out_specs=[pl.BlockSpec((DEC_BATCH, qk_w), lambda i: (0, 0)),
                   pl.BlockSpec((DEC_BATCH, qk_w), lambda i: (0, 0))],
        out_shape=[shape, shape],
        compiler_params=_cparams("arbitrary"),
        name="rope_sample",
    )(proj, proj, cos, sin)


def _ret_sample_kernel(q_ref, k_ref, v_ref, z_ref, qt_ref, kt_ref, gh_ref, s0_ref, y_ref, s1_ref):
    r0 = pl.program_id(0) * SAMPLE_TB
    shift = lax.rem(DEC_BATCH - r0, DEC_BATCH)
    for h in range(H_C):
        gamma = math.exp(LOG_GAMMA[h])
        ks = slice(h * DK_C, (h + 1) * DK_C)
        vs = slice(h * DV_C, (h + 1) * DV_C)
        q_cols = pltpu.roll(qt_ref[h], shift, axis=1)
        k_cols = pltpu.roll(kt_ref[h], shift, axis=1)
        for j in range(SAMPLE_TB):
            rows = pl.ds(r0 + j, 1)
            v_row = v_ref[rows, vs]
            s_old = s0_ref[j, h]
            qk = jnp.sum(q_ref[rows, ks] * k_ref[rows, ks], axis=1, keepdims=True) * SCALE_C
            o = qk * v_row + jnp.sum(s_old * q_cols[:, j:j + 1], axis=0, keepdims=True) * gamma
            y_ref[rows, vs] = _rms(o, gh_ref[h:h + 1, :]) * _silu(z_ref[rows, vs])
            s1_ref[j, h] = gamma * s_old + (k_cols[:, j:j + 1] * SCALE_C) * v_row


def _ret_sample(q, k, proj, qt, kt, g_head, s0):
    qk_w = H_C * DK_C
    once = dict(pipeline_mode=pl.Buffered(1))
    fixed = lambda shape, *idx: pl.BlockSpec(shape, lambda r: idx or (0,) * len(shape), **once)
    state = pl.BlockSpec((SAMPLE_TB, H_C, DK_C, DV_C), lambda r: (r, 0, 0, 0))
    return pl.pallas_call(
        _ret_sample_kernel,
        grid=(DEC_BATCH // SAMPLE_TB,),
        in_specs=[fixed((DEC_BATCH, qk_w)), fixed((DEC_BATCH, qk_w)),
                  fixed((DEC_BATCH, W_C), 0, 1), fixed((DEC_BATCH, W_C), 0, 2),
                  fixed((H_C, DK_C, DEC_BATCH)), fixed((H_C, DK_C, DEC_BATCH)),
                  fixed((H_C, DV_C)), state],
        out_specs=[pl.BlockSpec((DEC_BATCH, W_C), lambda r: (0, 0)), state],
        out_shape=[jax.ShapeDtypeStruct((DEC_BATCH, W_C), F32),
                   jax.ShapeDtypeStruct((DEC_BATCH, H_C, DK_C, DV_C), F32)],
        compiler_params=pltpu.CompilerParams(dimension_semantics=("arbitrary",),
                                             vmem_limit_bytes=BIG_VMEM_LIMIT),
        name="ret_sample",
    )(q, k, proj, proj, qt, kt, g_head, s0)


def kernel(x_prompt, x_sample, state_mlstm_C, state_mlstm_n, state_mlstm_m, state_s5_re, state_s5_im, state_ret, g_pre, g_post, w_in0, b_gates0, g_head_a, lam_re, lam_im, log_dt, b_re, b_im, c_re, c_im, d_skip, w_glu, b_glu, w_out0, w_in1, g_head_c, w_out1):
    mp = BATCH * SEQ
    xp = x_prompt.reshape(mp, D_MODEL).astype(F32)
    xs = x_sample.reshape(DEC_BATCH, D_MODEL).astype(F32)

    w0t = w_in0.T.astype(F32)
    w0g = jnp.pad(w0t[QKV0:GATE0], ((0, LANES - 2 * H_A), (0, 0)))
    w0b = w0t[GATE0:]
    w1 = w_in1.astype(F32)
    wo0 = w_out0.astype(BF16)
    wo1 = w_out1.astype(BF16)
    wg = w_glu.astype(BF16)
    bglu = b_glu.reshape(1, W_B).astype(F32)
    g_pre = g_pre.astype(F32)
    g_post = g_post.astype(F32)
    bg = b_gates0.astype(F32)
    gh_a = g_head_a.astype(F32)
    gh_c = g_head_c.astype(F32)

    dup = lambda a: jnp.concatenate([a, a], axis=-1).astype(F32)
    lamr, lami = dup(lam_re), dup(lam_im)
    b1 = jnp.concatenate([b_re.transpose(0, 2, 1), b_im.transpose(0, 2, 1)], axis=-1)
    cc = jnp.concatenate([c_re.transpose(0, 2, 1), c_im.transpose(0, 2, 1)], axis=1)
    s5m, s5w, s5v, s5p1, s5p2 = _s5_build(
        log_dt.reshape(G_B, 1, 1).astype(F32), lamr[:, None, :], lami[:, None, :],
        jnp.tile(b1.astype(F32), (1, S5_T, 1)), jnp.tile(cc.astype(F32), (1, 1, S5_T)),
        jnp.tile(d_skip.astype(F32), (1, S5_T))[:, None, :])

    a0, gates = _norm_gates(xp, g_pre[0:1], w0g, 256)
    a0s, gates_s = _norm_gates(xs, g_pre[0:1], w0g, DEC_BATCH)
    pa, pas = _proj(a0, a0s, w0t, QKV0, BF16, 1024, 1024, True)
    pb, pbs = _proj(a0, a0s, w0b, 2 * W_B, F32, 1024, 1024, True)
    gates3 = gates.reshape(BATCH, SEQ, LANES)
    gates_row = gates3[:, :, :2 * H_A].transpose(0, 2, 1)
    ya, c_p, n_p, m_p = _mlstm_prompt(bg, pa.reshape(BATCH, SEQ, QKV0), gates3, gates_row, gh_a)
    yb, xf = _s5_prompt(pb, s5m, s5w, s5v, s5p1, s5p2)
    yb = _glu(yb, wg, bglu, pb, 512)
    s5r_p = xf[:, :, :P_B].transpose(1, 0, 2)
    s5i_p = xf[:, :, P_B:].transpose(1, 0, 2)
    qk_a = H_A * DK_A
    to_cols = lambda a, nh, dk: a.reshape(DEC_BATCH, nh, dk).transpose(1, 2, 0)
    yas, c_s, n_s, m_s = _mlstm_sample(
        gates_s[:, :H_A], gates_s[:, H_A:2 * H_A], bg[None, :H_A], bg[None, H_A:],
        state_mlstm_m.astype(F32), pas,
        to_cols(pas[:, :qk_a], H_A, DK_A), to_cols(pas[:, qk_a:2 * qk_a], H_A, DK_A),
        state_mlstm_n.reshape(DEC_BATCH, qk_a).astype(F32), gh_a, state_mlstm_C.astype(F32))
    ybs, s5r_s, s5i_s = _s5_sample(
        pbs, state_s5_re.reshape(DEC_BATCH, G_B * P_B).astype(F32),
        state_s5_im.reshape(DEC_BATCH, G_B * P_B).astype(F32), s5m, s5w, s5v, s5p1, s5p2)
    s5r_s = s5r_s.reshape(DEC_BATCH, G_B, P_B)
    s5i_s = s5i_s.reshape(DEC_BATCH, G_B, P_B)
    ybs = _glu(ybs, wg, bglu, pbs, DEC_BATCH)
    h1, h1s, a1, a1s = _outproj(ya.reshape(mp, W_A), yb, yas, ybs, wo0, xp, xs, g_post[0:1],
                                g_pre[1:2], 512)

    p1, p1s = _proj(a1, a1s, w1, 2 * H_C * DK_C + 2 * W_C, BF16, 1024, 1024, False)
    cos_p, sin_p = _rope_table(SEQ, 0)
    yc, s_p = _ret_prompt(p1.reshape(BATCH, SEQ, -1), cos_p, sin_p, gh_c)
    cos_s, sin_s = _rope_table(8, PAST_LEN)
    qs, ks = _rope_sample(p1s, cos_s, sin_s)
    ycs, s_s = _ret_sample(qs, ks, p1s, to_cols(qs, H_C, DK_C), to_cols(ks, H_C, DK_C),
                           gh_c, state_ret.astype(F32))
    y_p, y_s = _outproj(yc.reshape(mp, W_C), None, ycs, None, wo1, h1, h1s, g_post[1:2], None, 512)

    return (y_p.reshape(BATCH, SEQ, D_MODEL), y_s.reshape(DEC_BATCH, 1, D_MODEL),
            c_p, n_p.reshape(BATCH, H_A, DK_A), m_p[:, :, 0, 0],
            s5r_p, s5i_p, s_p,
            c_s, n_s.reshape(DEC_BATCH, H_A, DK_A), m_s,
            s5r_s, s5i_s, s_s)
```

```python
import functools
import math

import jax
import jax.numpy as jnp
from jax import lax
from jax.experimental import pallas as pl
from jax.experimental.pallas import tpu as pltpu

F32 = jnp.float32
BF16 = jnp.bfloat16

D_MODEL = 2048
BATCH = 4
SEQ = 2048
DEC_BATCH = 128
PAST_LEN = 16384
H_A = 4
DK_A = 256
DV_A = 512
W_A = H_A * DV_A
GROUP_B = 16
G_B = 64
P_B = 64
W_B = G_B * GROUP_B
H_C = 8
DK_C = 256
DV_C = 512
W_C = H_C * DV_C
CHUNK = 256
NORM_EPS = 1e-6
ROPE_BASE = 10000.0
QKV0 = 2 * H_A * DK_A + 3 * W_A
GATE0 = QKV0 + 2 * H_A
SCALE_A = DK_A ** -0.5
SCALE_C = DK_C ** -0.5
LOG_GAMMA = tuple(math.log1p(-(2.0 ** (-5.0 - h))) for h in range(H_C))
S5_T = 16
S5_W = S5_T * GROUP_B
LANES = 128
OCT = LANES // GROUP_B
S5_LEVELS = int(math.log2(SEQ // S5_T))
S5_A1_ROW = S5_LEVELS
VMEM_LIMIT = 48 * 1024 * 1024
BIG_VMEM_LIMIT = 56 * 1024 * 1024


def _cparams(*sem):
    return pltpu.CompilerParams(dimension_semantics=sem, vmem_limit_bytes=VMEM_LIMIT)


def _dot(a, b):
    return jnp.dot(a, b, preferred_element_type=F32)


def _dot_nt(a, b):
    return lax.dot_general(a, b, (((1,), (1,)), ((), ())), preferred_element_type=F32)


def _dot_tn(a, b):
    return lax.dot_general(a, b, (((0,), (0,)), ((), ())), preferred_element_type=F32)


def _split(x):
    hi = x.astype(BF16)
    return hi, (x - hi.astype(F32)).astype(BF16)


def _dot3(a, b):
    a_hi, a_lo = _split(a)
    b_hi, b_lo = _split(b)
    return _dot(a_hi, b_hi) + _dot(a_hi, b_lo) + _dot(a_lo, b_hi)


def _log_sigmoid(x):
    return jnp.minimum(x, 0.0) - jnp.log1p(jnp.exp(-jnp.abs(x)))


def _silu(x):
    return x * jax.nn.sigmoid(x)


def _gelu_tanh(x):
    return 0.5 * x * (1.0 + jnp.tanh(math.sqrt(2.0 / math.pi) * (x + 0.044715 * (x * x * x))))


def _rms(x, g):
    return x * lax.rsqrt(jnp.mean(x * x, axis=-1, keepdims=True) + NORM_EPS) * g


def _norm_gates_kernel(x_ref, g_ref, wg_ref, a_ref, gates_ref):
    a = _rms(x_ref[...], g_ref[...])
    a_ref[...] = a.astype(BF16)
    a_hi, a_lo = _split(a)
    w_hi, w_lo = _split(wg_ref[...])
    gates_ref[...] = _dot_nt(a_hi, w_hi) + _dot_nt(a_hi, w_lo) + _dot_nt(a_lo, w_hi)


def _norm_gates(x, g, wg, tm):
    m = x.shape[0]
    return pl.pallas_call(
        _norm_gates_kernel,
        grid=(m // tm,),
        in_specs=[pl.BlockSpec((tm, D_MODEL), lambda i: (i, 0)),
                  pl.BlockSpec((1, D_MODEL), lambda i: (0, 0)),
                  pl.BlockSpec((LANES, D_MODEL), lambda i: (0, 0))],
        out_specs=[pl.BlockSpec((tm, D_MODEL), lambda i: (i, 0)),
                   pl.BlockSpec((tm, LANES), lambda i: (i, 0))],
        out_shape=[jax.ShapeDtypeStruct((m, D_MODEL), BF16),
                   jax.ShapeDtypeStruct((m, LANES), F32)],
        compiler_params=_cparams("parallel"),
        name="norm_gates",
    )(x, g, wg)


def _proj_kernel(xp_ref, xs_ref, w_ref, op_ref, os_ref, wb_ref, *, w_transposed):
    mm = _dot_nt if w_transposed else _dot

    @pl.when(pl.program_id(1) == 0)
    def _():
        wb_ref[...] = w_ref[...].astype(BF16)
        os_ref[...] = mm(xs_ref[...], wb_ref[...])

    op_ref[...] = mm(xp_ref[...], wb_ref[...]).astype(op_ref.dtype)


def _proj(xp, xs, w, n_cols, out_dtype, tm, tn, w_transposed):
    m, k = xp.shape
    ms = xs.shape[0]
    if w_transposed:
        w_spec = pl.BlockSpec((tn, k), lambda j, i: (j, 0))
        wb_shape = (tn, k)
    else:
        w_spec = pl.BlockSpec((k, tn), lambda j, i: (0, j))
        wb_shape = (k, tn)
    return pl.pallas_call(
        functools.partial(_proj_kernel, w_transposed=w_transposed),
        grid=(n_cols // tn, m // tm),
        in_specs=[pl.BlockSpec((tm, k), lambda j, i: (i, 0)),
                  pl.BlockSpec((ms, k), lambda j, i: (0, 0)),
                  w_spec],
        out_specs=[pl.BlockSpec((tm, tn), lambda j, i: (i, j)),
                   pl.BlockSpec((ms, tn), lambda j, i: (0, j))],
        out_shape=[jax.ShapeDtypeStruct((m, n_cols), out_dtype),
                   jax.ShapeDtypeStruct((ms, n_cols), F32)],
        scratch_shapes=[pltpu.VMEM(wb_shape, BF16)],
        compiler_params=_cparams("parallel", "arbitrary"),
        name="proj",
    )(xp, xs, w)


def _proj_rider_kernel(*refs, n_in, n_out, body, batch_of_step, n_i, w_transposed):
    xp_ref, w_ref = refs[:2]
    rider_in = refs[2:2 + n_in]
    op_ref = refs[2 + n_in]
    rider_out = refs[3 + n_in:3 + n_in + n_out]
    wb_ref = refs[-1]
    mm = _dot_nt if w_transposed else _dot
    i = pl.program_id(1)

    @pl.when(i == 0)
    def _():
        wb_ref[...] = w_ref[...].astype(BF16)

    op_ref[...] = mm(xp_ref[...], wb_ref[...]).astype(op_ref.dtype)
    body(batch_of_step(pl.program_id(0) * n_i + i), *rider_in, *rider_out)


def _proj_rider(xp, w, n_cols, out_dtype, tm, tn, w_transposed, body, batch_of_step, rider):
    m, k = xp.shape
    n_i = m // tm
    bmap = lambda j, i: batch_of_step(j * n_i + i)
    r_args, r_in_specs, r_out_shapes, r_out_specs = rider(bmap)
    if w_transposed:
        w_spec = pl.BlockSpec((tn, k), lambda j, i: (j, 0))
        wb_shape = (tn, k)
    else:
        w_spec = pl.BlockSpec((k, tn), lambda j, i: (0, j))
        wb_shape = (k, tn)
    return pl.pallas_call(
        functools.partial(_proj_rider_kernel, n_in=len(r_args), n_out=len(r_out_shapes), body=body,
                          batch_of_step=batch_of_step, n_i=n_i, w_transposed=w_transposed),
        grid=(n_cols // tn, n_i),
        in_specs=[pl.BlockSpec((tm, k), lambda j, i: (i, 0)), w_spec] + r_in_specs,
        out_specs=[pl.BlockSpec((tm, tn), lambda j, i: (i, j))] + r_out_specs,
        out_shape=[jax.ShapeDtypeStruct((m, n_cols), out_dtype)] + r_out_shapes,
        scratch_shapes=[pltpu.VMEM(wb_shape, BF16)],
        compiler_params=pltpu.CompilerParams(dimension_semantics=("arbitrary", "arbitrary"),
                                             vmem_limit_bytes=BIG_VMEM_LIMIT),
        name="proj_rider",
    )(xp, w, *r_args)


def _outproj_kernel(*refs, two, next_norm):
    refs = list(refs)
    ya_ref = refs.pop(0)
    yb_ref = refs.pop(0) if two else None
    yas_ref = refs.pop(0)
    ybs_ref = refs.pop(0) if two else None
    wa_ref = refs.pop(0)
    wb_ref = refs.pop(0) if two else None
    h_ref, hs_ref, g_ref = refs.pop(0), refs.pop(0), refs.pop(0)
    gn_ref = refs.pop(0) if next_norm else None
    o_ref, os_ref = refs.pop(0), refs.pop(0)
    a_ref, as_ref = (refs.pop(0), refs.pop(0)) if next_norm else (None, None)

    def run(x_ref, b_ref, res_ref, out_ref, nxt_ref):
        mix = _dot(x_ref[...].astype(BF16), wa_ref[...])
        if two:
            mix = mix + _dot(b_ref[...].astype(BF16), wb_ref[...])
        new = res_ref[...] + _rms(mix, g_ref[...])
        out_ref[...] = new
        if next_norm:
            nxt_ref[...] = _rms(new, gn_ref[...]).astype(BF16)

    @pl.when(pl.program_id(0) == 0)
    def _():
        run(yas_ref, ybs_ref, hs_ref, os_ref, as_ref)

    run(ya_ref, yb_ref, h_ref, o_ref, a_ref)


def _outproj(ya, yb, yas, ybs, w, h, hs, g, g_next, tm):
    m = h.shape[0]
    ms = hs.shape[0]
    ka = ya.shape[1]
    two = yb is not None
    next_norm = g_next is not None
    once = dict(pipeline_mode=pl.Buffered(1))
    row = lambda width: pl.BlockSpec((tm, width), lambda i: (i, 0))
    fixed = lambda rows, width: pl.BlockSpec((rows, width), lambda i: (0, 0), **once)
    if two:
        kb = yb.shape[1]
        in_specs = [row(ka), row(kb), fixed(ms, ka), fixed(ms, kb), fixed(ka, D_MODEL),
                    pl.BlockSpec((kb, D_MODEL), lambda i: (ka // kb, 0), **once)]
        args = [ya, yb, yas, ybs, w, w]
    else:
        in_specs = [row(ka), fixed(ms, ka), fixed(ka, D_MODEL)]
        args = [ya, yas, w]
    in_specs += [row(D_MODEL), fixed(ms, D_MODEL), fixed(1, D_MODEL)]
    args += [h, hs, g]
    sample_out = pl.BlockSpec((ms, D_MODEL), lambda i: (0, 0))
    out_specs = [row(D_MODEL), sample_out]
    out_shape = [jax.ShapeDtypeStruct((m, D_MODEL), F32), jax.ShapeDtypeStruct((ms, D_MODEL), F32)]
    if next_norm:
        in_specs.append(fixed(1, D_MODEL))
        args.append(g_next)
        out_specs += [row(D_MODEL), sample_out]
        out_shape += [jax.ShapeDtypeStruct((m, D_MODEL), BF16), jax.ShapeDtypeStruct((ms, D_MODEL), BF16)]
    return pl.pallas_call(
        functools.partial(_outproj_kernel, two=two, next_norm=next_norm),
        grid=(m // tm,),
        in_specs=in_specs,
        out_specs=out_specs,
        out_shape=out_shape,
        compiler_params=pltpu.CompilerParams(dimension_semantics=("arbitrary",),
                                             vmem_limit_bytes=BIG_VMEM_LIMIT),
        name="outproj",
    )(*args)


def _mlstm_prompt_kernel(bg_ref, q_ref, k_ref, v_ref, o_ref, z_ref, gc_ref, gr_ref, gh_ref,
                         y_ref, c_out, n_out, m_out, c_s, n_s, m_s):
    c = pl.program_id(1)
    t = CHUNK

    @pl.when(c == 0)
    def _():
        c_s[...] = jnp.zeros_like(c_s)
        n_s[...] = jnp.zeros_like(n_s)
        m_s[...] = jnp.zeros_like(m_s)

    row = lax.broadcasted_iota(jnp.int32, (t, t), 0)
    col = lax.broadcasted_iota(jnp.int32, (t, t), 1)
    tril = col <= row
    triu = row <= col
    gc = gc_ref[...]
    gr = gr_ref[...]
    for h in range(H_A):
        b_i = bg_ref[h]
        b_f = bg_ref[H_A + h]
        i_col = gc[:, h:h + 1] + b_i
        i_row = gr[h:h + 1, :] + b_i
        lf_col = _log_sigmoid(gc[:, H_A + h:H_A + h + 1] + b_f)
        lf_row = _log_sigmoid(gr[H_A + h:H_A + h + 1, :] + b_f)
        b_col = jnp.sum(jnp.where(tril, lf_row, 0.0), axis=1, keepdims=True)
        b_row = jnp.sum(jnp.where(triu, lf_col, 0.0), axis=0, keepdims=True)
        m_prev = m_s[h][:, 0:1]
        d = jnp.where(tril, b_col - b_row + i_row, -jnp.inf)
        inter = b_col + m_prev
        m_t = jnp.maximum(inter, jnp.max(d, axis=1, keepdims=True))
        w_intra = jnp.exp(d - m_t)
        w_inter = jnp.exp(inter - m_t) * SCALE_A
        q = q_ref[:, h * DK_A:(h + 1) * DK_A]
        k = k_ref[:, h * DK_A:(h + 1) * DK_A]
        v = v_ref[:, h * DV_A:(h + 1) * DV_A]
        s = _dot_nt(q, k) * (w_intra * SCALE_A)
        c_old = c_s[h]
        n_old = n_s[h]
        num = _dot(s.astype(BF16), v) + w_inter * _dot(q, c_old.astype(BF16))
        qn = jnp.sum(q.astype(F32) * n_old, axis=1, keepdims=True)
        den = jnp.sum(s, axis=1, keepdims=True) + w_inter * qn
        hh = num * (1.0 / jnp.maximum(jnp.abs(den), jnp.exp(-m_t)))
        hn = _rms(hh, gh_ref[h:h + 1, :])
        o = o_ref[:, h * DV_A:(h + 1) * DV_A].astype(F32)
        z = z_ref[:, h * DV_A:(h + 1) * DV_A].astype(F32)
        gate = z / ((1.0 + jnp.exp(-o)) * (1.0 + jnp.exp(-z)))
        y_ref[:, h * DV_A:(h + 1) * DV_A] = (hn * gate).astype(BF16)
        b_last = b_col[t - 1:t, :]
        g_col = b_last - b_col + i_col
        m_new = jnp.maximum(b_last + m_prev, jnp.max(g_col, axis=0, keepdims=True))
        e_col = jnp.exp(g_col - m_new)
        decay = jnp.exp(b_last + m_prev - m_new)
        ke = k.astype(F32) * e_col
        c_s[h] = decay * c_old + _dot_tn(ke.astype(BF16), v)
        n_s[h] = decay * n_old + jnp.sum(ke, axis=0, keepdims=True)
        m_s[h] = jnp.broadcast_to(m_new, (1, LANES))

    @pl.when(c == pl.num_programs(1) - 1)
    def _():
        c_out[...] = c_s[...]
        n_out[...] = n_s[...]
        m_out[...] = m_s[...]


def _mlstm_prompt(bg, proj, gates_col, gates_row, g_head):
    nc = SEQ // CHUNK
    t = CHUNK
    qk_w = H_A * DK_A
    return pl.pallas_call(
        _mlstm_prompt_kernel,
        grid=(BATCH, nc),
        in_specs=[pl.BlockSpec(memory_space=pltpu.SMEM),
                  pl.BlockSpec((None, t, qk_w), lambda b, c: (b, c, 0)),
                  pl.BlockSpec((None, t, qk_w), lambda b, c: (b, c, 1)),
                  pl.BlockSpec((None, t, W_A), lambda b, c: (b, c, 1)),
                  pl.BlockSpec((None, t, W_A), lambda b, c: (b, c, 2)),
                  pl.BlockSpec((None, t, W_A), lambda b, c: (b, c, 3)),
                  pl.BlockSpec((None, t, LANES), lambda b, c: (b, c, 0)),
                  pl.BlockSpec((None, 2 * H_A, t), lambda b, c: (b, 0, c)),
                  pl.BlockSpec((H_A, DV_A), lambda b, c: (0, 0))],
        out_specs=[pl.BlockSpec((None, t, W_A), lambda b, c: (b, c, 0)),
                   pl.BlockSpec((None, H_A, DK_A, DV_A), lambda b, c: (b, 0, 0, 0)),
                   pl.BlockSpec((None, H_A, 1, DK_A), lambda b, c: (b, 0, 0, 0)),
                   pl.BlockSpec((None, H_A, 1, LANES), lambda b, c: (b, 0, 0, 0))],
        out_shape=[jax.ShapeDtypeStruct((BATCH, SEQ, W_A), BF16),
                   jax.ShapeDtypeStruct((BATCH, H_A, DK_A, DV_A), F32),
                   jax.ShapeDtypeStruct((BATCH, H_A, 1, DK_A), F32),
                   jax.ShapeDtypeStruct((BATCH, H_A, 1, LANES), F32)],
        scratch_shapes=[pltpu.VMEM((H_A, DK_A, DV_A), F32),
                        pltpu.VMEM((H_A, 1, DK_A), F32),
                        pltpu.VMEM((H_A, 1, LANES), F32)],
        compiler_params=_cparams("parallel", "arbitrary"),
        name="mlstm_prompt",
    )(bg, proj, proj, proj, proj, proj, gates_col, gates_row, g_head)


def _mlstm_sample_body(r0, gi_ref, gf_ref, bi_ref, bf_ref, m0_ref, q_ref, k_ref, v_ref, o_ref, z_ref,
                       qt_ref, kt_ref, n0_ref, gh_ref, c0_ref,
                       y_ref, c1_ref, n1_ref, m1_ref):
    shift = lax.rem(DEC_BATCH - r0, DEC_BATCH)
    q_cols = [pltpu.roll(qt_ref[h], shift, axis=1) for h in range(H_A)]
    k_cols = [pltpu.roll(kt_ref[h], shift, axis=1) for h in range(H_A)]
    for j in range(c0_ref.shape[0]):
        rows = pl.ds(r0 + j, 1)
        i_v = gi_ref[rows, :] + bi_ref[...]
        lf_v = _log_sigmoid(gf_ref[rows, :] + bf_ref[...])
        m0_v = m0_ref[rows, :]
        m_t = jnp.maximum(lf_v + m0_v, i_v)
        w_in = jnp.exp(i_v - m_t)
        w_st = jnp.exp(lf_v + m0_v - m_t)
        floor = jnp.exp(-m_t)
        m1_ref[rows, :] = m_t
        for h in range(H_A):
            ks = slice(h * DK_A, (h + 1) * DK_A)
            vs = slice(h * DV_A, (h + 1) * DV_A)
            wi = w_in[:, h:h + 1]
            ws = w_st[:, h:h + 1]
            q_col = q_cols[h][:, j:j + 1]
            k_col = k_cols[h][:, j:j + 1]
            q_row = q_ref[rows, ks]
            k_row = k_ref[rows, ks]
            v_row = v_ref[rows, vs]
            n_row = n0_ref[rows, ks]
            c_old = c0_ref[j, h]
            qk = jnp.sum(q_row * k_row, axis=1, keepdims=True) * SCALE_A
            s = qk * wi
            q_c = jnp.sum(c_old * q_col, axis=0, keepdims=True) * SCALE_A
            qn = jnp.sum(q_row * n_row, axis=1, keepdims=True) * SCALE_A
            num = s * v_row + ws * q_c
            den = s + ws * qn
            hh = num / jnp.maximum(jnp.abs(den), floor[:, h:h + 1])
            hn = _rms(hh, gh_ref[h:h + 1, :])
            y_ref[rows, vs] = hn * jax.nn.sigmoid(o_ref[rows, vs]) * _silu(z_ref[rows, vs])
            c1_ref[j, h] = ws * c_old + (wi * k_col) * v_row
            n1_ref[rows, ks] = ws * n_row + wi * k_row


def _mlstm_sample_rider(gi, gf, bi, bf, m0, proj, qt, kt, n0, g_head, c0):
    qk_w = H_A * DK_A

    def build(bmap):
        once = dict(pipeline_mode=pl.Buffered(1))
        full = lambda shape: pl.BlockSpec(shape, lambda j, i: (0,) * len(shape))
        cols = lambda width, c: pl.BlockSpec((DEC_BATCH, width), lambda j, i: (0, c), **once)
        state = pl.BlockSpec((1, H_A, DK_A, DV_A), lambda j, i: (bmap(j, i), 0, 0, 0))
        lanes = pl.BlockSpec((H_A, DK_A, DEC_BATCH), lambda j, i: (0, 0, 0), **once)
        args = [gi, gf, bi, bf, m0, proj, proj, proj, proj, proj, qt, kt, n0, g_head, c0]
        in_specs = [full((DEC_BATCH, H_A)), full((DEC_BATCH, H_A)), full((1, H_A)), full((1, H_A)),
                    full((DEC_BATCH, H_A)),
                    cols(qk_w, 0), cols(qk_w, 1), cols(W_A, 1), cols(W_A, 2), cols(W_A, 3),
                    lanes, lanes, cols(qk_w, 0), full((H_A, DV_A)), state]
        out_shapes = [jax.ShapeDtypeStruct((DEC_BATCH, W_A), F32),
                      jax.ShapeDtypeStruct((DEC_BATCH, H_A, DK_A, DV_A), F32),
                      jax.ShapeDtypeStruct((DEC_BATCH, qk_w), F32),
                      jax.ShapeDtypeStruct((DEC_BATCH, H_A), F32)]
        out_specs = [full((DEC_BATCH, W_A)), state, full((DEC_BATCH, qk_w)), full((DEC_BATCH, H_A))]
        return args, in_specs, out_shapes, out_specs

    return build


def _s5_build_kernel(ldt_ref, lamr_ref, lami_ref, b1_ref, cc_ref, dsk_ref,
                     m_ref, w_ref, v_ref, p1_ref, p2_ref):
    dt = jnp.exp(ldt_ref[...])
    lam_re, lam_im = lamr_ref[...], lami_ref[...]
    mag = jnp.exp(lam_re * dt)
    ang = lam_im * dt
    a_re, a_im = mag * jnp.cos(ang), mag * jnp.sin(ang)
    den = lam_re * lam_re + lam_im * lam_im
    f_re = ((a_re - 1.0) * lam_re + a_im * lam_im) / den
    f_im = (a_im * lam_re - (a_re - 1.0) * lam_im) / den
    pows = [(jnp.ones_like(a_re), jnp.zeros_like(a_im))]
    for _ in range(S5_T):
        r_re, r_im = pows[-1]
        pows.append((r_re * a_re - r_im * a_im, r_re * a_im + r_im * a_re))

    def tall(vals):
        return jnp.concatenate([jnp.broadcast_to(x, (GROUP_B, LANES)) for x in vals], axis=0)

    af_re = tall([pows[S5_T - 1 - s][0] * f_re - pows[S5_T - 1 - s][1] * f_im for s in range(S5_T)])
    af_im = tall([pows[S5_T - 1 - s][0] * f_im + pows[S5_T - 1 - s][1] * f_re for s in range(S5_T)])
    lane = lax.broadcasted_iota(jnp.int32, (1, LANES), 1)
    sgn_lane = jnp.where(lane < P_B, -1.0, 1.0)
    b1 = b1_ref[...]
    b2 = sgn_lane * pltpu.roll(b1, P_B, axis=1)
    w_ref[...] = (af_re * b1 + af_im * b2).astype(BF16)

    eye = (lax.broadcasted_iota(jnp.int32, (LANES, LANES), 0)
           == lax.broadcasted_iota(jnp.int32, (LANES, LANES), 1))

    def column(row):
        return jnp.sum(jnp.where(eye, row, 0.0), axis=1, keepdims=True)

    ac_re, ac_im, fc_re, fc_im = column(a_re), column(a_im), column(f_re), column(f_im)
    kid = lax.broadcasted_iota(jnp.int32, (S5_T, LANES), 0)
    expand = (lax.broadcasted_iota(jnp.int32, (S5_T, S5_W), 1) // GROUP_B
              == lax.broadcasted_iota(jnp.int32, (S5_T, S5_W), 0)).astype(BF16)

    def lane_blocks(part):
        stacked = jnp.zeros((S5_T, LANES), F32)
        for k in range(S5_T):
            stacked = jnp.where(kid == k, pows[k][part], stacked)
        hi = stacked.astype(BF16)
        rest = stacked - hi.astype(F32)
        mid = rest.astype(BF16)
        lo = (rest - mid.astype(F32)).astype(BF16)
        return _dot_tn(hi, expand) + _dot_tn(mid, expand) + _dot_tn(lo, expand)

    q_re, q_im = lane_blocks(0), lane_blocks(1)
    cc = cc_ref[...]
    cs = pltpu.roll(cc, P_B, axis=0)
    rowi = lax.broadcasted_iota(jnp.int32, (LANES, 1), 0)
    sgn_row = jnp.where(rowi < P_B, 1.0, -1.0)

    def readout(r_re, r_im):
        return sgn_row * (r_re * cc) - r_im * cs

    v_ref[...] = readout(q_re * ac_re - q_im * ac_im, q_re * ac_im + q_im * ac_re).astype(BF16)
    vf = readout(q_re * fc_re - q_im * fc_im, q_re * fc_im + q_im * fc_re)
    kw = _dot3(b1[0:GROUP_B, :], vf)
    lane_w = lax.broadcasted_iota(jnp.int32, (GROUP_B, S5_W), 1)
    blocks = [kw]
    for s in range(1, S5_T):
        blocks.append(jnp.where(lane_w >= s * GROUP_B, pltpu.roll(kw, s * GROUP_B, axis=1), 0.0))
    toep = jnp.concatenate(blocks, axis=0)
    ri = lax.broadcasted_iota(jnp.int32, (S5_W, S5_W), 0)
    ci = lax.broadcasted_iota(jnp.int32, (S5_W, S5_W), 1)
    m_ref[...] = (toep + jnp.where(ri == ci, dsk_ref[...], 0.0)).astype(BF16)

    r_re, r_im = pows[S5_T]
    rid = lax.broadcasted_iota(jnp.int32, (8, LANES), 0)
    p1 = jnp.where(rid == S5_A1_ROW, a_re, 0.0)
    p2 = jnp.where(rid == S5_A1_ROW, sgn_lane * a_im, 0.0)
    for kk in range(S5_LEVELS):
        p1 = jnp.where(rid == kk, r_re, p1)
        p2 = jnp.where(rid == kk, sgn_lane * r_im, p2)
        r_re, r_im = r_re * r_re - r_im * r_im, 2.0 * (r_re * r_im)
    p1_ref[...] = p1
    p2_ref[...] = p2


def _s5_build(ldt, lamr, lami, b1, cc, dsk):
    g3 = lambda a, b: pl.BlockSpec((None, a, b), lambda g: (g, 0, 0))
    return pl.pallas_call(
        _s5_build_kernel,
        grid=(G_B,),
        in_specs=[g3(1, 1), g3(1, LANES), g3(1, LANES),
                  g3(S5_W, LANES), g3(LANES, S5_W), g3(1, S5_W)],
        out_specs=[g3(S5_W, S5_W), g3(S5_W, LANES), g3(LANES, S5_W), g3(8, LANES), g3(8, LANES)],
        out_shape=[jax.ShapeDtypeStruct((G_B, S5_W, S5_W), BF16),
                   jax.ShapeDtypeStruct((G_B, S5_W, LANES), BF16),
                   jax.ShapeDtypeStruct((G_B, LANES, S5_W), BF16),
                   jax.ShapeDtypeStruct((G_B, 8, LANES), F32),
                   jax.ShapeDtypeStruct((G_B, 8, LANES), F32)],
        compiler_params=_cparams("parallel"),
        name="s5_build",
    )(ldt, lamr, lami, b1, cc, dsk)


def _s5_prompt_kernel(u_ref, m_ref, w_ref, v_ref, p1_ref, p2_ref, y_ref, xf_ref, y_s):
    nblk = SEQ // S5_T
    rows = BATCH * nblk
    bidx = lax.broadcasted_iota(jnp.int32, (rows, LANES), 0) & (nblk - 1)
    steps = [u_ref[pl.ds(s, rows, stride=S5_T), :] for s in range(S5_T)]
    for g in range(OCT):
        gl = slice(g * GROUP_B, (g + 1) * GROUP_B)
        u = jnp.concatenate([x[:, gl] for x in steps], axis=-1).astype(BF16)
        s = _dot(u, w_ref[g])
        for kk in range(S5_LEVELS):
            sh = 1 << kk
            r = jnp.where(bidx >= sh, pltpu.roll(s, sh, axis=0), 0.0)
            s = s + p1_ref[g, kk:kk + 1, :] * r + p2_ref[g, kk:kk + 1, :] * pltpu.roll(r, P_B, axis=1)
        x_prev = jnp.where(bidx >= 1, pltpu.roll(s, 1, axis=0), 0.0)
        y = _dot(u, m_ref[g]) + _dot(x_prev.astype(BF16), v_ref[g])
        y_s[g] = _gelu_tanh(y)
        for b in range(BATCH):
            xf_ref[g, b:b + 1, :] = s[(b + 1) * nblk - 1:(b + 1) * nblk, :]
    for t in range(S5_T):
        half = slice((t // OCT) * LANES, (t // OCT + 1) * LANES)
        tl = slice((t % OCT) * GROUP_B, (t % OCT + 1) * GROUP_B)
        y_ref[pl.ds(t, rows, stride=S5_T), :] = jnp.concatenate(
            [y_s[g, :, half][:, tl] for g in range(OCT)], axis=-1)


def _s5_prompt(u, m, w, v, p1, p2):
    mp = BATCH * SEQ
    rows = BATCH * (SEQ // S5_T)
    o3 = lambda a, b: pl.BlockSpec((OCT, a, b), lambda g: (g, 0, 0))
    return pl.pallas_call(
        _s5_prompt_kernel,
        grid=(G_B // OCT,),
        in_specs=[pl.BlockSpec((mp, LANES), lambda g: (0, g)),
                  o3(S5_W, S5_W), o3(S5_W, LANES), o3(LANES, S5_W), o3(8, LANES), o3(8, LANES)],
        out_specs=[pl.BlockSpec((mp, LANES), lambda g: (0, g)), o3(BATCH, LANES)],
        out_shape=[jax.ShapeDtypeStruct((mp, W_B), F32),
                   jax.ShapeDtypeStruct((G_B, BATCH, LANES), F32)],
        scratch_shapes=[pltpu.VMEM((OCT, rows, S5_W), F32)],
        compiler_params=_cparams("parallel"),
        name="s5_prompt",
    )(u, m, w, v, p1, p2)


def _s5_sample_kernel(u_ref, xr_ref, xi_ref, m_ref, w_ref, v_ref, p1_ref, p2_ref,
                      y_ref, x1r_ref, x1i_ref):
    lane = lax.broadcasted_iota(jnp.int32, (DEC_BATCH, LANES), 1)
    last = LANES - GROUP_B
    half = S5_W // 2
    a1 = slice(S5_A1_ROW, S5_A1_ROW + 1)
    u_all = u_ref[...]
    ys = []
    for g in range(OCT):
        ps = slice(g * P_B, (g + 1) * P_B)
        u = jnp.where(lane >= last, pltpu.roll(u_all, (last - g * GROUP_B) % LANES, axis=1), 0.0)
        u = u.astype(BF16)
        x0 = jnp.concatenate([xr_ref[:, ps], xi_ref[:, ps]], axis=-1)
        x1 = (p1_ref[g, a1, :] * x0 + p2_ref[g, a1, :] * pltpu.roll(x0, P_B, axis=1)
              + _dot(u, w_ref[g, half:, :]))
        x1r_ref[:, ps] = x1[:, :P_B]
        x1i_ref[:, ps] = x1[:, P_B:]
        y = (_dot(x0.astype(BF16), v_ref[g])[:, 0:GROUP_B]
             + _dot(u, m_ref[g, half:, :])[:, S5_W - GROUP_B:])
        ys.append(_gelu_tanh(y))
    y_ref[...] = jnp.concatenate(ys, axis=-1)


def _s5_sample(u, xr, xi, m, w, v, p1, p2):
    o3 = lambda a, b: pl.BlockSpec((OCT, a, b), lambda g: (g, 0, 0))
    tile = pl.BlockSpec((DEC_BATCH, LANES), lambda g: (0, g))
    st = pl.BlockSpec((DEC_BATCH, OCT * P_B), lambda g: (0, g))
    st_shape = jax.ShapeDtypeStruct((DEC_BATCH, G_B * P_B), F32)
    return pl.pallas_call(
        _s5_sample_kernel,
        grid=(G_B // OCT,),
        in_specs=[tile, st, st, o3(S5_W, S5_W), o3(S5_W, LANES), o3(LANES, S5_W), o3(8, LANES), o3(8, LANES)],
        out_specs=[tile, st, st],
        out_shape=[jax.ShapeDtypeStruct((DEC_BATCH, W_B), F32), st_shape, st_shape],
        compiler_params=_cparams("parallel"),
        name="s5_sample",
    )(u, xr, xi, m, w, v, p1, p2)


def _glu_kernel(y_ref, w_ref, b_ref, z_ref, o_ref):
    y = y_ref[...]
    gate = jax.nn.sigmoid(_dot(y.astype(BF16), w_ref[...]) + b_ref[...])
    o_ref[...] = (y.astype(F32) * gate * _silu(z_ref[...].astype(F32))).astype(o_ref.dtype)


def _glu(y, w, b, proj_b, tm):
    m = y.shape[0]
    return pl.pallas_call(
        _glu_kernel,
        grid=(m // tm,),
        in_specs=[pl.BlockSpec((tm, W_B), lambda i: (i, 0)),
                  pl.BlockSpec((W_B, W_B), lambda i: (0, 0)),
                  pl.BlockSpec((1, W_B), lambda i: (0, 0)),
                  pl.BlockSpec((tm, W_B), lambda i: (i, 1))],
        out_specs=pl.BlockSpec((tm, W_B), lambda i: (i, 0)),
        out_shape=jax.ShapeDtypeStruct((m, W_B), BF16),
        compiler_params=_cparams("parallel"),
        name="glu",
    )(y, w, b, proj_b)


def _rope_table_kernel(cos_ref, sin_ref, *, pos0):
    shape = cos_ref.shape
    pos = lax.broadcasted_iota(jnp.int32, shape, 0).astype(F32) + pos0
    j = lax.broadcasted_iota(jnp.int32, shape, 1).astype(F32)
    ang = pos * jnp.power(ROPE_BASE, -(j / (DK_C // 2)))
    cos_ref[...] = jnp.cos(ang)
    sin_ref[...] = jnp.sin(ang)


def _rope_table(rows, pos0):
    shape = jax.ShapeDtypeStruct((rows, DK_C // 2), F32)
    return pl.pallas_call(functools.partial(_rope_table_kernel, pos0=float(pos0)),
                          out_shape=[shape, shape], name="rope_table")()


def _rope(x, cos, sin):
    half = DK_C // 2
    x1, x2 = x[:, :half], x[:, half:]
    return jnp.concatenate([x1 * cos - x2 * sin, x1 * sin + x2 * cos], axis=-1)


def _ret_prompt_kernel(q_ref, k_ref, v_ref, z_ref, cos_ref, sin_ref, gh_ref, y_ref, s_out, s_s):
    c = pl.program_id(1)
    t = CHUNK

    @pl.when(c == 0)
    def _():
        s_s[...] = jnp.zeros_like(s_s)

    row = lax.broadcasted_iota(jnp.int32, (t, t), 0)
    col = lax.broadcasted_iota(jnp.int32, (t, t), 1)
    tril = col <= row
    diff = (row - col).astype(F32)
    tpos = lax.broadcasted_iota(jnp.int32, (t, 1), 0).astype(F32)
    cos, sin = cos_ref[...], sin_ref[...]
    for h in range(H_C):
        lg = LOG_GAMMA[h]
        ks = slice(h * DK_C, (h + 1) * DK_C)
        vs = slice(h * DV_C, (h + 1) * DV_C)
        q = _rope(q_ref[:, ks].astype(F32), cos, sin).astype(BF16)
        k32 = _rope(k_ref[:, ks].astype(F32), cos, sin)
        v = v_ref[:, vs]
        mask = jnp.where(tril, jnp.exp(diff * lg), 0.0) * SCALE_C
        s = _dot_nt(q, k32.astype(BF16)) * mask
        s_old = s_s[h]
        o = _dot(s.astype(BF16), v) + _dot(q, s_old.astype(BF16)) * jnp.exp((tpos + 1.0) * lg)
        y_ref[:, vs] = (_rms(o, gh_ref[h:h + 1, :]) * _silu(z_ref[:, vs].astype(F32))).astype(BF16)
        k_tail = k32 * (jnp.exp((t - 1.0 - tpos) * lg) * SCALE_C)
        s_s[h] = math.exp(t * lg) * s_old + _dot_tn(k_tail.astype(BF16), v)

    @pl.when(c == pl.num_programs(1) - 1)
    def _():
        s_out[...] = s_s[...]


def _ret_prompt(proj, cos, sin, g_head):
    nc = SEQ // CHUNK
    t = CHUNK
    qk_w = H_C * DK_C
    return pl.pallas_call(
        _ret_prompt_kernel,
        grid=(BATCH, nc),
        in_specs=[pl.BlockSpec((None, t, qk_w), lambda b, c: (b, c, 0)),
                  pl.BlockSpec((None, t, qk_w), lambda b, c: (b, c, 1)),
                  pl.BlockSpec((None, t, W_C), lambda b, c: (b, c, 1)),
                  pl.BlockSpec((None, t, W_C), lambda b, c: (b, c, 2)),
                  pl.BlockSpec((t, DK_C // 2), lambda b, c: (c, 0)),
                  pl.BlockSpec((t, DK_C // 2), lambda b, c: (c, 0)),
                  pl.BlockSpec((H_C, DV_C), lambda b, c: (0, 0))],
        out_specs=[pl.BlockSpec((None, t, W_C), lambda b, c: (b, c, 0)),
                   pl.BlockSpec((None, H_C, DK_C, DV_C), lambda b, c: (b, 0, 0, 0))],
        out_shape=[jax.ShapeDtypeStruct((BATCH, SEQ, W_C), BF16),
                   jax.ShapeDtypeStruct((BATCH, H_C, DK_C, DV_C), F32)],
        scratch_shapes=[pltpu.VMEM((H_C, DK_C, DV_C), F32)],
        compiler_params=_cparams("parallel", "arbitrary"),
        name="ret_prompt",
    )(proj, proj, proj, proj, cos, sin, g_head)


def _rope_sample_kernel(q_ref, k_ref, cos_ref, sin_ref, qo_ref, ko_ref):
    cos, sin = cos_ref[0:1, :], sin_ref[0:1, :]
    for h in range(H_C):
        ks = slice(h * DK_C, (h + 1) * DK_C)
        qo_ref[:, ks] = _rope(q_ref[:, ks], cos, sin)
        ko_ref[:, ks] = _rope(k_ref[:, ks], cos, sin)


def _rope_sample(proj, cos, sin):
    qk_w = H_C * DK_C
    shape = jax.ShapeDtypeStruct((DEC_BATCH, qk_w), F32)
    return pl.pallas_call(
        _rope_sample_kernel,
        grid=(1,),
        in_specs=[pl.BlockSpec((DEC_BATCH, qk_w), lambda i: (0, 0)),
                  pl.BlockSpec((DEC_BATCH, qk_w), lambda i: (0, 1)),
                  pl.BlockSpec((8, DK_C // 2), lambda i: (0, 0)),
                  pl.BlockSpec((8, DK_C // 2), lambda i: (0, 0))],
        out_specs=[pl.BlockSpec((DEC_BATCH, qk_w), lambda i: (0, 0)),
                   pl.BlockSpec((DEC_BATCH, qk_w), lambda i: (0, 0))],
        out_shape=[shape, shape],
        compiler_params=_cparams("arbitrary"),
        name="rope_sample",
    )(proj, proj, cos, sin)


def _ret_sample_body(r0, q_ref, k_ref, v_ref, z_ref, qt_ref, kt_ref, gh_ref, s0_ref, y_ref, s1_ref):
    shift = lax.rem(DEC_BATCH - r0, DEC_BATCH)
    for h in range(H_C):
        gamma = math.exp(LOG_GAMMA[h])
        ks = slice(h * DK_C, (h + 1) * DK_C)
        vs = slice(h * DV_C, (h + 1) * DV_C)
        q_cols = pltpu.roll(qt_ref[h], shift, axis=1)
        k_cols = pltpu.roll(kt_ref[h], shift, axis=1)
        for j in range(s0_ref.shape[0]):
            rows = pl.ds(r0 + j, 1)
            v_row = v_ref[rows, vs]
            s_old = s0_ref[j, h]
            qk = jnp.sum(q_ref[rows, ks] * k_ref[rows, ks], axis=1, keepdims=True) * SCALE_C
            o = qk * v_row + jnp.sum(s_old * q_cols[:, j:j + 1], axis=0, keepdims=True) * gamma
            y_ref[rows, vs] = _rms(o, gh_ref[h:h + 1, :]) * _silu(z_ref[rows, vs])
            s1_ref[j, h] = gamma * s_old + (k_cols[:, j:j + 1] * SCALE_C) * v_row


def _ret_sample_rider(q, k, proj, qt, kt, g_head, s0):
    qk_w = H_C * DK_C

    def build(bmap):
        once = dict(pipeline_mode=pl.Buffered(1))
        fixed = lambda shape, *idx: pl.BlockSpec(shape, lambda j, i: idx or (0,) * len(shape), **once)
        state = pl.BlockSpec((1, H_C, DK_C, DV_C), lambda j, i: (bmap(j, i), 0, 0, 0))
        args = [q, k, proj, proj, qt, kt, g_head, s0]
        in_specs = [fixed((DEC_BATCH, qk_w)), fixed((DEC_BATCH, qk_w)),
                    fixed((DEC_BATCH, W_C), 0, 1), fixed((DEC_BATCH, W_C), 0, 2),
                    fixed((H_C, DK_C, DEC_BATCH)), fixed((H_C, DK_C, DEC_BATCH)),
                    fixed((H_C, DV_C)), state]
        out_shapes = [jax.ShapeDtypeStruct((DEC_BATCH, W_C), F32),
                      jax.ShapeDtypeStruct((DEC_BATCH, H_C, DK_C, DV_C), F32)]
        out_specs = [pl.BlockSpec((DEC_BATCH, W_C), lambda j, i: (0, 0)), state]
        return args, in_specs, out_shapes, out_specs

    return build


def kernel(x_prompt, x_sample, state_mlstm_C, state_mlstm_n, state_mlstm_m, state_s5_re, state_s5_im, state_ret, g_pre, g_post, w_in0, b_gates0, g_head_a, lam_re, lam_im, log_dt, b_re, b_im, c_re, c_im, d_skip, w_glu, b_glu, w_out0, w_in1, g_head_c, w_out1):
    mp = BATCH * SEQ
    xp = x_prompt.reshape(mp, D_MODEL).astype(F32)
    xs = x_sample.reshape(DEC_BATCH, D_MODEL).astype(F32)

    w0t = w_in0.T.astype(F32)
    w0g = jnp.pad(w0t[QKV0:GATE0], ((0, LANES - 2 * H_A), (0, 0)))
    w0b = w0t[GATE0:]
    w1 = w_in1.astype(F32)
    wo0 = w_out0.astype(BF16)
    wo1 = w_out1.astype(BF16)
    wg = w_glu.astype(BF16)
    bglu = b_glu.reshape(1, W_B).astype(F32)
    g_pre = g_pre.astype(F32)
    g_post = g_post.astype(F32)
    bg = b_gates0.astype(F32)
    gh_a = g_head_a.astype(F32)
    gh_c = g_head_c.astype(F32)

    dup = lambda a: jnp.concatenate([a, a], axis=-1).astype(F32)
    lamr, lami = dup(lam_re), dup(lam_im)
    b1 = jnp.concatenate([b_re.transpose(0, 2, 1), b_im.transpose(0, 2, 1)], axis=-1)
    cc = jnp.concatenate([c_re.transpose(0, 2, 1), c_im.transpose(0, 2, 1)], axis=1)
    s5m, s5w, s5v, s5p1, s5p2 = _s5_build(
        log_dt.reshape(G_B, 1, 1).astype(F32), lamr[:, None, :], lami[:, None, :],
        jnp.tile(b1.astype(F32), (1, S5_T, 1)), jnp.tile(cc.astype(F32), (1, 1, S5_T)),
        jnp.tile(d_skip.astype(F32), (1, S5_T))[:, None, :])

    a0, gates = _norm_gates(xp, g_pre[0:1], w0g, 256)
    a0s, gates_s = _norm_gates(xs, g_pre[0:1], w0g, DEC_BATCH)
    pas = _proj(a0s, a0s, w0t, QKV0, F32, DEC_BATCH, 1024, True)[0]
    qk_a = H_A * DK_A
    to_cols = lambda a, nh, dk: a.reshape(DEC_BATCH, nh, dk).transpose(1, 2, 0)
    pa, yas, c_s, n_s, m_s = _proj_rider(
        a0, w0t, QKV0, BF16, 512, 1024, True, _mlstm_sample_body, lambda step: step,
        _mlstm_sample_rider(
            gates_s[:, :H_A], gates_s[:, H_A:2 * H_A], bg[None, :H_A], bg[None, H_A:],
            state_mlstm_m.astype(F32), pas,
            to_cols(pas[:, :qk_a], H_A, DK_A), to_cols(pas[:, qk_a:2 * qk_a], H_A, DK_A),
            state_mlstm_n.reshape(DEC_BATCH, qk_a).astype(F32), gh_a, state_mlstm_C.astype(F32)))
    pb, pbs = _proj(a0, a0s, w0b, 2 * W_B, F32, 1024, 1024, True)
    gates3 = gates.reshape(BATCH, SEQ, LANES)
    gates_row = gates3[:, :, :2 * H_A].transpose(0, 2, 1)
    ya, c_p, n_p, m_p = _mlstm_prompt(bg, pa.reshape(BATCH, SEQ, QKV0), gates3, gates_row, gh_a)
    yb, xf = _s5_prompt(pb, s5m, s5w, s5v, s5p1, s5p2)
    yb = _glu(yb, wg, bglu, pb, 512)
    s5r_p = xf[:, :, :P_B].transpose(1, 0, 2)
    s5i_p = xf[:, :, P_B:].transpose(1, 0, 2)
    ybs, s5r_s, s5i_s = _s5_sample(
        pbs, state_s5_re.reshape(DEC_BATCH, G_B * P_B).astype(F32),
        state_s5_im.reshape(DEC_BATCH, G_B * P_B).astype(F32), s5m, s5w, s5v, s5p1, s5p2)
    s5r_s = s5r_s.reshape(DEC_BATCH, G_B, P_B)
    s5i_s = s5i_s.reshape(DEC_BATCH, G_B, P_B)
    ybs = _glu(ybs, wg, bglu, pbs, DEC_BATCH)
    h1, h1s, a1, a1s = _outproj(ya.reshape(mp, W_A), yb, yas, ybs, wo0, xp, xs, g_post[0:1],
                                g_pre[1:2], 512)

    cols1 = 2 * H_C * DK_C + 2 * W_C
    p1s = _proj(a1s, a1s, w1, cols1, F32, DEC_BATCH, 1024, False)[0]
    cos_s, sin_s = _rope_table(8, PAST_LEN)
    qs, ks = _rope_sample(p1s, cos_s, sin_s)
    spread = lambda step: step - step // 3 - (lax.rem(step, 3) == 2).astype(jnp.int32)
    p1, ycs, s_s = _proj_rider(
        a1, w1, cols1, BF16, 512, 1024, False, _ret_sample_body, spread,
        _ret_sample_rider(qs, ks, p1s, to_cols(qs, H_C, DK_C), to_cols(ks, H_C, DK_C),
                          gh_c, state_ret.astype(F32)))
    cos_p, sin_p = _rope_table(SEQ, 0)
    yc, s_p = _ret_prompt(p1.reshape(BATCH, SEQ, -1), cos_p, sin_p, gh_c)
    y_p, y_s = _outproj(yc.reshape(mp, W_C), None, ycs, None, wo1, h1, h1s, g_post[1:2], None, 512)

    return (y_p.reshape(BATCH, SEQ, D_MODEL), y_s.reshape(DEC_BATCH, 1, D_MODEL),
            c_p, n_p.reshape(BATCH, H_A, DK_A), m_p[:, :, 0, 0],
            s5r_p, s5i_p, s_p,
            c_s, n_s.reshape(DEC_BATCH, H_A, DK_A), m_s,
            s5r_s, s5i_s, s_s)
```

```python
import functools
import math

import jax
import jax.numpy as jnp
from jax import lax
from jax.experimental import pallas as pl
from jax.experimental.pallas import tpu as pltpu

F32 = jnp.float32
BF16 = jnp.bfloat16

D_MODEL = 2048
BATCH = 4
SEQ = 2048
DEC_BATCH = 128
PAST_LEN = 16384
H_A = 4
DK_A = 256
DV_A = 512
W_A = H_A * DV_A
GROUP_B = 16
G_B = 64
P_B = 64
W_B = G_B * GROUP_B
H_C = 8
DK_C = 256
DV_C = 512
W_C = H_C * DV_C
CHUNK = 256
NORM_EPS = 1e-6
ROPE_BASE = 10000.0
QKV0 = 2 * H_A * DK_A + 3 * W_A
GATE0 = QKV0 + 2 * H_A
SCALE_A = DK_A ** -0.5
SCALE_C = DK_C ** -0.5
LOG_GAMMA = tuple(math.log1p(-(2.0 ** (-5.0 - h))) for h in range(H_C))
S5_T = 16
S5_W = S5_T * GROUP_B
LANES = 128
OCT = LANES // GROUP_B
S5_LEVELS = int(math.log2(SEQ // S5_T))
S5_A1_ROW = S5_LEVELS
VMEM_LIMIT = 48 * 1024 * 1024
BIG_VMEM_LIMIT = 56 * 1024 * 1024


def _cparams(*sem):
    return pltpu.CompilerParams(dimension_semantics=sem, vmem_limit_bytes=VMEM_LIMIT)


def _dot(a, b):
    return jnp.dot(a, b, preferred_element_type=F32)


def _dot_nt(a, b):
    return lax.dot_general(a, b, (((1,), (1,)), ((), ())), preferred_element_type=F32)


def _dot_tn(a, b):
    return lax.dot_general(a, b, (((0,), (0,)), ((), ())), preferred_element_type=F32)


def _split(x):
    hi = x.astype(BF16)
    return hi, (x - hi.astype(F32)).astype(BF16)


def _dot3(a, b):
    a_hi, a_lo = _split(a)
    b_hi, b_lo = _split(b)
    return _dot(a_hi, b_hi) + _dot(a_hi, b_lo) + _dot(a_lo, b_hi)


def _log_sigmoid(x):
    return jnp.minimum(x, 0.0) - jnp.log1p(jnp.exp(-jnp.abs(x)))


def _silu(x):
    return x * jax.nn.sigmoid(x)


def _gelu_tanh(x):
    return 0.5 * x * (1.0 + jnp.tanh(math.sqrt(2.0 / math.pi) * (x + 0.044715 * (x * x * x))))


def _rms(x, g):
    return x * lax.rsqrt(jnp.mean(x * x, axis=-1, keepdims=True) + NORM_EPS) * g


def _norm_gates_kernel(x_ref, g_ref, wg_ref, bg_ref, a_ref, gates_ref):
    a = _rms(x_ref[...], g_ref[...])
    a_ref[...] = a.astype(BF16)
    a_hi, a_lo = _split(a)
    w_hi, w_lo = _split(wg_ref[...])
    pre = _dot_nt(a_hi, w_hi) + _dot_nt(a_hi, w_lo) + _dot_nt(a_lo, w_hi) + bg_ref[...]
    lane = lax.broadcasted_iota(jnp.int32, pre.shape, 1)
    gates_ref[...] = jnp.where((lane >= H_A) & (lane < 2 * H_A), _log_sigmoid(pre), pre)


def _norm_gates(x, g, wg, bg, tm):
    m = x.shape[0]
    return pl.pallas_call(
        _norm_gates_kernel,
        grid=(m // tm,),
        in_specs=[pl.BlockSpec((tm, D_MODEL), lambda i: (i, 0)),
                  pl.BlockSpec((1, D_MODEL), lambda i: (0, 0)),
                  pl.BlockSpec((LANES, D_MODEL), lambda i: (0, 0)),
                  pl.BlockSpec((1, LANES), lambda i: (0, 0))],
        out_specs=[pl.BlockSpec((tm, D_MODEL), lambda i: (i, 0)),
                   pl.BlockSpec((tm, LANES), lambda i: (i, 0))],
        out_shape=[jax.ShapeDtypeStruct((m, D_MODEL), BF16),
                   jax.ShapeDtypeStruct((m, LANES), F32)],
        compiler_params=_cparams("parallel"),
        name="norm_gates",
    )(x, g, wg, bg)


def _proj_kernel(xp_ref, xs_ref, w_ref, op_ref, os_ref, wb_ref, *, w_transposed):
    mm = _dot_nt if w_transposed else _dot

    @pl.when(pl.program_id(1) == 0)
    def _():
        wb_ref[...] = w_ref[...].astype(BF16)
        os_ref[...] = mm(xs_ref[...], wb_ref[...])

    op_ref[...] = mm(xp_ref[...], wb_ref[...]).astype(op_ref.dtype)


def _proj(xp, xs, w, n_cols, out_dtype, tm, tn, w_transposed):
    m, k = xp.shape
    ms = xs.shape[0]
    if w_transposed:
        w_spec = pl.BlockSpec((tn, k), lambda j, i: (j, 0))
        wb_shape = (tn, k)
    else:
        w_spec = pl.BlockSpec((k, tn), lambda j, i: (0, j))
        wb_shape = (k, tn)
    return pl.pallas_call(
        functools.partial(_proj_kernel, w_transposed=w_transposed),
        grid=(n_cols // tn, m // tm),
        in_specs=[pl.BlockSpec((tm, k), lambda j, i: (i, 0)),
                  pl.BlockSpec((ms, k), lambda j, i: (0, 0)),
                  w_spec],
        out_specs=[pl.BlockSpec((tm, tn), lambda j, i: (i, j)),
                   pl.BlockSpec((ms, tn), lambda j, i: (0, j))],
        out_shape=[jax.ShapeDtypeStruct((m, n_cols), out_dtype),
                   jax.ShapeDtypeStruct((ms, n_cols), F32)],
        scratch_shapes=[pltpu.VMEM(wb_shape, BF16)],
        compiler_params=_cparams("parallel", "arbitrary"),
        name="proj",
    )(xp, xs, w)


def _proj_rider_kernel(*refs, n_in, n_out, body, batch_of_step, n_i, w_transposed):
    xp_ref, w_ref = refs[:2]
    rider_in = refs[2:2 + n_in]
    op_ref = refs[2 + n_in]
    rider_out = refs[3 + n_in:3 + n_in + n_out]
    mm = _dot_nt if w_transposed else _dot
    i = pl.program_id(1)

    if w_ref.dtype == BF16:
        wb_ref = w_ref
    else:
        wb_ref = refs[-1]

        @pl.when(i == 0)
        def _():
            wb_ref[...] = w_ref[...].astype(BF16)

    op_ref[...] = mm(xp_ref[...], wb_ref[...]).astype(op_ref.dtype)
    body(batch_of_step(pl.program_id(0) * n_i + i), *rider_in, *rider_out)


def _proj_rider(xp, w, n_cols, out_dtype, tm, tn, w_transposed, body, batch_of_step, rider):
    m, k = xp.shape
    n_i = m // tm
    bmap = lambda j, i: batch_of_step(j * n_i + i)
    r_args, r_in_specs, r_out_shapes, r_out_specs = rider(bmap)
    if w_transposed:
        w_spec = pl.BlockSpec((tn, k), lambda j, i: (j, 0))
        wb_shape = (tn, k)
    else:
        w_spec = pl.BlockSpec((k, tn), lambda j, i: (0, j))
        wb_shape = (k, tn)
    return pl.pallas_call(
        functools.partial(_proj_rider_kernel, n_in=len(r_args), n_out=len(r_out_shapes), body=body,
                          batch_of_step=batch_of_step, n_i=n_i, w_transposed=w_transposed),
        grid=(n_cols // tn, n_i),
        in_specs=[pl.BlockSpec((tm, k), lambda j, i: (i, 0)), w_spec] + r_in_specs,
        out_specs=[pl.BlockSpec((tm, tn), lambda j, i: (i, j))] + r_out_specs,
        out_shape=[jax.ShapeDtypeStruct((m, n_cols), out_dtype)] + r_out_shapes,
        scratch_shapes=[] if w.dtype == BF16 else [pltpu.VMEM(wb_shape, BF16)],
        compiler_params=pltpu.CompilerParams(dimension_semantics=("arbitrary", "arbitrary"),
                                             vmem_limit_bytes=BIG_VMEM_LIMIT),
        name="proj_rider",
    )(xp, w, *r_args)


def _outproj_kernel(*refs, two, next_norm):
    refs = list(refs)
    ya_ref = refs.pop(0)
    yb_ref = refs.pop(0) if two else None
    yas_ref = refs.pop(0)
    ybs_ref = refs.pop(0) if two else None
    wa_ref = refs.pop(0)
    wb_ref = refs.pop(0) if two else None
    h_ref, hs_ref, g_ref = refs.pop(0), refs.pop(0), refs.pop(0)
    gn_ref = refs.pop(0) if next_norm else None
    o_ref, os_ref = refs.pop(0), refs.pop(0)
    a_ref, as_ref = (refs.pop(0), refs.pop(0)) if next_norm else (None, None)

    def run(x_ref, b_ref, res_ref, out_ref, nxt_ref):
        mix = _dot(x_ref[...].astype(BF16), wa_ref[...])
        if two:
            mix = mix + _dot(b_ref[...].astype(BF16), wb_ref[...])
        new = res_ref[...] + _rms(mix, g_ref[...])
        out_ref[...] = new
        if next_norm:
            nxt_ref[...] = _rms(new, gn_ref[...]).astype(BF16)

    @pl.when(pl.program_id(0) == 0)
    def _():
        run(yas_ref, ybs_ref, hs_ref, os_ref, as_ref)

    run(ya_ref, yb_ref, h_ref, o_ref, a_ref)


def _outproj(ya, yb, yas, ybs, w, h, hs, g, g_next, tm):
    m = h.shape[0]
    ms = hs.shape[0]
    ka = ya.shape[1]
    two = yb is not None
    next_norm = g_next is not None
    once = dict(pipeline_mode=pl.Buffered(1))
    row = lambda width: pl.BlockSpec((tm, width), lambda i: (i, 0))
    fixed = lambda rows, width: pl.BlockSpec((rows, width), lambda i: (0, 0), **once)
    if two:
        kb = yb.shape[1]
        in_specs = [row(ka), row(kb), fixed(ms, ka), fixed(ms, kb), fixed(ka, D_MODEL),
                    pl.BlockSpec((kb, D_MODEL), lambda i: (ka // kb, 0), **once)]
        args = [ya, yb, yas, ybs, w, w]
    else:
        in_specs = [row(ka), fixed(ms, ka), fixed(ka, D_MODEL)]
        args = [ya, yas, w]
    in_specs += [row(D_MODEL), fixed(ms, D_MODEL), fixed(1, D_MODEL)]
    args += [h, hs, g]
    sample_out = pl.BlockSpec((ms, D_MODEL), lambda i: (0, 0))
    out_specs = [row(D_MODEL), sample_out]
    out_shape = [jax.ShapeDtypeStruct((m, D_MODEL), F32), jax.ShapeDtypeStruct((ms, D_MODEL), F32)]
    if next_norm:
        in_specs.append(fixed(1, D_MODEL))
        args.append(g_next)
        out_specs += [row(D_MODEL), sample_out]
        out_shape += [jax.ShapeDtypeStruct((m, D_MODEL), BF16), jax.ShapeDtypeStruct((ms, D_MODEL), BF16)]
    return pl.pallas_call(
        functools.partial(_outproj_kernel, two=two, next_norm=next_norm),
        grid=(m // tm,),
        in_specs=in_specs,
        out_specs=out_specs,
        out_shape=out_shape,
        compiler_params=pltpu.CompilerParams(dimension_semantics=("arbitrary",),
                                             vmem_limit_bytes=BIG_VMEM_LIMIT),
        name="outproj",
    )(*args)


def _mlstm_prompt_kernel(q_ref, k_ref, v_ref, o_ref, z_ref, gc_ref, gr_ref, gh_ref,
                         y_ref, c_out, n_out, m_out, c_s, n_s, m_s):
    c = pl.program_id(1)
    t = CHUNK

    @pl.when(c == 0)
    def _():
        c_s[...] = jnp.zeros_like(c_s)
        n_s[...] = jnp.zeros_like(n_s)
        m_s[...] = jnp.zeros_like(m_s)

    row = lax.broadcasted_iota(jnp.int32, (t, t), 0)
    col = lax.broadcasted_iota(jnp.int32, (t, t), 1)
    tril = col <= row
    triu = row <= col
    gc = gc_ref[...]
    gr = gr_ref[...]
    for h in range(H_A):
        i_col = gc[:, h:h + 1]
        i_row = gr[h:h + 1, :]
        lf_col = gc[:, H_A + h:H_A + h + 1]
        lf_row = gr[H_A + h:H_A + h + 1, :]
        b_col = jnp.sum(jnp.where(tril, lf_row, 0.0), axis=1, keepdims=True)
        b_row = jnp.sum(jnp.where(triu, lf_col, 0.0), axis=0, keepdims=True)
        m_prev = m_s[h][:, 0:1]
        d = jnp.where(tril, b_col - b_row + i_row, -jnp.inf)
        inter = b_col + m_prev
        m_t = jnp.maximum(inter, jnp.max(d, axis=1, keepdims=True))
        w_intra = jnp.exp(d - m_t)
        w_inter = jnp.exp(inter - m_t) * SCALE_A
        q = q_ref[:, h * DK_A:(h + 1) * DK_A]
        k = k_ref[:, h * DK_A:(h + 1) * DK_A]
        v = v_ref[:, h * DV_A:(h + 1) * DV_A]
        s = _dot_nt(q, k) * (w_intra * SCALE_A)
        c_old = c_s[h]
        n_old = n_s[h]
        num = _dot(s.astype(BF16), v) + w_inter * _dot(q, c_old.astype(BF16))
        qn = jnp.sum(q.astype(F32) * n_old, axis=1, keepdims=True)
        den = jnp.sum(s, axis=1, keepdims=True) + w_inter * qn
        hh = num * (1.0 / jnp.maximum(jnp.abs(den), jnp.exp(-m_t)))
        hn = _rms(hh, gh_ref[h:h + 1, :])
        o = o_ref[:, h * DV_A:(h + 1) * DV_A].astype(F32)
        z = z_ref[:, h * DV_A:(h + 1) * DV_A].astype(F32)
        gate = z / ((1.0 + jnp.exp(-o)) * (1.0 + jnp.exp(-z)))
        y_ref[:, h * DV_A:(h + 1) * DV_A] = (hn * gate).astype(BF16)
        b_last = b_col[t - 1:t, :]
        g_col = b_last - b_col + i_col
        m_new = jnp.maximum(b_last + m_prev, jnp.max(g_col, axis=0, keepdims=True))
        e_col = jnp.exp(g_col - m_new)
        decay = jnp.exp(b_last + m_prev - m_new)
        ke = k.astype(F32) * e_col
        c_s[h] = decay * c_old + _dot_tn(ke.astype(BF16), v)
        n_s[h] = decay * n_old + jnp.sum(ke, axis=0, keepdims=True)
        m_s[h] = jnp.broadcast_to(m_new, (1, LANES))

    @pl.when(c == pl.num_programs(1) - 1)
    def _():
        c_out[...] = c_s[...]
        n_out[...] = n_s[...]
        m_out[...] = m_s[...]


def _mlstm_prompt(proj, gates_col, gates_row, g_head):
    nc = SEQ // CHUNK
    t = CHUNK
    qk_w = H_A * DK_A
    return pl.pallas_call(
        _mlstm_prompt_kernel,
        grid=(BATCH, nc),
        in_specs=[pl.BlockSpec((None, t, qk_w), lambda b, c: (b, c, 0)),
                  pl.BlockSpec((None, t, qk_w), lambda b, c: (b, c, 1)),
                  pl.BlockSpec((None, t, W_A), lambda b, c: (b, c, 1)),
                  pl.BlockSpec((None, t, W_A), lambda b, c: (b, c, 2)),
                  pl.BlockSpec((None, t, W_A), lambda b, c: (b, c, 3)),
                  pl.BlockSpec((None, t, LANES), lambda b, c: (b, c, 0)),
                  pl.BlockSpec((None, 2 * H_A, t), lambda b, c: (b, 0, c)),
                  pl.BlockSpec((H_A, DV_A), lambda b, c: (0, 0))],
        out_specs=[pl.BlockSpec((None, t, W_A), lambda b, c: (b, c, 0)),
                   pl.BlockSpec((None, H_A, DK_A, DV_A), lambda b, c: (b, 0, 0, 0)),
                   pl.BlockSpec((None, H_A, 1, DK_A), lambda b, c: (b, 0, 0, 0)),
                   pl.BlockSpec((None, H_A, 1, LANES), lambda b, c: (b, 0, 0, 0))],
        out_shape=[jax.ShapeDtypeStruct((BATCH, SEQ, W_A), BF16),
                   jax.ShapeDtypeStruct((BATCH, H_A, DK_A, DV_A), F32),
                   jax.ShapeDtypeStruct((BATCH, H_A, 1, DK_A), F32),
                   jax.ShapeDtypeStruct((BATCH, H_A, 1, LANES), F32)],
        scratch_shapes=[pltpu.VMEM((H_A, DK_A, DV_A), F32),
                        pltpu.VMEM((H_A, 1, DK_A), F32),
                        pltpu.VMEM((H_A, 1, LANES), F32)],
        compiler_params=_cparams("parallel", "arbitrary"),
        name="mlstm_prompt",
    )(proj, proj, proj, proj, proj, gates_col, gates_row, g_head)


def _mlstm_sample_body(r0, gi_ref, lf_ref, m0_ref, q_ref, k_ref, v_ref, o_ref, z_ref,
                       qt_ref, kt_ref, n0_ref, gh_ref, c0_ref,
                       y_ref, c1_ref, n1_ref, m1_ref):
    shift = lax.rem(DEC_BATCH - r0, DEC_BATCH)
    q_cols = [pltpu.roll(qt_ref[h], shift, axis=1) for h in range(H_A)]
    k_cols = [pltpu.roll(kt_ref[h], shift, axis=1) for h in range(H_A)]
    for j in range(c0_ref.shape[0]):
        rows = pl.ds(r0 + j, 1)
        i_v = gi_ref[rows, :]
        lf_v = lf_ref[rows, :]
        m0_v = m0_ref[rows, :]
        m_t = jnp.maximum(lf_v + m0_v, i_v)
        w_in = jnp.exp(i_v - m_t)
        w_st = jnp.exp(lf_v + m0_v - m_t)
        floor = jnp.exp(-m_t)
        m1_ref[rows, :] = m_t
        for h in range(H_A):
            ks = slice(h * DK_A, (h + 1) * DK_A)
            vs = slice(h * DV_A, (h + 1) * DV_A)
            wi = w_in[:, h:h + 1]
            ws = w_st[:, h:h + 1]
            q_col = q_cols[h][:, j:j + 1]
            k_col = k_cols[h][:, j:j + 1]
            q_row = q_ref[rows, ks]
            k_row = k_ref[rows, ks]
            v_row = v_ref[rows, vs]
            n_row = n0_ref[rows, ks]
            c_old = c0_ref[j, h]
            qk = jnp.sum(q_row * k_row, axis=1, keepdims=True) * SCALE_A
            s = qk * wi
            q_c = jnp.sum(c_old * q_col, axis=0, keepdims=True) * SCALE_A
            qn = jnp.sum(q_row * n_row, axis=1, keepdims=True) * SCALE_A
            num = s * v_row + ws * q_c
            den = s + ws * qn
            hh = num / jnp.maximum(jnp.abs(den), floor[:, h:h + 1])
            hn = _rms(hh, gh_ref[h:h + 1, :])
            y_ref[rows, vs] = hn * jax.nn.sigmoid(o_ref[rows, vs]) * _silu(z_ref[rows, vs])
            c1_ref[j, h] = ws * c_old + (wi * k_col) * v_row
            n1_ref[rows, ks] = ws * n_row + wi * k_row


def _mlstm_sample_rider(gi, lf, m0, proj, qt, kt, n0, g_head, c0):
    qk_w = H_A * DK_A

    def build(bmap):
        once = dict(pipeline_mode=pl.Buffered(1))
        full = lambda shape: pl.BlockSpec(shape, lambda j, i: (0,) * len(shape))
        cols = lambda width, c: pl.BlockSpec((DEC_BATCH, width), lambda j, i: (0, c), **once)
        state = pl.BlockSpec((1, H_A, DK_A, DV_A), lambda j, i: (bmap(j, i), 0, 0, 0))
        lanes = pl.BlockSpec((H_A, DK_A, DEC_BATCH), lambda j, i: (0, 0, 0), **once)
        args = [gi, lf, m0, proj, proj, proj, proj, proj, qt, kt, n0, g_head, c0]
        in_specs = [full((DEC_BATCH, H_A)), full((DEC_BATCH, H_A)), full((DEC_BATCH, H_A)),
                    cols(qk_w, 0), cols(qk_w, 1), cols(W_A, 1), cols(W_A, 2), cols(W_A, 3),
                    lanes, lanes, cols(qk_w, 0), full((H_A, DV_A)), state]
        out_shapes = [jax.ShapeDtypeStruct((DEC_BATCH, W_A), F32),
                      jax.ShapeDtypeStruct((DEC_BATCH, H_A, DK_A, DV_A), F32),
                      jax.ShapeDtypeStruct((DEC_BATCH, qk_w), F32),
                      jax.ShapeDtypeStruct((DEC_BATCH, H_A), F32)]
        out_specs = [full((DEC_BATCH, W_A)), state, full((DEC_BATCH, qk_w)), full((DEC_BATCH, H_A))]
        return args, in_specs, out_shapes, out_specs

    return build


def _s5_build_kernel(ldt_ref, lamr_ref, lami_ref, b1_ref, cc_ref, dsk_ref,
                     m_ref, w_ref, v_ref, p1_ref, p2_ref):
    dt = jnp.exp(ldt_ref[...])
    lam_re, lam_im = lamr_ref[...], lami_ref[...]
    mag = jnp.exp(lam_re * dt)
    ang = lam_im * dt
    a_re, a_im = mag * jnp.cos(ang), mag * jnp.sin(ang)
    den = lam_re * lam_re + lam_im * lam_im
    f_re = ((a_re - 1.0) * lam_re + a_im * lam_im) / den
    f_im = (a_im * lam_re - (a_re - 1.0) * lam_im) / den
    pows = [(jnp.ones_like(a_re), jnp.zeros_like(a_im))]
    for _ in range(S5_T):
        r_re, r_im = pows[-1]
        pows.append((r_re * a_re - r_im * a_im, r_re * a_im + r_im * a_re))

    def tall(vals):
        return jnp.concatenate([jnp.broadcast_to(x, (GROUP_B, LANES)) for x in vals], axis=0)

    af_re = tall([pows[S5_T - 1 - s][0] * f_re - pows[S5_T - 1 - s][1] * f_im for s in range(S5_T)])
    af_im = tall([pows[S5_T - 1 - s][0] * f_im + pows[S5_T - 1 - s][1] * f_re for s in range(S5_T)])
    lane = lax.broadcasted_iota(jnp.int32, (1, LANES), 1)
    sgn_lane = jnp.where(lane < P_B, -1.0, 1.0)
    b1 = b1_ref[...]
    b2 = sgn_lane * pltpu.roll(b1, P_B, axis=1)
    w_ref[...] = (af_re * b1 + af_im * b2).astype(BF16)

    eye = (lax.broadcasted_iota(jnp.int32, (LANES, LANES), 0)
           == lax.broadcasted_iota(jnp.int32, (LANES, LANES), 1))

    def column(row):
        return jnp.sum(jnp.where(eye, row, 0.0), axis=1, keepdims=True)

    ac_re, ac_im, fc_re, fc_im = column(a_re), column(a_im), column(f_re), column(f_im)
    kid = lax.broadcasted_iota(jnp.int32, (S5_T, LANES), 0)
    expand = (lax.broadcasted_iota(jnp.int32, (S5_T, S5_W), 1) // GROUP_B
              == lax.broadcasted_iota(jnp.int32, (S5_T, S5_W), 0)).astype(BF16)

    def lane_blocks(part):
        stacked = jnp.zeros((S5_T, LANES), F32)
        for k in range(S5_T):
            stacked = jnp.where(kid == k, pows[k][part], stacked)
        hi = stacked.astype(BF16)
        rest = stacked - hi.astype(F32)
        mid = rest.astype(BF16)
        lo = (rest - mid.astype(F32)).astype(BF16)
        return _dot_tn(hi, expand) + _dot_tn(mid, expand) + _dot_tn(lo, expand)

    q_re, q_im = lane_blocks(0), lane_blocks(1)
    cc = cc_ref[...]
    cs = pltpu.roll(cc, P_B, axis=0)
    rowi = lax.broadcasted_iota(jnp.int32, (LANES, 1), 0)
    sgn_row = jnp.where(rowi < P_B, 1.0, -1.0)

    def readout(r_re, r_im):
        return sgn_row * (r_re * cc) - r_im * cs

    v_ref[...] = readout(q_re * ac_re - q_im * ac_im, q_re * ac_im + q_im * ac_re).astype(BF16)
    vf = readout(q_re * fc_re - q_im * fc_im, q_re * fc_im + q_im * fc_re)
    kw = _dot3(b1[0:GROUP_B, :], vf)
    lane_w = lax.broadcasted_iota(jnp.int32, (GROUP_B, S5_W), 1)
    blocks = [kw]
    for s in range(1, S5_T):
        blocks.append(jnp.where(lane_w >= s * GROUP_B, pltpu.roll(kw, s * GROUP_B, axis=1), 0.0))
    toep = jnp.concatenate(blocks, axis=0)
    ri = lax.broadcasted_iota(jnp.int32, (S5_W, S5_W), 0)
    ci = lax.broadcasted_iota(jnp.int32, (S5_W, S5_W), 1)
    m_ref[...] = (toep + jnp.where(ri == ci, dsk_ref[...], 0.0)).astype(BF16)

    r_re, r_im = pows[S5_T]
    rid = lax.broadcasted_iota(jnp.int32, (8, LANES), 0)
    p1 = jnp.where(rid == S5_A1_ROW, a_re, 0.0)
    p2 = jnp.where(rid == S5_A1_ROW, sgn_lane * a_im, 0.0)
    for kk in range(S5_LEVELS):
        p1 = jnp.where(rid == kk, r_re, p1)
        p2 = jnp.where(rid == kk, sgn_lane * r_im, p2)
        r_re, r_im = r_re * r_re - r_im * r_im, 2.0 * (r_re * r_im)
    p1_ref[...] = p1
    p2_ref[...] = p2


def _s5_build(ldt, lamr, lami, b1, cc, dsk):
    g3 = lambda a, b: pl.BlockSpec((None, a, b), lambda g: (g, 0, 0))
    return pl.pallas_call(
        _s5_build_kernel,
        grid=(G_B,),
        in_specs=[g3(1, 1), g3(1, LANES), g3(1, LANES),
                  g3(S5_W, LANES), g3(LANES, S5_W), g3(1, S5_W)],
        out_specs=[g3(S5_W, S5_W), g3(S5_W, LANES), g3(LANES, S5_W), g3(8, LANES), g3(8, LANES)],
        out_shape=[jax.ShapeDtypeStruct((G_B, S5_W, S5_W), BF16),
                   jax.ShapeDtypeStruct((G_B, S5_W, LANES), BF16),
                   jax.ShapeDtypeStruct((G_B, LANES, S5_W), BF16),
                   jax.ShapeDtypeStruct((G_B, 8, LANES), F32),
                   jax.ShapeDtypeStruct((G_B, 8, LANES), F32)],
        compiler_params=_cparams("parallel"),
        name="s5_build",
    )(ldt, lamr, lami, b1, cc, dsk)


def _s5_prompt_kernel(u_ref, m_ref, w_ref, v_ref, p1_ref, p2_ref, y_ref, xf_ref, y_s):
    nblk = SEQ // S5_T
    rows = BATCH * nblk
    bidx = lax.broadcasted_iota(jnp.int32, (rows, LANES), 0) & (nblk - 1)
    steps = [u_ref[pl.ds(s, rows, stride=S5_T), :] for s in range(S5_T)]
    for g in range(OCT):
        gl = slice(g * GROUP_B, (g + 1) * GROUP_B)
        u = jnp.concatenate([x[:, gl] for x in steps], axis=-1).astype(BF16)
        s = _dot(u, w_ref[g])
        for kk in range(S5_LEVELS):
            sh = 1 << kk
            r = jnp.where(bidx >= sh, pltpu.roll(s, sh, axis=0), 0.0)
            s = s + p1_ref[g, kk:kk + 1, :] * r + p2_ref[g, kk:kk + 1, :] * pltpu.roll(r, P_B, axis=1)
        x_prev = jnp.where(bidx >= 1, pltpu.roll(s, 1, axis=0), 0.0)
        y = _dot(u, m_ref[g]) + _dot(x_prev.astype(BF16), v_ref[g])
        y_s[g] = y
        for b in range(BATCH):
            xf_ref[g, b:b + 1, :] = s[(b + 1) * nblk - 1:(b + 1) * nblk, :]
    for t in range(S5_T):
        half = slice((t // OCT) * LANES, (t // OCT + 1) * LANES)
        tl = slice((t % OCT) * GROUP_B, (t % OCT + 1) * GROUP_B)
        y_ref[pl.ds(t, rows, stride=S5_T), :] = jnp.concatenate(
            [y_s[g, :, half][:, tl] for g in range(OCT)], axis=-1)


def _s5_prompt(u, m, w, v, p1, p2):
    mp = BATCH * SEQ
    rows = BATCH * (SEQ // S5_T)
    o3 = lambda a, b: pl.BlockSpec((OCT, a, b), lambda g: (g, 0, 0))
    return pl.pallas_call(
        _s5_prompt_kernel,
        grid=(G_B // OCT,),
        in_specs=[pl.BlockSpec((mp, LANES), lambda g: (0, g)),
                  o3(S5_W, S5_W), o3(S5_W, LANES), o3(LANES, S5_W), o3(8, LANES), o3(8, LANES)],
        out_specs=[pl.BlockSpec((mp, LANES), lambda g: (0, g)), o3(BATCH, LANES)],
        out_shape=[jax.ShapeDtypeStruct((mp, W_B), F32),
                   jax.ShapeDtypeStruct((G_B, BATCH, LANES), F32)],
        scratch_shapes=[pltpu.VMEM((OCT, rows, S5_W), F32)],
        compiler_params=_cparams("parallel"),
        name="s5_prompt",
    )(u, m, w, v, p1, p2)


def _s5_sample_kernel(u_ref, xr_ref, xi_ref, m_ref, w_ref, v_ref, p1_ref, p2_ref,
                      y_ref, x1r_ref, x1i_ref):
    lane = lax.broadcasted_iota(jnp.int32, (DEC_BATCH, LANES), 1)
    last = LANES - GROUP_B
    half = S5_W // 2
    a1 = slice(S5_A1_ROW, S5_A1_ROW + 1)
    u_all = u_ref[...]
    ys = []
    for g in range(OCT):
        ps = slice(g * P_B, (g + 1) * P_B)
        u = jnp.where(lane >= last, pltpu.roll(u_all, (last - g * GROUP_B) % LANES, axis=1), 0.0)
        u = u.astype(BF16)
        x0 = jnp.concatenate([xr_ref[:, ps], xi_ref[:, ps]], axis=-1)
        x1 = (p1_ref[g, a1, :] * x0 + p2_ref[g, a1, :] * pltpu.roll(x0, P_B, axis=1)
              + _dot(u, w_ref[g, half:, :]))
        x1r_ref[:, ps] = x1[:, :P_B]
        x1i_ref[:, ps] = x1[:, P_B:]
        y = (_dot(x0.astype(BF16), v_ref[g])[:, 0:GROUP_B]
             + _dot(u, m_ref[g, half:, :])[:, S5_W - GROUP_B:])
        ys.append(y)
    y_ref[...] = jnp.concatenate(ys, axis=-1)


def _s5_sample(u, xr, xi, m, w, v, p1, p2):
    o3 = lambda a, b: pl.BlockSpec((OCT, a, b), lambda g: (g, 0, 0))
    tile = pl.BlockSpec((DEC_BATCH, LANES), lambda g: (0, g))
    st = pl.BlockSpec((DEC_BATCH, OCT * P_B), lambda g: (0, g))
    st_shape = jax.ShapeDtypeStruct((DEC_BATCH, G_B * P_B), F32)
    return pl.pallas_call(
        _s5_sample_kernel,
        grid=(G_B // OCT,),
        in_specs=[tile, st, st, o3(S5_W, S5_W), o3(S5_W, LANES), o3(LANES, S5_W), o3(8, LANES), o3(8, LANES)],
        out_specs=[tile, st, st],
        out_shape=[jax.ShapeDtypeStruct((DEC_BATCH, W_B), F32), st_shape, st_shape],
        compiler_params=_cparams("parallel"),
        name="s5_sample",
    )(u, xr, xi, m, w, v, p1, p2)


def _glu_kernel(y_ref, w_ref, b_ref, z_ref, o_ref):
    y = _gelu_tanh(y_ref[...])
    gate = jax.nn.sigmoid(_dot(y.astype(BF16), w_ref[...]) + b_ref[...])
    o_ref[...] = (y * gate * _silu(z_ref[...].astype(F32))).astype(o_ref.dtype)


def _glu(y, w, b, proj_b, tm):
    m = y.shape[0]
    return pl.pallas_call(
        _glu_kernel,
        grid=(m // tm,),
        in_specs=[pl.BlockSpec((tm, W_B), lambda i: (i, 0)),
                  pl.BlockSpec((W_B, W_B), lambda i: (0, 0)),
                  pl.BlockSpec((1, W_B), lambda i: (0, 0)),
                  pl.BlockSpec((tm, W_B), lambda i: (i, 1))],
        out_specs=pl.BlockSpec((tm, W_B), lambda i: (i, 0)),
        out_shape=jax.ShapeDtypeStruct((m, W_B), BF16),
        compiler_params=_cparams("parallel"),
        name="glu",
    )(y, w, b, proj_b)


def _rope_table_kernel(cos_ref, sin_ref, *, pos0):
    shape = cos_ref.shape
    pos = lax.broadcasted_iota(jnp.int32, shape, 0).astype(F32) + pos0
    j = lax.broadcasted_iota(jnp.int32, shape, 1).astype(F32)
    ang = pos * jnp.power(ROPE_BASE, -(j / (DK_C // 2)))
    cos_ref[...] = jnp.cos(ang)
    sin_ref[...] = jnp.sin(ang)


def _rope_table(rows, pos0):
    shape = jax.ShapeDtypeStruct((rows, DK_C // 2), F32)
    return pl.pallas_call(functools.partial(_rope_table_kernel, pos0=float(pos0)),
                          out_shape=[shape, shape], name="rope_table")()


def _rope(x, cos, sin):
    half = DK_C // 2
    x1, x2 = x[:, :half], x[:, half:]
    return jnp.concatenate([x1 * cos - x2 * sin, x1 * sin + x2 * cos], axis=-1)


def _ret_prompt_kernel(q_ref, k_ref, v_ref, z_ref, cos_ref, sin_ref, gh_ref, y_ref, s_out, s_s):
    c = pl.program_id(1)
    t = CHUNK

    @pl.when(c == 0)
    def _():
        s_s[...] = jnp.zeros_like(s_s)

    row = lax.broadcasted_iota(jnp.int32, (t, t), 0)
    col = lax.broadcasted_iota(jnp.int32, (t, t), 1)
    tril = col <= row
    diff = (row - col).astype(F32)
    tpos = lax.broadcasted_iota(jnp.int32, (t, 1), 0).astype(F32)
    cos, sin = cos_ref[...], sin_ref[...]
    for h in range(H_C):
        lg = LOG_GAMMA[h]
        ks = slice(h * DK_C, (h + 1) * DK_C)
        vs = slice(h * DV_C, (h + 1) * DV_C)
        q = _rope(q_ref[:, ks].astype(F32), cos, sin).astype(BF16)
        k32 = _rope(k_ref[:, ks].astype(F32), cos, sin)
        v = v_ref[:, vs]
        mask = jnp.where(tril, jnp.exp(diff * lg), 0.0) * SCALE_C
        s = _dot_nt(q, k32.astype(BF16)) * mask
        s_old = s_s[h]
        o = _dot(s.astype(BF16), v) + _dot(q, s_old.astype(BF16)) * jnp.exp((tpos + 1.0) * lg)
        y_ref[:, vs] = (_rms(o, gh_ref[h:h + 1, :]) * _silu(z_ref[:, vs].astype(F32))).astype(BF16)
        k_tail = k32 * (jnp.exp((t - 1.0 - tpos) * lg) * SCALE_C)
        s_s[h] = math.exp(t * lg) * s_old + _dot_tn(k_tail.astype(BF16), v)

    @pl.when(c == pl.num_programs(1) - 1)
    def _():
        s_out[...] = s_s[...]


def _ret_prompt(proj, cos, sin, g_head):
    nc = SEQ // CHUNK
    t = CHUNK
    qk_w = H_C * DK_C
    return pl.pallas_call(
        _ret_prompt_kernel,
        grid=(BATCH, nc),
        in_specs=[pl.BlockSpec((None, t, qk_w), lambda b, c: (b, c, 0)),
                  pl.BlockSpec((None, t, qk_w), lambda b, c: (b, c, 1)),
                  pl.BlockSpec((None, t, W_C), lambda b, c: (b, c, 1)),
                  pl.BlockSpec((None, t, W_C), lambda b, c: (b, c, 2)),
                  pl.BlockSpec((t, DK_C // 2), lambda b, c: (c, 0)),
                  pl.BlockSpec((t, DK_C // 2), lambda b, c: (c, 0)),
                  pl.BlockSpec((H_C, DV_C), lambda b, c: (0, 0))],
        out_specs=[pl.BlockSpec((None, t, W_C), lambda b, c: (b, c, 0)),
                   pl.BlockSpec((None, H_C, DK_C, DV_C), lambda b, c: (b, 0, 0, 0))],
        out_shape=[jax.ShapeDtypeStruct((BATCH, SEQ, W_C), BF16),
                   jax.ShapeDtypeStruct((BATCH, H_C, DK_C, DV_C), F32)],
        scratch_shapes=[pltpu.VMEM((H_C, DK_C, DV_C), F32)],
        compiler_params=_cparams("parallel", "arbitrary"),
        name="ret_prompt",
    )(proj, proj, proj, proj, cos, sin, g_head)


def _rope_sample_kernel(q_ref, k_ref, cos_ref, sin_ref, qo_ref, ko_ref):
    cos, sin = cos_ref[0:1, :], sin_ref[0:1, :]
    for h in range(H_C):
        ks = slice(h * DK_C, (h + 1) * DK_C)
        qo_ref[:, ks] = _rope(q_ref[:, ks], cos, sin)
        ko_ref[:, ks] = _rope(k_ref[:, ks], cos, sin)


def _rope_sample(proj, cos, sin):
    qk_w = H_C * DK_C
    shape = jax.ShapeDtypeStruct((DEC_BATCH, qk_w), F32)
    return pl.pallas_call(
        _rope_sample_kernel,
        grid=(1,),
        in_specs=[pl.BlockSpec((DEC_BATCH, qk_w), lambda i: (0, 0)),
                  pl.BlockSpec((DEC_BATCH, qk_w), lambda i: (0, 1)),
                  pl.BlockSpec((8, DK_C // 2), lambda i: (0, 0)),
                  pl.BlockSpec((8, DK_C // 2), lambda i: (0, 0))],
        out_specs=[pl.BlockSpec((DEC_BATCH, qk_w), lambda i: (0, 0)),
                   pl.BlockSpec((DEC_BATCH, qk_w), lambda i: (0, 0))],
        out_shape=[shape, shape],
        compiler_params=_cparams("arbitrary"),
        name="rope_sample",
    )(proj, proj, cos, sin)


def _ret_sample_body(r0, q_ref, k_ref, v_ref, z_ref, qt_ref, kt_ref, gh_ref, s0_ref, y_ref, s1_ref):
    shift = lax.rem(DEC_BATCH - r0, DEC_BATCH)
    for h in range(H_C):
        gamma = math.exp(LOG_GAMMA[h])
        ks = slice(h * DK_C, (h + 1) * DK_C)
        vs = slice(h * DV_C, (h + 1) * DV_C)
        q_cols = pltpu.roll(qt_ref[h], shift, axis=1)
        k_cols = pltpu.roll(kt_ref[h], shift, axis=1)
        for j in range(s0_ref.shape[0]):
            rows = pl.ds(r0 + j, 1)
            v_row = v_ref[rows, vs]
            s_old = s0_ref[j, h]
            qk = jnp.sum(q_ref[rows, ks] * k_ref[rows, ks], axis=1, keepdims=True) * SCALE_C
            o = qk * v_row + jnp.sum(s_old * q_cols[:, j:j + 1], axis=0, keepdims=True) * gamma
            y_ref[rows, vs] = _rms(o, gh_ref[h:h + 1, :]) * _silu(z_ref[rows, vs])
            s1_ref[j, h] = gamma * s_old + (k_cols[:, j:j + 1] * SCALE_C) * v_row


def _ret_sample_rider(q, k, proj, qt, kt, g_head, s0):
    qk_w = H_C * DK_C

    def build(bmap):
        once = dict(pipeline_mode=pl.Buffered(1))
        fixed = lambda shape, *idx: pl.BlockSpec(shape, lambda j, i: idx or (0,) * len(shape), **once)
        state = pl.BlockSpec((1, H_C, DK_C, DV_C), lambda j, i: (bmap(j, i), 0, 0, 0))
        args = [q, k, proj, proj, qt, kt, g_head, s0]
        in_specs = [fixed((DEC_BATCH, qk_w)), fixed((DEC_BATCH, qk_w)),
                    fixed((DEC_BATCH, W_C), 0, 1), fixed((DEC_BATCH, W_C), 0, 2),
                    fixed((H_C, DK_C, DEC_BATCH)), fixed((H_C, DK_C, DEC_BATCH)),
                    fixed((H_C, DV_C)), state]
        out_shapes = [jax.ShapeDtypeStruct((DEC_BATCH, W_C), F32),
                      jax.ShapeDtypeStruct((DEC_BATCH, H_C, DK_C, DV_C), F32)]
        out_specs = [pl.BlockSpec((DEC_BATCH, W_C), lambda j, i: (0, 0)), state]
        return args, in_specs, out_shapes, out_specs

    return build


def kernel(x_prompt, x_sample, state_mlstm_C, state_mlstm_n, state_mlstm_m, state_s5_re, state_s5_im, state_ret, g_pre, g_post, w_in0, b_gates0, g_head_a, lam_re, lam_im, log_dt, b_re, b_im, c_re, c_im, d_skip, w_glu, b_glu, w_out0, w_in1, g_head_c, w_out1):
    mp = BATCH * SEQ
    xp = x_prompt.reshape(mp, D_MODEL).astype(F32)
    xs = x_sample.reshape(DEC_BATCH, D_MODEL).astype(F32)

    w0t = w_in0.T.astype(F32)
    w0g = jnp.pad(w0t[QKV0:GATE0], ((0, LANES - 2 * H_A), (0, 0)))
    w0b = w0t[GATE0:]
    w1 = w_in1.astype(BF16)
    wo0 = w_out0.astype(BF16)
    wo1 = w_out1.astype(BF16)
    wg = w_glu.astype(BF16)
    bglu = b_glu.reshape(1, W_B).astype(F32)
    g_pre = g_pre.astype(F32)
    g_post = g_post.astype(F32)
    bg = jnp.pad(b_gates0.astype(F32), (0, LANES - 2 * H_A))[None, :]
    gh_a = g_head_a.astype(F32)
    gh_c = g_head_c.astype(F32)

    dup = lambda a: jnp.concatenate([a, a], axis=-1).astype(F32)
    lamr, lami = dup(lam_re), dup(lam_im)
    b1 = jnp.concatenate([b_re.transpose(0, 2, 1), b_im.transpose(0, 2, 1)], axis=-1)
    cc = jnp.concatenate([c_re.transpose(0, 2, 1), c_im.transpose(0, 2, 1)], axis=1)
    s5m, s5w, s5v, s5p1, s5p2 = _s5_build(
        log_dt.reshape(G_B, 1, 1).astype(F32), lamr[:, None, :], lami[:, None, :],
        jnp.tile(b1.astype(F32), (1, S5_T, 1)), jnp.tile(cc.astype(F32), (1, 1, S5_T)),
        jnp.tile(d_skip.astype(F32), (1, S5_T))[:, None, :])

    a0, gates = _norm_gates(xp, g_pre[0:1], w0g, bg, 512)
    a0s, gates_s = _norm_gates(xs, g_pre[0:1], w0g, bg, DEC_BATCH)
    pas = _proj(a0s, a0s, w0t, QKV0, F32, DEC_BATCH, 1024, True)[0]
    qk_a = H_A * DK_A
    to_cols = lambda a, nh, dk: a.reshape(DEC_BATCH, nh, dk).transpose(1, 2, 0)
    pa, yas, c_s, n_s, m_s = _proj_rider(
        a0, w0t, QKV0, BF16, 512, 1024, True, _mlstm_sample_body, lambda step: step,
        _mlstm_sample_rider(
            gates_s[:, :H_A], gates_s[:, H_A:2 * H_A],
            state_mlstm_m.astype(F32), pas,
            to_cols(pas[:, :qk_a], H_A, DK_A), to_cols(pas[:, qk_a:2 * qk_a], H_A, DK_A),
            state_mlstm_n.reshape(DEC_BATCH, qk_a).astype(F32), gh_a, state_mlstm_C.astype(F32)))
    pb, pbs = _proj(a0, a0s, w0b, 2 * W_B, F32, 1024, 1024, True)
    gates3 = gates.reshape(BATCH, SEQ, LANES)
    gates_row = gates3[:, :, :2 * H_A].transpose(0, 2, 1)
    ya, c_p, n_p, m_p = _mlstm_prompt(pa.reshape(BATCH, SEQ, QKV0), gates3, gates_row, gh_a)
    yb, xf = _s5_prompt(pb, s5m, s5w, s5v, s5p1, s5p2)
    yb = _glu(yb, wg, bglu, pb, 512)
    s5r_p = xf[:, :, :P_B].transpose(1, 0, 2)
    s5i_p = xf[:, :, P_B:].transpose(1, 0, 2)
    ybs, s5r_s, s5i_s = _s5_sample(
        pbs, state_s5_re.reshape(DEC_BATCH, G_B * P_B).astype(F32),
        state_s5_im.reshape(DEC_BATCH, G_B * P_B).astype(F32), s5m, s5w, s5v, s5p1, s5p2)
    s5r_s = s5r_s.reshape(DEC_BATCH, G_B, P_B)
    s5i_s = s5i_s.reshape(DEC_BATCH, G_B, P_B)
    ybs = _glu(ybs, wg, bglu, pbs, DEC_BATCH)
    h1, h1s, a1, a1s = _outproj(ya.reshape(mp, W_A), yb, yas, ybs, wo0, xp, xs, g_post[0:1],
                                g_pre[1:2], 512)

    cols1 = 2 * H_C * DK_C + 2 * W_C
    p1s = _proj(a1s, a1s, w1, cols1, F32, DEC_BATCH, 1024, False)[0]
    cos_s, sin_s = _rope_table(8, PAST_LEN)
    qs, ks = _rope_sample(p1s, cos_s, sin_s)
    p1, ycs, s_s = _proj_rider(
        a1, w1, cols1, BF16, 512, cols1 // 8, False, _ret_sample_body, lambda step: step,
        _ret_sample_rider(qs, ks, p1s, to_cols(qs, H_C, DK_C), to_cols(ks, H_C, DK_C),
                          gh_c, state_ret.astype(F32)))
    cos_p, sin_p = _rope_table(SEQ, 0)
    yc, s_p = _ret_prompt(p1.reshape(BATCH, SEQ, -1), cos_p, sin_p, gh_c)
    y_p, y_s = _outproj(yc.reshape(mp, W_C), None, ycs, None, wo1, h1, h1s, g_post[1:2], None, 512)

    return (y_p.reshape(BATCH, SEQ, D_MODEL), y_s.reshape(DEC_BATCH, 1, D_MODEL),
            c_p, n_p.reshape(BATCH, H_A, DK_A), m_p[:, :, 0, 0],
            s5r_p, s5i_p, s_p,
            c_s, n_s.reshape(DEC_BATCH, H_A, DK_A), m_s,
            s5r_s, s5i_s, s_s)
```

```python
import functools
import math

import jax
import jax.numpy as jnp
from jax import lax
from jax.experimental import pallas as pl
from jax.experimental.pallas import tpu as pltpu

F32 = jnp.float32
BF16 = jnp.bfloat16

D_MODEL = 2048
BATCH = 4
SEQ = 2048
DEC_BATCH = 128
PAST_LEN = 16384
H_A = 4
DK_A = 256
DV_A = 512
W_A = H_A * DV_A
GROUP_B = 16
G_B = 64
P_B = 64
W_B = G_B * GROUP_B
H_C = 8
DK_C = 256
DV_C = 512
W_C = H_C * DV_C
CHUNK = 256
NORM_EPS = 1e-6
ROPE_BASE = 10000.0
QKV0 = 2 * H_A * DK_A + 3 * W_A
GATE0 = QKV0 + 2 * H_A
SCALE_A = DK_A ** -0.5
SCALE_C = DK_C ** -0.5
LOG_GAMMA = tuple(math.log1p(-(2.0 ** (-5.0 - h))) for h in range(H_C))
S5_T = 16
S5_W = S5_T * GROUP_B
LANES = 128
OCT = LANES // GROUP_B
S5_LEVELS = int(math.log2(SEQ // S5_T))
S5_A1_ROW = S5_LEVELS
VMEM_LIMIT = 48 * 1024 * 1024
BIG_VMEM_LIMIT = 56 * 1024 * 1024


def _cparams(*sem):
    return pltpu.CompilerParams(dimension_semantics=sem, vmem_limit_bytes=VMEM_LIMIT)


def _dot(a, b):
    return jnp.dot(a, b, preferred_element_type=F32)


def _dot_nt(a, b):
    return lax.dot_general(a, b, (((1,), (1,)), ((), ())), preferred_element_type=F32)


def _dot_tn(a, b):
    return lax.dot_general(a, b, (((0,), (0,)), ((), ())), preferred_element_type=F32)


def _split(x):
    hi = x.astype(BF16)
    return hi, (x - hi.astype(F32)).astype(BF16)


def _dot3(a, b):
    a_hi, a_lo = _split(a)
    b_hi, b_lo = _split(b)
    return _dot(a_hi, b_hi) + _dot(a_hi, b_lo) + _dot(a_lo, b_hi)


def _log_sigmoid(x):
    return jnp.minimum(x, 0.0) - jnp.log1p(jnp.exp(-jnp.abs(x)))


def _silu(x):
    return x * jax.nn.sigmoid(x)


def _gelu_tanh(x):
    return 0.5 * x * (1.0 + jnp.tanh(math.sqrt(2.0 / math.pi) * (x + 0.044715 * (x * x * x))))


def _rms(x, g):
    return x * lax.rsqrt(jnp.mean(x * x, axis=-1, keepdims=True) + NORM_EPS) * g


def _norm_gates_kernel(x_ref, g_ref, wg_ref, bg_ref, a_ref, gates_ref):
    a = _rms(x_ref[...], g_ref[...])
    a_ref[...] = a.astype(BF16)
    a_hi, a_lo = _split(a)
    w_hi, w_lo = _split(wg_ref[...])
    pre = _dot_nt(a_hi, w_hi) + _dot_nt(a_hi, w_lo) + _dot_nt(a_lo, w_hi) + bg_ref[...]
    lane = lax.broadcasted_iota(jnp.int32, pre.shape, 1)
    gates_ref[...] = jnp.where((lane >= H_A) & (lane < 2 * H_A), _log_sigmoid(pre), pre)


def _norm_gates(x, g, wg, bg, tm):
    m = x.shape[0]
    return pl.pallas_call(
        _norm_gates_kernel,
        grid=(m // tm,),
        in_specs=[pl.BlockSpec((tm, D_MODEL), lambda i: (i, 0)),
                  pl.BlockSpec((1, D_MODEL), lambda i: (0, 0)),
                  pl.BlockSpec((LANES, D_MODEL), lambda i: (0, 0)),
                  pl.BlockSpec((1, LANES), lambda i: (0, 0))],
        out_specs=[pl.BlockSpec((tm, D_MODEL), lambda i: (i, 0)),
                   pl.BlockSpec((tm, LANES), lambda i: (i, 0))],
        out_shape=[jax.ShapeDtypeStruct((m, D_MODEL), BF16),
                   jax.ShapeDtypeStruct((m, LANES), F32)],
        compiler_params=_cparams("parallel"),
        name="norm_gates",
    )(x, g, wg, bg)


def _proj_kernel(xp_ref, xs_ref, w_ref, op_ref, os_ref, wb_ref, *, w_transposed):
    mm = _dot_nt if w_transposed else _dot

    @pl.when(pl.program_id(1) == 0)
    def _():
        wb_ref[...] = w_ref[...].astype(BF16)
        os_ref[...] = mm(xs_ref[...], wb_ref[...])

    op_ref[...] = mm(xp_ref[...], wb_ref[...]).astype(op_ref.dtype)


def _proj(xp, xs, w, n_cols, out_dtype, tm, tn, w_transposed):
    m, k = xp.shape
    ms = xs.shape[0]
    if w_transposed:
        w_spec = pl.BlockSpec((tn, k), lambda j, i: (j, 0))
        wb_shape = (tn, k)
    else:
        w_spec = pl.BlockSpec((k, tn), lambda j, i: (0, j))
        wb_shape = (k, tn)
    return pl.pallas_call(
        functools.partial(_proj_kernel, w_transposed=w_transposed),
        grid=(n_cols // tn, m // tm),
        in_specs=[pl.BlockSpec((tm, k), lambda j, i: (i, 0)),
                  pl.BlockSpec((ms, k), lambda j, i: (0, 0)),
                  w_spec],
        out_specs=[pl.BlockSpec((tm, tn), lambda j, i: (i, j)),
                   pl.BlockSpec((ms, tn), lambda j, i: (0, j))],
        out_shape=[jax.ShapeDtypeStruct((m, n_cols), out_dtype),
                   jax.ShapeDtypeStruct((ms, n_cols), F32)],
        scratch_shapes=[pltpu.VMEM(wb_shape, BF16)],
        compiler_params=_cparams("parallel", "arbitrary"),
        name="proj",
    )(xp, xs, w)


def _proj_rider_kernel(*refs, n_in, n_out, body, batch_of_step, n_i, w_transposed):
    xp_ref, w_ref = refs[:2]
    rider_in = refs[2:2 + n_in]
    op_ref = refs[2 + n_in]
    rider_out = refs[3 + n_in:3 + n_in + n_out]
    mm = _dot_nt if w_transposed else _dot
    i = pl.program_id(1)

    if w_ref.dtype == BF16:
        wb_ref = w_ref
    else:
        wb_ref = refs[-1]

        @pl.when(i == 0)
        def _():
            wb_ref[...] = w_ref[...].astype(BF16)

    op_ref[...] = mm(xp_ref[...], wb_ref[...]).astype(op_ref.dtype)
    body(batch_of_step(pl.program_id(0) * n_i + i), *rider_in, *rider_out)


def _proj_rider(xp, w, n_cols, out_dtype, tm, tn, w_transposed, body, batch_of_step, rider):
    m, k = xp.shape
    n_i = m // tm
    bmap = lambda j, i: batch_of_step(j * n_i + i)
    r_args, r_in_specs, r_out_shapes, r_out_specs = rider(bmap)
    if w_transposed:
        w_spec = pl.BlockSpec((tn, k), lambda j, i: (j, 0))
        wb_shape = (tn, k)
    else:
        w_spec = pl.BlockSpec((k, tn), lambda j, i: (0, j))
        wb_shape = (k, tn)
    return pl.pallas_call(
        functools.partial(_proj_rider_kernel, n_in=len(r_args), n_out=len(r_out_shapes), body=body,
                          batch_of_step=batch_of_step, n_i=n_i, w_transposed=w_transposed),
        grid=(n_cols // tn, n_i),
        in_specs=[pl.BlockSpec((tm, k), lambda j, i: (i, 0)), w_spec] + r_in_specs,
        out_specs=[pl.BlockSpec((tm, tn), lambda j, i: (i, j))] + r_out_specs,
        out_shape=[jax.ShapeDtypeStruct((m, n_cols), out_dtype)] + r_out_shapes,
        scratch_shapes=[] if w.dtype == BF16 else [pltpu.VMEM(wb_shape, BF16)],
        compiler_params=pltpu.CompilerParams(dimension_semantics=("arbitrary", "arbitrary"),
                                             vmem_limit_bytes=BIG_VMEM_LIMIT),
        name="proj_rider",
    )(xp, w, *r_args)


def _outproj_kernel(*refs, two, next_norm):
    refs = list(refs)
    ya_ref = refs.pop(0)
    yb_ref = refs.pop(0) if two else None
    yas_ref = refs.pop(0)
    ybs_ref = refs.pop(0) if two else None
    wa_ref = refs.pop(0)
    wb_ref = refs.pop(0) if two else None
    h_ref, hs_ref, g_ref = refs.pop(0), refs.pop(0), refs.pop(0)
    gn_ref = refs.pop(0) if next_norm else None
    o_ref, os_ref = refs.pop(0), refs.pop(0)
    a_ref, as_ref = (refs.pop(0), refs.pop(0)) if next_norm else (None, None)

    def run(x_ref, b_ref, res_ref, out_ref, nxt_ref):
        mix = _dot(x_ref[...].astype(BF16), wa_ref[...])
        if two:
            mix = mix + _dot(b_ref[...].astype(BF16), wb_ref[...])
        new = res_ref[...] + _rms(mix, g_ref[...])
        out_ref[...] = new
        if next_norm:
            nxt_ref[...] = _rms(new, gn_ref[...]).astype(BF16)

    @pl.when(pl.program_id(0) == 0)
    def _():
        run(yas_ref, ybs_ref, hs_ref, os_ref, as_ref)

    run(ya_ref, yb_ref, h_ref, o_ref, a_ref)


def _outproj(ya, yb, yas, ybs, w, h, hs, g, g_next, tm):
    m = h.shape[0]
    ms = hs.shape[0]
    ka = ya.shape[1]
    two = yb is not None
    next_norm = g_next is not None
    once = dict(pipeline_mode=pl.Buffered(1))
    row = lambda width: pl.BlockSpec((tm, width), lambda i: (i, 0))
    fixed = lambda rows, width: pl.BlockSpec((rows, width), lambda i: (0, 0), **once)
    if two:
        kb = yb.shape[1]
        in_specs = [row(ka), row(kb), fixed(ms, ka), fixed(ms, kb), fixed(ka, D_MODEL),
                    pl.BlockSpec((kb, D_MODEL), lambda i: (ka // kb, 0), **once)]
        args = [ya, yb, yas, ybs, w, w]
    else:
        in_specs = [row(ka), fixed(ms, ka), fixed(ka, D_MODEL)]
        args = [ya, yas, w]
    in_specs += [row(D_MODEL), fixed(ms, D_MODEL), fixed(1, D_MODEL)]
    args += [h, hs, g]
    sample_out = pl.BlockSpec((ms, D_MODEL), lambda i: (0, 0))
    out_specs = [row(D_MODEL), sample_out]
    out_shape = [jax.ShapeDtypeStruct((m, D_MODEL), F32), jax.ShapeDtypeStruct((ms, D_MODEL), F32)]
    if next_norm:
        in_specs.append(fixed(1, D_MODEL))
        args.append(g_next)
        out_specs += [row(D_MODEL), sample_out]
        out_shape += [jax.ShapeDtypeStruct((m, D_MODEL), BF16), jax.ShapeDtypeStruct((ms, D_MODEL), BF16)]
    return pl.pallas_call(
        functools.partial(_outproj_kernel, two=two, next_norm=next_norm),
        grid=(m // tm,),
        in_specs=in_specs,
        out_specs=out_specs,
        out_shape=out_shape,
        compiler_params=pltpu.CompilerParams(dimension_semantics=("arbitrary",),
                                             vmem_limit_bytes=BIG_VMEM_LIMIT),
        name="outproj",
    )(*args)


def _mlstm_prompt_kernel(*refs, n_cast):
    q_ref, k_ref, v_ref, o_ref, z_ref, gc_ref, gr_ref, gh_ref = refs[:8]
    cast_in = refs[8:8 + n_cast]
    y_ref, c_out, n_out, m_out = refs[8 + n_cast:12 + n_cast]
    cast_out = refs[12 + n_cast:12 + 2 * n_cast]
    c_s, n_s, m_s = refs[12 + 2 * n_cast:]
    c = pl.program_id(1)
    t = CHUNK
    for src, dst in zip(cast_in, cast_out):
        dst[...] = src[...].astype(BF16)

    @pl.when(c == 0)
    def _():
        c_s[...] = jnp.zeros_like(c_s)
        n_s[...] = jnp.zeros_like(n_s)
        m_s[...] = jnp.zeros_like(m_s)

    row = lax.broadcasted_iota(jnp.int32, (t, t), 0)
    col = lax.broadcasted_iota(jnp.int32, (t, t), 1)
    tril = col <= row
    triu = row <= col
    gc = gc_ref[...]
    gr = gr_ref[...]
    for h in range(H_A):
        i_col = gc[:, h:h + 1]
        i_row = gr[h:h + 1, :]
        lf_col = gc[:, H_A + h:H_A + h + 1]
        lf_row = gr[H_A + h:H_A + h + 1, :]
        b_col = jnp.sum(jnp.where(tril, lf_row, 0.0), axis=1, keepdims=True)
        b_row = jnp.sum(jnp.where(triu, lf_col, 0.0), axis=0, keepdims=True)
        m_prev = m_s[h][:, 0:1]
        d = jnp.where(tril, b_col - b_row + i_row, -jnp.inf)
        inter = b_col + m_prev
        m_t = jnp.maximum(inter, jnp.max(d, axis=1, keepdims=True))
        w_intra = jnp.exp(d - m_t)
        w_inter = jnp.exp(inter - m_t) * SCALE_A
        q = q_ref[:, h * DK_A:(h + 1) * DK_A]
        k = k_ref[:, h * DK_A:(h + 1) * DK_A]
        v = v_ref[:, h * DV_A:(h + 1) * DV_A]
        s = _dot_nt(q, k) * (w_intra * SCALE_A)
        c_old = c_s[h]
        n_old = n_s[h]
        num = _dot(s.astype(BF16), v) + w_inter * _dot(q, c_old.astype(BF16))
        qn = jnp.sum(q.astype(F32) * n_old, axis=1, keepdims=True)
        den = jnp.sum(s, axis=1, keepdims=True) + w_inter * qn
        hh = num * (1.0 / jnp.maximum(jnp.abs(den), jnp.exp(-m_t)))
        hn = _rms(hh, gh_ref[h:h + 1, :])
        o = o_ref[:, h * DV_A:(h + 1) * DV_A].astype(F32)
        z = z_ref[:, h * DV_A:(h + 1) * DV_A].astype(F32)
        gate = z / ((1.0 + jnp.exp(-o)) * (1.0 + jnp.exp(-z)))
        y_ref[:, h * DV_A:(h + 1) * DV_A] = (hn * gate).astype(BF16)
        b_last = b_col[t - 1:t, :]
        g_col = b_last - b_col + i_col
        m_new = jnp.maximum(b_last + m_prev, jnp.max(g_col, axis=0, keepdims=True))
        e_col = jnp.exp(g_col - m_new)
        decay = jnp.exp(b_last + m_prev - m_new)
        ke = k.astype(F32) * e_col
        c_s[h] = decay * c_old + _dot_tn(ke.astype(BF16), v)
        n_s[h] = decay * n_old + jnp.sum(ke, axis=0, keepdims=True)
        m_s[h] = jnp.broadcast_to(m_new, (1, LANES))

    @pl.when(c == pl.num_programs(1) - 1)
    def _():
        c_out[...] = c_s[...]
        n_out[...] = n_s[...]
        m_out[...] = m_s[...]


def _mlstm_prompt(proj, gates_col, gates_row, g_head, weights):
    nc = SEQ // CHUNK
    t = CHUNK
    qk_w = H_A * DK_A
    steps = BATCH * nc
    slab = lambda w: pl.BlockSpec((w.shape[0] // steps, w.shape[1]), lambda b, c: (b * nc + c, 0))
    return pl.pallas_call(
        functools.partial(_mlstm_prompt_kernel, n_cast=len(weights)),
        grid=(BATCH, nc),
        in_specs=[pl.BlockSpec((None, t, qk_w), lambda b, c: (b, c, 0)),
                  pl.BlockSpec((None, t, qk_w), lambda b, c: (b, c, 1)),
                  pl.BlockSpec((None, t, W_A), lambda b, c: (b, c, 1)),
                  pl.BlockSpec((None, t, W_A), lambda b, c: (b, c, 2)),
                  pl.BlockSpec((None, t, W_A), lambda b, c: (b, c, 3)),
                  pl.BlockSpec((None, t, LANES), lambda b, c: (b, c, 0)),
                  pl.BlockSpec((None, 2 * H_A, t), lambda b, c: (b, 0, c)),
                  pl.BlockSpec((H_A, DV_A), lambda b, c: (0, 0))] + [slab(w) for w in weights],
        out_specs=[pl.BlockSpec((None, t, W_A), lambda b, c: (b, c, 0)),
                   pl.BlockSpec((None, H_A, DK_A, DV_A), lambda b, c: (b, 0, 0, 0)),
                   pl.BlockSpec((None, H_A, 1, DK_A), lambda b, c: (b, 0, 0, 0)),
                   pl.BlockSpec((None, H_A, 1, LANES), lambda b, c: (b, 0, 0, 0))]
        + [slab(w) for w in weights],
        out_shape=[jax.ShapeDtypeStruct((BATCH, SEQ, W_A), BF16),
                   jax.ShapeDtypeStruct((BATCH, H_A, DK_A, DV_A), F32),
                   jax.ShapeDtypeStruct((BATCH, H_A, 1, DK_A), F32),
                   jax.ShapeDtypeStruct((BATCH, H_A, 1, LANES), F32)]
        + [jax.ShapeDtypeStruct(w.shape, BF16) for w in weights],
        scratch_shapes=[pltpu.VMEM((H_A, DK_A, DV_A), F32),
                        pltpu.VMEM((H_A, 1, DK_A), F32),
                        pltpu.VMEM((H_A, 1, LANES), F32)],
        compiler_params=_cparams("arbitrary", "arbitrary"),
        name="mlstm_prompt",
    )(proj, proj, proj, proj, proj, gates_col, gates_row, g_head, *weights)


def _mlstm_sample_body(r0, gi_ref, lf_ref, m0_ref, q_ref, k_ref, v_ref, o_ref, z_ref,
                       qt_ref, kt_ref, n0_ref, gh_ref, c0_ref,
                       y_ref, c1_ref, n1_ref, m1_ref):
    shift = lax.rem(DEC_BATCH - r0, DEC_BATCH)
    q_cols = [pltpu.roll(qt_ref[h], shift, axis=1) for h in range(H_A)]
    k_cols = [pltpu.roll(kt_ref[h], shift, axis=1) for h in range(H_A)]
    for j in range(c0_ref.shape[0]):
        rows = pl.ds(r0 + j, 1)
        i_v = gi_ref[rows, :]
        lf_v = lf_ref[rows, :]
        m0_v = m0_ref[rows, :]
        m_t = jnp.maximum(lf_v + m0_v, i_v)
        w_in = jnp.exp(i_v - m_t)
        w_st = jnp.exp(lf_v + m0_v - m_t)
        floor = jnp.exp(-m_t)
        m1_ref[rows, :] = m_t
        for h in range(H_A):
            ks = slice(h * DK_A, (h + 1) * DK_A)
            vs = slice(h * DV_A, (h + 1) * DV_A)
            wi = w_in[:, h:h + 1]
            ws = w_st[:, h:h + 1]
            q_col = q_cols[h][:, j:j + 1]
            k_col = k_cols[h][:, j:j + 1]
            q_row = q_ref[rows, ks]
            k_row = k_ref[rows, ks]
            v_row = v_ref[rows, vs]
            n_row = n0_ref[rows, ks]
            c_old = c0_ref[j, h]
            qk = jnp.sum(q_row * k_row, axis=1, keepdims=True) * SCALE_A
            s = qk * wi
            q_c = jnp.sum(c_old * q_col, axis=0, keepdims=True) * SCALE_A
            qn = jnp.sum(q_row * n_row, axis=1, keepdims=True) * SCALE_A
            num = s * v_row + ws * q_c
            den = s + ws * qn
            hh = num / jnp.maximum(jnp.abs(den), floor[:, h:h + 1])
            hn = _rms(hh, gh_ref[h:h + 1, :])
            y_ref[rows, vs] = hn * jax.nn.sigmoid(o_ref[rows, vs]) * _silu(z_ref[rows, vs])
            c1_ref[j, h] = ws * c_old + (wi * k_col) * v_row
            n1_ref[rows, ks] = ws * n_row + wi * k_row


def _mlstm_sample_rider(gi, lf, m0, proj, qt, kt, n0, g_head, c0):
    qk_w = H_A * DK_A

    def build(bmap):
        once = dict(pipeline_mode=pl.Buffered(1))
        full = lambda shape: pl.BlockSpec(shape, lambda j, i: (0,) * len(shape))
        cols = lambda width, c: pl.BlockSpec((DEC_BATCH, width), lambda j, i: (0, c), **once)
        state = pl.BlockSpec((1, H_A, DK_A, DV_A), lambda j, i: (bmap(j, i), 0, 0, 0))
        lanes = pl.BlockSpec((H_A, DK_A, DEC_BATCH), lambda j, i: (0, 0, 0), **once)
        args = [gi, lf, m0, proj, proj, proj, proj, proj, qt, kt, n0, g_head, c0]
        in_specs = [full((DEC_BATCH, H_A)), full((DEC_BATCH, H_A)), full((DEC_BATCH, H_A)),
                    cols(qk_w, 0), cols(qk_w, 1), cols(W_A, 1), cols(W_A, 2), cols(W_A, 3),
                    lanes, lanes, cols(qk_w, 0), full((H_A, DV_A)), state]
        out_shapes = [jax.ShapeDtypeStruct((DEC_BATCH, W_A), F32),
                      jax.ShapeDtypeStruct((DEC_BATCH, H_A, DK_A, DV_A), F32),
                      jax.ShapeDtypeStruct((DEC_BATCH, qk_w), F32),
                      jax.ShapeDtypeStruct((DEC_BATCH, H_A), F32)]
        out_specs = [full((DEC_BATCH, W_A)), state, full((DEC_BATCH, qk_w)), full((DEC_BATCH, H_A))]
        return args, in_specs, out_shapes, out_specs

    return build


def _s5_build_group(g, ldt_ref, lamr_ref, lami_ref, b1_ref, cc_ref, dsk_ref,
                    m_ref, w_ref, v_ref, p1_ref, p2_ref):
    dt = jnp.exp(ldt_ref[g])
    lam_re, lam_im = lamr_ref[g], lami_ref[g]
    mag = jnp.exp(lam_re * dt)
    ang = lam_im * dt
    a_re, a_im = mag * jnp.cos(ang), mag * jnp.sin(ang)
    den = lam_re * lam_re + lam_im * lam_im
    f_re = ((a_re - 1.0) * lam_re + a_im * lam_im) / den
    f_im = (a_im * lam_re - (a_re - 1.0) * lam_im) / den
    pows = [(jnp.ones_like(a_re), jnp.zeros_like(a_im))]
    for _ in range(S5_T):
        r_re, r_im = pows[-1]
        pows.append((r_re * a_re - r_im * a_im, r_re * a_im + r_im * a_re))

    def tall(vals):
        return jnp.concatenate([jnp.broadcast_to(x, (GROUP_B, LANES)) for x in vals], axis=0)

    af_re = tall([pows[S5_T - 1 - s][0] * f_re - pows[S5_T - 1 - s][1] * f_im for s in range(S5_T)])
    af_im = tall([pows[S5_T - 1 - s][0] * f_im + pows[S5_T - 1 - s][1] * f_re for s in range(S5_T)])
    lane = lax.broadcasted_iota(jnp.int32, (1, LANES), 1)
    sgn_lane = jnp.where(lane < P_B, -1.0, 1.0)
    b1 = b1_ref[g]
    b2 = sgn_lane * pltpu.roll(b1, P_B, axis=1)
    w_ref[g] = (af_re * b1 + af_im * b2).astype(BF16)

    eye = (lax.broadcasted_iota(jnp.int32, (LANES, LANES), 0)
           == lax.broadcasted_iota(jnp.int32, (LANES, LANES), 1))

    def column(row):
        return jnp.sum(jnp.where(eye, row, 0.0), axis=1, keepdims=True)

    ac_re, ac_im, fc_re, fc_im = column(a_re), column(a_im), column(f_re), column(f_im)
    kid = lax.broadcasted_iota(jnp.int32, (S5_T, LANES), 0)
    expand = (lax.broadcasted_iota(jnp.int32, (S5_T, S5_W), 1) // GROUP_B
              == lax.broadcasted_iota(jnp.int32, (S5_T, S5_W), 0)).astype(BF16)

    def lane_blocks(part):
        stacked = jnp.zeros((S5_T, LANES), F32)
        for k in range(S5_T):
            stacked = jnp.where(kid == k, pows[k][part], stacked)
        hi = stacked.astype(BF16)
        rest = stacked - hi.astype(F32)
        mid = rest.astype(BF16)
        lo = (rest - mid.astype(F32)).astype(BF16)
        return _dot_tn(hi, expand) + _dot_tn(mid, expand) + _dot_tn(lo, expand)

    q_re, q_im = lane_blocks(0), lane_blocks(1)
    cc = cc_ref[g]
    cs = pltpu.roll(cc, P_B, axis=0)
    rowi = lax.broadcasted_iota(jnp.int32, (LANES, 1), 0)
    sgn_row = jnp.where(rowi < P_B, 1.0, -1.0)

    def readout(r_re, r_im):
        return sgn_row * (r_re * cc) - r_im * cs

    v_ref[g] = readout(q_re * ac_re - q_im * ac_im, q_re * ac_im + q_im * ac_re).astype(BF16)
    vf = readout(q_re * fc_re - q_im * fc_im, q_re * fc_im + q_im * fc_re)
    kw = _dot3(b1[0:GROUP_B, :], vf)
    lane_w = lax.broadcasted_iota(jnp.int32, (GROUP_B, S5_W), 1)
    blocks = [kw]
    for s in range(1, S5_T):
        blocks.append(jnp.where(lane_w >= s * GROUP_B, pltpu.roll(kw, s * GROUP_B, axis=1), 0.0))
    toep = jnp.concatenate(blocks, axis=0)
    ri = lax.broadcasted_iota(jnp.int32, (S5_W, S5_W), 0)
    ci = lax.broadcasted_iota(jnp.int32, (S5_W, S5_W), 1)
    m_ref[g] = (toep + jnp.where(ri == ci, dsk_ref[g], 0.0)).astype(BF16)

    r_re, r_im = pows[S5_T]
    rid = lax.broadcasted_iota(jnp.int32, (8, LANES), 0)
    p1 = jnp.where(rid == S5_A1_ROW, a_re, 0.0)
    p2 = jnp.where(rid == S5_A1_ROW, sgn_lane * a_im, 0.0)
    for kk in range(S5_LEVELS):
        p1 = jnp.where(rid == kk, r_re, p1)
        p2 = jnp.where(rid == kk, sgn_lane * r_im, p2)
        r_re, r_im = r_re * r_re - r_im * r_im, 2.0 * (r_re * r_im)
    p1_ref[g] = p1
    p2_ref[g] = p2


def _s5_build_kernel(*refs):
    for g in range(OCT):
        _s5_build_group(g, *refs)


def _s5_build(ldt, lamr, lami, b1, cc, dsk):
    o3 = lambda a, b: pl.BlockSpec((OCT, a, b), lambda g: (g, 0, 0))
    return pl.pallas_call(
        _s5_build_kernel,
        grid=(G_B // OCT,),
        in_specs=[o3(1, 1), o3(1, LANES), o3(1, LANES),
                  o3(S5_W, LANES), o3(LANES, S5_W), o3(1, S5_W)],
        out_specs=[o3(S5_W, S5_W), o3(S5_W, LANES), o3(LANES, S5_W), o3(8, LANES), o3(8, LANES)],
        out_shape=[jax.ShapeDtypeStruct((G_B, S5_W, S5_W), BF16),
                   jax.ShapeDtypeStruct((G_B, S5_W, LANES), BF16),
                   jax.ShapeDtypeStruct((G_B, LANES, S5_W), BF16),
                   jax.ShapeDtypeStruct((G_B, 8, LANES), F32),
                   jax.ShapeDtypeStruct((G_B, 8, LANES), F32)],
        compiler_params=_cparams("parallel"),
        name="s5_build",
    )(ldt, lamr, lami, b1, cc, dsk)


def _s5_prompt_kernel(u_ref, m_ref, w_ref, v_ref, p1_ref, p2_ref, y_ref, xf_ref, y_s):
    nblk = SEQ // S5_T
    rows = BATCH * nblk
    bidx = lax.broadcasted_iota(jnp.int32, (rows, LANES), 0) & (nblk - 1)
    steps = [u_ref[pl.ds(s, rows, stride=S5_T), :] for s in range(S5_T)]
    for g in range(OCT):
        gl = slice(g * GROUP_B, (g + 1) * GROUP_B)
        u = jnp.concatenate([x[:, gl] for x in steps], axis=-1).astype(BF16)
        s = _dot(u, w_ref[g])
        for kk in range(S5_LEVELS):
            sh = 1 << kk
            r = jnp.where(bidx >= sh, pltpu.roll(s, sh, axis=0), 0.0)
            s = s + p1_ref[g, kk:kk + 1, :] * r + p2_ref[g, kk:kk + 1, :] * pltpu.roll(r, P_B, axis=1)
        x_prev = jnp.where(bidx >= 1, pltpu.roll(s, 1, axis=0), 0.0)
        y = _dot(u, m_ref[g]) + _dot(x_prev.astype(BF16), v_ref[g])
        y_s[g] = y
        for b in range(BATCH):
            xf_ref[g, b:b + 1, :] = s[(b + 1) * nblk - 1:(b + 1) * nblk, :]
    for t in range(S5_T):
        half = slice((t // OCT) * LANES, (t // OCT + 1) * LANES)
        tl = slice((t % OCT) * GROUP_B, (t % OCT + 1) * GROUP_B)
        y_ref[pl.ds(t, rows, stride=S5_T), :] = jnp.concatenate(
            [y_s[g, :, half][:, tl] for g in range(OCT)], axis=-1)


def _s5_prompt(u, m, w, v, p1, p2):
    mp = BATCH * SEQ
    rows = BATCH * (SEQ // S5_T)
    o3 = lambda a, b: pl.BlockSpec((OCT, a, b), lambda g: (g, 0, 0))
    return pl.pallas_call(
        _s5_prompt_kernel,
        grid=(G_B // OCT,),
        in_specs=[pl.BlockSpec((mp, LANES), lambda g: (0, g)),
                  o3(S5_W, S5_W), o3(S5_W, LANES), o3(LANES, S5_W), o3(8, LANES), o3(8, LANES)],
        out_specs=[pl.BlockSpec((mp, LANES), lambda g: (0, g)), o3(BATCH, LANES)],
        out_shape=[jax.ShapeDtypeStruct((mp, W_B), F32),
                   jax.ShapeDtypeStruct((G_B, BATCH, LANES), F32)],
        scratch_shapes=[pltpu.VMEM((OCT, rows, S5_W), F32)],
        compiler_params=_cparams("parallel"),
        name="s5_prompt",
    )(u, m, w, v, p1, p2)


def _s5_sample_kernel(u_ref, xr_ref, xi_ref, m_ref, w_ref, v_ref, p1_ref, p2_ref,
                      y_ref, x1r_ref, x1i_ref):
    lane = lax.broadcasted_iota(jnp.int32, (DEC_BATCH, LANES), 1)
    last = LANES - GROUP_B
    half = S5_W // 2
    a1 = slice(S5_A1_ROW, S5_A1_ROW + 1)
    u_all = u_ref[...]
    ys = []
    for g in range(OCT):
        ps = slice(g * P_B, (g + 1) * P_B)
        u = jnp.where(lane >= last, pltpu.roll(u_all, (last - g * GROUP_B) % LANES, axis=1), 0.0)
        u = u.astype(BF16)
        x0 = jnp.concatenate([xr_ref[:, ps], xi_ref[:, ps]], axis=-1)
        x1 = (p1_ref[g, a1, :] * x0 + p2_ref[g, a1, :] * pltpu.roll(x0, P_B, axis=1)
              + _dot(u, w_ref[g, half:, :]))
        x1r_ref[:, ps] = x1[:, :P_B]
        x1i_ref[:, ps] = x1[:, P_B:]
        y = (_dot(x0.astype(BF16), v_ref[g])[:, 0:GROUP_B]
             + _dot(u, m_ref[g, half:, :])[:, S5_W - GROUP_B:])
        ys.append(y)
    y_ref[...] = jnp.concatenate(ys, axis=-1)


def _s5_sample(u, xr, xi, m, w, v, p1, p2):
    o3 = lambda a, b: pl.BlockSpec((OCT, a, b), lambda g: (g, 0, 0))
    tile = pl.BlockSpec((DEC_BATCH, LANES), lambda g: (0, g))
    st = pl.BlockSpec((DEC_BATCH, OCT * P_B), lambda g: (0, g))
    st_shape = jax.ShapeDtypeStruct((DEC_BATCH, G_B * P_B), F32)
    return pl.pallas_call(
        _s5_sample_kernel,
        grid=(G_B // OCT,),
        in_specs=[tile, st, st, o3(S5_W, S5_W), o3(S5_W, LANES), o3(LANES, S5_W), o3(8, LANES), o3(8, LANES)],
        out_specs=[tile, st, st],
        out_shape=[jax.ShapeDtypeStruct((DEC_BATCH, W_B), F32), st_shape, st_shape],
        compiler_params=_cparams("parallel"),
        name="s5_sample",
    )(u, xr, xi, m, w, v, p1, p2)


def _glu_kernel(y_ref, w_ref, b_ref, z_ref, o_ref):
    y = _gelu_tanh(y_ref[...])
    gate = jax.nn.sigmoid(_dot(y.astype(BF16), w_ref[...]) + b_ref[...])
    o_ref[...] = (y * gate * _silu(z_ref[...].astype(F32))).astype(o_ref.dtype)


def _glu(y, w, b, proj_b, tm):
    m = y.shape[0]
    return pl.pallas_call(
        _glu_kernel,
        grid=(m // tm,),
        in_specs=[pl.BlockSpec((tm, W_B), lambda i: (i, 0)),
                  pl.BlockSpec((W_B, W_B), lambda i: (0, 0)),
                  pl.BlockSpec((1, W_B), lambda i: (0, 0)),
                  pl.BlockSpec((tm, W_B), lambda i: (i, 1))],
        out_specs=pl.BlockSpec((tm, W_B), lambda i: (i, 0)),
        out_shape=jax.ShapeDtypeStruct((m, W_B), BF16),
        compiler_params=_cparams("parallel"),
        name="glu",
    )(y, w, b, proj_b)


def _rope_table_kernel(cos_ref, sin_ref, *, pos0):
    shape = cos_ref.shape
    pos = lax.broadcasted_iota(jnp.int32, shape, 0).astype(F32) + pos0
    j = lax.broadcasted_iota(jnp.int32, shape, 1).astype(F32)
    ang = pos * jnp.power(ROPE_BASE, -(j / (DK_C // 2)))
    cos_ref[...] = jnp.cos(ang)
    sin_ref[...] = jnp.sin(ang)


def _rope_table(rows, pos0):
    shape = jax.ShapeDtypeStruct((rows, DK_C // 2), F32)
    return pl.pallas_call(functools.partial(_rope_table_kernel, pos0=float(pos0)),
                          out_shape=[shape, shape], name="rope_table")()


def _rope(x, cos, sin):
    half = DK_C // 2
    x1, x2 = x[:, :half], x[:, half:]
    return jnp.concatenate([x1 * cos - x2 * sin, x1 * sin + x2 * cos], axis=-1)


def _ret_prompt_kernel(q_ref, k_ref, v_ref, z_ref, cos_ref, sin_ref, gh_ref, y_ref, s_out, s_s):
    c = pl.program_id(1)
    t = CHUNK

    @pl.when(c == 0)
    def _():
        s_s[...] = jnp.zeros_like(s_s)

    row = lax.broadcasted_iota(jnp.int32, (t, t), 0)
    col = lax.broadcasted_iota(jnp.int32, (t, t), 1)
    tril = col <= row
    diff = (row - col).astype(F32)
    tpos = lax.broadcasted_iota(jnp.int32, (t, 1), 0).astype(F32)
    cos, sin = cos_ref[...], sin_ref[...]
    for h in range(H_C):
        lg = LOG_GAMMA[h]
        ks = slice(h * DK_C, (h + 1) * DK_C)
        vs = slice(h * DV_C, (h + 1) * DV_C)
        q = _rope(q_ref[:, ks].astype(F32), cos, sin).astype(BF16)
        k32 = _rope(k_ref[:, ks].astype(F32), cos, sin)
        v = v_ref[:, vs]
        mask = jnp.where(tril, jnp.exp(diff * lg), 0.0) * SCALE_C
        s = _dot_nt(q, k32.astype(BF16)) * mask
        s_old = s_s[h]
        o = _dot(s.astype(BF16), v) + _dot(q, s_old.astype(BF16)) * jnp.exp((tpos + 1.0) * lg)
        y_ref[:, vs] = (_rms(o, gh_ref[h:h + 1, :]) * _silu(z_ref[:, vs].astype(F32))).astype(BF16)
        k_tail = k32 * (jnp.exp((t - 1.0 - tpos) * lg) * SCALE_C)
        s_s[h] = math.exp(t * lg) * s_old + _dot_tn(k_tail.astype(BF16), v)

    @pl.when(c == pl.num_programs(1) - 1)
    def _():
        s_out[...] = s_s[...]


def _ret_prompt(proj, cos, sin, g_head):
    nc = SEQ // CHUNK
    t = CHUNK
    qk_w = H_C * DK_C
    return pl.pallas_call(
        _ret_prompt_kernel,
        grid=(BATCH, nc),
        in_specs=[pl.BlockSpec((None, t, qk_w), lambda b, c: (b, c, 0)),
                  pl.BlockSpec((None, t, qk_w), lambda b, c: (b, c, 1)),
                  pl.BlockSpec((None, t, W_C), lambda b, c: (b, c, 1)),
                  pl.BlockSpec((None, t, W_C), lambda b, c: (b, c, 2)),
                  pl.BlockSpec((t, DK_C // 2), lambda b, c: (c, 0)),
                  pl.BlockSpec((t, DK_C // 2), lambda b, c: (c, 0)),
                  pl.BlockSpec((H_C, DV_C), lambda b, c: (0, 0))],
        out_specs=[pl.BlockSpec((None, t, W_C), lambda b, c: (b, c, 0)),
                   pl.BlockSpec((None, H_C, DK_C, DV_C), lambda b, c: (b, 0, 0, 0))],
        out_shape=[jax.ShapeDtypeStruct((BATCH, SEQ, W_C), BF16),
                   jax.ShapeDtypeStruct((BATCH, H_C, DK_C, DV_C), F32)],
        scratch_shapes=[pltpu.VMEM((H_C, DK_C, DV_C), F32)],
        compiler_params=_cparams("parallel", "arbitrary"),
        name="ret_prompt",
    )(proj, proj, proj, proj, cos, sin, g_head)


def _rope_sample_kernel(q_ref, k_ref, cos_ref, sin_ref, qo_ref, ko_ref):
    cos, sin = cos_ref[0:1, :], sin_ref[0:1, :]
    for h in range(H_C):
        ks = slice(h * DK_C, (h + 1) * DK_C)
        qo_ref[:, ks] = _rope(q_ref[:, ks], cos, sin)
        ko_ref[:, ks] = _rope(k_ref[:, ks], cos, sin)


def _rope_sample(proj, cos, sin):
    qk_w = H_C * DK_C
    shape = jax.ShapeDtypeStruct((DEC_BATCH, qk_w), F32)
    return pl.pallas_call(
        _rope_sample_kernel,
        grid=(1,),
        in_specs=[pl.BlockSpec((DEC_BATCH, qk_w), lambda i: (0, 0)),
                  pl.BlockSpec((DEC_BATCH, qk_w), lambda i: (0, 1)),
                  pl.BlockSpec((8, DK_C // 2), lambda i: (0, 0)),
                  pl.BlockSpec((8, DK_C // 2), lambda i: (0, 0))],
        out_specs=[pl.BlockSpec((DEC_BATCH, qk_w), lambda i: (0, 0)),
                   pl.BlockSpec((DEC_BATCH, qk_w), lambda i: (0, 0))],
        out_shape=[shape, shape],
        compiler_params=_cparams("arbitrary"),
        name="rope_sample",
    )(proj, proj, cos, sin)


def _ret_sample_body(r0, q_ref, k_ref, v_ref, z_ref, qt_ref, kt_ref, gh_ref, s0_ref, y_ref, s1_ref):
    shift = lax.rem(DEC_BATCH - r0, DEC_BATCH)
    for h in range(H_C):
        gamma = math.exp(LOG_GAMMA[h])
        ks = slice(h * DK_C, (h + 1) * DK_C)
        vs = slice(h * DV_C, (h + 1) * DV_C)
        q_cols = pltpu.roll(qt_ref[h], shift, axis=1)
        k_cols = pltpu.roll(kt_ref[h], shift, axis=1)
        for j in range(s0_ref.shape[0]):
            rows = pl.ds(r0 + j, 1)
            v_row = v_ref[rows, vs]
            s_old = s0_ref[j, h]
            qk = jnp.sum(q_ref[rows, ks] * k_ref[rows, ks], axis=1, keepdims=True) * SCALE_C
            o = qk * v_row + jnp.sum(s_old * q_cols[:, j:j + 1], axis=0, keepdims=True) * gamma
            y_ref[rows, vs] = _rms(o, gh_ref[h:h + 1, :]) * _silu(z_ref[rows, vs])
            s1_ref[j, h] = gamma * s_old + (k_cols[:, j:j + 1] * SCALE_C) * v_row


def _ret_sample_rider(q, k, proj, qt, kt, g_head, s0):
    qk_w = H_C * DK_C

    def build(bmap):
        once = dict(pipeline_mode=pl.Buffered(1))
        fixed = lambda shape, *idx: pl.BlockSpec(shape, lambda j, i: idx or (0,) * len(shape), **once)
        state = pl.BlockSpec((1, H_C, DK_C, DV_C), lambda j, i: (bmap(j, i), 0, 0, 0))
        args = [q, k, proj, proj, qt, kt, g_head, s0]
        in_specs = [fixed((DEC_BATCH, qk_w)), fixed((DEC_BATCH, qk_w)),
                    fixed((DEC_BATCH, W_C), 0, 1), fixed((DEC_BATCH, W_C), 0, 2),
                    fixed((H_C, DK_C, DEC_BATCH)), fixed((H_C, DK_C, DEC_BATCH)),
                    fixed((H_C, DV_C)), state]
        out_shapes = [jax.ShapeDtypeStruct((DEC_BATCH, W_C), F32),
                      jax.ShapeDtypeStruct((DEC_BATCH, H_C, DK_C, DV_C), F32)]
        out_specs = [pl.BlockSpec((DEC_BATCH, W_C), lambda j, i: (0, 0)), state]
        return args, in_specs, out_shapes, out_specs

    return build


def kernel(x_prompt, x_sample, state_mlstm_C, state_mlstm_n, state_mlstm_m, state_s5_re, state_s5_im, state_ret, g_pre, g_post, w_in0, b_gates0, g_head_a, lam_re, lam_im, log_dt, b_re, b_im, c_re, c_im, d_skip, w_glu, b_glu, w_out0, w_in1, g_head_c, w_out1):
    mp = BATCH * SEQ
    xp = x_prompt.reshape(mp, D_MODEL).astype(F32)
    xs = x_sample.reshape(DEC_BATCH, D_MODEL).astype(F32)

    w0t = w_in0.T.astype(F32)
    w0g = jnp.pad(w0t[QKV0:GATE0], ((0, LANES - 2 * H_A), (0, 0)))
    w0b = w0t[GATE0:]
    wg = w_glu.astype(BF16)
    bglu = b_glu.reshape(1, W_B).astype(F32)
    g_pre = g_pre.astype(F32)
    g_post = g_post.astype(F32)
    bg = jnp.pad(b_gates0.astype(F32), (0, LANES - 2 * H_A))[None, :]
    gh_a = g_head_a.astype(F32)
    gh_c = g_head_c.astype(F32)

    dup = lambda a: jnp.concatenate([a, a], axis=-1).astype(F32)
    lamr, lami = dup(lam_re), dup(lam_im)
    b1 = jnp.concatenate([b_re.transpose(0, 2, 1), b_im.transpose(0, 2, 1)], axis=-1)
    cc = jnp.concatenate([c_re.transpose(0, 2, 1), c_im.transpose(0, 2, 1)], axis=1)
    s5m, s5w, s5v, s5p1, s5p2 = _s5_build(
        log_dt.reshape(G_B, 1, 1).astype(F32), lamr[:, None, :], lami[:, None, :],
        jnp.tile(b1.astype(F32), (1, S5_T, 1)), jnp.tile(cc.astype(F32), (1, 1, S5_T)),
        jnp.tile(d_skip.astype(F32), (1, S5_T))[:, None, :])

    a0, gates = _norm_gates(xp, g_pre[0:1], w0g, bg, 512)
    a0s, gates_s = _norm_gates(xs, g_pre[0:1], w0g, bg, DEC_BATCH)
    pas = _proj(a0s, a0s, w0t, QKV0, F32, DEC_BATCH, 1024, True)[0]
    qk_a = H_A * DK_A
    to_cols = lambda a, nh, dk: a.reshape(DEC_BATCH, nh, dk).transpose(1, 2, 0)
    pa, yas, c_s, n_s, m_s = _proj_rider(
        a0, w0t, QKV0, BF16, 512, 1024, True, _mlstm_sample_body, lambda step: step,
        _mlstm_sample_rider(
            gates_s[:, :H_A], gates_s[:, H_A:2 * H_A],
            state_mlstm_m.astype(F32), pas,
            to_cols(pas[:, :qk_a], H_A, DK_A), to_cols(pas[:, qk_a:2 * qk_a], H_A, DK_A),
            state_mlstm_n.reshape(DEC_BATCH, qk_a).astype(F32), gh_a, state_mlstm_C.astype(F32)))
    pb, pbs = _proj(a0, a0s, w0b, 2 * W_B, F32, 1024, 1024, True)
    gates3 = gates.reshape(BATCH, SEQ, LANES)
    gates_row = gates3[:, :, :2 * H_A].transpose(0, 2, 1)
    ya, c_p, n_p, m_p, w1, wo0, wo1 = _mlstm_prompt(
        pa.reshape(BATCH, SEQ, QKV0), gates3, gates_row, gh_a,
        [w_in1.astype(F32), w_out0.astype(F32), w_out1.astype(F32)])
    yb, xf = _s5_prompt(pb, s5m, s5w, s5v, s5p1, s5p2)
    yb = _glu(yb, wg, bglu, pb, 512)
    s5r_p = xf[:, :, :P_B].transpose(1, 0, 2)
    s5i_p = xf[:, :, P_B:].transpose(1, 0, 2)
    ybs, s5r_s, s5i_s = _s5_sample(
        pbs, state_s5_re.reshape(DEC_BATCH, G_B * P_B).astype(F32),
        state_s5_im.reshape(DEC_BATCH, G_B * P_B).astype(F32), s5m, s5w, s5v, s5p1, s5p2)
    s5r_s = s5r_s.reshape(DEC_BATCH, G_B, P_B)
    s5i_s = s5i_s.reshape(DEC_BATCH, G_B, P_B)
    ybs = _glu(ybs, wg, bglu, pbs, DEC_BATCH)
    h1, h1s, a1, a1s = _outproj(ya.reshape(mp, W_A), yb, yas, ybs, wo0, xp, xs, g_post[0:1],
                                g_pre[1:2], 512)

    cols1 = 2 * H_C * DK_C + 2 * W_C
    p1s = _proj(a1s, a1s, w1, cols1, F32, DEC_BATCH, 1024, False)[0]
    cos_s, sin_s = _rope_table(8, PAST_LEN)
    qs, ks = _rope_sample(p1s, cos_s, sin_s)
    p1, ycs, s_s = _proj_rider(
        a1, w1, cols1, BF16, 512, cols1 // 8, False, _ret_sample_body, lambda step: step,
        _ret_sample_rider(qs, ks, p1s, to_cols(qs, H_C, DK_C), to_cols(ks, H_C, DK_C),
                          gh_c, state_ret.astype(F32)))
    cos_p, sin_p = _rope_table(SEQ, 0)
    yc, s_p = _ret_prompt(p1.reshape(BATCH, SEQ, -1), cos_p, sin_p, gh_c)
    y_p, y_s = _outproj(yc.reshape(mp, W_C), None, ycs, None, wo1, h1, h1s, g_post[1:2], None, 512)

    return (y_p.reshape(BATCH, SEQ, D_MODEL), y_s.reshape(DEC_BATCH, 1, D_MODEL),
            c_p, n_p.reshape(BATCH, H_A, DK_A), m_p[:, :, 0, 0],
            s5r_p, s5i_p, s_p,
            c_s, n_s.reshape(DEC_BATCH, H_A, DK_A), m_s,
            s5r_s, s5i_s, s_s)
```

```python
import functools
import math

import jax
import jax.numpy as jnp
from jax import lax
from jax.experimental import pallas as pl
from jax.experimental.pallas import tpu as pltpu

F32 = jnp.float32
BF16 = jnp.bfloat16

D_MODEL = 2048
BATCH = 4
SEQ = 2048
DEC_BATCH = 128
PAST_LEN = 16384
H_A = 4
DK_A = 256
DV_A = 512
W_A = H_A * DV_A
GROUP_B = 16
G_B = 64
P_B = 64
W_B = G_B * GROUP_B
H_C = 8
DK_C = 256
DV_C = 512
W_C = H_C * DV_C
CHUNK = 256
NORM_EPS = 1e-6
ROPE_BASE = 10000.0
QKV0 = 2 * H_A * DK_A + 3 * W_A
GATE0 = QKV0 + 2 * H_A
SCALE_A = DK_A ** -0.5
SCALE_C = DK_C ** -0.5
LOG_GAMMA = tuple(math.log1p(-(2.0 ** (-5.0 - h))) for h in range(H_C))
S5_T = 16
S5_W = S5_T * GROUP_B
LANES = 128
OCT = LANES // GROUP_B
S5_LEVELS = int(math.log2(SEQ // S5_T))
S5_A1_ROW = S5_LEVELS
VMEM_LIMIT = 48 * 1024 * 1024
OUT_HEADS = 4
BIG_VMEM_LIMIT = 56 * 1024 * 1024


def _cparams(*sem):
    return pltpu.CompilerParams(dimension_semantics=sem, vmem_limit_bytes=VMEM_LIMIT)


def _dot(a, b):
    return jnp.dot(a, b, preferred_element_type=F32)


def _dot_nt(a, b):
    return lax.dot_general(a, b, (((1,), (1,)), ((), ())), preferred_element_type=F32)


def _dot_tn(a, b):
    return lax.dot_general(a, b, (((0,), (0,)), ((), ())), preferred_element_type=F32)


def _split(x):
    hi = x.astype(BF16)
    return hi, (x - hi.astype(F32)).astype(BF16)


def _dot3(a, b):
    a_hi, a_lo = _split(a)
    b_hi, b_lo = _split(b)
    return _dot(a_hi, b_hi) + _dot(a_hi, b_lo) + _dot(a_lo, b_hi)


def _log_sigmoid(x):
    return jnp.minimum(x, 0.0) - jnp.log1p(jnp.exp(-jnp.abs(x)))


def _silu(x):
    return x * jax.nn.sigmoid(x)


def _gelu_tanh(x):
    return 0.5 * x * (1.0 + jnp.tanh(math.sqrt(2.0 / math.pi) * (x + 0.044715 * (x * x * x))))


def _rms(x, g):
    return x * lax.rsqrt(jnp.mean(x * x, axis=-1, keepdims=True) + NORM_EPS) * g


def _norm_gates_kernel(x_ref, g_ref, wg_ref, bg_ref, a_ref, gates_ref):
    a = _rms(x_ref[...], g_ref[...])
    a_ref[...] = a.astype(BF16)
    a_hi, a_lo = _split(a)
    w_hi, w_lo = _split(wg_ref[...])
    pre = _dot_nt(a_hi, w_hi) + _dot_nt(a_hi, w_lo) + _dot_nt(a_lo, w_hi) + bg_ref[...]
    lane = lax.broadcasted_iota(jnp.int32, pre.shape, 1)
    gates_ref[...] = jnp.where((lane >= H_A) & (lane < 2 * H_A), _log_sigmoid(pre), pre)


def _norm_gates(x, g, wg, bg, tm):
    m = x.shape[0]
    return pl.pallas_call(
        _norm_gates_kernel,
        grid=(m // tm,),
        in_specs=[pl.BlockSpec((tm, D_MODEL), lambda i: (i, 0)),
                  pl.BlockSpec((1, D_MODEL), lambda i: (0, 0)),
                  pl.BlockSpec((LANES, D_MODEL), lambda i: (0, 0)),
                  pl.BlockSpec((1, LANES), lambda i: (0, 0))],
        out_specs=[pl.BlockSpec((tm, D_MODEL), lambda i: (i, 0)),
                   pl.BlockSpec((tm, LANES), lambda i: (i, 0))],
        out_shape=[jax.ShapeDtypeStruct((m, D_MODEL), BF16),
                   jax.ShapeDtypeStruct((m, LANES), F32)],
        compiler_params=_cparams("parallel"),
        name="norm_gates",
    )(x, g, wg, bg)


def _proj_kernel(xp_ref, xs_ref, w_ref, op_ref, os_ref, wb_ref, *, w_transposed):
    mm = _dot_nt if w_transposed else _dot

    @pl.when(pl.program_id(1) == 0)
    def _():
        wb_ref[...] = w_ref[...].astype(BF16)
        os_ref[...] = mm(xs_ref[...], wb_ref[...])

    op_ref[...] = mm(xp_ref[...], wb_ref[...]).astype(op_ref.dtype)


def _proj(xp, xs, w, n_cols, out_dtype, tm, tn, w_transposed):
    m, k = xp.shape
    ms = xs.shape[0]
    if w_transposed:
        w_spec = pl.BlockSpec((tn, k), lambda j, i: (j, 0))
        wb_shape = (tn, k)
    else:
        w_spec = pl.BlockSpec((k, tn), lambda j, i: (0, j))
        wb_shape = (k, tn)
    return pl.pallas_call(
        functools.partial(_proj_kernel, w_transposed=w_transposed),
        grid=(n_cols // tn, m // tm),
        in_specs=[pl.BlockSpec((tm, k), lambda j, i: (i, 0)),
                  pl.BlockSpec((ms, k), lambda j, i: (0, 0)),
                  w_spec],
        out_specs=[pl.BlockSpec((tm, tn), lambda j, i: (i, j)),
                   pl.BlockSpec((ms, tn), lambda j, i: (0, j))],
        out_shape=[jax.ShapeDtypeStruct((m, n_cols), out_dtype),
                   jax.ShapeDtypeStruct((ms, n_cols), F32)],
        scratch_shapes=[pltpu.VMEM(wb_shape, BF16)],
        compiler_params=_cparams("parallel", "arbitrary"),
        name="proj",
    )(xp, xs, w)


def _proj_rider_kernel(*refs, n_in, n_out, body, batch_of_step, n_i, w_transposed):
    xp_ref, w_ref = refs[:2]
    rider_in = refs[2:2 + n_in]
    op_ref = refs[2 + n_in]
    rider_out = refs[3 + n_in:3 + n_in + n_out]
    mm = _dot_nt if w_transposed else _dot
    i = pl.program_id(1)

    if w_ref.dtype == BF16:
        wb_ref = w_ref
    else:
        wb_ref = refs[-1]

        @pl.when(i == 0)
        def _():
            wb_ref[...] = w_ref[...].astype(BF16)

    op_ref[...] = mm(xp_ref[...], wb_ref[...]).astype(op_ref.dtype)
    body(batch_of_step(pl.program_id(0) * n_i + i), *rider_in, *rider_out)


def _proj_rider(xp, w, n_cols, out_dtype, tm, tn, w_transposed, body, batch_of_step, rider):
    m, k = xp.shape
    n_i = m // tm
    bmap = lambda j, i: batch_of_step(j * n_i + i)
    r_args, r_in_specs, r_out_shapes, r_out_specs = rider(bmap)
    if w_transposed:
        w_spec = pl.BlockSpec((tn, k), lambda j, i: (j, 0))
        wb_shape = (tn, k)
    else:
        w_spec = pl.BlockSpec((k, tn), lambda j, i: (0, j))
        wb_shape = (k, tn)
    return pl.pallas_call(
        functools.partial(_proj_rider_kernel, n_in=len(r_args), n_out=len(r_out_shapes), body=body,
                          batch_of_step=batch_of_step, n_i=n_i, w_transposed=w_transposed),
        grid=(n_cols // tn, n_i),
        in_specs=[pl.BlockSpec((tm, k), lambda j, i: (i, 0)), w_spec] + r_in_specs,
        out_specs=[pl.BlockSpec((tm, tn), lambda j, i: (i, j))] + r_out_specs,
        out_shape=[jax.ShapeDtypeStruct((m, n_cols), out_dtype)] + r_out_shapes,
        scratch_shapes=[] if w.dtype == BF16 else [pltpu.VMEM(wb_shape, BF16)],
        compiler_params=pltpu.CompilerParams(dimension_semantics=("arbitrary", "arbitrary"),
                                             vmem_limit_bytes=BIG_VMEM_LIMIT),
        name="proj_rider",
    )(xp, w, *r_args)


def _outproj_kernel(*refs, two, next_norm):
    refs = list(refs)
    ya_ref = refs.pop(0)
    yb_ref = refs.pop(0) if two else None
    yas_ref = refs.pop(0)
    ybs_ref = refs.pop(0) if two else None
    wa_ref = refs.pop(0)
    wb_ref = refs.pop(0) if two else None
    h_ref, hs_ref, g_ref = refs.pop(0), refs.pop(0), refs.pop(0)
    gn_ref = refs.pop(0) if next_norm else None
    o_ref, os_ref = refs.pop(0), refs.pop(0)
    a_ref, as_ref = (refs.pop(0), refs.pop(0)) if next_norm else (None, None)

    def run(x_ref, b_ref, res_ref, out_ref, nxt_ref):
        mix = _dot(x_ref[...].astype(BF16), wa_ref[...])
        if two:
            mix = mix + _dot(b_ref[...].astype(BF16), wb_ref[...])
        new = res_ref[...] + _rms(mix, g_ref[...])
        out_ref[...] = new
        if next_norm:
            nxt_ref[...] = _rms(new, gn_ref[...]).astype(BF16)

    @pl.when(pl.program_id(0) == 0)
    def _():
        run(yas_ref, ybs_ref, hs_ref, os_ref, as_ref)

    run(ya_ref, yb_ref, h_ref, o_ref, a_ref)


def _outproj(ya, yb, yas, ybs, w, h, hs, g, g_next, tm):
    m = h.shape[0]
    ms = hs.shape[0]
    ka = ya.shape[1]
    two = yb is not None
    next_norm = g_next is not None
    once = dict(pipeline_mode=pl.Buffered(1))
    row = lambda width: pl.BlockSpec((tm, width), lambda i: (i, 0))
    fixed = lambda rows, width: pl.BlockSpec((rows, width), lambda i: (0, 0), **once)
    if two:
        kb = yb.shape[1]
        in_specs = [row(ka), row(kb), fixed(ms, ka), fixed(ms, kb), fixed(ka, D_MODEL),
                    pl.BlockSpec((kb, D_MODEL), lambda i: (ka // kb, 0), **once)]
        args = [ya, yb, yas, ybs, w, w]
    else:
        in_specs = [row(ka), fixed(ms, ka), fixed(ka, D_MODEL)]
        args = [ya, yas, w]
    in_specs += [row(D_MODEL), fixed(ms, D_MODEL), fixed(1, D_MODEL)]
    args += [h, hs, g]
    sample_out = pl.BlockSpec((ms, D_MODEL), lambda i: (0, 0))
    out_specs = [row(D_MODEL), sample_out]
    out_shape = [jax.ShapeDtypeStruct((m, D_MODEL), F32), jax.ShapeDtypeStruct((ms, D_MODEL), F32)]
    if next_norm:
        in_specs.append(fixed(1, D_MODEL))
        args.append(g_next)
        out_specs += [row(D_MODEL), sample_out]
        out_shape += [jax.ShapeDtypeStruct((m, D_MODEL), BF16), jax.ShapeDtypeStruct((ms, D_MODEL), BF16)]
    return pl.pallas_call(
        functools.partial(_outproj_kernel, two=two, next_norm=next_norm),
        grid=(m // tm,),
        in_specs=in_specs,
        out_specs=out_specs,
        out_shape=out_shape,
        compiler_params=pltpu.CompilerParams(dimension_semantics=("arbitrary",),
                                             vmem_limit_bytes=BIG_VMEM_LIMIT),
        name="outproj",
    )(*args)


def _mlstm_prompt_kernel(*refs, n_cast):
    q_ref, k_ref, v_ref, o_ref, z_ref, gc_ref, gr_ref, gh_ref = refs[:8]
    cast_in = refs[8:8 + n_cast]
    y_ref, c_out, n_out, m_out = refs[8 + n_cast:12 + n_cast]
    cast_out = refs[12 + n_cast:12 + 2 * n_cast]
    c_s, n_s, m_s = refs[12 + 2 * n_cast:]
    c = pl.program_id(1)
    t = CHUNK
    for src, dst in zip(cast_in, cast_out):
        dst[...] = src[...].astype(BF16)

    @pl.when(c == 0)
    def _():
        c_s[...] = jnp.zeros_like(c_s)
        n_s[...] = jnp.zeros_like(n_s)
        m_s[...] = jnp.zeros_like(m_s)

    row = lax.broadcasted_iota(jnp.int32, (t, t), 0)
    col = lax.broadcasted_iota(jnp.int32, (t, t), 1)
    tril = col <= row
    triu = row <= col
    gc = gc_ref[...]
    gr = gr_ref[...]
    for h in range(H_A):
        i_col = gc[:, h:h + 1]
        i_row = gr[h:h + 1, :]
        lf_col = gc[:, H_A + h:H_A + h + 1]
        lf_row = gr[H_A + h:H_A + h + 1, :]
        b_col = jnp.sum(jnp.where(tril, lf_row, 0.0), axis=1, keepdims=True)
        b_row = jnp.sum(jnp.where(triu, lf_col, 0.0), axis=0, keepdims=True)
        m_prev = m_s[h][:, 0:1]
        d = jnp.where(tril, b_col - b_row + i_row, -jnp.inf)
        inter = b_col + m_prev
        m_t = jnp.maximum(inter, jnp.max(d, axis=1, keepdims=True))
        w_intra = jnp.exp(d - m_t)
        w_inter = jnp.exp(inter - m_t) * SCALE_A
        q = q_ref[:, h * DK_A:(h + 1) * DK_A]
        k = k_ref[:, h * DK_A:(h + 1) * DK_A]
        v = v_ref[:, h * DV_A:(h + 1) * DV_A]
        s = _dot_nt(q, k) * (w_intra * SCALE_A)
        c_old = c_s[h]
        n_old = n_s[h]
        num = _dot(s.astype(BF16), v) + w_inter * _dot(q, c_old.astype(BF16))
        qn = jnp.sum(q.astype(F32) * n_old, axis=1, keepdims=True)
        den = jnp.sum(s, axis=1, keepdims=True) + w_inter * qn
        hh = num * (1.0 / jnp.maximum(jnp.abs(den), jnp.exp(-m_t)))
        hn = _rms(hh, gh_ref[h:h + 1, :])
        o = o_ref[:, h * DV_A:(h + 1) * DV_A].astype(F32)
        z = z_ref[:, h * DV_A:(h + 1) * DV_A].astype(F32)
        gate = z / ((1.0 + jnp.exp(-o)) * (1.0 + jnp.exp(-z)))
        y_ref[:, h * DV_A:(h + 1) * DV_A] = (hn * gate).astype(BF16)
        b_last = b_col[t - 1:t, :]
        g_col = b_last - b_col + i_col
        m_new = jnp.maximum(b_last + m_prev, jnp.max(g_col, axis=0, keepdims=True))
        e_col = jnp.exp(g_col - m_new)
        decay = jnp.exp(b_last + m_prev - m_new)
        ke = k.astype(F32) * e_col
        c_s[h] = decay * c_old + _dot_tn(ke.astype(BF16), v)
        n_s[h] = decay * n_old + jnp.sum(ke, axis=0, keepdims=True)
        m_s[h] = jnp.broadcast_to(m_new, (1, LANES))

    @pl.when(c == pl.num_programs(1) - 1)
    def _():
        c_out[...] = c_s[...]
        n_out[...] = n_s[...]
        m_out[...] = m_s[...]


def _mlstm_prompt(proj, gates_col, gates_row, g_head, weights):
    nc = SEQ // CHUNK
    t = CHUNK
    qk_w = H_A * DK_A
    steps = BATCH * nc
    slab = lambda w: pl.BlockSpec((w.shape[0] // steps, w.shape[1]), lambda b, c: (b * nc + c, 0))
    return pl.pallas_call(
        functools.partial(_mlstm_prompt_kernel, n_cast=len(weights)),
        grid=(BATCH, nc),
        in_specs=[pl.BlockSpec((None, t, qk_w), lambda b, c: (b, c, 0)),
                  pl.BlockSpec((None, t, qk_w), lambda b, c: (b, c, 1)),
                  pl.BlockSpec((None, t, W_A), lambda b, c: (b, c, 1)),
                  pl.BlockSpec((None, t, W_A), lambda b, c: (b, c, 2)),
                  pl.BlockSpec((None, t, W_A), lambda b, c: (b, c, 3)),
                  pl.BlockSpec((None, t, LANES), lambda b, c: (b, c, 0)),
                  pl.BlockSpec((None, 2 * H_A, t), lambda b, c: (b, 0, c)),
                  pl.BlockSpec((H_A, DV_A), lambda b, c: (0, 0))] + [slab(w) for w in weights],
        out_specs=[pl.BlockSpec((None, t, W_A), lambda b, c: (b, c, 0)),
                   pl.BlockSpec((None, H_A, DK_A, DV_A), lambda b, c: (b, 0, 0, 0)),
                   pl.BlockSpec((None, H_A, 1, DK_A), lambda b, c: (b, 0, 0, 0)),
                   pl.BlockSpec((None, H_A, 1, LANES), lambda b, c: (b, 0, 0, 0))]
        + [slab(w) for w in weights],
        out_shape=[jax.ShapeDtypeStruct((BATCH, SEQ, W_A), BF16),
                   jax.ShapeDtypeStruct((BATCH, H_A, DK_A, DV_A), F32),
                   jax.ShapeDtypeStruct((BATCH, H_A, 1, DK_A), F32),
                   jax.ShapeDtypeStruct((BATCH, H_A, 1, LANES), F32)]
        + [jax.ShapeDtypeStruct(w.shape, BF16) for w in weights],
        scratch_shapes=[pltpu.VMEM((H_A, DK_A, DV_A), F32),
                        pltpu.VMEM((H_A, 1, DK_A), F32),
                        pltpu.VMEM((H_A, 1, LANES), F32)],
        compiler_params=_cparams("arbitrary", "arbitrary"),
        name="mlstm_prompt",
    )(proj, proj, proj, proj, proj, gates_col, gates_row, g_head, *weights)


def _mlstm_sample_body(r0, gi_ref, lf_ref, m0_ref, q_ref, k_ref, v_ref, o_ref, z_ref,
                       qt_ref, kt_ref, n0_ref, gh_ref, c0_ref,
                       y_ref, c1_ref, n1_ref, m1_ref):
    shift = lax.rem(DEC_BATCH - r0, DEC_BATCH)
    q_cols = [pltpu.roll(qt_ref[h], shift, axis=1) for h in range(H_A)]
    k_cols = [pltpu.roll(kt_ref[h], shift, axis=1) for h in range(H_A)]
    for j in range(c0_ref.shape[0]):
        rows = pl.ds(r0 + j, 1)
        i_v = gi_ref[rows, :]
        lf_v = lf_ref[rows, :]
        m0_v = m0_ref[rows, :]
        m_t = jnp.maximum(lf_v + m0_v, i_v)
        w_in = jnp.exp(i_v - m_t)
        w_st = jnp.exp(lf_v + m0_v - m_t)
        floor = jnp.exp(-m_t)
        m1_ref[rows, :] = m_t
        for h in range(H_A):
            ks = slice(h * DK_A, (h + 1) * DK_A)
            vs = slice(h * DV_A, (h + 1) * DV_A)
            wi = w_in[:, h:h + 1]
            ws = w_st[:, h:h + 1]
            q_col = q_cols[h][:, j:j + 1]
            k_col = k_cols[h][:, j:j + 1]
            q_row = q_ref[rows, ks]
            k_row = k_ref[rows, ks]
            v_row = v_ref[rows, vs]
            n_row = n0_ref[rows, ks]
            c_old = c0_ref[j, h]
            qk = jnp.sum(q_row * k_row, axis=1, keepdims=True) * SCALE_A
            s = qk * wi
            q_c = jnp.sum(c_old * q_col, axis=0, keepdims=True) * SCALE_A
            qn = jnp.sum(q_row * n_row, axis=1, keepdims=True) * SCALE_A
            num = s * v_row + ws * q_c
            den = s + ws * qn
            hh = num / jnp.maximum(jnp.abs(den), floor[:, h:h + 1])
            hn = _rms(hh, gh_ref[h:h + 1, :])
            y_ref[rows, vs] = hn * jax.nn.sigmoid(o_ref[rows, vs]) * _silu(z_ref[rows, vs])
            c1_ref[j, h] = ws * c_old + (wi * k_col) * v_row
            n1_ref[rows, ks] = ws * n_row + wi * k_row


def _mlstm_sample_rider(gi, lf, m0, proj, qt, kt, n0, g_head, c0):
    qk_w = H_A * DK_A

    def build(bmap):
        once = dict(pipeline_mode=pl.Buffered(1))
        full = lambda shape: pl.BlockSpec(shape, lambda j, i: (0,) * len(shape))
        cols = lambda width, c: pl.BlockSpec((DEC_BATCH, width), lambda j, i: (0, c), **once)
        state = pl.BlockSpec((1, H_A, DK_A, DV_A), lambda j, i: (bmap(j, i), 0, 0, 0))
        lanes = pl.BlockSpec((H_A, DK_A, DEC_BATCH), lambda j, i: (0, 0, 0), **once)
        args = [gi, lf, m0, proj, proj, proj, proj, proj, qt, kt, n0, g_head, c0]
        in_specs = [full((DEC_BATCH, H_A)), full((DEC_BATCH, H_A)), full((DEC_BATCH, H_A)),
                    cols(qk_w, 0), cols(qk_w, 1), cols(W_A, 1), cols(W_A, 2), cols(W_A, 3),
                    lanes, lanes, cols(qk_w, 0), full((H_A, DV_A)), state]
        out_shapes = [jax.ShapeDtypeStruct((DEC_BATCH, W_A), F32),
                      jax.ShapeDtypeStruct((DEC_BATCH, H_A, DK_A, DV_A), F32),
                      jax.ShapeDtypeStruct((DEC_BATCH, qk_w), F32),
                      jax.ShapeDtypeStruct((DEC_BATCH, H_A), F32)]
        out_specs = [full((DEC_BATCH, W_A)), state, full((DEC_BATCH, qk_w)), full((DEC_BATCH, H_A))]
        return args, in_specs, out_shapes, out_specs

    return build


def _s5_build_group(g, ldt_ref, lamr_ref, lami_ref, b1_ref, cc_ref, dsk_ref,
                    m_ref, w_ref, v_ref, p1_ref, p2_ref):
    dt = jnp.exp(ldt_ref[g])
    lam_re, lam_im = lamr_ref[g], lami_ref[g]
    mag = jnp.exp(lam_re * dt)
    ang = lam_im * dt
    a_re, a_im = mag * jnp.cos(ang), mag * jnp.sin(ang)
    den = lam_re * lam_re + lam_im * lam_im
    f_re = ((a_re - 1.0) * lam_re + a_im * lam_im) / den
    f_im = (a_im * lam_re - (a_re - 1.0) * lam_im) / den
    pows = [(jnp.ones_like(a_re), jnp.zeros_like(a_im))]
    for _ in range(S5_T):
        r_re, r_im = pows[-1]
        pows.append((r_re * a_re - r_im * a_im, r_re * a_im + r_im * a_re))

    def tall(vals):
        return jnp.concatenate([jnp.broadcast_to(x, (GROUP_B, LANES)) for x in vals], axis=0)

    af_re = tall([pows[S5_T - 1 - s][0] * f_re - pows[S5_T - 1 - s][1] * f_im for s in range(S5_T)])
    af_im = tall([pows[S5_T - 1 - s][0] * f_im + pows[S5_T - 1 - s][1] * f_re for s in range(S5_T)])
    lane = lax.broadcasted_iota(jnp.int32, (1, LANES), 1)
    sgn_lane = jnp.where(lane < P_B, -1.0, 1.0)
    b1 = b1_ref[g]
    b2 = sgn_lane * pltpu.roll(b1, P_B, axis=1)
    w_ref[g] = (af_re * b1 + af_im * b2).astype(BF16)

    eye = (lax.broadcasted_iota(jnp.int32, (LANES, LANES), 0)
           == lax.broadcasted_iota(jnp.int32, (LANES, LANES), 1))

    def column(row):
        return jnp.sum(jnp.where(eye, row, 0.0), axis=1, keepdims=True)

    ac_re, ac_im, fc_re, fc_im = column(a_re), column(a_im), column(f_re), column(f_im)
    kid = lax.broadcasted_iota(jnp.int32, (S5_T, LANES), 0)
    expand = (lax.broadcasted_iota(jnp.int32, (S5_T, S5_W), 1) // GROUP_B
              == lax.broadcasted_iota(jnp.int32, (S5_T, S5_W), 0)).astype(BF16)

    def lane_blocks(part):
        stacked = jnp.zeros((S5_T, LANES), F32)
        for k in range(S5_T):
            stacked = jnp.where(kid == k, pows[k][part], stacked)
        hi = stacked.astype(BF16)
        rest = stacked - hi.astype(F32)
        mid = rest.astype(BF16)
        lo = (rest - mid.astype(F32)).astype(BF16)
        return _dot_tn(hi, expand) + _dot_tn(mid, expand) + _dot_tn(lo, expand)

    q_re, q_im = lane_blocks(0), lane_blocks(1)
    cc = cc_ref[g]
    cs = pltpu.roll(cc, P_B, axis=0)
    rowi = lax.broadcasted_iota(jnp.int32, (LANES, 1), 0)
    sgn_row = jnp.where(rowi < P_B, 1.0, -1.0)

    def readout(r_re, r_im):
        return sgn_row * (r_re * cc) - r_im * cs

    v_ref[g] = readout(q_re * ac_re - q_im * ac_im, q_re * ac_im + q_im * ac_re).astype(BF16)
    vf = readout(q_re * fc_re - q_im * fc_im, q_re * fc_im + q_im * fc_re)
    kw = _dot3(b1[0:GROUP_B, :], vf)
    lane_w = lax.broadcasted_iota(jnp.int32, (GROUP_B, S5_W), 1)
    blocks = [kw]
    for s in range(1, S5_T):
        blocks.append(jnp.where(lane_w >= s * GROUP_B, pltpu.roll(kw, s * GROUP_B, axis=1), 0.0))
    toep = jnp.concatenate(blocks, axis=0)
    ri = lax.broadcasted_iota(jnp.int32, (S5_W, S5_W), 0)
    ci = lax.broadcasted_iota(jnp.int32, (S5_W, S5_W), 1)
    m_ref[g] = (toep + jnp.where(ri == ci, dsk_ref[g], 0.0)).astype(BF16)

    r_re, r_im = pows[S5_T]
    rid = lax.broadcasted_iota(jnp.int32, (8, LANES), 0)
    p1 = jnp.where(rid == S5_A1_ROW, a_re, 0.0)
    p2 = jnp.where(rid == S5_A1_ROW, sgn_lane * a_im, 0.0)
    for kk in range(S5_LEVELS):
        p1 = jnp.where(rid == kk, r_re, p1)
        p2 = jnp.where(rid == kk, sgn_lane * r_im, p2)
        r_re, r_im = r_re * r_re - r_im * r_im, 2.0 * (r_re * r_im)
    p1_ref[g] = p1
    p2_ref[g] = p2


def _s5_build_kernel(*refs):
    for g in range(OCT):
        _s5_build_group(g, *refs)


def _s5_build(ldt, lamr, lami, b1, cc, dsk):
    o3 = lambda a, b: pl.BlockSpec((OCT, a, b), lambda g: (g, 0, 0))
    return pl.pallas_call(
        _s5_build_kernel,
        grid=(G_B // OCT,),
        in_specs=[o3(1, 1), o3(1, LANES), o3(1, LANES),
                  o3(S5_W, LANES), o3(LANES, S5_W), o3(1, S5_W)],
        out_specs=[o3(S5_W, S5_W), o3(S5_W, LANES), o3(LANES, S5_W), o3(8, LANES), o3(8, LANES)],
        out_shape=[jax.ShapeDtypeStruct((G_B, S5_W, S5_W), BF16),
                   jax.ShapeDtypeStruct((G_B, S5_W, LANES), BF16),
                   jax.ShapeDtypeStruct((G_B, LANES, S5_W), BF16),
                   jax.ShapeDtypeStruct((G_B, 8, LANES), F32),
                   jax.ShapeDtypeStruct((G_B, 8, LANES), F32)],
        compiler_params=_cparams("parallel"),
        name="s5_build",
    )(ldt, lamr, lami, b1, cc, dsk)


def _s5_prompt_kernel(u_ref, m_ref, w_ref, v_ref, p1_ref, p2_ref, y_ref, xf_ref, y_s):
    nblk = SEQ // S5_T
    rows = BATCH * nblk
    bidx = lax.broadcasted_iota(jnp.int32, (rows, LANES), 0) & (nblk - 1)
    steps = [u_ref[pl.ds(s, rows, stride=S5_T), :] for s in range(S5_T)]
    for g in range(OCT):
        gl = slice(g * GROUP_B, (g + 1) * GROUP_B)
        u = jnp.concatenate([x[:, gl] for x in steps], axis=-1).astype(BF16)
        s = _dot(u, w_ref[g])
        for kk in range(S5_LEVELS):
            sh = 1 << kk
            r = jnp.where(bidx >= sh, pltpu.roll(s, sh, axis=0), 0.0)
            s = s + p1_ref[g, kk:kk + 1, :] * r + p2_ref[g, kk:kk + 1, :] * pltpu.roll(r, P_B, axis=1)
        x_prev = jnp.where(bidx >= 1, pltpu.roll(s, 1, axis=0), 0.0)
        y = _dot(u, m_ref[g]) + _dot(x_prev.astype(BF16), v_ref[g])
        y_s[g] = y
        for b in range(BATCH):
            xf_ref[g, b:b + 1, :] = s[(b + 1) * nblk - 1:(b + 1) * nblk, :]
    for t in range(S5_T):
        half = slice((t // OCT) * LANES, (t // OCT + 1) * LANES)
        tl = slice((t % OCT) * GROUP_B, (t % OCT + 1) * GROUP_B)
        y_ref[pl.ds(t, rows, stride=S5_T), :] = jnp.concatenate(
            [y_s[g, :, half][:, tl] for g in range(OCT)], axis=-1)


def _s5_prompt(u, m, w, v, p1, p2):
    mp = BATCH * SEQ
    rows = BATCH * (SEQ // S5_T)
    o3 = lambda a, b: pl.BlockSpec((OCT, a, b), lambda g: (g, 0, 0))
    return pl.pallas_call(
        _s5_prompt_kernel,
        grid=(G_B // OCT,),
        in_specs=[pl.BlockSpec((mp, LANES), lambda g: (0, g)),
                  o3(S5_W, S5_W), o3(S5_W, LANES), o3(LANES, S5_W), o3(8, LANES), o3(8, LANES)],
        out_specs=[pl.BlockSpec((mp, LANES), lambda g: (0, g)), o3(BATCH, LANES)],
        out_shape=[jax.ShapeDtypeStruct((mp, W_B), F32),
                   jax.ShapeDtypeStruct((G_B, BATCH, LANES), F32)],
        scratch_shapes=[pltpu.VMEM((OCT, rows, S5_W), F32)],
        compiler_params=_cparams("parallel"),
        name="s5_prompt",
    )(u, m, w, v, p1, p2)


def _s5_sample_kernel(u_ref, xr_ref, xi_ref, m_ref, w_ref, v_ref, p1_ref, p2_ref,
                      y_ref, x1r_ref, x1i_ref):
    lane = lax.broadcasted_iota(jnp.int32, (DEC_BATCH, LANES), 1)
    last = LANES - GROUP_B
    half = S5_W // 2
    a1 = slice(S5_A1_ROW, S5_A1_ROW + 1)
    u_all = u_ref[...]
    ys = []
    for g in range(OCT):
        ps = slice(g * P_B, (g + 1) * P_B)
        u = jnp.where(lane >= last, pltpu.roll(u_all, (last - g * GROUP_B) % LANES, axis=1), 0.0)
        u = u.astype(BF16)
        x0 = jnp.concatenate([xr_ref[:, ps], xi_ref[:, ps]], axis=-1)
        x1 = (p1_ref[g, a1, :] * x0 + p2_ref[g, a1, :] * pltpu.roll(x0, P_B, axis=1)
              + _dot(u, w_ref[g, half:, :]))
        x1r_ref[:, ps] = x1[:, :P_B]
        x1i_ref[:, ps] = x1[:, P_B:]
        y = (_dot(x0.astype(BF16), v_ref[g])[:, 0:GROUP_B]
             + _dot(u, m_ref[g, half:, :])[:, S5_W - GROUP_B:])
        ys.append(y)
    y_ref[...] = jnp.concatenate(ys, axis=-1)


def _s5_sample(u, xr, xi, m, w, v, p1, p2):
    o3 = lambda a, b: pl.BlockSpec((OCT, a, b), lambda g: (g, 0, 0))
    tile = pl.BlockSpec((DEC_BATCH, LANES), lambda g: (0, g))
    st = pl.BlockSpec((DEC_BATCH, OCT * P_B), lambda g: (0, g))
    st_shape = jax.ShapeDtypeStruct((DEC_BATCH, G_B * P_B), F32)
    return pl.pallas_call(
        _s5_sample_kernel,
        grid=(G_B // OCT,),
        in_specs=[tile, st, st, o3(S5_W, S5_W), o3(S5_W, LANES), o3(LANES, S5_W), o3(8, LANES), o3(8, LANES)],
        out_specs=[tile, st, st],
        out_shape=[jax.ShapeDtypeStruct((DEC_BATCH, W_B), F32), st_shape, st_shape],
        compiler_params=_cparams("parallel"),
        name="s5_sample",
    )(u, xr, xi, m, w, v, p1, p2)


def _glu_kernel(y_ref, w_ref, b_ref, z_ref, o_ref):
    y = _gelu_tanh(y_ref[...])
    gate = jax.nn.sigmoid(_dot(y.astype(BF16), w_ref[...]) + b_ref[...])
    o_ref[...] = (y * gate * _silu(z_ref[...].astype(F32))).astype(o_ref.dtype)


def _glu(y, w, b, proj_b, tm):
    m = y.shape[0]
    return pl.pallas_call(
        _glu_kernel,
        grid=(m // tm,),
        in_specs=[pl.BlockSpec((tm, W_B), lambda i: (i, 0)),
                  pl.BlockSpec((W_B, W_B), lambda i: (0, 0)),
                  pl.BlockSpec((1, W_B), lambda i: (0, 0)),
                  pl.BlockSpec((tm, W_B), lambda i: (i, 1))],
        out_specs=pl.BlockSpec((tm, W_B), lambda i: (i, 0)),
        out_shape=jax.ShapeDtypeStruct((m, W_B), BF16),
        compiler_params=_cparams("parallel"),
        name="glu",
    )(y, w, b, proj_b)


def _rope_table_kernel(cos_ref, sin_ref, *, pos0):
    shape = cos_ref.shape
    pos = lax.broadcasted_iota(jnp.int32, shape, 0).astype(F32) + pos0
    j = lax.broadcasted_iota(jnp.int32, shape, 1).astype(F32)
    ang = pos * jnp.power(ROPE_BASE, -(j / (DK_C // 2)))
    cos_ref[...] = jnp.cos(ang)
    sin_ref[...] = jnp.sin(ang)


def _rope_table(rows, pos0):
    shape = jax.ShapeDtypeStruct((rows, DK_C // 2), F32)
    return pl.pallas_call(functools.partial(_rope_table_kernel, pos0=float(pos0)),
                          out_shape=[shape, shape], name="rope_table")()


def _rope(x, cos, sin):
    half = DK_C // 2
    x1, x2 = x[:, :half], x[:, half:]
    return jnp.concatenate([x1 * cos - x2 * sin, x1 * sin + x2 * cos], axis=-1)


def _ret_prompt_kernel(q_ref, k_ref, v_ref, z_ref, cos_ref, sin_ref, gh_ref, wo_ref, h_ref, gp_ref,
                       o_ref, s_out, s_s):
    c = pl.program_id(1)
    t = CHUNK

    @pl.when(c == 0)
    def _():
        s_s[...] = jnp.zeros_like(s_s)

    row = lax.broadcasted_iota(jnp.int32, (t, t), 0)
    col = lax.broadcasted_iota(jnp.int32, (t, t), 1)
    tril = col <= row
    diff = (row - col).astype(F32)
    tpos = lax.broadcasted_iota(jnp.int32, (t, 1), 0).astype(F32)
    cos, sin = cos_ref[...], sin_ref[...]
    mix, ys = None, []
    for h in range(H_C):
        lg = LOG_GAMMA[h]
        ks = slice(h * DK_C, (h + 1) * DK_C)
        vs = slice(h * DV_C, (h + 1) * DV_C)
        q = _rope(q_ref[:, ks].astype(F32), cos, sin).astype(BF16)
        k32 = _rope(k_ref[:, ks].astype(F32), cos, sin)
        v = v_ref[:, vs]
        mask = jnp.where(tril, jnp.exp(diff * lg), 0.0) * SCALE_C
        s = _dot_nt(q, k32.astype(BF16)) * mask
        s_old = s_s[h]
        o = _dot(s.astype(BF16), v) + _dot(q, s_old.astype(BF16)) * jnp.exp((tpos + 1.0) * lg)
        y = (_rms(o, gh_ref[h:h + 1, :]) * _silu(z_ref[:, vs].astype(F32))).astype(BF16)
        ys.append(y)
        if len(ys) == OUT_HEADS:
            rows = slice((h + 1 - OUT_HEADS) * DV_C, (h + 1) * DV_C)
            part = _dot(jnp.concatenate(ys, axis=-1), wo_ref[rows, :])
            mix = part if mix is None else mix + part
            ys = []
        k_tail = k32 * (jnp.exp((t - 1.0 - tpos) * lg) * SCALE_C)
        s_s[h] = math.exp(t * lg) * s_old + _dot_tn(k_tail.astype(BF16), v)
    o_ref[...] = h_ref[...] + _rms(mix, gp_ref[...])

    @pl.when(c == pl.num_programs(1) - 1)
    def _():
        s_out[...] = s_s[...]


def _ret_prompt(proj, cos, sin, g_head, wo, h, g_post):
    nc = SEQ // CHUNK
    t = CHUNK
    qk_w = H_C * DK_C
    return pl.pallas_call(
        _ret_prompt_kernel,
        grid=(BATCH, nc),
        in_specs=[pl.BlockSpec((None, t, qk_w), lambda b, c: (b, c, 0)),
                  pl.BlockSpec((None, t, qk_w), lambda b, c: (b, c, 1)),
                  pl.BlockSpec((None, t, W_C), lambda b, c: (b, c, 1)),
                  pl.BlockSpec((None, t, W_C), lambda b, c: (b, c, 2)),
                  pl.BlockSpec((t, DK_C // 2), lambda b, c: (c, 0)),
                  pl.BlockSpec((t, DK_C // 2), lambda b, c: (c, 0)),
                  pl.BlockSpec((H_C, DV_C), lambda b, c: (0, 0)),
                  pl.BlockSpec((W_C, D_MODEL), lambda b, c: (0, 0), pipeline_mode=pl.Buffered(1)),
                  pl.BlockSpec((None, t, D_MODEL), lambda b, c: (b, c, 0)),
                  pl.BlockSpec((1, D_MODEL), lambda b, c: (0, 0))],
        out_specs=[pl.BlockSpec((None, t, D_MODEL), lambda b, c: (b, c, 0)),
                   pl.BlockSpec((None, H_C, DK_C, DV_C), lambda b, c: (b, 0, 0, 0))],
        out_shape=[jax.ShapeDtypeStruct((BATCH, SEQ, D_MODEL), F32),
                   jax.ShapeDtypeStruct((BATCH, H_C, DK_C, DV_C), F32)],
        scratch_shapes=[pltpu.VMEM((H_C, DK_C, DV_C), F32)],
        compiler_params=pltpu.CompilerParams(dimension_semantics=("parallel", "arbitrary"),
                                             vmem_limit_bytes=BIG_VMEM_LIMIT),
        name="ret_prompt",
    )(proj, proj, proj, proj, cos, sin, g_head, wo, h, g_post)


def _rope_sample_kernel(q_ref, k_ref, cos_ref, sin_ref, qo_ref, ko_ref):
    cos, sin = cos_ref[0:1, :], sin_ref[0:1, :]
    for h in range(H_C):
        ks = slice(h * DK_C, (h + 1) * DK_C)
        qo_ref[:, ks] = _rope(q_ref[:, ks], cos, sin)
        ko_ref[:, ks] = _rope(k_ref[:, ks], cos, sin)


def _rope_sample(proj, cos, sin):
    qk_w = H_C * DK_C
    shape = jax.ShapeDtypeStruct((DEC_BATCH, qk_w), F32)
    return pl.pallas_call(
        _rope_sample_kernel,
        grid=(1,),
        in_specs=[pl.BlockSpec((DEC_BATCH, qk_w), lambda i: (0, 0)),
                  pl.BlockSpec((DEC_BATCH, qk_w), lambda i: (0, 1)),
                  pl.BlockSpec((8, DK_C // 2), lambda i: (0, 0)),
                  pl.BlockSpec((8, DK_C // 2), lambda i: (0, 0))],
        out_specs=[pl.BlockSpec((DEC_BATCH, qk_w), lambda i: (0, 0)),
                   pl.BlockSpec((DEC_BATCH, qk_w), lambda i: (0, 0))],
        out_shape=[shape, shape],
        compiler_params=_cparams("arbitrary"),
        name="rope_sample",
    )(proj, proj, cos, sin)


def _ret_sample_body(r0, q_ref, k_ref, v_ref, z_ref, qt_ref, kt_ref, gh_ref, s0_ref, y_ref, s1_ref):
    shift = lax.rem(DEC_BATCH - r0, DEC_BATCH)
    for h in range(H_C):
        gamma = math.exp(LOG_GAMMA[h])
        ks = slice(h * DK_C, (h + 1) * DK_C)
        vs = slice(h * DV_C, (h + 1) * DV_C)
        q_cols = pltpu.roll(qt_ref[h], shift, axis=1)
        k_cols = pltpu.roll(kt_ref[h], shift, axis=1)
        for j in range(s0_ref.shape[0]):
            rows = pl.ds(r0 + j, 1)
            v_row = v_ref[rows, vs]
            s_old = s0_ref[j, h]
            qk = jnp.sum(q_ref[rows, ks] * k_ref[rows, ks], axis=1, keepdims=True) * SCALE_C
            o = qk * v_row + jnp.sum(s_old * q_cols[:, j:j + 1], axis=0, keepdims=True) * gamma
            y_ref[rows, vs] = _rms(o, gh_ref[h:h + 1, :]) * _silu(z_ref[rows, vs])
            s1_ref[j, h] = gamma * s_old + (k_cols[:, j:j + 1] * SCALE_C) * v_row


def _ret_sample_rider(q, k, proj, qt, kt, g_head, s0):
    qk_w = H_C * DK_C

    def build(bmap):
        once = dict(pipeline_mode=pl.Buffered(1))
        fixed = lambda shape, *idx: pl.BlockSpec(shape, lambda j, i: idx or (0,) * len(shape), **once)
        state = pl.BlockSpec((1, H_C, DK_C, DV_C), lambda j, i: (bmap(j, i), 0, 0, 0))
        args = [q, k, proj, proj, qt, kt, g_head, s0]
        in_specs = [fixed((DEC_BATCH, qk_w)), fixed((DEC_BATCH, qk_w)),
                    fixed((DEC_BATCH, W_C), 0, 1), fixed((DEC_BATCH, W_C), 0, 2),
                    fixed((H_C, DK_C, DEC_BATCH)), fixed((H_C, DK_C, DEC_BATCH)),
                    fixed((H_C, DV_C)), state]
        out_shapes = [jax.ShapeDtypeStruct((DEC_BATCH, W_C), F32),
                      jax.ShapeDtypeStruct((DEC_BATCH, H_C, DK_C, DV_C), F32)]
        out_specs = [pl.BlockSpec((DEC_BATCH, W_C), lambda j, i: (0, 0)), state]
        return args, in_specs, out_shapes, out_specs

    return build


def kernel(x_prompt, x_sample, state_mlstm_C, state_mlstm_n, state_mlstm_m, state_s5_re, state_s5_im, state_ret, g_pre, g_post, w_in0, b_gates0, g_head_a, lam_re, lam_im, log_dt, b_re, b_im, c_re, c_im, d_skip, w_glu, b_glu, w_out0, w_in1, g_head_c, w_out1):
    mp = BATCH * SEQ
    xp = x_prompt.reshape(mp, D_MODEL).astype(F32)
    xs = x_sample.reshape(DEC_BATCH, D_MODEL).astype(F32)

    w0t = w_in0.T.astype(F32)
    w0g = jnp.pad(w0t[QKV0:GATE0], ((0, LANES - 2 * H_A), (0, 0)))
    w0b = w0t[GATE0:]
    wg = w_glu.astype(BF16)
    bglu = b_glu.reshape(1, W_B).astype(F32)
    g_pre = g_pre.astype(F32)
    g_post = g_post.astype(F32)
    bg = jnp.pad(b_gates0.astype(F32), (0, LANES - 2 * H_A))[None, :]
    gh_a = g_head_a.astype(F32)
    gh_c = g_head_c.astype(F32)

    dup = lambda a: jnp.concatenate([a, a], axis=-1).astype(F32)
    lamr, lami = dup(lam_re), dup(lam_im)
    b1 = jnp.concatenate([b_re.transpose(0, 2, 1), b_im.transpose(0, 2, 1)], axis=-1)
    cc = jnp.concatenate([c_re.transpose(0, 2, 1), c_im.transpose(0, 2, 1)], axis=1)
    s5m, s5w, s5v, s5p1, s5p2 = _s5_build(
        log_dt.reshape(G_B, 1, 1).astype(F32), lamr[:, None, :], lami[:, None, :],
        jnp.tile(b1.astype(F32), (1, S5_T, 1)), jnp.tile(cc.astype(F32), (1, 1, S5_T)),
        jnp.tile(d_skip.astype(F32), (1, S5_T))[:, None, :])

    a0, gates = _norm_gates(xp, g_pre[0:1], w0g, bg, 512)
    a0s, gates_s = _norm_gates(xs, g_pre[0:1], w0g, bg, DEC_BATCH)
    pas = _proj(a0s, a0s, w0t, QKV0, F32, DEC_BATCH, 1024, True)[0]
    qk_a = H_A * DK_A
    to_cols = lambda a, nh, dk: a.reshape(DEC_BATCH, nh, dk).transpose(1, 2, 0)
    pa, yas, c_s, n_s, m_s = _proj_rider(
        a0, w0t, QKV0, BF16, 512, 1024, True, _mlstm_sample_body, lambda step: step,
        _mlstm_sample_rider(
            gates_s[:, :H_A], gates_s[:, H_A:2 * H_A],
            state_mlstm_m.astype(F32), pas,
            to_cols(pas[:, :qk_a], H_A, DK_A), to_cols(pas[:, qk_a:2 * qk_a], H_A, DK_A),
            state_mlstm_n.reshape(DEC_BATCH, qk_a).astype(F32), gh_a, state_mlstm_C.astype(F32)))
    pb, pbs = _proj(a0, a0s, w0b, 2 * W_B, F32, 1024, 1024, True)
    gates3 = gates.reshape(BATCH, SEQ, LANES)
    gates_row = gates3[:, :, :2 * H_A].transpose(0, 2, 1)
    ya, c_p, n_p, m_p, w1, wo0, wo1 = _mlstm_prompt(
        pa.reshape(BATCH, SEQ, QKV0), gates3, gates_row, gh_a,
        [w_in1.astype(F32), w_out0.astype(F32), w_out1.astype(F32)])
    yb, xf = _s5_prompt(pb, s5m, s5w, s5v, s5p1, s5p2)
    yb = _glu(yb, wg, bglu, pb, 512)
    s5r_p = xf[:, :, :P_B].transpose(1, 0, 2)
    s5i_p = xf[:, :, P_B:].transpose(1, 0, 2)
    ybs, s5r_s, s5i_s = _s5_sample(
        pbs, state_s5_re.reshape(DEC_BATCH, G_B * P_B).astype(F32),
        state_s5_im.reshape(DEC_BATCH, G_B * P_B).astype(F32), s5m, s5w, s5v, s5p1, s5p2)
    s5r_s = s5r_s.reshape(DEC_BATCH, G_B, P_B)
    s5i_s = s5i_s.reshape(DEC_BATCH, G_B, P_B)
    ybs = _glu(ybs, wg, bglu, pbs, DEC_BATCH)
    h1, h1s, a1, a1s = _outproj(ya.reshape(mp, W_A), yb, yas, ybs, wo0, xp, xs, g_post[0:1],
                                g_pre[1:2], 512)

    cols1 = 2 * H_C * DK_C + 2 * W_C
    p1s = _proj(a1s, a1s, w1, cols1, F32, DEC_BATCH, 1024, False)[0]
    cos_s, sin_s = _rope_table(8, PAST_LEN)
    qs, ks = _rope_sample(p1s, cos_s, sin_s)
    p1, ycs, s_s = _proj_rider(
        a1, w1, cols1, BF16, 512, cols1 // 8, False, _ret_sample_body, lambda step: step,
        _ret_sample_rider(qs, ks, p1s, to_cols(qs, H_C, DK_C), to_cols(ks, H_C, DK_C),
                          gh_c, state_ret.astype(F32)))
    cos_p, sin_p = _rope_table(SEQ, 0)
    y_p, s_p = _ret_prompt(p1.reshape(BATCH, SEQ, -1), cos_p, sin_p, gh_c, wo1,
                           h1.reshape(BATCH, SEQ, D_MODEL), g_post[1:2])
    y_s = _outproj(ycs, None, ycs, None, wo1, h1s, h1s, g_post[1:2], None, DEC_BATCH)[1]

    return (y_p.reshape(BATCH, SEQ, D_MODEL), y_s.reshape(DEC_BATCH, 1, D_MODEL),
            c_p, n_p.reshape(BATCH, H_A, DK_A), m_p[:, :, 0, 0],
            s5r_p, s5i_p, s_p,
            c_s, n_s.reshape(DEC_BATCH, H_A, DK_A), m_s,
            s5r_s, s5i_s, s_s)
```

```python
import functools
import math

import jax
import jax.numpy as jnp
from jax import lax
from jax.experimental import pallas as pl
from jax.experimental.pallas import tpu as pltpu

F32 = jnp.float32
BF16 = jnp.bfloat16

D_MODEL = 2048
BATCH = 4
SEQ = 2048
DEC_BATCH = 128
PAST_LEN = 16384
H_A = 4
DK_A = 256
DV_A = 512
W_A = H_A * DV_A
GROUP_B = 16
G_B = 64
P_B = 64
W_B = G_B * GROUP_B
H_C = 8
DK_C = 256
DV_C = 512
W_C = H_C * DV_C
CHUNK = 256
NORM_EPS = 1e-6
ROPE_BASE = 10000.0
QKV0 = 2 * H_A * DK_A + 3 * W_A
GATE0 = QKV0 + 2 * H_A
SCALE_A = DK_A ** -0.5
SCALE_C = DK_C ** -0.5
LOG_GAMMA = tuple(math.log1p(-(2.0 ** (-5.0 - h))) for h in range(H_C))
S5_T = 16
S5_W = S5_T * GROUP_B
LANES = 128
OCT = LANES // GROUP_B
S5_LEVELS = int(math.log2(SEQ // S5_T))
S5_A1_ROW = S5_LEVELS
VMEM_LIMIT = 48 * 1024 * 1024
OUT_HEADS = 4
BIG_VMEM_LIMIT = 56 * 1024 * 1024


def _cparams(*sem):
    return pltpu.CompilerParams(dimension_semantics=sem, vmem_limit_bytes=VMEM_LIMIT)


def _dot(a, b):
    return jnp.dot(a, b, preferred_element_type=F32)


def _dot_nt(a, b):
    return lax.dot_general(a, b, (((1,), (1,)), ((), ())), preferred_element_type=F32)


def _dot_tn(a, b):
    return lax.dot_general(a, b, (((0,), (0,)), ((), ())), preferred_element_type=F32)


def _split(x):
    hi = x.astype(BF16)
    return hi, (x - hi.astype(F32)).astype(BF16)


def _dot3(a, b):
    a_hi, a_lo = _split(a)
    b_hi, b_lo = _split(b)
    return _dot(a_hi, b_hi) + _dot(a_hi, b_lo) + _dot(a_lo, b_hi)


def _log_sigmoid(x):
    return jnp.minimum(x, 0.0) - jnp.log1p(jnp.exp(-jnp.abs(x)))


def _silu(x):
    return x * jax.nn.sigmoid(x)


def _gelu_tanh(x):
    return 0.5 * x * (1.0 + jnp.tanh(math.sqrt(2.0 / math.pi) * (x + 0.044715 * (x * x * x))))


def _rms(x, g):
    return x * lax.rsqrt(jnp.mean(x * x, axis=-1, keepdims=True) + NORM_EPS) * g


def _norm_gates_kernel(x_ref, g_ref, wg_ref, bg_ref, a_ref, gates_ref):
    a = _rms(x_ref[...], g_ref[...])
    a_ref[...] = a.astype(BF16)
    a_hi, a_lo = _split(a)
    w_hi, w_lo = _split(wg_ref[...])
    pre = _dot_nt(a_hi, w_hi) + _dot_nt(a_hi, w_lo) + _dot_nt(a_lo, w_hi) + bg_ref[...]
    lane = lax.broadcasted_iota(jnp.int32, pre.shape, 1)
    gates_ref[...] = jnp.where((lane >= H_A) & (lane < 2 * H_A), _log_sigmoid(pre), pre)


def _norm_gates(x, g, wg, bg, tm):
    m = x.shape[0]
    return pl.pallas_call(
        _norm_gates_kernel,
        grid=(m // tm,),
        in_specs=[pl.BlockSpec((tm, D_MODEL), lambda i: (i, 0)),
                  pl.BlockSpec((1, D_MODEL), lambda i: (0, 0)),
                  pl.BlockSpec((LANES, D_MODEL), lambda i: (0, 0)),
                  pl.BlockSpec((1, LANES), lambda i: (0, 0))],
        out_specs=[pl.BlockSpec((tm, D_MODEL), lambda i: (i, 0)),
                   pl.BlockSpec((tm, LANES), lambda i: (i, 0))],
        out_shape=[jax.ShapeDtypeStruct((m, D_MODEL), BF16),
                   jax.ShapeDtypeStruct((m, LANES), F32)],
        compiler_params=_cparams("parallel"),
        name="norm_gates",
    )(x, g, wg, bg)


def _proj_kernel(xp_ref, xs_ref, w_ref, op_ref, os_ref, wb_ref, *, w_transposed):
    mm = _dot_nt if w_transposed else _dot

    @pl.when(pl.program_id(1) == 0)
    def _():
        wb_ref[...] = w_ref[...].astype(BF16)
        os_ref[...] = mm(xs_ref[...], wb_ref[...])

    op_ref[...] = mm(xp_ref[...], wb_ref[...]).astype(op_ref.dtype)


def _proj(xp, xs, w, n_cols, out_dtype, tm, tn, w_transposed):
    m, k = xp.shape
    ms = xs.shape[0]
    if w_transposed:
        w_spec = pl.BlockSpec((tn, k), lambda j, i: (j, 0))
        wb_shape = (tn, k)
    else:
        w_spec = pl.BlockSpec((k, tn), lambda j, i: (0, j))
        wb_shape = (k, tn)
    return pl.pallas_call(
        functools.partial(_proj_kernel, w_transposed=w_transposed),
        grid=(n_cols // tn, m // tm),
        in_specs=[pl.BlockSpec((tm, k), lambda j, i: (i, 0)),
                  pl.BlockSpec((ms, k), lambda j, i: (0, 0)),
                  w_spec],
        out_specs=[pl.BlockSpec((tm, tn), lambda j, i: (i, j)),
                   pl.BlockSpec((ms, tn), lambda j, i: (0, j))],
        out_shape=[jax.ShapeDtypeStruct((m, n_cols), out_dtype),
                   jax.ShapeDtypeStruct((ms, n_cols), F32)],
        scratch_shapes=[pltpu.VMEM(wb_shape, BF16)],
        compiler_params=_cparams("parallel", "arbitrary"),
        name="proj",
    )(xp, xs, w)


def _proj_rider_kernel(*refs, n_in, n_out, body, batch_of_step, n_i, w_transposed):
    xp_ref, w_ref = refs[:2]
    rider_in = refs[2:2 + n_in]
    op_ref = refs[2 + n_in]
    rider_out = refs[3 + n_in:3 + n_in + n_out]
    mm = _dot_nt if w_transposed else _dot
    i = pl.program_id(1)

    if w_ref.dtype == BF16:
        wb_ref = w_ref
    else:
        wb_ref = refs[-1]

        @pl.when(i == 0)
        def _():
            wb_ref[...] = w_ref[...].astype(BF16)

    op_ref[...] = mm(xp_ref[...], wb_ref[...]).astype(op_ref.dtype)
    body(batch_of_step(pl.program_id(0) * n_i + i), *rider_in, *rider_out)


def _proj_rider(xp, w, n_cols, out_dtype, tm, tn, w_transposed, body, batch_of_step, rider):
    m, k = xp.shape
    n_i = m // tm
    bmap = lambda j, i: batch_of_step(j * n_i + i)
    r_args, r_in_specs, r_out_shapes, r_out_specs = rider(bmap)
    if w_transposed:
        w_spec = pl.BlockSpec((tn, k), lambda j, i: (j, 0))
        wb_shape = (tn, k)
    else:
        w_spec = pl.BlockSpec((k, tn), lambda j, i: (0, j))
        wb_shape = (k, tn)
    return pl.pallas_call(
        functools.partial(_proj_rider_kernel, n_in=len(r_args), n_out=len(r_out_shapes), body=body,
                          batch_of_step=batch_of_step, n_i=n_i, w_transposed=w_transposed),
        grid=(n_cols // tn, n_i),
        in_specs=[pl.BlockSpec((tm, k), lambda j, i: (i, 0)), w_spec] + r_in_specs,
        out_specs=[pl.BlockSpec((tm, tn), lambda j, i: (i, j))] + r_out_specs,
        out_shape=[jax.ShapeDtypeStruct((m, n_cols), out_dtype)] + r_out_shapes,
        scratch_shapes=[] if w.dtype == BF16 else [pltpu.VMEM(wb_shape, BF16)],
        compiler_params=pltpu.CompilerParams(dimension_semantics=("arbitrary", "arbitrary"),
                                             vmem_limit_bytes=BIG_VMEM_LIMIT),
        name="proj_rider",
    )(xp, w, *r_args)


def _outproj_kernel(*refs, two, next_norm):
    refs = list(refs)
    ya_ref = refs.pop(0)
    yb_ref = refs.pop(0) if two else None
    yas_ref = refs.pop(0)
    ybs_ref = refs.pop(0) if two else None
    wa_ref = refs.pop(0)
    wb_ref = refs.pop(0) if two else None
    h_ref, hs_ref, g_ref = refs.pop(0), refs.pop(0), refs.pop(0)
    gn_ref = refs.pop(0) if next_norm else None
    o_ref, os_ref = refs.pop(0), refs.pop(0)
    a_ref, as_ref = (refs.pop(0), refs.pop(0)) if next_norm else (None, None)

    def run(x_ref, b_ref, res_ref, out_ref, nxt_ref):
        mix = _dot(x_ref[...].astype(BF16), wa_ref[...])
        if two:
            mix = mix + _dot(b_ref[...].astype(BF16), wb_ref[...])
        new = res_ref[...] + _rms(mix, g_ref[...])
        out_ref[...] = new
        if next_norm:
            nxt_ref[...] = _rms(new, gn_ref[...]).astype(BF16)

    @pl.when(pl.program_id(0) == 0)
    def _():
        run(yas_ref, ybs_ref, hs_ref, os_ref, as_ref)

    run(ya_ref, yb_ref, h_ref, o_ref, a_ref)


def _outproj(ya, yb, yas, ybs, w, h, hs, g, g_next, tm):
    m = h.shape[0]
    ms = hs.shape[0]
    ka = ya.shape[1]
    two = yb is not None
    next_norm = g_next is not None
    once = dict(pipeline_mode=pl.Buffered(1))
    row = lambda width: pl.BlockSpec((tm, width), lambda i: (i, 0))
    fixed = lambda rows, width: pl.BlockSpec((rows, width), lambda i: (0, 0), **once)
    if two:
        kb = yb.shape[1]
        in_specs = [row(ka), row(kb), fixed(ms, ka), fixed(ms, kb), fixed(ka, D_MODEL),
                    pl.BlockSpec((kb, D_MODEL), lambda i: (ka // kb, 0), **once)]
        args = [ya, yb, yas, ybs, w, w]
    else:
        in_specs = [row(ka), fixed(ms, ka), fixed(ka, D_MODEL)]
        args = [ya, yas, w]
    in_specs += [row(D_MODEL), fixed(ms, D_MODEL), fixed(1, D_MODEL)]
    args += [h, hs, g]
    sample_out = pl.BlockSpec((ms, D_MODEL), lambda i: (0, 0))
    out_specs = [row(D_MODEL), sample_out]
    out_shape = [jax.ShapeDtypeStruct((m, D_MODEL), F32), jax.ShapeDtypeStruct((ms, D_MODEL), F32)]
    if next_norm:
        in_specs.append(fixed(1, D_MODEL))
        args.append(g_next)
        out_specs += [row(D_MODEL), sample_out]
        out_shape += [jax.ShapeDtypeStruct((m, D_MODEL), BF16), jax.ShapeDtypeStruct((ms, D_MODEL), BF16)]
    return pl.pallas_call(
        functools.partial(_outproj_kernel, two=two, next_norm=next_norm),
        grid=(m // tm,),
        in_specs=in_specs,
        out_specs=out_specs,
        out_shape=out_shape,
        compiler_params=pltpu.CompilerParams(dimension_semantics=("arbitrary",),
                                             vmem_limit_bytes=BIG_VMEM_LIMIT),
        name="outproj",
    )(*args)


def _mlstm_prompt_kernel(*refs, n_cast):
    q_ref, k_ref, v_ref, o_ref, z_ref, gc_ref, gr_ref, gh_ref = refs[:8]
    cast_in = refs[8:8 + n_cast]
    y_ref, c_out, n_out, m_out = refs[8 + n_cast:12 + n_cast]
    cast_out = refs[12 + n_cast:12 + 2 * n_cast]
    c_s, n_s, m_s = refs[12 + 2 * n_cast:]
    c = pl.program_id(1)
    t = CHUNK
    for src, dst in zip(cast_in, cast_out):
        dst[...] = src[...].astype(BF16)

    @pl.when(c == 0)
    def _():
        c_s[...] = jnp.zeros_like(c_s)
        n_s[...] = jnp.zeros_like(n_s)
        m_s[...] = jnp.zeros_like(m_s)

    row = lax.broadcasted_iota(jnp.int32, (t, t), 0)
    col = lax.broadcasted_iota(jnp.int32, (t, t), 1)
    tril = col <= row
    triu = row <= col
    gc = gc_ref[...]
    gr = gr_ref[...]
    for h in range(H_A):
        i_col = gc[:, h:h + 1]
        i_row = gr[h:h + 1, :]
        lf_col = gc[:, H_A + h:H_A + h + 1]
        lf_row = gr[H_A + h:H_A + h + 1, :]
        b_col = jnp.sum(jnp.where(tril, lf_row, 0.0), axis=1, keepdims=True)
        b_row = jnp.sum(jnp.where(triu, lf_col, 0.0), axis=0, keepdims=True)
        m_prev = m_s[h][:, 0:1]
        d = jnp.where(tril, b_col - b_row + i_row, -jnp.inf)
        inter = b_col + m_prev
        m_t = jnp.maximum(inter, jnp.max(d, axis=1, keepdims=True))
        w_intra = jnp.exp(d - m_t)
        w_inter = jnp.exp(inter - m_t) * SCALE_A
        q = q_ref[:, h * DK_A:(h + 1) * DK_A]
        k = k_ref[:, h * DK_A:(h + 1) * DK_A]
        v = v_ref[:, h * DV_A:(h + 1) * DV_A]
        s = _dot_nt(q, k) * (w_intra * SCALE_A)
        c_old = c_s[h]
        n_old = n_s[h]
        num = _dot(s.astype(BF16), v) + w_inter * _dot(q, c_old.astype(BF16))
        qn = jnp.sum(q.astype(F32) * n_old, axis=1, keepdims=True)
        den = jnp.sum(s, axis=1, keepdims=True) + w_inter * qn
        hh = num * (1.0 / jnp.maximum(jnp.abs(den), jnp.exp(-m_t)))
        hn = _rms(hh, gh_ref[h:h + 1, :])
        o = o_ref[:, h * DV_A:(h + 1) * DV_A].astype(F32)
        z = z_ref[:, h * DV_A:(h + 1) * DV_A].astype(F32)
        gate = z / ((1.0 + jnp.exp(-o)) * (1.0 + jnp.exp(-z)))
        y_ref[:, h * DV_A:(h + 1) * DV_A] = (hn * gate).astype(BF16)
        b_last = b_col[t - 1:t, :]
        g_col = b_last - b_col + i_col
        m_new = jnp.maximum(b_last + m_prev, jnp.max(g_col, axis=0, keepdims=True))
        e_col = jnp.exp(g_col - m_new)
        decay = jnp.exp(b_last + m_prev - m_new)
        ke = k.astype(F32) * e_col
        c_s[h] = decay * c_old + _dot_tn(ke.astype(BF16), v)
        n_s[h] = decay * n_old + jnp.sum(ke, axis=0, keepdims=True)
        m_s[h] = jnp.broadcast_to(m_new, (1, LANES))

    @pl.when(c == pl.num_programs(1) - 1)
    def _():
        c_out[...] = c_s[...]
        n_out[...] = n_s[...]
        m_out[...] = m_s[...]


def _mlstm_prompt(proj, gates_col, gates_row, g_head, weights):
    nc = SEQ // CHUNK
    t = CHUNK
    qk_w = H_A * DK_A
    steps = BATCH * nc
    slab = lambda w: pl.BlockSpec((w.shape[0] // steps, w.shape[1]), lambda b, c: (b * nc + c, 0))
    return pl.pallas_call(
        functools.partial(_mlstm_prompt_kernel, n_cast=len(weights)),
        grid=(BATCH, nc),
        in_specs=[pl.BlockSpec((None, t, qk_w), lambda b, c: (b, c, 0)),
                  pl.BlockSpec((None, t, qk_w), lambda b, c: (b, c, 1)),
                  pl.BlockSpec((None, t, W_A), lambda b, c: (b, c, 1)),
                  pl.BlockSpec((None, t, W_A), lambda b, c: (b, c, 2)),
                  pl.BlockSpec((None, t, W_A), lambda b, c: (b, c, 3)),
                  pl.BlockSpec((None, t, LANES), lambda b, c: (b, c, 0)),
                  pl.BlockSpec((None, 2 * H_A, t), lambda b, c: (b, 0, c)),
                  pl.BlockSpec((H_A, DV_A), lambda b, c: (0, 0))] + [slab(w) for w in weights],
        out_specs=[pl.BlockSpec((None, t, W_A), lambda b, c: (b, c, 0)),
                   pl.BlockSpec((None, H_A, DK_A, DV_A), lambda b, c: (b, 0, 0, 0)),
                   pl.BlockSpec((None, H_A, 1, DK_A), lambda b, c: (b, 0, 0, 0)),
                   pl.BlockSpec((None, H_A, 1, LANES), lambda b, c: (b, 0, 0, 0))]
        + [slab(w) for w in weights],
        out_shape=[jax.ShapeDtypeStruct((BATCH, SEQ, W_A), BF16),
                   jax.ShapeDtypeStruct((BATCH, H_A, DK_A, DV_A), F32),
                   jax.ShapeDtypeStruct((BATCH, H_A, 1, DK_A), F32),
                   jax.ShapeDtypeStruct((BATCH, H_A, 1, LANES), F32)]
        + [jax.ShapeDtypeStruct(w.shape, BF16) for w in weights],
        scratch_shapes=[pltpu.VMEM((H_A, DK_A, DV_A), F32),
                        pltpu.VMEM((H_A, 1, DK_A), F32),
                        pltpu.VMEM((H_A, 1, LANES), F32)],
        compiler_params=_cparams("arbitrary", "arbitrary"),
        name="mlstm_prompt",
    )(proj, proj, proj, proj, proj, gates_col, gates_row, g_head, *weights)


def _mlstm_sample_body(r0, gi_ref, lf_ref, m0_ref, q_ref, k_ref, v_ref, o_ref, z_ref,
                       qt_ref, kt_ref, n0_ref, gh_ref, c0_ref,
                       y_ref, c1_ref, n1_ref, m1_ref):
    shift = lax.rem(DEC_BATCH - r0, DEC_BATCH)
    q_cols = [pltpu.roll(qt_ref[h], shift, axis=1) for h in range(H_A)]
    k_cols = [pltpu.roll(kt_ref[h], shift, axis=1) for h in range(H_A)]
    for j in range(c0_ref.shape[0]):
        rows = pl.ds(r0 + j, 1)
        i_v = gi_ref[rows, :]
        lf_v = lf_ref[rows, :]
        m0_v = m0_ref[rows, :]
        m_t = jnp.maximum(lf_v + m0_v, i_v)
        w_in = jnp.exp(i_v - m_t)
        w_st = jnp.exp(lf_v + m0_v - m_t)
        floor = jnp.exp(-m_t)
        m1_ref[rows, :] = m_t
        for h in range(H_A):
            ks = slice(h * DK_A, (h + 1) * DK_A)
            vs = slice(h * DV_A, (h + 1) * DV_A)
            wi = w_in[:, h:h + 1]
            ws = w_st[:, h:h + 1]
            q_col = q_cols[h][:, j:j + 1]
            k_col = k_cols[h][:, j:j + 1]
            q_row = q_ref[rows, ks]
            k_row = k_ref[rows, ks]
            v_row = v_ref[rows, vs]
            n_row = n0_ref[rows, ks]
            c_old = c0_ref[j, h]
            qk = jnp.sum(q_row * k_row, axis=1, keepdims=True) * SCALE_A
            s = qk * wi
            q_c = jnp.sum(c_old * q_col, axis=0, keepdims=True) * SCALE_A
            qn = jnp.sum(q_row * n_row, axis=1, keepdims=True) * SCALE_A
            num = s * v_row + ws * q_c
            den = s + ws * qn
            hh = num / jnp.maximum(jnp.abs(den), floor[:, h:h + 1])
            hn = _rms(hh, gh_ref[h:h + 1, :])
            y_ref[rows, vs] = hn * jax.nn.sigmoid(o_ref[rows, vs]) * _silu(z_ref[rows, vs])
            c1_ref[j, h] = ws * c_old + (wi * k_col) * v_row
            n1_ref[rows, ks] = ws * n_row + wi * k_row


def _mlstm_sample_rider(gi, lf, m0, proj, qt, kt, n0, g_head, c0, tb):
    qk_w = H_A * DK_A

    def build(bmap):
        once = dict(pipeline_mode=pl.Buffered(1))
        full = lambda shape: pl.BlockSpec(shape, lambda j, i: (0,) * len(shape))
        cols = lambda width, c: pl.BlockSpec((DEC_BATCH, width), lambda j, i: (0, c), **once)
        state = pl.BlockSpec((tb, H_A, DK_A, DV_A), lambda j, i: (bmap(j, i), 0, 0, 0))
        lanes = pl.BlockSpec((H_A, DK_A, DEC_BATCH), lambda j, i: (0, 0, 0), **once)
        args = [gi, lf, m0, proj, proj, proj, proj, proj, qt, kt, n0, g_head, c0]
        in_specs = [full((DEC_BATCH, H_A)), full((DEC_BATCH, H_A)), full((DEC_BATCH, H_A)),
                    cols(qk_w, 0), cols(qk_w, 1), cols(W_A, 1), cols(W_A, 2), cols(W_A, 3),
                    lanes, lanes, cols(qk_w, 0), full((H_A, DV_A)), state]
        out_shapes = [jax.ShapeDtypeStruct((DEC_BATCH, W_A), F32),
                      jax.ShapeDtypeStruct((DEC_BATCH, H_A, DK_A, DV_A), F32),
                      jax.ShapeDtypeStruct((DEC_BATCH, qk_w), F32),
                      jax.ShapeDtypeStruct((DEC_BATCH, H_A), F32)]
        out_specs = [full((DEC_BATCH, W_A)), state, full((DEC_BATCH, qk_w)), full((DEC_BATCH, H_A))]
        return args, in_specs, out_shapes, out_specs

    return build


def _s5_build_group(g, ldt_ref, lamr_ref, lami_ref, b1_ref, cc_ref, dsk_ref,
                    m_ref, w_ref, v_ref, p1_ref, p2_ref):
    dt = jnp.exp(ldt_ref[g])
    lam_re, lam_im = lamr_ref[g], lami_ref[g]
    mag = jnp.exp(lam_re * dt)
    ang = lam_im * dt
    a_re, a_im = mag * jnp.cos(ang), mag * jnp.sin(ang)
    den = lam_re * lam_re + lam_im * lam_im
    f_re = ((a_re - 1.0) * lam_re + a_im * lam_im) / den
    f_im = (a_im * lam_re - (a_re - 1.0) * lam_im) / den
    pows = [(jnp.ones_like(a_re), jnp.zeros_like(a_im))]
    for _ in range(S5_T):
        r_re, r_im = pows[-1]
        pows.append((r_re * a_re - r_im * a_im, r_re * a_im + r_im * a_re))

    def tall(vals):
        return jnp.concatenate([jnp.broadcast_to(x, (GROUP_B, LANES)) for x in vals], axis=0)

    af_re = tall([pows[S5_T - 1 - s][0] * f_re - pows[S5_T - 1 - s][1] * f_im for s in range(S5_T)])
    af_im = tall([pows[S5_T - 1 - s][0] * f_im + pows[S5_T - 1 - s][1] * f_re for s in range(S5_T)])
    lane = lax.broadcasted_iota(jnp.int32, (1, LANES), 1)
    sgn_lane = jnp.where(lane < P_B, -1.0, 1.0)
    b1 = b1_ref[g]
    b2 = sgn_lane * pltpu.roll(b1, P_B, axis=1)
    w_ref[g] = (af_re * b1 + af_im * b2).astype(BF16)

    eye = (lax.broadcasted_iota(jnp.int32, (LANES, LANES), 0)
           == lax.broadcasted_iota(jnp.int32, (LANES, LANES), 1))

    def column(row):
        return jnp.sum(jnp.where(eye, row, 0.0), axis=1, keepdims=True)

    ac_re, ac_im, fc_re, fc_im = column(a_re), column(a_im), column(f_re), column(f_im)
    kid = lax.broadcasted_iota(jnp.int32, (S5_T, LANES), 0)
    expand = (lax.broadcasted_iota(jnp.int32, (S5_T, S5_W), 1) // GROUP_B
              == lax.broadcasted_iota(jnp.int32, (S5_T, S5_W), 0)).astype(BF16)

    def lane_blocks(part):
        stacked = jnp.zeros((S5_T, LANES), F32)
        for k in range(S5_T):
            stacked = jnp.where(kid == k, pows[k][part], stacked)
        hi = stacked.astype(BF16)
        rest = stacked - hi.astype(F32)
        mid = rest.astype(BF16)
        lo = (rest - mid.astype(F32)).astype(BF16)
        return _dot_tn(hi, expand) + _dot_tn(mid, expand) + _dot_tn(lo, expand)

    q_re, q_im = lane_blocks(0), lane_blocks(1)
    cc = cc_ref[g]
    cs = pltpu.roll(cc, P_B, axis=0)
    rowi = lax.broadcasted_iota(jnp.int32, (LANES, 1), 0)
    sgn_row = jnp.where(rowi < P_B, 1.0, -1.0)

    def readout(r_re, r_im):
        return sgn_row * (r_re * cc) - r_im * cs

    v_ref[g] = readout(q_re * ac_re - q_im * ac_im, q_re * ac_im + q_im * ac_re).astype(BF16)
    vf = readout(q_re * fc_re - q_im * fc_im, q_re * fc_im + q_im * fc_re)
    kw = _dot3(b1[0:GROUP_B, :], vf)
    lane_w = lax.broadcasted_iota(jnp.int32, (GROUP_B, S5_W), 1)
    blocks = [kw]
    for s in range(1, S5_T):
        blocks.append(jnp.where(lane_w >= s * GROUP_B, pltpu.roll(kw, s * GROUP_B, axis=1), 0.0))
    toep = jnp.concatenate(blocks, axis=0)
    ri = lax.broadcasted_iota(jnp.int32, (S5_W, S5_W), 0)
    ci = lax.broadcasted_iota(jnp.int32, (S5_W, S5_W), 1)
    m_ref[g] = (toep + jnp.where(ri == ci, dsk_ref[g], 0.0)).astype(BF16)

    r_re, r_im = pows[S5_T]
    rid = lax.broadcasted_iota(jnp.int32, (8, LANES), 0)
    p1 = jnp.where(rid == S5_A1_ROW, a_re, 0.0)
    p2 = jnp.where(rid == S5_A1_ROW, sgn_lane * a_im, 0.0)
    for kk in range(S5_LEVELS):
        p1 = jnp.where(rid == kk, r_re, p1)
        p2 = jnp.where(rid == kk, sgn_lane * r_im, p2)
        r_re, r_im = r_re * r_re - r_im * r_im, 2.0 * (r_re * r_im)
    p1_ref[g] = p1
    p2_ref[g] = p2


def _s5_build_kernel(*refs):
    wf_ref, wb_ref = refs[6], refs[-1]
    wb_ref[...] = wf_ref[...].astype(BF16)
    for g in range(OCT):
        _s5_build_group(g, *refs[:6], *refs[7:-1])


def _s5_build(ldt, lamr, lami, b1, cc, dsk, w, w_rows):
    steps = G_B // OCT
    o3 = lambda a, b: pl.BlockSpec((OCT, a, b), lambda g: (g, 0, 0))
    slab = pl.BlockSpec((w_rows // steps, w.shape[1]), lambda g: (g, 0))
    return pl.pallas_call(
        _s5_build_kernel,
        grid=(steps,),
        in_specs=[o3(1, 1), o3(1, LANES), o3(1, LANES),
                  o3(S5_W, LANES), o3(LANES, S5_W), o3(1, S5_W), slab],
        out_specs=[o3(S5_W, S5_W), o3(S5_W, LANES), o3(LANES, S5_W), o3(8, LANES), o3(8, LANES), slab],
        out_shape=[jax.ShapeDtypeStruct((G_B, S5_W, S5_W), BF16),
                   jax.ShapeDtypeStruct((G_B, S5_W, LANES), BF16),
                   jax.ShapeDtypeStruct((G_B, LANES, S5_W), BF16),
                   jax.ShapeDtypeStruct((G_B, 8, LANES), F32),
                   jax.ShapeDtypeStruct((G_B, 8, LANES), F32),
                   jax.ShapeDtypeStruct((w_rows, w.shape[1]), BF16)],
        compiler_params=_cparams("parallel"),
        name="s5_build",
    )(ldt, lamr, lami, b1, cc, dsk, w)


def _s5_prompt_kernel(u_ref, m_ref, w_ref, v_ref, p1_ref, p2_ref, y_ref, xf_ref, y_s):
    nblk = SEQ // S5_T
    rows = BATCH * nblk
    bidx = lax.broadcasted_iota(jnp.int32, (rows, LANES), 0) & (nblk - 1)
    steps = [u_ref[pl.ds(s, rows, stride=S5_T), :] for s in range(S5_T)]
    for g in range(OCT):
        gl = slice(g * GROUP_B, (g + 1) * GROUP_B)
        u = jnp.concatenate([x[:, gl] for x in steps], axis=-1).astype(BF16)
        s = _dot(u, w_ref[g])
        for kk in range(S5_LEVELS):
            sh = 1 << kk
            r = jnp.where(bidx >= sh, pltpu.roll(s, sh, axis=0), 0.0)
            s = s + p1_ref[g, kk:kk + 1, :] * r + p2_ref[g, kk:kk + 1, :] * pltpu.roll(r, P_B, axis=1)
        x_prev = jnp.where(bidx >= 1, pltpu.roll(s, 1, axis=0), 0.0)
        y = _dot(u, m_ref[g]) + _dot(x_prev.astype(BF16), v_ref[g])
        y_s[g] = y
        for b in range(BATCH):
            xf_ref[g, b:b + 1, :] = s[(b + 1) * nblk - 1:(b + 1) * nblk, :]
    for t in range(S5_T):
        half = slice((t // OCT) * LANES, (t // OCT + 1) * LANES)
        tl = slice((t % OCT) * GROUP_B, (t % OCT + 1) * GROUP_B)
        y_ref[pl.ds(t, rows, stride=S5_T), :] = jnp.concatenate(
            [y_s[g, :, half][:, tl] for g in range(OCT)], axis=-1)


def _s5_prompt(u, m, w, v, p1, p2):
    mp = BATCH * SEQ
    rows = BATCH * (SEQ // S5_T)
    o3 = lambda a, b: pl.BlockSpec((OCT, a, b), lambda g: (g, 0, 0))
    return pl.pallas_call(
        _s5_prompt_kernel,
        grid=(G_B // OCT,),
        in_specs=[pl.BlockSpec((mp, LANES), lambda g: (0, g)),
                  o3(S5_W, S5_W), o3(S5_W, LANES), o3(LANES, S5_W), o3(8, LANES), o3(8, LANES)],
        out_specs=[pl.BlockSpec((mp, LANES), lambda g: (0, g)), o3(BATCH, LANES)],
        out_shape=[jax.ShapeDtypeStruct((mp, W_B), F32),
                   jax.ShapeDtypeStruct((G_B, BATCH, LANES), F32)],
        scratch_shapes=[pltpu.VMEM((OCT, rows, S5_W), F32)],
        compiler_params=_cparams("parallel"),
        name="s5_prompt",
    )(u, m, w, v, p1, p2)


def _s5_sample_kernel(u_ref, xr_ref, xi_ref, m_ref, w_ref, v_ref, p1_ref, p2_ref,
                      y_ref, x1r_ref, x1i_ref):
    lane = lax.broadcasted_iota(jnp.int32, (DEC_BATCH, LANES), 1)
    last = LANES - GROUP_B
    half = S5_W // 2
    a1 = slice(S5_A1_ROW, S5_A1_ROW + 1)
    u_all = u_ref[...]
    ys = []
    for g in range(OCT):
        ps = slice(g * P_B, (g + 1) * P_B)
        u = jnp.where(lane >= last, pltpu.roll(u_all, (last - g * GROUP_B) % LANES, axis=1), 0.0)
        u = u.astype(BF16)
        x0 = jnp.concatenate([xr_ref[:, ps], xi_ref[:, ps]], axis=-1)
        x1 = (p1_ref[g, a1, :] * x0 + p2_ref[g, a1, :] * pltpu.roll(x0, P_B, axis=1)
              + _dot(u, w_ref[g, half:, :]))
        x1r_ref[:, ps] = x1[:, :P_B]
        x1i_ref[:, ps] = x1[:, P_B:]
        y = (_dot(x0.astype(BF16), v_ref[g])[:, 0:GROUP_B]
             + _dot(u, m_ref[g, half:, :])[:, S5_W - GROUP_B:])
        ys.append(y)
    y_ref[...] = jnp.concatenate(ys, axis=-1)


def _s5_sample(u, xr, xi, m, w, v, p1, p2):
    o3 = lambda a, b: pl.BlockSpec((OCT, a, b), lambda g: (g, 0, 0))
    tile = pl.BlockSpec((DEC_BATCH, LANES), lambda g: (0, g))
    st = pl.BlockSpec((DEC_BATCH, OCT * P_B), lambda g: (0, g))
    st_shape = jax.ShapeDtypeStruct((DEC_BATCH, G_B * P_B), F32)
    return pl.pallas_call(
        _s5_sample_kernel,
        grid=(G_B // OCT,),
        in_specs=[tile, st, st, o3(S5_W, S5_W), o3(S5_W, LANES), o3(LANES, S5_W), o3(8, LANES), o3(8, LANES)],
        out_specs=[tile, st, st],
        out_shape=[jax.ShapeDtypeStruct((DEC_BATCH, W_B), F32), st_shape, st_shape],
        compiler_params=_cparams("parallel"),
        name="s5_sample",
    )(u, xr, xi, m, w, v, p1, p2)


def _glu_kernel(y_ref, w_ref, b_ref, z_ref, o_ref):
    y = _gelu_tanh(y_ref[...])
    gate = jax.nn.sigmoid(_dot(y.astype(BF16), w_ref[...]) + b_ref[...])
    o_ref[...] = (y * gate * _silu(z_ref[...].astype(F32))).astype(o_ref.dtype)


def _glu(y, w, b, proj_b, tm):
    m = y.shape[0]
    return pl.pallas_call(
        _glu_kernel,
        grid=(m // tm,),
        in_specs=[pl.BlockSpec((tm, W_B), lambda i: (i, 0)),
                  pl.BlockSpec((W_B, W_B), lambda i: (0, 0)),
                  pl.BlockSpec((1, W_B), lambda i: (0, 0)),
                  pl.BlockSpec((tm, W_B), lambda i: (i, 1))],
        out_specs=pl.BlockSpec((tm, W_B), lambda i: (i, 0)),
        out_shape=jax.ShapeDtypeStruct((m, W_B), BF16),
        compiler_params=_cparams("parallel"),
        name="glu",
    )(y, w, b, proj_b)


def _rope_table_kernel(cos_ref, sin_ref, *, pos0):
    shape = cos_ref.shape
    pos = lax.broadcasted_iota(jnp.int32, shape, 0).astype(F32) + pos0
    j = lax.broadcasted_iota(jnp.int32, shape, 1).astype(F32)
    ang = pos * jnp.power(ROPE_BASE, -(j / (DK_C // 2)))
    cos_ref[...] = jnp.cos(ang)
    sin_ref[...] = jnp.sin(ang)


def _rope_table(rows, pos0):
    shape = jax.ShapeDtypeStruct((rows, DK_C // 2), F32)
    return pl.pallas_call(functools.partial(_rope_table_kernel, pos0=float(pos0)),
                          out_shape=[shape, shape], name="rope_table")()


def _rope(x, cos, sin):
    half = DK_C // 2
    x1, x2 = x[:, :half], x[:, half:]
    return jnp.concatenate([x1 * cos - x2 * sin, x1 * sin + x2 * cos], axis=-1)


def _ret_prompt_kernel(q_ref, k_ref, v_ref, z_ref, cos_ref, sin_ref, gh_ref, wo_ref, h_ref, gp_ref,
                       o_ref, s_out, s_s):
    c = pl.program_id(1)
    t = CHUNK

    @pl.when(c == 0)
    def _():
        s_s[...] = jnp.zeros_like(s_s)

    row = lax.broadcasted_iota(jnp.int32, (t, t), 0)
    col = lax.broadcasted_iota(jnp.int32, (t, t), 1)
    tril = col <= row
    diff = (row - col).astype(F32)
    tpos = lax.broadcasted_iota(jnp.int32, (t, 1), 0).astype(F32)
    cos, sin = cos_ref[...], sin_ref[...]
    mix, ys = None, []
    for h in range(H_C):
        lg = LOG_GAMMA[h]
        ks = slice(h * DK_C, (h + 1) * DK_C)
        vs = slice(h * DV_C, (h + 1) * DV_C)
        q = _rope(q_ref[:, ks].astype(F32), cos, sin).astype(BF16)
        k32 = _rope(k_ref[:, ks].astype(F32), cos, sin)
        v = v_ref[:, vs]
        mask = jnp.where(tril, jnp.exp(diff * lg), 0.0) * SCALE_C
        s = _dot_nt(q, k32.astype(BF16)) * mask
        s_old = s_s[h]
        o = _dot(s.astype(BF16), v) + _dot(q, s_old.astype(BF16)) * jnp.exp((tpos + 1.0) * lg)
        y = (_rms(o, gh_ref[h:h + 1, :]) * _silu(z_ref[:, vs].astype(F32))).astype(BF16)
        ys.append(y)
        if len(ys) == OUT_HEADS:
            rows = slice((h + 1 - OUT_HEADS) * DV_C, (h + 1) * DV_C)
            part = _dot(jnp.concatenate(ys, axis=-1), wo_ref[rows, :])
            mix = part if mix is None else mix + part
            ys = []
        k_tail = k32 * (jnp.exp((t - 1.0 - tpos) * lg) * SCALE_C)
        s_s[h] = math.exp(t * lg) * s_old + _dot_tn(k_tail.astype(BF16), v)
    o_ref[...] = h_ref[...] + _rms(mix, gp_ref[...])

    @pl.when(c == pl.num_programs(1) - 1)
    def _():
        s_out[...] = s_s[...]


def _ret_prompt(proj, cos, sin, g_head, wo, h, g_post):
    nc = SEQ // CHUNK
    t = CHUNK
    qk_w = H_C * DK_C
    return pl.pallas_call(
        _ret_prompt_kernel,
        grid=(BATCH, nc),
        in_specs=[pl.BlockSpec((None, t, qk_w), lambda b, c: (b, c, 0)),
                  pl.BlockSpec((None, t, qk_w), lambda b, c: (b, c, 1)),
                  pl.BlockSpec((None, t, W_C), lambda b, c: (b, c, 1)),
                  pl.BlockSpec((None, t, W_C), lambda b, c: (b, c, 2)),
                  pl.BlockSpec((t, DK_C // 2), lambda b, c: (c, 0)),
                  pl.BlockSpec((t, DK_C // 2), lambda b, c: (c, 0)),
                  pl.BlockSpec((H_C, DV_C), lambda b, c: (0, 0)),
                  pl.BlockSpec((W_C, D_MODEL), lambda b, c: (0, 0), pipeline_mode=pl.Buffered(1)),
                  pl.BlockSpec((None, t, D_MODEL), lambda b, c: (b, c, 0)),
                  pl.BlockSpec((1, D_MODEL), lambda b, c: (0, 0))],
        out_specs=[pl.BlockSpec((None, t, D_MODEL), lambda b, c: (b, c, 0)),
                   pl.BlockSpec((None, H_C, DK_C, DV_C), lambda b, c: (b, 0, 0, 0))],
        out_shape=[jax.ShapeDtypeStruct((BATCH, SEQ, D_MODEL), F32),
                   jax.ShapeDtypeStruct((BATCH, H_C, DK_C, DV_C), F32)],
        scratch_shapes=[pltpu.VMEM((H_C, DK_C, DV_C), F32)],
        compiler_params=pltpu.CompilerParams(dimension_semantics=("parallel", "arbitrary"),
                                             vmem_limit_bytes=BIG_VMEM_LIMIT),
        name="ret_prompt",
    )(proj, proj, proj, proj, cos, sin, g_head, wo, h, g_post)


def _rope_sample_kernel(q_ref, k_ref, cos_ref, sin_ref, qo_ref, ko_ref):
    cos, sin = cos_ref[0:1, :], sin_ref[0:1, :]
    for h in range(H_C):
        ks = slice(h * DK_C, (h + 1) * DK_C)
        qo_ref[:, ks] = _rope(q_ref[:, ks], cos, sin)
        ko_ref[:, ks] = _rope(k_ref[:, ks], cos, sin)


def _rope_sample(proj, cos, sin):
    qk_w = H_C * DK_C
    shape = jax.ShapeDtypeStruct((DEC_BATCH, qk_w), F32)
    return pl.pallas_call(
        _rope_sample_kernel,
        grid=(1,),
        in_specs=[pl.BlockSpec((DEC_BATCH, qk_w), lambda i: (0, 0)),
                  pl.BlockSpec((DEC_BATCH, qk_w), lambda i: (0, 1)),
                  pl.BlockSpec((8, DK_C // 2), lambda i: (0, 0)),
                  pl.BlockSpec((8, DK_C // 2), lambda i: (0, 0))],
        out_specs=[pl.BlockSpec((DEC_BATCH, qk_w), lambda i: (0, 0)),
                   pl.BlockSpec((DEC_BATCH, qk_w), lambda i: (0, 0))],
        out_shape=[shape, shape],
        compiler_params=_cparams("arbitrary"),
        name="rope_sample",
    )(proj, proj, cos, sin)


def _ret_sample_body(r0, q_ref, k_ref, v_ref, z_ref, qt_ref, kt_ref, gh_ref, s0_ref, y_ref, s1_ref):
    shift = lax.rem(DEC_BATCH - r0, DEC_BATCH)
    for h in range(H_C):
        gamma = math.exp(LOG_GAMMA[h])
        ks = slice(h * DK_C, (h + 1) * DK_C)
        vs = slice(h * DV_C, (h + 1) * DV_C)
        q_cols = pltpu.roll(qt_ref[h], shift, axis=1)
        k_cols = pltpu.roll(kt_ref[h], shift, axis=1)
        for j in range(s0_ref.shape[0]):
            rows = pl.ds(r0 + j, 1)
            v_row = v_ref[rows, vs]
            s_old = s0_ref[j, h]
            qk = jnp.sum(q_ref[rows, ks] * k_ref[rows, ks], axis=1, keepdims=True) * SCALE_C
            o = qk * v_row + jnp.sum(s_old * q_cols[:, j:j + 1], axis=0, keepdims=True) * gamma
            y_ref[rows, vs] = _rms(o, gh_ref[h:h + 1, :]) * _silu(z_ref[rows, vs])
            s1_ref[j, h] = gamma * s_old + (k_cols[:, j:j + 1] * SCALE_C) * v_row


def _ret_sample_rider(q, k, proj, qt, kt, g_head, s0):
    qk_w = H_C * DK_C

    def build(bmap):
        once = dict(pipeline_mode=pl.Buffered(1))
        fixed = lambda shape, *idx: pl.BlockSpec(shape, lambda j, i: idx or (0,) * len(shape), **once)
        state = pl.BlockSpec((1, H_C, DK_C, DV_C), lambda j, i: (bmap(j, i), 0, 0, 0))
        args = [q, k, proj, proj, qt, kt, g_head, s0]
        in_specs = [fixed((DEC_BATCH, qk_w)), fixed((DEC_BATCH, qk_w)),
                    fixed((DEC_BATCH, W_C), 0, 1), fixed((DEC_BATCH, W_C), 0, 2),
                    fixed((H_C, DK_C, DEC_BATCH)), fixed((H_C, DK_C, DEC_BATCH)),
                    fixed((H_C, DV_C)), state]
        out_shapes = [jax.ShapeDtypeStruct((DEC_BATCH, W_C), F32),
                      jax.ShapeDtypeStruct((DEC_BATCH, H_C, DK_C, DV_C), F32)]
        out_specs = [pl.BlockSpec((DEC_BATCH, W_C), lambda j, i: (0, 0)), state]
        return args, in_specs, out_shapes, out_specs

    return build


def kernel(x_prompt, x_sample, state_mlstm_C, state_mlstm_n, state_mlstm_m, state_s5_re, state_s5_im, state_ret, g_pre, g_post, w_in0, b_gates0, g_head_a, lam_re, lam_im, log_dt, b_re, b_im, c_re, c_im, d_skip, w_glu, b_glu, w_out0, w_in1, g_head_c, w_out1):
    mp = BATCH * SEQ
    xp = x_prompt.reshape(mp, D_MODEL).astype(F32)
    xs = x_sample.reshape(DEC_BATCH, D_MODEL).astype(F32)

    w0t = w_in0.T.astype(F32)
    w0g = jnp.pad(w0t[QKV0:GATE0], ((0, LANES - 2 * H_A), (0, 0)))
    w0b = w0t[GATE0:]
    wg = w_glu.astype(BF16)
    bglu = b_glu.reshape(1, W_B).astype(F32)
    g_pre = g_pre.astype(F32)
    g_post = g_post.astype(F32)
    bg = jnp.pad(b_gates0.astype(F32), (0, LANES - 2 * H_A))[None, :]
    gh_a = g_head_a.astype(F32)
    gh_c = g_head_c.astype(F32)

    dup = lambda a: jnp.concatenate([a, a], axis=-1).astype(F32)
    lamr, lami = dup(lam_re), dup(lam_im)
    b1 = jnp.concatenate([b_re.transpose(0, 2, 1), b_im.transpose(0, 2, 1)], axis=-1)
    cc = jnp.concatenate([c_re.transpose(0, 2, 1), c_im.transpose(0, 2, 1)], axis=1)
    s5m, s5w, s5v, s5p1, s5p2, w0a = _s5_build(
        log_dt.reshape(G_B, 1, 1).astype(F32), lamr[:, None, :], lami[:, None, :],
        jnp.tile(b1.astype(F32), (1, S5_T, 1)), jnp.tile(cc.astype(F32), (1, 1, S5_T)),
        jnp.tile(d_skip.astype(F32), (1, S5_T))[:, None, :], w0t, QKV0)

    a0, gates = _norm_gates(xp, g_pre[0:1], w0g, bg, 512)
    a0s, gates_s = _norm_gates(xs, g_pre[0:1], w0g, bg, DEC_BATCH)
    pas = _proj(a0s, a0s, w0a, QKV0, F32, DEC_BATCH, 1024, True)[0]
    qk_a = H_A * DK_A
    to_cols = lambda a, nh, dk: a.reshape(DEC_BATCH, nh, dk).transpose(1, 2, 0)
    tb_a = DEC_BATCH // ((QKV0 // 2048) * (mp // 512))
    pa, yas, c_s, n_s, m_s = _proj_rider(
        a0, w0a, QKV0, BF16, 512, 2048, True,
        lambda blk, *refs: _mlstm_sample_body(blk * tb_a, *refs), lambda step: step,
        _mlstm_sample_rider(
            gates_s[:, :H_A], gates_s[:, H_A:2 * H_A],
            state_mlstm_m.astype(F32), pas,
            to_cols(pas[:, :qk_a], H_A, DK_A), to_cols(pas[:, qk_a:2 * qk_a], H_A, DK_A),
            state_mlstm_n.reshape(DEC_BATCH, qk_a).astype(F32), gh_a, state_mlstm_C.astype(F32),
            tb_a))
    pb, pbs = _proj(a0, a0s, w0b, 2 * W_B, F32, 1024, 1024, True)
    gates3 = gates.reshape(BATCH, SEQ, LANES)
    gates_row = gates3[:, :, :2 * H_A].transpose(0, 2, 1)
    ya, c_p, n_p, m_p, w1, wo0, wo1 = _mlstm_prompt(
        pa.reshape(BATCH, SEQ, QKV0), gates3, gates_row, gh_a,
        [w_in1.astype(F32), w_out0.astype(F32), w_out1.astype(F32)])
    yb, xf = _s5_prompt(pb, s5m, s5w, s5v, s5p1, s5p2)
    yb = _glu(yb, wg, bglu, pb, 512)
    s5r_p = xf[:, :, :P_B].transpose(1, 0, 2)
    s5i_p = xf[:, :, P_B:].transpose(1, 0, 2)
    ybs, s5r_s, s5i_s = _s5_sample(
        pbs, state_s5_re.reshape(DEC_BATCH, G_B * P_B).astype(F32),
        state_s5_im.reshape(DEC_BATCH, G_B * P_B).astype(F32), s5m, s5w, s5v, s5p1, s5p2)
    s5r_s = s5r_s.reshape(DEC_BATCH, G_B, P_B)
    s5i_s = s5i_s.reshape(DEC_BATCH, G_B, P_B)
    ybs = _glu(ybs, wg, bglu, pbs, DEC_BATCH)
    h1, h1s, a1, a1s = _outproj(ya.reshape(mp, W_A), yb, yas, ybs, wo0, xp, xs, g_post[0:1],
                                g_pre[1:2], 512)

    cols1 = 2 * H_C * DK_C + 2 * W_C
    p1s = _proj(a1s, a1s, w1, cols1, F32, DEC_BATCH, 1024, False)[0]
    cos_s, sin_s = _rope_table(8, PAST_LEN)
    qs, ks = _rope_sample(p1s, cos_s, sin_s)
    p1, ycs, s_s = _proj_rider(
        a1, w1, cols1, BF16, 512, cols1 // 8, False, _ret_sample_body, lambda step: step,
        _ret_sample_rider(qs, ks, p1s, to_cols(qs, H_C, DK_C), to_cols(ks, H_C, DK_C),
                          gh_c, state_ret.astype(F32)))
    cos_p, sin_p = _rope_table(SEQ, 0)
    y_p, s_p = _ret_prompt(p1.reshape(BATCH, SEQ, -1), cos_p, sin_p, gh_c, wo1,
                           h1.reshape(BATCH, SEQ, D_MODEL), g_post[1:2])
    y_s = _outproj(ycs, None, ycs, None, wo1, h1s, h1s, g_post[1:2], None, DEC_BATCH)[1]

    return (y_p.reshape(BATCH, SEQ, D_MODEL), y_s.reshape(DEC_BATCH, 1, D_MODEL),
            c_p, n_p.reshape(BATCH, H_A, DK_A), m_p[:, :, 0, 0],
            s5r_p, s5i_p, s_p,
            c_s, n_s.reshape(DEC_BATCH, H_A, DK_A), m_s,
            s5r_s, s5i_s, s_s)
```

```python
import functools
import math

import jax
import jax.numpy as jnp
from jax import lax
from jax.experimental import pallas as pl
from jax.experimental.pallas import tpu as pltpu

F32 = jnp.float32
BF16 = jnp.bfloat16

D_MODEL = 2048
BATCH = 4
SEQ = 2048
DEC_BATCH = 128
PAST_LEN = 16384
H_A = 4
DK_A = 256
DV_A = 512
W_A = H_A * DV_A
GROUP_B = 16
G_B = 64
P_B = 64
W_B = G_B * GROUP_B
H_C = 8
DK_C = 256
DV_C = 512
W_C = H_C * DV_C
CHUNK = 256
NORM_EPS = 1e-6
ROPE_BASE = 10000.0
QKV0 = 2 * H_A * DK_A + 3 * W_A
GATE0 = QKV0 + 2 * H_A
SCALE_A = DK_A ** -0.5
SCALE_C = DK_C ** -0.5
LOG_GAMMA = tuple(math.log1p(-(2.0 ** (-5.0 - h))) for h in range(H_C))
S5_T = 16
S5_W = S5_T * GROUP_B
LANES = 128
OCT = LANES // GROUP_B
S5_LEVELS = int(math.log2(SEQ // S5_T))
S5_A1_ROW = S5_LEVELS
VMEM_LIMIT = 48 * 1024 * 1024
OUT_HEADS = 4
OUT_HEADS_A = 2
BIG_VMEM_LIMIT = 56 * 1024 * 1024


def _cparams(*sem):
    return pltpu.CompilerParams(dimension_semantics=sem, vmem_limit_bytes=VMEM_LIMIT)


def _dot(a, b):
    return jnp.dot(a, b, preferred_element_type=F32)


def _dot_nt(a, b):
    return lax.dot_general(a, b, (((1,), (1,)), ((), ())), preferred_element_type=F32)


def _dot_tn(a, b):
    return lax.dot_general(a, b, (((0,), (0,)), ((), ())), preferred_element_type=F32)


def _split(x):
    hi = x.astype(BF16)
    return hi, (x - hi.astype(F32)).astype(BF16)


def _dot3(a, b):
    a_hi, a_lo = _split(a)
    b_hi, b_lo = _split(b)
    return _dot(a_hi, b_hi) + _dot(a_hi, b_lo) + _dot(a_lo, b_hi)


def _log_sigmoid(x):
    return jnp.minimum(x, 0.0) - jnp.log1p(jnp.exp(-jnp.abs(x)))


def _silu(x):
    return x * jax.nn.sigmoid(x)


def _gelu_tanh(x):
    return 0.5 * x * (1.0 + jnp.tanh(math.sqrt(2.0 / math.pi) * (x + 0.044715 * (x * x * x))))


def _rms(x, g):
    return x * lax.rsqrt(jnp.mean(x * x, axis=-1, keepdims=True) + NORM_EPS) * g


def _norm_gates_kernel(x_ref, g_ref, wg_ref, bg_ref, a_ref, gates_ref):
    a = _rms(x_ref[...], g_ref[...])
    a_ref[...] = a.astype(BF16)
    a_hi, a_lo = _split(a)
    w_hi, w_lo = _split(wg_ref[...])
    pre = _dot_nt(a_hi, w_hi) + _dot_nt(a_hi, w_lo) + _dot_nt(a_lo, w_hi) + bg_ref[...]
    lane = lax.broadcasted_iota(jnp.int32, pre.shape, 1)
    gates_ref[...] = jnp.where((lane >= H_A) & (lane < 2 * H_A), _log_sigmoid(pre), pre)


def _norm_gates(x, g, wg, bg, tm):
    m = x.shape[0]
    return pl.pallas_call(
        _norm_gates_kernel,
        grid=(m // tm,),
        in_specs=[pl.BlockSpec((tm, D_MODEL), lambda i: (i, 0)),
                  pl.BlockSpec((1, D_MODEL), lambda i: (0, 0)),
                  pl.BlockSpec((LANES, D_MODEL), lambda i: (0, 0)),
                  pl.BlockSpec((1, LANES), lambda i: (0, 0))],
        out_specs=[pl.BlockSpec((tm, D_MODEL), lambda i: (i, 0)),
                   pl.BlockSpec((tm, LANES), lambda i: (i, 0))],
        out_shape=[jax.ShapeDtypeStruct((m, D_MODEL), BF16),
                   jax.ShapeDtypeStruct((m, LANES), F32)],
        compiler_params=_cparams("parallel"),
        name="norm_gates",
    )(x, g, wg, bg)


def _proj_kernel(xp_ref, xs_ref, w_ref, op_ref, os_ref, wb_ref, *, w_transposed):
    mm = _dot_nt if w_transposed else _dot

    @pl.when(pl.program_id(1) == 0)
    def _():
        wb_ref[...] = w_ref[...].astype(BF16)
        os_ref[...] = mm(xs_ref[...], wb_ref[...])

    op_ref[...] = mm(xp_ref[...], wb_ref[...]).astype(op_ref.dtype)


def _proj(xp, xs, w, n_cols, out_dtype, tm, tn, w_transposed):
    m, k = xp.shape
    ms = xs.shape[0]
    if w_transposed:
        w_spec = pl.BlockSpec((tn, k), lambda j, i: (j, 0))
        wb_shape = (tn, k)
    else:
        w_spec = pl.BlockSpec((k, tn), lambda j, i: (0, j))
        wb_shape = (k, tn)
    return pl.pallas_call(
        functools.partial(_proj_kernel, w_transposed=w_transposed),
        grid=(n_cols // tn, m // tm),
        in_specs=[pl.BlockSpec((tm, k), lambda j, i: (i, 0)),
                  pl.BlockSpec((ms, k), lambda j, i: (0, 0)),
                  w_spec],
        out_specs=[pl.BlockSpec((tm, tn), lambda j, i: (i, j)),
                   pl.BlockSpec((ms, tn), lambda j, i: (0, j))],
        out_shape=[jax.ShapeDtypeStruct((m, n_cols), out_dtype),
                   jax.ShapeDtypeStruct((ms, n_cols), F32)],
        scratch_shapes=[pltpu.VMEM(wb_shape, BF16)],
        compiler_params=_cparams("parallel", "arbitrary"),
        name="proj",
    )(xp, xs, w)


def _proj_rider_kernel(*refs, n_in, n_out, body, batch_of_step, n_i, w_transposed):
    xp_ref, w_ref = refs[:2]
    rider_in = refs[2:2 + n_in]
    op_ref = refs[2 + n_in]
    rider_out = refs[3 + n_in:3 + n_in + n_out]
    mm = _dot_nt if w_transposed else _dot
    i = pl.program_id(1)

    if w_ref.dtype == BF16:
        wb_ref = w_ref
    else:
        wb_ref = refs[-1]

        @pl.when(i == 0)
        def _():
            wb_ref[...] = w_ref[...].astype(BF16)

    op_ref[...] = mm(xp_ref[...], wb_ref[...]).astype(op_ref.dtype)
    body(batch_of_step(pl.program_id(0) * n_i + i), *rider_in, *rider_out)


def _proj_rider(xp, w, n_cols, out_dtype, tm, tn, w_transposed, body, batch_of_step, rider):
    m, k = xp.shape
    n_i = m // tm
    bmap = lambda j, i: batch_of_step(j * n_i + i)
    r_args, r_in_specs, r_out_shapes, r_out_specs = rider(bmap)
    if w_transposed:
        w_spec = pl.BlockSpec((tn, k), lambda j, i: (j, 0))
        wb_shape = (tn, k)
    else:
        w_spec = pl.BlockSpec((k, tn), lambda j, i: (0, j))
        wb_shape = (k, tn)
    return pl.pallas_call(
        functools.partial(_proj_rider_kernel, n_in=len(r_args), n_out=len(r_out_shapes), body=body,
                          batch_of_step=batch_of_step, n_i=n_i, w_transposed=w_transposed),
        grid=(n_cols // tn, n_i),
        in_specs=[pl.BlockSpec((tm, k), lambda j, i: (i, 0)), w_spec] + r_in_specs,
        out_specs=[pl.BlockSpec((tm, tn), lambda j, i: (i, j))] + r_out_specs,
        out_shape=[jax.ShapeDtypeStruct((m, n_cols), out_dtype)] + r_out_shapes,
        scratch_shapes=[] if w.dtype == BF16 else [pltpu.VMEM(wb_shape, BF16)],
        compiler_params=pltpu.CompilerParams(dimension_semantics=("arbitrary", "arbitrary"),
                                             vmem_limit_bytes=BIG_VMEM_LIMIT),
        name="proj_rider",
    )(xp, w, *r_args)


def _outproj_kernel(*refs, two, next_norm):
    refs = list(refs)
    ya_ref = refs.pop(0)
    yb_ref = refs.pop(0) if two else None
    yas_ref = refs.pop(0)
    ybs_ref = refs.pop(0) if two else None
    wa_ref = refs.pop(0)
    wb_ref = refs.pop(0) if two else None
    h_ref, hs_ref, g_ref = refs.pop(0), refs.pop(0), refs.pop(0)
    gn_ref = refs.pop(0) if next_norm else None
    o_ref, os_ref = refs.pop(0), refs.pop(0)
    a_ref, as_ref = (refs.pop(0), refs.pop(0)) if next_norm else (None, None)

    def run(x_ref, b_ref, res_ref, out_ref, nxt_ref):
        mix = _dot(x_ref[...].astype(BF16), wa_ref[...])
        if two:
            mix = mix + _dot(b_ref[...].astype(BF16), wb_ref[...])
        new = res_ref[...] + _rms(mix, g_ref[...])
        out_ref[...] = new
        if next_norm:
            nxt_ref[...] = _rms(new, gn_ref[...]).astype(BF16)

    @pl.when(pl.program_id(0) == 0)
    def _():
        run(yas_ref, ybs_ref, hs_ref, os_ref, as_ref)

    run(ya_ref, yb_ref, h_ref, o_ref, a_ref)


def _outproj(ya, yb, yas, ybs, w, h, hs, g, g_next, tm):
    m = h.shape[0]
    ms = hs.shape[0]
    ka = ya.shape[1]
    two = yb is not None
    next_norm = g_next is not None
    once = dict(pipeline_mode=pl.Buffered(1))
    row = lambda width: pl.BlockSpec((tm, width), lambda i: (i, 0))
    fixed = lambda rows, width: pl.BlockSpec((rows, width), lambda i: (0, 0), **once)
    if two:
        kb = yb.shape[1]
        in_specs = [row(ka), row(kb), fixed(ms, ka), fixed(ms, kb), fixed(ka, D_MODEL),
                    pl.BlockSpec((kb, D_MODEL), lambda i: (ka // kb, 0), **once)]
        args = [ya, yb, yas, ybs, w, w]
    else:
        in_specs = [row(ka), fixed(ms, ka), fixed(ka, D_MODEL)]
        args = [ya, yas, w]
    in_specs += [row(D_MODEL), fixed(ms, D_MODEL), fixed(1, D_MODEL)]
    args += [h, hs, g]
    sample_out = pl.BlockSpec((ms, D_MODEL), lambda i: (0, 0))
    out_specs = [row(D_MODEL), sample_out]
    out_shape = [jax.ShapeDtypeStruct((m, D_MODEL), F32), jax.ShapeDtypeStruct((ms, D_MODEL), F32)]
    if next_norm:
        in_specs.append(fixed(1, D_MODEL))
        args.append(g_next)
        out_specs += [row(D_MODEL), sample_out]
        out_shape += [jax.ShapeDtypeStruct((m, D_MODEL), BF16), jax.ShapeDtypeStruct((ms, D_MODEL), BF16)]
    return pl.pallas_call(
        functools.partial(_outproj_kernel, two=two, next_norm=next_norm),
        grid=(m // tm,),
        in_specs=in_specs,
        out_specs=out_specs,
        out_shape=out_shape,
        compiler_params=pltpu.CompilerParams(dimension_semantics=("arbitrary",),
                                             vmem_limit_bytes=BIG_VMEM_LIMIT),
        name="outproj",
    )(*args)


def _mlstm_prompt_kernel(*refs, n_cast):
    (q_ref, k_ref, v_ref, o_ref, z_ref, gc_ref, gr_ref, gh_ref,
     yb_ref, wo_ref, x_ref, gp_ref, gn_ref) = refs[:13]
    cast_in = refs[13:13 + n_cast]
    h1_ref, a1_ref, c_out, n_out, m_out = refs[13 + n_cast:18 + n_cast]
    cast_out = refs[18 + n_cast:18 + 2 * n_cast]
    c_s, n_s, m_s = refs[18 + 2 * n_cast:]
    c = pl.program_id(1)
    t = CHUNK
    for src, dst in zip(cast_in, cast_out):
        dst[...] = src[...].astype(BF16)

    @pl.when(c == 0)
    def _():
        c_s[...] = jnp.zeros_like(c_s)
        n_s[...] = jnp.zeros_like(n_s)
        m_s[...] = jnp.zeros_like(m_s)

    row = lax.broadcasted_iota(jnp.int32, (t, t), 0)
    col = lax.broadcasted_iota(jnp.int32, (t, t), 1)
    tril = col <= row
    triu = row <= col
    gc = gc_ref[...]
    gr = gr_ref[...]
    mix = _dot(yb_ref[...], wo_ref[W_A:, :])
    ys = []
    for h in range(H_A):
        i_col = gc[:, h:h + 1]
        i_row = gr[h:h + 1, :]
        lf_col = gc[:, H_A + h:H_A + h + 1]
        lf_row = gr[H_A + h:H_A + h + 1, :]
        b_col = jnp.sum(jnp.where(tril, lf_row, 0.0), axis=1, keepdims=True)
        b_row = jnp.sum(jnp.where(triu, lf_col, 0.0), axis=0, keepdims=True)
        m_prev = m_s[h][:, 0:1]
        d = jnp.where(tril, b_col - b_row + i_row, -jnp.inf)
        inter = b_col + m_prev
        m_t = jnp.maximum(inter, jnp.max(d, axis=1, keepdims=True))
        w_intra = jnp.exp(d - m_t)
        w_inter = jnp.exp(inter - m_t) * SCALE_A
        q = q_ref[:, h * DK_A:(h + 1) * DK_A]
        k = k_ref[:, h * DK_A:(h + 1) * DK_A]
        v = v_ref[:, h * DV_A:(h + 1) * DV_A]
        s = _dot_nt(q, k) * (w_intra * SCALE_A)
        c_old = c_s[h]
        n_old = n_s[h]
        num = _dot(s.astype(BF16), v) + w_inter * _dot(q, c_old.astype(BF16))
        qn = jnp.sum(q.astype(F32) * n_old, axis=1, keepdims=True)
        den = jnp.sum(s, axis=1, keepdims=True) + w_inter * qn
        hh = num * (1.0 / jnp.maximum(jnp.abs(den), jnp.exp(-m_t)))
        hn = _rms(hh, gh_ref[h:h + 1, :])
        o = o_ref[:, h * DV_A:(h + 1) * DV_A].astype(F32)
        z = z_ref[:, h * DV_A:(h + 1) * DV_A].astype(F32)
        gate = z / ((1.0 + jnp.exp(-o)) * (1.0 + jnp.exp(-z)))
        ys.append((hn * gate).astype(BF16))
        if len(ys) == OUT_HEADS_A:
            rows = slice((h + 1 - OUT_HEADS_A) * DV_A, (h + 1) * DV_A)
            mix = mix + _dot(jnp.concatenate(ys, axis=-1), wo_ref[rows, :])
            ys = []
        b_last = b_col[t - 1:t, :]
        g_col = b_last - b_col + i_col
        m_new = jnp.maximum(b_last + m_prev, jnp.max(g_col, axis=0, keepdims=True))
        e_col = jnp.exp(g_col - m_new)
        decay = jnp.exp(b_last + m_prev - m_new)
        ke = k.astype(F32) * e_col
        c_s[h] = decay * c_old + _dot_tn(ke.astype(BF16), v)
        n_s[h] = decay * n_old + jnp.sum(ke, axis=0, keepdims=True)
        m_s[h] = jnp.broadcast_to(m_new, (1, LANES))
    new = x_ref[...] + _rms(mix, gp_ref[...])
    h1_ref[...] = new
    a1_ref[...] = _rms(new, gn_ref[...]).astype(BF16)

    @pl.when(c == pl.num_programs(1) - 1)
    def _():
        c_out[...] = c_s[...]
        n_out[...] = n_s[...]
        m_out[...] = m_s[...]


def _mlstm_prompt(proj, gates_col, gates_row, g_head, yb, wo, x, g_post, g_next, weights):
    nc = SEQ // CHUNK
    t = CHUNK
    qk_w = H_A * DK_A
    steps = BATCH * nc
    slab = lambda w: pl.BlockSpec((w.shape[0] // steps, w.shape[1]), lambda b, c: (b * nc + c, 0))
    tile = lambda width: pl.BlockSpec((None, t, width), lambda b, c: (b, c, 0))
    vec = pl.BlockSpec((1, D_MODEL), lambda b, c: (0, 0))
    return pl.pallas_call(
        functools.partial(_mlstm_prompt_kernel, n_cast=len(weights)),
        grid=(BATCH, nc),
        in_specs=[pl.BlockSpec((None, t, qk_w), lambda b, c: (b, c, 0)),
                  pl.BlockSpec((None, t, qk_w), lambda b, c: (b, c, 1)),
                  pl.BlockSpec((None, t, W_A), lambda b, c: (b, c, 1)),
                  pl.BlockSpec((None, t, W_A), lambda b, c: (b, c, 2)),
                  pl.BlockSpec((None, t, W_A), lambda b, c: (b, c, 3)),
                  pl.BlockSpec((None, t, LANES), lambda b, c: (b, c, 0)),
                  pl.BlockSpec((None, 2 * H_A, t), lambda b, c: (b, 0, c)),
                  pl.BlockSpec((H_A, DV_A), lambda b, c: (0, 0)),
                  tile(W_B),
                  pl.BlockSpec((W_A + W_B, D_MODEL), lambda b, c: (0, 0), pipeline_mode=pl.Buffered(1)),
                  tile(D_MODEL), vec, vec] + [slab(w) for w in weights],
        out_specs=[tile(D_MODEL), tile(D_MODEL),
                   pl.BlockSpec((None, H_A, DK_A, DV_A), lambda b, c: (b, 0, 0, 0)),
                   pl.BlockSpec((None, H_A, 1, DK_A), lambda b, c: (b, 0, 0, 0)),
                   pl.BlockSpec((None, H_A, 1, LANES), lambda b, c: (b, 0, 0, 0))]
        + [slab(w) for w in weights],
        out_shape=[jax.ShapeDtypeStruct((BATCH, SEQ, D_MODEL), F32),
                   jax.ShapeDtypeStruct((BATCH, SEQ, D_MODEL), BF16),
                   jax.ShapeDtypeStruct((BATCH, H_A, DK_A, DV_A), F32),
                   jax.ShapeDtypeStruct((BATCH, H_A, 1, DK_A), F32),
                   jax.ShapeDtypeStruct((BATCH, H_A, 1, LANES), F32)]
        + [jax.ShapeDtypeStruct(w.shape, BF16) for w in weights],
        scratch_shapes=[pltpu.VMEM((H_A, DK_A, DV_A), F32),
                        pltpu.VMEM((H_A, 1, DK_A), F32),
                        pltpu.VMEM((H_A, 1, LANES), F32)],
        compiler_params=pltpu.CompilerParams(dimension_semantics=("arbitrary", "arbitrary"),
                                             vmem_limit_bytes=BIG_VMEM_LIMIT),
        name="mlstm_prompt",
    )(proj, proj, proj, proj, proj, gates_col, gates_row, g_head, yb, wo, x, g_post, g_next, *weights)


def _mlstm_sample_body(r0, gi_ref, lf_ref, m0_ref, q_ref, k_ref, v_ref, o_ref, z_ref,
                       qt_ref, kt_ref, n0_ref, gh_ref, c0_ref,
                       y_ref, c1_ref, n1_ref, m1_ref):
    shift = lax.rem(DEC_BATCH - r0, DEC_BATCH)
    q_cols = [pltpu.roll(qt_ref[h], shift, axis=1) for h in range(H_A)]
    k_cols = [pltpu.roll(kt_ref[h], shift, axis=1) for h in range(H_A)]
    for j in range(c0_ref.shape[0]):
        rows = pl.ds(r0 + j, 1)
        i_v = gi_ref[rows, :]
        lf_v = lf_ref[rows, :]
        m0_v = m0_ref[rows, :]
        m_t = jnp.maximum(lf_v + m0_v, i_v)
        w_in = jnp.exp(i_v - m_t)
        w_st = jnp.exp(lf_v + m0_v - m_t)
        floor = jnp.exp(-m_t)
        m1_ref[rows, :] = m_t
        for h in range(H_A):
            ks = slice(h * DK_A, (h + 1) * DK_A)
            vs = slice(h * DV_A, (h + 1) * DV_A)
            wi = w_in[:, h:h + 1]
            ws = w_st[:, h:h + 1]
            q_col = q_cols[h][:, j:j + 1]
            k_col = k_cols[h][:, j:j + 1]
            q_row = q_ref[rows, ks]
            k_row = k_ref[rows, ks]
            v_row = v_ref[rows, vs]
            n_row = n0_ref[rows, ks]
            c_old = c0_ref[j, h]
            qk = jnp.sum(q_row * k_row, axis=1, keepdims=True) * SCALE_A
            s = qk * wi
            q_c = jnp.sum(c_old * q_col, axis=0, keepdims=True) * SCALE_A
            qn = jnp.sum(q_row * n_row, axis=1, keepdims=True) * SCALE_A
            num = s * v_row + ws * q_c
            den = s + ws * qn
            hh = num / jnp.maximum(jnp.abs(den), floor[:, h:h + 1])
            hn = _rms(hh, gh_ref[h:h + 1, :])
            y_ref[rows, vs] = hn * jax.nn.sigmoid(o_ref[rows, vs]) * _silu(z_ref[rows, vs])
            c1_ref[j, h] = ws * c_old + (wi * k_col) * v_row
            n1_ref[rows, ks] = ws * n_row + wi * k_row


def _mlstm_sample_rider(gi, lf, m0, proj, qt, kt, n0, g_head, c0, tb):
    qk_w = H_A * DK_A

    def build(bmap):
        once = dict(pipeline_mode=pl.Buffered(1))
        full = lambda shape: pl.BlockSpec(shape, lambda j, i: (0,) * len(shape))
        cols = lambda width, c: pl.BlockSpec((DEC_BATCH, width), lambda j, i: (0, c), **once)
        state = pl.BlockSpec((tb, H_A, DK_A, DV_A), lambda j, i: (bmap(j, i), 0, 0, 0))
        lanes = pl.BlockSpec((H_A, DK_A, DEC_BATCH), lambda j, i: (0, 0, 0), **once)
        args = [gi, lf, m0, proj, proj, proj, proj, proj, qt, kt, n0, g_head, c0]
        in_specs = [full((DEC_BATCH, H_A)), full((DEC_BATCH, H_A)), full((DEC_BATCH, H_A)),
                    cols(qk_w, 0), cols(qk_w, 1), cols(W_A, 1), cols(W_A, 2), cols(W_A, 3),
                    lanes, lanes, cols(qk_w, 0), full((H_A, DV_A)), state]
        out_shapes = [jax.ShapeDtypeStruct((DEC_BATCH, W_A), F32),
                      jax.ShapeDtypeStruct((DEC_BATCH, H_A, DK_A, DV_A), F32),
                      jax.ShapeDtypeStruct((DEC_BATCH, qk_w), F32),
                      jax.ShapeDtypeStruct((DEC_BATCH, H_A), F32)]
        out_specs = [full((DEC_BATCH, W_A)), state, full((DEC_BATCH, qk_w)), full((DEC_BATCH, H_A))]
        return args, in_specs, out_shapes, out_specs

    return build


def _s5_build_group(g, ldt_ref, lamr_ref, lami_ref, b1_ref, cc_ref, dsk_ref,
                    m_ref, w_ref, v_ref, p1_ref, p2_ref):
    dt = jnp.exp(ldt_ref[g])
    lam_re, lam_im = lamr_ref[g], lami_ref[g]
    mag = jnp.exp(lam_re * dt)
    ang = lam_im * dt
    a_re, a_im = mag * jnp.cos(ang), mag * jnp.sin(ang)
    den = lam_re * lam_re + lam_im * lam_im
    f_re = ((a_re - 1.0) * lam_re + a_im * lam_im) / den
    f_im = (a_im * lam_re - (a_re - 1.0) * lam_im) / den
    pows = [(jnp.ones_like(a_re), jnp.zeros_like(a_im))]
    for _ in range(S5_T):
        r_re, r_im = pows[-1]
        pows.append((r_re * a_re - r_im * a_im, r_re * a_im + r_im * a_re))

    def tall(vals):
        return jnp.concatenate([jnp.broadcast_to(x, (GROUP_B, LANES)) for x in vals], axis=0)

    af_re = tall([pows[S5_T - 1 - s][0] * f_re - pows[S5_T - 1 - s][1] * f_im for s in range(S5_T)])
    af_im = tall([pows[S5_T - 1 - s][0] * f_im + pows[S5_T - 1 - s][1] * f_re for s in range(S5_T)])
    lane = lax.broadcasted_iota(jnp.int32, (1, LANES), 1)
    sgn_lane = jnp.where(lane < P_B, -1.0, 1.0)
    b1 = b1_ref[g]
    b2 = sgn_lane * pltpu.roll(b1, P_B, axis=1)
    w_ref[g] = (af_re * b1 + af_im * b2).astype(BF16)

    eye = (lax.broadcasted_iota(jnp.int32, (LANES, LANES), 0)
           == lax.broadcasted_iota(jnp.int32, (LANES, LANES), 1))

    def column(row):
        return jnp.sum(jnp.where(eye, row, 0.0), axis=1, keepdims=True)

    ac_re, ac_im, fc_re, fc_im = column(a_re), column(a_im), column(f_re), column(f_im)
    kid = lax.broadcasted_iota(jnp.int32, (S5_T, LANES), 0)
    expand = (lax.broadcasted_iota(jnp.int32, (S5_T, S5_W), 1) // GROUP_B
              == lax.broadcasted_iota(jnp.int32, (S5_T, S5_W), 0)).astype(BF16)

    def lane_blocks(part):
        stacked = jnp.zeros((S5_T, LANES), F32)
        for k in range(S5_T):
            stacked = jnp.where(kid == k, pows[k][part], stacked)
        hi = stacked.astype(BF16)
        rest = stacked - hi.astype(F32)
        mid = rest.astype(BF16)
        lo = (rest - mid.astype(F32)).astype(BF16)
        return _dot_tn(hi, expand) + _dot_tn(mid, expand) + _dot_tn(lo, expand)

    q_re, q_im = lane_blocks(0), lane_blocks(1)
    cc = cc_ref[g]
    cs = pltpu.roll(cc, P_B, axis=0)
    rowi = lax.broadcasted_iota(jnp.int32, (LANES, 1), 0)
    sgn_row = jnp.where(rowi < P_B, 1.0, -1.0)

    def readout(r_re, r_im):
        return sgn_row * (r_re * cc) - r_im * cs

    v_ref[g] = readout(q_re * ac_re - q_im * ac_im, q_re * ac_im + q_im * ac_re).astype(BF16)
    vf = readout(q_re * fc_re - q_im * fc_im, q_re * fc_im + q_im * fc_re)
    kw = _dot3(b1[0:GROUP_B, :], vf)
    lane_w = lax.broadcasted_iota(jnp.int32, (GROUP_B, S5_W), 1)
    blocks = [kw]
    for s in range(1, S5_T):
        blocks.append(jnp.where(lane_w >= s * GROUP_B, pltpu.roll(kw, s * GROUP_B, axis=1), 0.0))
    toep = jnp.concatenate(blocks, axis=0)
    ri = lax.broadcasted_iota(jnp.int32, (S5_W, S5_W), 0)
    ci = lax.broadcasted_iota(jnp.int32, (S5_W, S5_W), 1)
    m_ref[g] = (toep + jnp.where(ri == ci, dsk_ref[g], 0.0)).astype(BF16)

    r_re, r_im = pows[S5_T]
    rid = lax.broadcasted_iota(jnp.int32, (8, LANES), 0)
    p1 = jnp.where(rid == S5_A1_ROW, a_re, 0.0)
    p2 = jnp.where(rid == S5_A1_ROW, sgn_lane * a_im, 0.0)
    for kk in range(S5_LEVELS):
        p1 = jnp.where(rid == kk, r_re, p1)
        p2 = jnp.where(rid == kk, sgn_lane * r_im, p2)
        r_re, r_im = r_re * r_re - r_im * r_im, 2.0 * (r_re * r_im)
    p1_ref[g] = p1
    p2_ref[g] = p2


def _s5_build_kernel(*refs, n_cast):
    for src, dst in zip(refs[6:6 + n_cast], refs[11 + n_cast:]):
        dst[...] = src[...].astype(BF16)
    for g in range(OCT):
        _s5_build_group(g, *refs[:6], *refs[6 + n_cast:11 + n_cast])


def _s5_build(ldt, lamr, lami, b1, cc, dsk, casts):
    steps = G_B // OCT
    o3 = lambda a, b: pl.BlockSpec((OCT, a, b), lambda g: (g, 0, 0))
    slabs = [pl.BlockSpec((rows // steps, w.shape[1]), lambda g: (g, 0)) for w, rows in casts]
    return pl.pallas_call(
        functools.partial(_s5_build_kernel, n_cast=len(casts)),
        grid=(steps,),
        in_specs=[o3(1, 1), o3(1, LANES), o3(1, LANES),
                  o3(S5_W, LANES), o3(LANES, S5_W), o3(1, S5_W)] + slabs,
        out_specs=[o3(S5_W, S5_W), o3(S5_W, LANES), o3(LANES, S5_W), o3(8, LANES), o3(8, LANES)] + slabs,
        out_shape=[jax.ShapeDtypeStruct((G_B, S5_W, S5_W), BF16),
                   jax.ShapeDtypeStruct((G_B, S5_W, LANES), BF16),
                   jax.ShapeDtypeStruct((G_B, LANES, S5_W), BF16),
                   jax.ShapeDtypeStruct((G_B, 8, LANES), F32),
                   jax.ShapeDtypeStruct((G_B, 8, LANES), F32)]
        + [jax.ShapeDtypeStruct((rows, w.shape[1]), BF16) for w, rows in casts],
        compiler_params=_cparams("parallel"),
        name="s5_build",
    )(ldt, lamr, lami, b1, cc, dsk, *[w for w, _ in casts])


def _s5_prompt_kernel(u_ref, m_ref, w_ref, v_ref, p1_ref, p2_ref, y_ref, xf_ref, y_s):
    nblk = SEQ // S5_T
    rows = BATCH * nblk
    bidx = lax.broadcasted_iota(jnp.int32, (rows, LANES), 0) & (nblk - 1)
    steps = [u_ref[pl.ds(s, rows, stride=S5_T), :] for s in range(S5_T)]
    for g in range(OCT):
        gl = slice(g * GROUP_B, (g + 1) * GROUP_B)
        u = jnp.concatenate([x[:, gl] for x in steps], axis=-1).astype(BF16)
        s = _dot(u, w_ref[g])
        for kk in range(S5_LEVELS):
            sh = 1 << kk
            r = jnp.where(bidx >= sh, pltpu.roll(s, sh, axis=0), 0.0)
            s = s + p1_ref[g, kk:kk + 1, :] * r + p2_ref[g, kk:kk + 1, :] * pltpu.roll(r, P_B, axis=1)
        x_prev = jnp.where(bidx >= 1, pltpu.roll(s, 1, axis=0), 0.0)
        y = _dot(u, m_ref[g]) + _dot(x_prev.astype(BF16), v_ref[g])
        y_s[g] = y
        for b in range(BATCH):
            xf_ref[g, b:b + 1, :] = s[(b + 1) * nblk - 1:(b + 1) * nblk, :]
    for t in range(S5_T):
        half = slice((t // OCT) * LANES, (t // OCT + 1) * LANES)
        tl = slice((t % OCT) * GROUP_B, (t % OCT + 1) * GROUP_B)
        y_ref[pl.ds(t, rows, stride=S5_T), :] = jnp.concatenate(
            [y_s[g, :, half][:, tl] for g in range(OCT)], axis=-1)


def _s5_prompt(u, m, w, v, p1, p2):
    mp = BATCH * SEQ
    rows = BATCH * (SEQ // S5_T)
    o3 = lambda a, b: pl.BlockSpec((OCT, a, b), lambda g: (g, 0, 0))
    return pl.pallas_call(
        _s5_prompt_kernel,
        grid=(G_B // OCT,),
        in_specs=[pl.BlockSpec((mp, LANES), lambda g: (0, g)),
                  o3(S5_W, S5_W), o3(S5_W, LANES), o3(LANES, S5_W), o3(8, LANES), o3(8, LANES)],
        out_specs=[pl.BlockSpec((mp, LANES), lambda g: (0, g)), o3(BATCH, LANES)],
        out_shape=[jax.ShapeDtypeStruct((mp, W_B), F32),
                   jax.ShapeDtypeStruct((G_B, BATCH, LANES), F32)],
        scratch_shapes=[pltpu.VMEM((OCT, rows, S5_W), F32)],
        compiler_params=_cparams("parallel"),
        name="s5_prompt",
    )(u, m, w, v, p1, p2)


def _s5_sample_kernel(u_ref, xr_ref, xi_ref, m_ref, w_ref, v_ref, p1_ref, p2_ref,
                      y_ref, x1r_ref, x1i_ref):
    lane = lax.broadcasted_iota(jnp.int32, (DEC_BATCH, LANES), 1)
    last = LANES - GROUP_B
    half = S5_W // 2
    a1 = slice(S5_A1_ROW, S5_A1_ROW + 1)
    u_all = u_ref[...]
    ys = []
    for g in range(OCT):
        ps = slice(g * P_B, (g + 1) * P_B)
        u = jnp.where(lane >= last, pltpu.roll(u_all, (last - g * GROUP_B) % LANES, axis=1), 0.0)
        u = u.astype(BF16)
        x0 = jnp.concatenate([xr_ref[:, ps], xi_ref[:, ps]], axis=-1)
        x1 = (p1_ref[g, a1, :] * x0 + p2_ref[g, a1, :] * pltpu.roll(x0, P_B, axis=1)
              + _dot(u, w_ref[g, half:, :]))
        x1r_ref[:, ps] = x1[:, :P_B]
        x1i_ref[:, ps] = x1[:, P_B:]
        y = (_dot(x0.astype(BF16), v_ref[g])[:, 0:GROUP_B]
             + _dot(u, m_ref[g, half:, :])[:, S5_W - GROUP_B:])
        ys.append(y)
    y_ref[...] = jnp.concatenate(ys, axis=-1)


def _s5_sample(u, xr, xi, m, w, v, p1, p2):
    o3 = lambda a, b: pl.BlockSpec((OCT, a, b), lambda g: (g, 0, 0))
    tile = pl.BlockSpec((DEC_BATCH, LANES), lambda g: (0, g))
    st = pl.BlockSpec((DEC_BATCH, OCT * P_B), lambda g: (0, g))
    st_shape = jax.ShapeDtypeStruct((DEC_BATCH, G_B * P_B), F32)
    return pl.pallas_call(
        _s5_sample_kernel,
        grid=(G_B // OCT,),
        in_specs=[tile, st, st, o3(S5_W, S5_W), o3(S5_W, LANES), o3(LANES, S5_W), o3(8, LANES), o3(8, LANES)],
        out_specs=[tile, st, st],
        out_shape=[jax.ShapeDtypeStruct((DEC_BATCH, W_B), F32), st_shape, st_shape],
        compiler_params=_cparams("parallel"),
        name="s5_sample",
    )(u, xr, xi, m, w, v, p1, p2)


def _glu_kernel(y_ref, w_ref, b_ref, z_ref, o_ref):
    y = _gelu_tanh(y_ref[...])
    gate = jax.nn.sigmoid(_dot(y.astype(BF16), w_ref[...]) + b_ref[...])
    o_ref[...] = (y * gate * _silu(z_ref[...].astype(F32))).astype(o_ref.dtype)


def _glu(y, w, b, proj_b, tm):
    m = y.shape[0]
    return pl.pallas_call(
        _glu_kernel,
        grid=(m // tm,),
        in_specs=[pl.BlockSpec((tm, W_B), lambda i: (i, 0)),
                  pl.BlockSpec((W_B, W_B), lambda i: (0, 0)),
                  pl.BlockSpec((1, W_B), lambda i: (0, 0)),
                  pl.BlockSpec((tm, W_B), lambda i: (i, 1))],
        out_specs=pl.BlockSpec((tm, W_B), lambda i: (i, 0)),
        out_shape=jax.ShapeDtypeStruct((m, W_B), BF16),
        compiler_params=_cparams("parallel"),
        name="glu",
    )(y, w, b, proj_b)


def _rope_table_kernel(cos_ref, sin_ref, *, pos0):
    shape = cos_ref.shape
    pos = lax.broadcasted_iota(jnp.int32, shape, 0).astype(F32) + pos0
    j = lax.broadcasted_iota(jnp.int32, shape, 1).astype(F32)
    ang = pos * jnp.power(ROPE_BASE, -(j / (DK_C // 2)))
    cos_ref[...] = jnp.cos(ang)
    sin_ref[...] = jnp.sin(ang)


def _rope_table(rows, pos0):
    shape = jax.ShapeDtypeStruct((rows, DK_C // 2), F32)
    return pl.pallas_call(functools.partial(_rope_table_kernel, pos0=float(pos0)),
                          out_shape=[shape, shape], name="rope_table")()


def _rope(x, cos, sin):
    half = DK_C // 2
    x1, x2 = x[:, :half], x[:, half:]
    return jnp.concatenate([x1 * cos - x2 * sin, x1 * sin + x2 * cos], axis=-1)


def _ret_prompt_kernel(q_ref, k_ref, v_ref, z_ref, cos_ref, sin_ref, gh_ref, wo_ref, h_ref, gp_ref,
                       o_ref, s_out, s_s):
    c = pl.program_id(1)
    t = CHUNK

    @pl.when(c == 0)
    def _():
        s_s[...] = jnp.zeros_like(s_s)

    row = lax.broadcasted_iota(jnp.int32, (t, t), 0)
    col = lax.broadcasted_iota(jnp.int32, (t, t), 1)
    tril = col <= row
    diff = (row - col).astype(F32)
    tpos = lax.broadcasted_iota(jnp.int32, (t, 1), 0).astype(F32)
    cos, sin = cos_ref[...], sin_ref[...]
    mix, ys = None, []
    for h in range(H_C):
        lg = LOG_GAMMA[h]
        ks = slice(h * DK_C, (h + 1) * DK_C)
        vs = slice(h * DV_C, (h + 1) * DV_C)
        q = _rope(q_ref[:, ks].astype(F32), cos, sin).astype(BF16)
        k32 = _rope(k_ref[:, ks].astype(F32), cos, sin)
        v = v_ref[:, vs]
        mask = jnp.where(tril, jnp.exp(diff * lg), 0.0) * SCALE_C
        s = _dot_nt(q, k32.astype(BF16)) * mask
        s_old = s_s[h]
        o = _dot(s.astype(BF16), v) + _dot(q, s_old.astype(BF16)) * jnp.exp((tpos + 1.0) * lg)
        y = (_rms(o, gh_ref[h:h + 1, :]) * _silu(z_ref[:, vs].astype(F32))).astype(BF16)
        ys.append(y)
        if len(ys) == OUT_HEADS:
            rows = slice((h + 1 - OUT_HEADS) * DV_C, (h + 1) * DV_C)
            part = _dot(jnp.concatenate(ys, axis=-1), wo_ref[rows, :])
            mix = part if mix is None else mix + part
            ys = []
        k_tail = k32 * (jnp.exp((t - 1.0 - tpos) * lg) * SCALE_C)
        s_s[h] = math.exp(t * lg) * s_old + _dot_tn(k_tail.astype(BF16), v)
    o_ref[...] = h_ref[...] + _rms(mix, gp_ref[...])

    @pl.when(c == pl.num_programs(1) - 1)
    def _():
        s_out[...] = s_s[...]


def _ret_prompt(proj, cos, sin, g_head, wo, h, g_post):
    nc = SEQ // CHUNK
    t = CHUNK
    qk_w = H_C * DK_C
    return pl.pallas_call(
        _ret_prompt_kernel,
        grid=(BATCH, nc),
        in_specs=[pl.BlockSpec((None, t, qk_w), lambda b, c: (b, c, 0)),
                  pl.BlockSpec((None, t, qk_w), lambda b, c: (b, c, 1)),
                  pl.BlockSpec((None, t, W_C), lambda b, c: (b, c, 1)),
                  pl.BlockSpec((None, t, W_C), lambda b, c: (b, c, 2)),
                  pl.BlockSpec((t, DK_C // 2), lambda b, c: (c, 0)),
                  pl.BlockSpec((t, DK_C // 2), lambda b, c: (c, 0)),
                  pl.BlockSpec((H_C, DV_C), lambda b, c: (0, 0)),
                  pl.BlockSpec((W_C, D_MODEL), lambda b, c: (0, 0), pipeline_mode=pl.Buffered(1)),
                  pl.BlockSpec((None, t, D_MODEL), lambda b, c: (b, c, 0)),
                  pl.BlockSpec((1, D_MODEL), lambda b, c: (0, 0))],
        out_specs=[pl.BlockSpec((None, t, D_MODEL), lambda b, c: (b, c, 0)),
                   pl.BlockSpec((None, H_C, DK_C, DV_C), lambda b, c: (b, 0, 0, 0))],
        out_shape=[jax.ShapeDtypeStruct((BATCH, SEQ, D_MODEL), F32),
                   jax.ShapeDtypeStruct((BATCH, H_C, DK_C, DV_C), F32)],
        scratch_shapes=[pltpu.VMEM((H_C, DK_C, DV_C), F32)],
        compiler_params=pltpu.CompilerParams(dimension_semantics=("parallel", "arbitrary"),
                                             vmem_limit_bytes=BIG_VMEM_LIMIT),
        name="ret_prompt",
    )(proj, proj, proj, proj, cos, sin, g_head, wo, h, g_post)


def _rope_sample_kernel(q_ref, k_ref, cos_ref, sin_ref, qo_ref, ko_ref):
    cos, sin = cos_ref[0:1, :], sin_ref[0:1, :]
    for h in range(H_C):
        ks = slice(h * DK_C, (h + 1) * DK_C)
        qo_ref[:, ks] = _rope(q_ref[:, ks], cos, sin)
        ko_ref[:, ks] = _rope(k_ref[:, ks], cos, sin)


def _rope_sample(proj, cos, sin):
    qk_w = H_C * DK_C
    shape = jax.ShapeDtypeStruct((DEC_BATCH, qk_w), F32)
    return pl.pallas_call(
        _rope_sample_kernel,
        grid=(1,),
        in_specs=[pl.BlockSpec((DEC_BATCH, qk_w), lambda i: (0, 0)),
                  pl.BlockSpec((DEC_BATCH, qk_w), lambda i: (0, 1)),
                  pl.BlockSpec((8, DK_C // 2), lambda i: (0, 0)),
                  pl.BlockSpec((8, DK_C // 2), lambda i: (0, 0))],
        out_specs=[pl.BlockSpec((DEC_BATCH, qk_w), lambda i: (0, 0)),
                   pl.BlockSpec((DEC_BATCH, qk_w), lambda i: (0, 0))],
        out_shape=[shape, shape],
        compiler_params=_cparams("arbitrary"),
        name="rope_sample",
    )(proj, proj, cos, sin)


def _ret_sample_body(r0, q_ref, k_ref, v_ref, z_ref, qt_ref, kt_ref, gh_ref, s0_ref, y_ref, s1_ref):
    shift = lax.rem(DEC_BATCH - r0, DEC_BATCH)
    for h in range(H_C):
        gamma = math.exp(LOG_GAMMA[h])
        ks = slice(h * DK_C, (h + 1) * DK_C)
        vs = slice(h * DV_C, (h + 1) * DV_C)
        q_cols = pltpu.roll(qt_ref[h], shift, axis=1)
        k_cols = pltpu.roll(kt_ref[h], shift, axis=1)
        for j in range(s0_ref.shape[0]):
            rows = pl.ds(r0 + j, 1)
            v_row = v_ref[rows, vs]
            s_old = s0_ref[j, h]
            qk = jnp.sum(q_ref[rows, ks] * k_ref[rows, ks], axis=1, keepdims=True) * SCALE_C
            o = qk * v_row + jnp.sum(s_old * q_cols[:, j:j + 1], axis=0, keepdims=True) * gamma
            y_ref[rows, vs] = _rms(o, gh_ref[h:h + 1, :]) * _silu(z_ref[rows, vs])
            s1_ref[j, h] = gamma * s_old + (k_cols[:, j:j + 1] * SCALE_C) * v_row


def _ret_sample_rider(q, k, proj, qt, kt, g_head, s0):
    qk_w = H_C * DK_C

    def build(bmap):
        once = dict(pipeline_mode=pl.Buffered(1))
        fixed = lambda shape, *idx: pl.BlockSpec(shape, lambda j, i: idx or (0,) * len(shape), **once)
        state = pl.BlockSpec((1, H_C, DK_C, DV_C), lambda j, i: (bmap(j, i), 0, 0, 0))
        args = [q, k, proj, proj, qt, kt, g_head, s0]
        in_specs = [fixed((DEC_BATCH, qk_w)), fixed((DEC_BATCH, qk_w)),
                    fixed((DEC_BATCH, W_C), 0, 1), fixed((DEC_BATCH, W_C), 0, 2),
                    fixed((H_C, DK_C, DEC_BATCH)), fixed((H_C, DK_C, DEC_BATCH)),
                    fixed((H_C, DV_C)), state]
        out_shapes = [jax.ShapeDtypeStruct((DEC_BATCH, W_C), F32),
                      jax.ShapeDtypeStruct((DEC_BATCH, H_C, DK_C, DV_C), F32)]
        out_specs = [pl.BlockSpec((DEC_BATCH, W_C), lambda j, i: (0, 0)), state]
        return args, in_specs, out_shapes, out_specs

    return build


def kernel(x_prompt, x_sample, state_mlstm_C, state_mlstm_n, state_mlstm_m, state_s5_re, state_s5_im, state_ret, g_pre, g_post, w_in0, b_gates0, g_head_a, lam_re, lam_im, log_dt, b_re, b_im, c_re, c_im, d_skip, w_glu, b_glu, w_out0, w_in1, g_head_c, w_out1):
    mp = BATCH * SEQ
    xp = x_prompt.reshape(mp, D_MODEL).astype(F32)
    xs = x_sample.reshape(DEC_BATCH, D_MODEL).astype(F32)

    w0t = w_in0.T.astype(F32)
    w0g = jnp.pad(w0t[QKV0:GATE0], ((0, LANES - 2 * H_A), (0, 0)))
    w0b = w0t[GATE0:]
    wg = w_glu.astype(BF16)
    bglu = b_glu.reshape(1, W_B).astype(F32)
    g_pre = g_pre.astype(F32)
    g_post = g_post.astype(F32)
    bg = jnp.pad(b_gates0.astype(F32), (0, LANES - 2 * H_A))[None, :]
    gh_a = g_head_a.astype(F32)
    gh_c = g_head_c.astype(F32)

    dup = lambda a: jnp.concatenate([a, a], axis=-1).astype(F32)
    lamr, lami = dup(lam_re), dup(lam_im)
    b1 = jnp.concatenate([b_re.transpose(0, 2, 1), b_im.transpose(0, 2, 1)], axis=-1)
    cc = jnp.concatenate([c_re.transpose(0, 2, 1), c_im.transpose(0, 2, 1)], axis=1)
    s5m, s5w, s5v, s5p1, s5p2, w0a, wo0 = _s5_build(
        log_dt.reshape(G_B, 1, 1).astype(F32), lamr[:, None, :], lami[:, None, :],
        jnp.tile(b1.astype(F32), (1, S5_T, 1)), jnp.tile(cc.astype(F32), (1, 1, S5_T)),
        jnp.tile(d_skip.astype(F32), (1, S5_T))[:, None, :],
        [(w0t, QKV0), (w_out0.astype(F32), W_A + W_B)])

    a0, gates = _norm_gates(xp, g_pre[0:1], w0g, bg, 512)
    a0s, gates_s = _norm_gates(xs, g_pre[0:1], w0g, bg, DEC_BATCH)
    pas = _proj(a0s, a0s, w0a, QKV0, F32, DEC_BATCH, 1024, True)[0]
    qk_a = H_A * DK_A
    to_cols = lambda a, nh, dk: a.reshape(DEC_BATCH, nh, dk).transpose(1, 2, 0)
    tb_a = DEC_BATCH // ((QKV0 // 2048) * (mp // 512))
    pa, yas, c_s, n_s, m_s = _proj_rider(
        a0, w0a, QKV0, BF16, 512, 2048, True,
        lambda blk, *refs: _mlstm_sample_body(blk * tb_a, *refs), lambda step: step,
        _mlstm_sample_rider(
            gates_s[:, :H_A], gates_s[:, H_A:2 * H_A],
            state_mlstm_m.astype(F32), pas,
            to_cols(pas[:, :qk_a], H_A, DK_A), to_cols(pas[:, qk_a:2 * qk_a], H_A, DK_A),
            state_mlstm_n.reshape(DEC_BATCH, qk_a).astype(F32), gh_a, state_mlstm_C.astype(F32),
            tb_a))
    pb, pbs = _proj(a0, a0s, w0b, 2 * W_B, F32, 1024, 1024, True)
    gates3 = gates.reshape(BATCH, SEQ, LANES)
    gates_row = gates3[:, :, :2 * H_A].transpose(0, 2, 1)
    yb, xf = _s5_prompt(pb, s5m, s5w, s5v, s5p1, s5p2)
    yb = _glu(yb, wg, bglu, pb, 512)
    s5r_p = xf[:, :, :P_B].transpose(1, 0, 2)
    s5i_p = xf[:, :, P_B:].transpose(1, 0, 2)
    h1, a1, c_p, n_p, m_p, w1, wo1 = _mlstm_prompt(
        pa.reshape(BATCH, SEQ, QKV0), gates3, gates_row, gh_a, yb.reshape(BATCH, SEQ, W_B), wo0,
        x_prompt.astype(F32), g_post[0:1], g_pre[1:2], [w_in1.astype(F32), w_out1.astype(F32)])
    h1 = h1.reshape(mp, D_MODEL)
    a1 = a1.reshape(mp, D_MODEL)
    ybs, s5r_s, s5i_s = _s5_sample(
        pbs, state_s5_re.reshape(DEC_BATCH, G_B * P_B).astype(F32),
        state_s5_im.reshape(DEC_BATCH, G_B * P_B).astype(F32), s5m, s5w, s5v, s5p1, s5p2)
    s5r_s = s5r_s.reshape(DEC_BATCH, G_B, P_B)
    s5i_s = s5i_s.reshape(DEC_BATCH, G_B, P_B)
    ybs = _glu(ybs, wg, bglu, pbs, DEC_BATCH)
    _, h1s, _, a1s = _outproj(yas, ybs, yas, ybs, wo0, xs, xs, g_post[0:1], g_pre[1:2], DEC_BATCH)

    cols1 = 2 * H_C * DK_C + 2 * W_C
    p1s = _proj(a1s, a1s, w1, cols1, F32, DEC_BATCH, 1024, False)[0]
    cos_s, sin_s = _rope_table(8, PAST_LEN)
    qs, ks = _rope_sample(p1s, cos_s, sin_s)
    p1, ycs, s_s = _proj_rider(
        a1, w1, cols1, BF16, 512, cols1 // 8, False, _ret_sample_body, lambda step: step,
        _ret_sample_rider(qs, ks, p1s, to_cols(qs, H_C, DK_C), to_cols(ks, H_C, DK_C),
                          gh_c, state_ret.astype(F32)))
    cos_p, sin_p = _rope_table(SEQ, 0)
    y_p, s_p = _ret_prompt(p1.reshape(BATCH, SEQ, -1), cos_p, sin_p, gh_c, wo1,
                           h1.reshape(BATCH, SEQ, D_MODEL), g_post[1:2])
    y_s = _outproj(ycs, None, ycs, None, wo1, h1s, h1s, g_post[1:2], None, DEC_BATCH)[1]

    return (y_p.reshape(BATCH, SEQ, D_MODEL), y_s.reshape(DEC_BATCH, 1, D_MODEL),
            c_p, n_p.reshape(BATCH, H_A, DK_A), m_p[:, :, 0, 0],
            s5r_p, s5i_p, s_p,
            c_s, n_s.reshape(DEC_BATCH, H_A, DK_A), m_s,
            s5r_s, s5i_s, s_s)
```

```python
import functools
import math

import jax
import jax.numpy as jnp
from jax import lax
from jax.experimental import pallas as pl
from jax.experimental.pallas import tpu as pltpu

F32 = jnp.float32
BF16 = jnp.bfloat16

D_MODEL = 2048
BATCH = 4
SEQ = 2048
DEC_BATCH = 128
PAST_LEN = 16384
H_A = 4
DK_A = 256
DV_A = 512
W_A = H_A * DV_A
GROUP_B = 16
G_B = 64
P_B = 64
W_B = G_B * GROUP_B
H_C = 8
DK_C = 256
DV_C = 512
W_C = H_C * DV_C
CHUNK = 256
NORM_EPS = 1e-6
ROPE_BASE = 10000.0
QKV0 = 2 * H_A * DK_A + 3 * W_A
GATE0 = QKV0 + 2 * H_A
SCALE_A = DK_A ** -0.5
SCALE_C = DK_C ** -0.5
LOG_GAMMA = tuple(math.log1p(-(2.0 ** (-5.0 - h))) for h in range(H_C))
S5_T = 16
S5_W = S5_T * GROUP_B
LANES = 128
OCT = LANES // GROUP_B
S5_LEVELS = int(math.log2(SEQ // S5_T))
S5_A1_ROW = S5_LEVELS
VMEM_LIMIT = 48 * 1024 * 1024
OUT_HEADS = 4
OUT_HEADS_A = 2
BIG_VMEM_LIMIT = 56 * 1024 * 1024


def _cparams(*sem):
    return pltpu.CompilerParams(dimension_semantics=sem, vmem_limit_bytes=VMEM_LIMIT)


def _dot(a, b):
    return jnp.dot(a, b, preferred_element_type=F32)


def _dot_nt(a, b):
    return lax.dot_general(a, b, (((1,), (1,)), ((), ())), preferred_element_type=F32)


def _dot_tn(a, b):
    return lax.dot_general(a, b, (((0,), (0,)), ((), ())), preferred_element_type=F32)


def _split(x):
    hi = x.astype(BF16)
    return hi, (x - hi.astype(F32)).astype(BF16)


def _dot3(a, b):
    a_hi, a_lo = _split(a)
    b_hi, b_lo = _split(b)
    return _dot(a_hi, b_hi) + _dot(a_hi, b_lo) + _dot(a_lo, b_hi)


def _log_sigmoid(x):
    return jnp.minimum(x, 0.0) - jnp.log1p(jnp.exp(-jnp.abs(x)))


def _silu(x):
    return x * jax.nn.sigmoid(x)


def _gelu_tanh(x):
    return 0.5 * x * (1.0 + jnp.tanh(math.sqrt(2.0 / math.pi) * (x + 0.044715 * (x * x * x))))


def _rms(x, g):
    return x * lax.rsqrt(jnp.mean(x * x, axis=-1, keepdims=True) + NORM_EPS) * g


def _norm_gates_kernel(x_ref, g_ref, wg_ref, bg_ref, a_ref, gates_ref):
    a = _rms(x_ref[...], g_ref[...])
    a_ref[...] = a.astype(BF16)
    a_hi, a_lo = _split(a)
    w_hi, w_lo = _split(wg_ref[...])
    pre = _dot_nt(a_hi, w_hi) + _dot_nt(a_hi, w_lo) + _dot_nt(a_lo, w_hi) + bg_ref[...]
    lane = lax.broadcasted_iota(jnp.int32, pre.shape, 1)
    gates_ref[...] = jnp.where((lane >= H_A) & (lane < 2 * H_A), _log_sigmoid(pre), pre)


def _norm_gates(x, g, wg, bg, tm):
    m = x.shape[0]
    return pl.pallas_call(
        _norm_gates_kernel,
        grid=(m // tm,),
        in_specs=[pl.BlockSpec((tm, D_MODEL), lambda i: (i, 0)),
                  pl.BlockSpec((1, D_MODEL), lambda i: (0, 0)),
                  pl.BlockSpec((LANES, D_MODEL), lambda i: (0, 0)),
                  pl.BlockSpec((1, LANES), lambda i: (0, 0))],
        out_specs=[pl.BlockSpec((tm, D_MODEL), lambda i: (i, 0)),
                   pl.BlockSpec((tm, LANES), lambda i: (i, 0))],
        out_shape=[jax.ShapeDtypeStruct((m, D_MODEL), BF16),
                   jax.ShapeDtypeStruct((m, LANES), F32)],
        compiler_params=_cparams("parallel"),
        name="norm_gates",
    )(x, g, wg, bg)


def _proj_kernel(xp_ref, xs_ref, w_ref, op_ref, os_ref, wb_ref, *, w_transposed):
    mm = _dot_nt if w_transposed else _dot

    @pl.when(pl.program_id(1) == 0)
    def _():
        wb_ref[...] = w_ref[...].astype(BF16)
        os_ref[...] = mm(xs_ref[...], wb_ref[...])

    op_ref[...] = mm(xp_ref[...], wb_ref[...]).astype(op_ref.dtype)


def _proj(xp, xs, w, n_cols, out_dtype, tm, tn, w_transposed):
    m, k = xp.shape
    ms = xs.shape[0]
    if w_transposed:
        w_spec = pl.BlockSpec((tn, k), lambda j, i: (j, 0))
        wb_shape = (tn, k)
    else:
        w_spec = pl.BlockSpec((k, tn), lambda j, i: (0, j))
        wb_shape = (k, tn)
    return pl.pallas_call(
        functools.partial(_proj_kernel, w_transposed=w_transposed),
        grid=(n_cols // tn, m // tm),
        in_specs=[pl.BlockSpec((tm, k), lambda j, i: (i, 0)),
                  pl.BlockSpec((ms, k), lambda j, i: (0, 0)),
                  w_spec],
        out_specs=[pl.BlockSpec((tm, tn), lambda j, i: (i, j)),
                   pl.BlockSpec((ms, tn), lambda j, i: (0, j))],
        out_shape=[jax.ShapeDtypeStruct((m, n_cols), out_dtype),
                   jax.ShapeDtypeStruct((ms, n_cols), F32)],
        scratch_shapes=[pltpu.VMEM(wb_shape, BF16)],
        compiler_params=_cparams("parallel", "arbitrary"),
        name="proj",
    )(xp, xs, w)


def _proj_rider_kernel(*refs, n_in, n_out, body, batch_of_step, n_i, w_transposed):
    xp_ref, w_ref = refs[:2]
    rider_in = refs[2:2 + n_in]
    op_ref = refs[2 + n_in]
    rider_out = refs[3 + n_in:3 + n_in + n_out]
    mm = _dot_nt if w_transposed else _dot
    i = pl.program_id(1)

    if w_ref.dtype == BF16:
        wb_ref = w_ref
    else:
        wb_ref = refs[-1]

        @pl.when(i == 0)
        def _():
            wb_ref[...] = w_ref[...].astype(BF16)

    op_ref[...] = mm(xp_ref[...], wb_ref[...]).astype(op_ref.dtype)
    body(batch_of_step(pl.program_id(0) * n_i + i), *rider_in, *rider_out)


def _proj_rider(xp, w, n_cols, out_dtype, tm, tn, w_transposed, body, batch_of_step, rider):
    m, k = xp.shape
    n_i = m // tm
    bmap = lambda j, i: batch_of_step(j * n_i + i)
    r_args, r_in_specs, r_out_shapes, r_out_specs = rider(bmap)
    if w_transposed:
        w_spec = pl.BlockSpec((tn, k), lambda j, i: (j, 0))
        wb_shape = (tn, k)
    else:
        w_spec = pl.BlockSpec((k, tn), lambda j, i: (0, j))
        wb_shape = (k, tn)
    return pl.pallas_call(
        functools.partial(_proj_rider_kernel, n_in=len(r_args), n_out=len(r_out_shapes), body=body,
                          batch_of_step=batch_of_step, n_i=n_i, w_transposed=w_transposed),
        grid=(n_cols // tn, n_i),
        in_specs=[pl.BlockSpec((tm, k), lambda j, i: (i, 0)), w_spec] + r_in_specs,
        out_specs=[pl.BlockSpec((tm, tn), lambda j, i: (i, j))] + r_out_specs,
        out_shape=[jax.ShapeDtypeStruct((m, n_cols), out_dtype)] + r_out_shapes,
        scratch_shapes=[] if w.dtype == BF16 else [pltpu.VMEM(wb_shape, BF16)],
        compiler_params=pltpu.CompilerParams(dimension_semantics=("arbitrary", "arbitrary"),
                                             vmem_limit_bytes=BIG_VMEM_LIMIT),
        name="proj_rider",
    )(xp, w, *r_args)


def _outproj_kernel(*refs, next_norm):
    y_ref, w_ref, h_ref, g_ref = refs[:4]
    gn_ref = refs[4] if next_norm else None
    o_ref = refs[4 + next_norm]
    a_ref = refs[5 + next_norm] if next_norm else None
    acc_ref = refs[-1]
    k = pl.program_id(0)

    @pl.when(k == 0)
    def _():
        acc_ref[...] = jnp.zeros_like(acc_ref)

    acc_ref[...] += _dot(y_ref[...].astype(BF16), w_ref[...])

    @pl.when(k == pl.num_programs(0) - 1)
    def _():
        new = h_ref[...] + _rms(acc_ref[...], g_ref[...])
        o_ref[...] = new
        if next_norm:
            a_ref[...] = _rms(new, gn_ref[...]).astype(BF16)


def _outproj(y, w, h, g, g_next, tk):
    rows, kdim = y.shape
    next_norm = g_next is not None
    fixed = lambda r: pl.BlockSpec((r, D_MODEL), lambda k: (0, 0))
    in_specs = [pl.BlockSpec((rows, tk), lambda k: (0, k)),
                pl.BlockSpec((tk, D_MODEL), lambda k: (k, 0)), fixed(rows), fixed(1)]
    args = [y, w, h, g]
    out_specs = [fixed(rows)]
    out_shape = [jax.ShapeDtypeStruct((rows, D_MODEL), F32)]
    if next_norm:
        in_specs.append(fixed(1))
        args.append(g_next)
        out_specs.append(fixed(rows))
        out_shape.append(jax.ShapeDtypeStruct((rows, D_MODEL), BF16))
    return pl.pallas_call(
        functools.partial(_outproj_kernel, next_norm=next_norm),
        grid=(kdim // tk,),
        in_specs=in_specs,
        out_specs=out_specs,
        out_shape=out_shape,
        scratch_shapes=[pltpu.VMEM((rows, D_MODEL), F32)],
        compiler_params=_cparams("arbitrary"),
        name="outproj",
    )(*args)


def _mlstm_prompt_kernel(*refs, n_cast):
    (q_ref, k_ref, v_ref, o_ref, z_ref, gc_ref, gr_ref, gh_ref,
     yb_ref, wo_ref, x_ref, gp_ref, gn_ref) = refs[:13]
    cast_in = refs[13:13 + n_cast]
    h1_ref, a1_ref, c_out, n_out, m_out = refs[13 + n_cast:18 + n_cast]
    cast_out = refs[18 + n_cast:18 + 2 * n_cast]
    c_s, n_s, m_s = refs[18 + 2 * n_cast:]
    c = pl.program_id(1)
    t = CHUNK
    for src, dst in zip(cast_in, cast_out):
        dst[...] = src[...].astype(BF16)

    @pl.when(c == 0)
    def _():
        c_s[...] = jnp.zeros_like(c_s)
        n_s[...] = jnp.zeros_like(n_s)
        m_s[...] = jnp.zeros_like(m_s)

    row = lax.broadcasted_iota(jnp.int32, (t, t), 0)
    col = lax.broadcasted_iota(jnp.int32, (t, t), 1)
    tril = col <= row
    triu = row <= col
    gc = gc_ref[...]
    gr = gr_ref[...]
    mix = _dot(yb_ref[...], wo_ref[W_A:, :])
    ys = []
    for h in range(H_A):
        i_col = gc[:, h:h + 1]
        i_row = gr[h:h + 1, :]
        lf_col = gc[:, H_A + h:H_A + h + 1]
        lf_row = gr[H_A + h:H_A + h + 1, :]
        b_col = jnp.sum(jnp.where(tril, lf_row, 0.0), axis=1, keepdims=True)
        b_row = jnp.sum(jnp.where(triu, lf_col, 0.0), axis=0, keepdims=True)
        m_prev = m_s[h][:, 0:1]
        d = jnp.where(tril, b_col - b_row + i_row, -jnp.inf)
        inter = b_col + m_prev
        m_t = jnp.maximum(inter, jnp.max(d, axis=1, keepdims=True))
        w_intra = jnp.exp(d - m_t)
        w_inter = jnp.exp(inter - m_t) * SCALE_A
        q = q_ref[:, h * DK_A:(h + 1) * DK_A]
        k = k_ref[:, h * DK_A:(h + 1) * DK_A]
        v = v_ref[:, h * DV_A:(h + 1) * DV_A]
        s = _dot_nt(q, k) * (w_intra * SCALE_A)
        c_old = c_s[h]
        n_old = n_s[h]
        num = _dot(s.astype(BF16), v) + w_inter * _dot(q, c_old.astype(BF16))
        qn = jnp.sum(q.astype(F32) * n_old, axis=1, keepdims=True)
        den = jnp.sum(s, axis=1, keepdims=True) + w_inter * qn
        hh = num * (1.0 / jnp.maximum(jnp.abs(den), jnp.exp(-m_t)))
        hn = _rms(hh, gh_ref[h:h + 1, :])
        o = o_ref[:, h * DV_A:(h + 1) * DV_A].astype(F32)
        z = z_ref[:, h * DV_A:(h + 1) * DV_A].astype(F32)
        gate = z / ((1.0 + jnp.exp(-o)) * (1.0 + jnp.exp(-z)))
        ys.append((hn * gate).astype(BF16))
        if len(ys) == OUT_HEADS_A:
            rows = slice((h + 1 - OUT_HEADS_A) * DV_A, (h + 1) * DV_A)
            mix = mix + _dot(jnp.concatenate(ys, axis=-1), wo_ref[rows, :])
            ys = []
        b_last = b_col[t - 1:t, :]
        g_col = b_last - b_col + i_col
        m_new = jnp.maximum(b_last + m_prev, jnp.max(g_col, axis=0, keepdims=True))
        e_col = jnp.exp(g_col - m_new)
        decay = jnp.exp(b_last + m_prev - m_new)
        ke = k.astype(F32) * e_col
        c_s[h] = decay * c_old + _dot_tn(ke.astype(BF16), v)
        n_s[h] = decay * n_old + jnp.sum(ke, axis=0, keepdims=True)
        m_s[h] = jnp.broadcast_to(m_new, (1, LANES))
    new = x_ref[...] + _rms(mix, gp_ref[...])
    h1_ref[...] = new
    a1_ref[...] = _rms(new, gn_ref[...]).astype(BF16)

    @pl.when(c == pl.num_programs(1) - 1)
    def _():
        c_out[...] = c_s[...]
        n_out[...] = n_s[...]
        m_out[...] = m_s[...]


def _mlstm_prompt(proj, gates_col, gates_row, g_head, yb, wo, x, g_post, g_next, weights):
    nc = SEQ // CHUNK
    t = CHUNK
    qk_w = H_A * DK_A
    steps = BATCH * nc
    slab = lambda w: pl.BlockSpec((w.shape[0] // steps, w.shape[1]), lambda b, c: (b * nc + c, 0))
    tile = lambda width: pl.BlockSpec((None, t, width), lambda b, c: (b, c, 0))
    vec = pl.BlockSpec((1, D_MODEL), lambda b, c: (0, 0))
    return pl.pallas_call(
        functools.partial(_mlstm_prompt_kernel, n_cast=len(weights)),
        grid=(BATCH, nc),
        in_specs=[pl.BlockSpec((None, t, qk_w), lambda b, c: (b, c, 0)),
                  pl.BlockSpec((None, t, qk_w), lambda b, c: (b, c, 1)),
                  pl.BlockSpec((None, t, W_A), lambda b, c: (b, c, 1)),
                  pl.BlockSpec((None, t, W_A), lambda b, c: (b, c, 2)),
                  pl.BlockSpec((None, t, W_A), lambda b, c: (b, c, 3)),
                  pl.BlockSpec((None, t, LANES), lambda b, c: (b, c, 0)),
                  pl.BlockSpec((None, 2 * H_A, t), lambda b, c: (b, 0, c)),
                  pl.BlockSpec((H_A, DV_A), lambda b, c: (0, 0)),
                  tile(W_B),
                  pl.BlockSpec((W_A + W_B, D_MODEL), lambda b, c: (0, 0), pipeline_mode=pl.Buffered(1)),
                  tile(D_MODEL), vec, vec] + [slab(w) for w in weights],
        out_specs=[tile(D_MODEL), tile(D_MODEL),
                   pl.BlockSpec((None, H_A, DK_A, DV_A), lambda b, c: (b, 0, 0, 0)),
                   pl.BlockSpec((None, H_A, 1, DK_A), lambda b, c: (b, 0, 0, 0)),
                   pl.BlockSpec((None, H_A, 1, LANES), lambda b, c: (b, 0, 0, 0))]
        + [slab(w) for w in weights],
        out_shape=[jax.ShapeDtypeStruct((BATCH, SEQ, D_MODEL), F32),
                   jax.ShapeDtypeStruct((BATCH, SEQ, D_MODEL), BF16),
                   jax.ShapeDtypeStruct((BATCH, H_A, DK_A, DV_A), F32),
                   jax.ShapeDtypeStruct((BATCH, H_A, 1, DK_A), F32),
                   jax.ShapeDtypeStruct((BATCH, H_A, 1, LANES), F32)]
        + [jax.ShapeDtypeStruct(w.shape, BF16) for w in weights],
        scratch_shapes=[pltpu.VMEM((H_A, DK_A, DV_A), F32),
                        pltpu.VMEM((H_A, 1, DK_A), F32),
                        pltpu.VMEM((H_A, 1, LANES), F32)],
        compiler_params=pltpu.CompilerParams(dimension_semantics=("arbitrary", "arbitrary"),
                                             vmem_limit_bytes=BIG_VMEM_LIMIT),
        name="mlstm_prompt",
    )(proj, proj, proj, proj, proj, gates_col, gates_row, g_head, yb, wo, x, g_post, g_next, *weights)


def _mlstm_sample_body(r0, gi_ref, lf_ref, m0_ref, q_ref, k_ref, v_ref, o_ref, z_ref,
                       qt_ref, kt_ref, n0_ref, gh_ref, c0_ref,
                       y_ref, c1_ref, n1_ref, m1_ref):
    shift = lax.rem(DEC_BATCH - r0, DEC_BATCH)
    q_cols = [pltpu.roll(qt_ref[h], shift, axis=1) for h in range(H_A)]
    k_cols = [pltpu.roll(kt_ref[h], shift, axis=1) for h in range(H_A)]
    for j in range(c0_ref.shape[0]):
        rows = pl.ds(r0 + j, 1)
        i_v = gi_ref[rows, :]
        lf_v = lf_ref[rows, :]
        m0_v = m0_ref[rows, :]
        m_t = jnp.maximum(lf_v + m0_v, i_v)
        w_in = jnp.exp(i_v - m_t)
        w_st = jnp.exp(lf_v + m0_v - m_t)
        floor = jnp.exp(-m_t)
        m1_ref[rows, :] = m_t
        for h in range(H_A):
            ks = slice(h * DK_A, (h + 1) * DK_A)
            vs = slice(h * DV_A, (h + 1) * DV_A)
            wi = w_in[:, h:h + 1]
            ws = w_st[:, h:h + 1]
            q_col = q_cols[h][:, j:j + 1]
            k_col = k_cols[h][:, j:j + 1]
            q_row = q_ref[rows, ks]
            k_row = k_ref[rows, ks]
            v_row = v_ref[rows, vs]
            n_row = n0_ref[rows, ks]
            c_old = c0_ref[j, h]
            qk = jnp.sum(q_row * k_row, axis=1, keepdims=True) * SCALE_A
            s = qk * wi
            q_c = jnp.sum(c_old * q_col, axis=0, keepdims=True) * SCALE_A
            qn = jnp.sum(q_row * n_row, axis=1, keepdims=True) * SCALE_A
            num = s * v_row + ws * q_c
            den = s + ws * qn
            hh = num / jnp.maximum(jnp.abs(den), floor[:, h:h + 1])
            hn = _rms(hh, gh_ref[h:h + 1, :])
            y_ref[rows, vs] = hn * jax.nn.sigmoid(o_ref[rows, vs]) * _silu(z_ref[rows, vs])
            c1_ref[j, h] = ws * c_old + (wi * k_col) * v_row
            n1_ref[rows, ks] = ws * n_row + wi * k_row


def _mlstm_sample_rider(gi, lf, m0, proj, qt, kt, n0, g_head, c0, tb):
    qk_w = H_A * DK_A

    def build(bmap):
        once = dict(pipeline_mode=pl.Buffered(1))
        full = lambda shape: pl.BlockSpec(shape, lambda j, i: (0,) * len(shape))
        cols = lambda width, c: pl.BlockSpec((DEC_BATCH, width), lambda j, i: (0, c), **once)
        state = pl.BlockSpec((tb, H_A, DK_A, DV_A), lambda j, i: (bmap(j, i), 0, 0, 0))
        lanes = pl.BlockSpec((H_A, DK_A, DEC_BATCH), lambda j, i: (0, 0, 0), **once)
        args = [gi, lf, m0, proj, proj, proj, proj, proj, qt, kt, n0, g_head, c0]
        in_specs = [full((DEC_BATCH, H_A)), full((DEC_BATCH, H_A)), full((DEC_BATCH, H_A)),
                    cols(qk_w, 0), cols(qk_w, 1), cols(W_A, 1), cols(W_A, 2), cols(W_A, 3),
                    lanes, lanes, cols(qk_w, 0), full((H_A, DV_A)), state]
        out_shapes = [jax.ShapeDtypeStruct((DEC_BATCH, W_A), F32),
                      jax.ShapeDtypeStruct((DEC_BATCH, H_A, DK_A, DV_A), F32),
                      jax.ShapeDtypeStruct((DEC_BATCH, qk_w), F32),
                      jax.ShapeDtypeStruct((DEC_BATCH, H_A), F32)]
        out_specs = [full((DEC_BATCH, W_A)), state, full((DEC_BATCH, qk_w)), full((DEC_BATCH, H_A))]
        return args, in_specs, out_shapes, out_specs

    return build


def _s5_build_group(g, ldt_ref, lamr_ref, lami_ref, b1_ref, cc_ref, dsk_ref,
                    m_ref, w_ref, v_ref, p1_ref, p2_ref):
    dt = jnp.exp(ldt_ref[g])
    lam_re, lam_im = lamr_ref[g], lami_ref[g]
    mag = jnp.exp(lam_re * dt)
    ang = lam_im * dt
    a_re, a_im = mag * jnp.cos(ang), mag * jnp.sin(ang)
    den = lam_re * lam_re + lam_im * lam_im
    f_re = ((a_re - 1.0) * lam_re + a_im * lam_im) / den
    f_im = (a_im * lam_re - (a_re - 1.0) * lam_im) / den
    pows = [(jnp.ones_like(a_re), jnp.zeros_like(a_im))]
    for _ in range(S5_T):
        r_re, r_im = pows[-1]
        pows.append((r_re * a_re - r_im * a_im, r_re * a_im + r_im * a_re))

    def tall(vals):
        return jnp.concatenate([jnp.broadcast_to(x, (GROUP_B, LANES)) for x in vals], axis=0)

    af_re = tall([pows[S5_T - 1 - s][0] * f_re - pows[S5_T - 1 - s][1] * f_im for s in range(S5_T)])
    af_im = tall([pows[S5_T - 1 - s][0] * f_im + pows[S5_T - 1 - s][1] * f_re for s in range(S5_T)])
    lane = lax.broadcasted_iota(jnp.int32, (1, LANES), 1)
    sgn_lane = jnp.where(lane < P_B, -1.0, 1.0)
    b1 = b1_ref[g]
    b2 = sgn_lane * pltpu.roll(b1, P_B, axis=1)
    w_ref[g] = (af_re * b1 + af_im * b2).astype(BF16)

    eye = (lax.broadcasted_iota(jnp.int32, (LANES, LANES), 0)
           == lax.broadcasted_iota(jnp.int32, (LANES, LANES), 1))

    def column(row):
        return jnp.sum(jnp.where(eye, row, 0.0), axis=1, keepdims=True)

    ac_re, ac_im, fc_re, fc_im = column(a_re), column(a_im), column(f_re), column(f_im)
    kid = lax.broadcasted_iota(jnp.int32, (S5_T, LANES), 0)
    expand = (lax.broadcasted_iota(jnp.int32, (S5_T, S5_W), 1) // GROUP_B
              == lax.broadcasted_iota(jnp.int32, (S5_T, S5_W), 0)).astype(BF16)

    def lane_blocks(part):
        stacked = jnp.zeros((S5_T, LANES), F32)
        for k in range(S5_T):
            stacked = jnp.where(kid == k, pows[k][part], stacked)
        hi = stacked.astype(BF16)
        rest = stacked - hi.astype(F32)
        mid = rest.astype(BF16)
        lo = (rest - mid.astype(F32)).astype(BF16)
        return _dot_tn(hi, expand) + _dot_tn(mid, expand) + _dot_tn(lo, expand)

    q_re, q_im = lane_blocks(0), lane_blocks(1)
    cc = cc_ref[g]
    cs = pltpu.roll(cc, P_B, axis=0)
    rowi = lax.broadcasted_iota(jnp.int32, (LANES, 1), 0)
    sgn_row = jnp.where(rowi < P_B, 1.0, -1.0)

    def readout(r_re, r_im):
        return sgn_row * (r_re * cc) - r_im * cs

    v_ref[g] = readout(q_re * ac_re - q_im * ac_im, q_re * ac_im + q_im * ac_re).astype(BF16)
    vf = readout(q_re * fc_re - q_im * fc_im, q_re * fc_im + q_im * fc_re)
    kw = _dot3(b1[0:GROUP_B, :], vf)
    lane_w = lax.broadcasted_iota(jnp.int32, (GROUP_B, S5_W), 1)
    blocks = [kw]
    for s in range(1, S5_T):
        blocks.append(jnp.where(lane_w >= s * GROUP_B, pltpu.roll(kw, s * GROUP_B, axis=1), 0.0))
    toep = jnp.concatenate(blocks, axis=0)
    ri = lax.broadcasted_iota(jnp.int32, (S5_W, S5_W), 0)
    ci = lax.broadcasted_iota(jnp.int32, (S5_W, S5_W), 1)
    m_ref[g] = (toep + jnp.where(ri == ci, dsk_ref[g], 0.0)).astype(BF16)

    r_re, r_im = pows[S5_T]
    rid = lax.broadcasted_iota(jnp.int32, (8, LANES), 0)
    p1 = jnp.where(rid == S5_A1_ROW, a_re, 0.0)
    p2 = jnp.where(rid == S5_A1_ROW, sgn_lane * a_im, 0.0)
    for kk in range(S5_LEVELS):
        p1 = jnp.where(rid == kk, r_re, p1)
        p2 = jnp.where(rid == kk, sgn_lane * r_im, p2)
        r_re, r_im = r_re * r_re - r_im * r_im, 2.0 * (r_re * r_im)
    p1_ref[g] = p1
    p2_ref[g] = p2


def _s5_build_kernel(*refs, n_cast):
    for src, dst in zip(refs[6:6 + n_cast], refs[11 + n_cast:]):
        dst[...] = src[...].astype(BF16)
    for g in range(OCT):
        _s5_build_group(g, *refs[:6], *refs[6 + n_cast:11 + n_cast])


def _s5_build(ldt, lamr, lami, b1, cc, dsk, casts):
    steps = G_B // OCT
    o3 = lambda a, b: pl.BlockSpec((OCT, a, b), lambda g: (g, 0, 0))
    slabs = [pl.BlockSpec((rows // steps, w.shape[1]), lambda g: (g, 0)) for w, rows in casts]
    return pl.pallas_call(
        functools.partial(_s5_build_kernel, n_cast=len(casts)),
        grid=(steps,),
        in_specs=[o3(1, 1), o3(1, LANES), o3(1, LANES),
                  o3(S5_W, LANES), o3(LANES, S5_W), o3(1, S5_W)] + slabs,
        out_specs=[o3(S5_W, S5_W), o3(S5_W, LANES), o3(LANES, S5_W), o3(8, LANES), o3(8, LANES)] + slabs,
        out_shape=[jax.ShapeDtypeStruct((G_B, S5_W, S5_W), BF16),
                   jax.ShapeDtypeStruct((G_B, S5_W, LANES), BF16),
                   jax.ShapeDtypeStruct((G_B, LANES, S5_W), BF16),
                   jax.ShapeDtypeStruct((G_B, 8, LANES), F32),
                   jax.ShapeDtypeStruct((G_B, 8, LANES), F32)]
        + [jax.ShapeDtypeStruct((rows, w.shape[1]), BF16) for w, rows in casts],
        compiler_params=_cparams("parallel"),
        name="s5_build",
    )(ldt, lamr, lami, b1, cc, dsk, *[w for w, _ in casts])


def _s5_prompt_kernel(u_ref, m_ref, w_ref, v_ref, p1_ref, p2_ref, y_ref, xf_ref, y_s):
    nblk = SEQ // S5_T
    rows = BATCH * nblk
    bidx = lax.broadcasted_iota(jnp.int32, (rows, LANES), 0) & (nblk - 1)
    steps = [u_ref[pl.ds(s, rows, stride=S5_T), :] for s in range(S5_T)]
    for g in range(OCT):
        gl = slice(g * GROUP_B, (g + 1) * GROUP_B)
        u = jnp.concatenate([x[:, gl] for x in steps], axis=-1).astype(BF16)
        s = _dot(u, w_ref[g])
        for kk in range(S5_LEVELS):
            sh = 1 << kk
            r = jnp.where(bidx >= sh, pltpu.roll(s, sh, axis=0), 0.0)
            s = s + p1_ref[g, kk:kk + 1, :] * r + p2_ref[g, kk:kk + 1, :] * pltpu.roll(r, P_B, axis=1)
        x_prev = jnp.where(bidx >= 1, pltpu.roll(s, 1, axis=0), 0.0)
        y = _dot(u, m_ref[g]) + _dot(x_prev.astype(BF16), v_ref[g])
        y_s[g] = y
        for b in range(BATCH):
            xf_ref[g, b:b + 1, :] = s[(b + 1) * nblk - 1:(b + 1) * nblk, :]
    for t in range(S5_T):
        half = slice((t // OCT) * LANES, (t // OCT + 1) * LANES)
        tl = slice((t % OCT) * GROUP_B, (t % OCT + 1) * GROUP_B)
        y_ref[pl.ds(t, rows, stride=S5_T), :] = jnp.concatenate(
            [y_s[g, :, half][:, tl] for g in range(OCT)], axis=-1)


def _s5_prompt(u, m, w, v, p1, p2):
    mp = BATCH * SEQ
    rows = BATCH * (SEQ // S5_T)
    o3 = lambda a, b: pl.BlockSpec((OCT, a, b), lambda g: (g, 0, 0))
    return pl.pallas_call(
        _s5_prompt_kernel,
        grid=(G_B // OCT,),
        in_specs=[pl.BlockSpec((mp, LANES), lambda g: (0, g)),
                  o3(S5_W, S5_W), o3(S5_W, LANES), o3(LANES, S5_W), o3(8, LANES), o3(8, LANES)],
        out_specs=[pl.BlockSpec((mp, LANES), lambda g: (0, g)), o3(BATCH, LANES)],
        out_shape=[jax.ShapeDtypeStruct((mp, W_B), F32),
                   jax.ShapeDtypeStruct((G_B, BATCH, LANES), F32)],
        scratch_shapes=[pltpu.VMEM((OCT, rows, S5_W), F32)],
        compiler_params=_cparams("parallel"),
        name="s5_prompt",
    )(u, m, w, v, p1, p2)


def _s5_sample_kernel(u_ref, xr_ref, xi_ref, m_ref, w_ref, v_ref, p1_ref, p2_ref,
                      y_ref, x1r_ref, x1i_ref):
    lane = lax.broadcasted_iota(jnp.int32, (DEC_BATCH, LANES), 1)
    last = LANES - GROUP_B
    half = S5_W // 2
    a1 = slice(S5_A1_ROW, S5_A1_ROW + 1)
    u_all = u_ref[...]
    ys = []
    for g in range(OCT):
        ps = slice(g * P_B, (g + 1) * P_B)
        u = jnp.where(lane >= last, pltpu.roll(u_all, (last - g * GROUP_B) % LANES, axis=1), 0.0)
        u = u.astype(BF16)
        x0 = jnp.concatenate([xr_ref[:, ps], xi_ref[:, ps]], axis=-1)
        x1 = (p1_ref[g, a1, :] * x0 + p2_ref[g, a1, :] * pltpu.roll(x0, P_B, axis=1)
              + _dot(u, w_ref[g, half:, :]))
        x1r_ref[:, ps] = x1[:, :P_B]
        x1i_ref[:, ps] = x1[:, P_B:]
        y = (_dot(x0.astype(BF16), v_ref[g])[:, 0:GROUP_B]
             + _dot(u, m_ref[g, half:, :])[:, S5_W - GROUP_B:])
        ys.append(y)
    y_ref[...] = jnp.concatenate(ys, axis=-1)


def _s5_sample(u, xr, xi, m, w, v, p1, p2):
    o3 = lambda a, b: pl.BlockSpec((OCT, a, b), lambda g: (g, 0, 0))
    tile = pl.BlockSpec((DEC_BATCH, LANES), lambda g: (0, g))
    st = pl.BlockSpec((DEC_BATCH, OCT * P_B), lambda g: (0, g))
    st_shape = jax.ShapeDtypeStruct((DEC_BATCH, G_B * P_B), F32)
    return pl.pallas_call(
        _s5_sample_kernel,
        grid=(G_B // OCT,),
        in_specs=[tile, st, st, o3(S5_W, S5_W), o3(S5_W, LANES), o3(LANES, S5_W), o3(8, LANES), o3(8, LANES)],
        out_specs=[tile, st, st],
        out_shape=[jax.ShapeDtypeStruct((DEC_BATCH, W_B), F32), st_shape, st_shape],
        compiler_params=_cparams("parallel"),
        name="s5_sample",
    )(u, xr, xi, m, w, v, p1, p2)


def _glu_kernel(y_ref, w_ref, b_ref, z_ref, o_ref):
    y = _gelu_tanh(y_ref[...])
    gate = jax.nn.sigmoid(_dot(y.astype(BF16), w_ref[...]) + b_ref[...])
    o_ref[...] = (y * gate * _silu(z_ref[...].astype(F32))).astype(o_ref.dtype)


def _glu(y, w, b, proj_b, tm):
    m = y.shape[0]
    return pl.pallas_call(
        _glu_kernel,
        grid=(m // tm,),
        in_specs=[pl.BlockSpec((tm, W_B), lambda i: (i, 0)),
                  pl.BlockSpec((W_B, W_B), lambda i: (0, 0)),
                  pl.BlockSpec((1, W_B), lambda i: (0, 0)),
                  pl.BlockSpec((tm, W_B), lambda i: (i, 1))],
        out_specs=pl.BlockSpec((tm, W_B), lambda i: (i, 0)),
        out_shape=jax.ShapeDtypeStruct((m, W_B), BF16),
        compiler_params=_cparams("parallel"),
        name="glu",
    )(y, w, b, proj_b)


def _rope_table_kernel(cos_ref, sin_ref, *, pos0):
    shape = cos_ref.shape
    pos = lax.broadcasted_iota(jnp.int32, shape, 0).astype(F32) + pos0
    j = lax.broadcasted_iota(jnp.int32, shape, 1).astype(F32)
    ang = pos * jnp.power(ROPE_BASE, -(j / (DK_C // 2)))
    cos_ref[...] = jnp.cos(ang)
    sin_ref[...] = jnp.sin(ang)


def _rope_table(rows, pos0):
    shape = jax.ShapeDtypeStruct((rows, DK_C // 2), F32)
    return pl.pallas_call(functools.partial(_rope_table_kernel, pos0=float(pos0)),
                          out_shape=[shape, shape], name="rope_table")()


def _rope(x, cos, sin):
    half = DK_C // 2
    x1, x2 = x[:, :half], x[:, half:]
    return jnp.concatenate([x1 * cos - x2 * sin, x1 * sin + x2 * cos], axis=-1)


def _ret_prompt_kernel(q_ref, k_ref, v_ref, z_ref, cos_ref, sin_ref, gh_ref, wo_ref, h_ref, gp_ref,
                       o_ref, s_out, s_s):
    c = pl.program_id(1)
    t = CHUNK

    @pl.when(c == 0)
    def _():
        s_s[...] = jnp.zeros_like(s_s)

    row = lax.broadcasted_iota(jnp.int32, (t, t), 0)
    col = lax.broadcasted_iota(jnp.int32, (t, t), 1)
    tril = col <= row
    diff = (row - col).astype(F32)
    tpos = lax.broadcasted_iota(jnp.int32, (t, 1), 0).astype(F32)
    cos, sin = cos_ref[...], sin_ref[...]
    mix, ys = None, []
    for h in range(H_C):
        lg = LOG_GAMMA[h]
        ks = slice(h * DK_C, (h + 1) * DK_C)
        vs = slice(h * DV_C, (h + 1) * DV_C)
        q = _rope(q_ref[:, ks].astype(F32), cos, sin).astype(BF16)
        k32 = _rope(k_ref[:, ks].astype(F32), cos, sin)
        v = v_ref[:, vs]
        mask = jnp.where(tril, jnp.exp(diff * lg), 0.0) * SCALE_C
        s = _dot_nt(q, k32.astype(BF16)) * mask
        s_old = s_s[h]
        o = _dot(s.astype(BF16), v) + _dot(q, s_old.astype(BF16)) * jnp.exp((tpos + 1.0) * lg)
        y = (_rms(o, gh_ref[h:h + 1, :]) * _silu(z_ref[:, vs].astype(F32))).astype(BF16)
        ys.append(y)
        if len(ys) == OUT_HEADS:
            rows = slice((h + 1 - OUT_HEADS) * DV_C, (h + 1) * DV_C)
            part = _dot(jnp.concatenate(ys, axis=-1), wo_ref[rows, :])
            mix = part if mix is None else mix + part
            ys = []
        k_tail = k32 * (jnp.exp((t - 1.0 - tpos) * lg) * SCALE_C)
        s_s[h] = math.exp(t * lg) * s_old + _dot_tn(k_tail.astype(BF16), v)
    o_ref[...] = h_ref[...] + _rms(mix, gp_ref[...])

    @pl.when(c == pl.num_programs(1) - 1)
    def _():
        s_out[...] = s_s[...]


def _ret_prompt(proj, cos, sin, g_head, wo, h, g_post):
    nc = SEQ // CHUNK
    t = CHUNK
    qk_w = H_C * DK_C
    return pl.pallas_call(
        _ret_prompt_kernel,
        grid=(BATCH, nc),
        in_specs=[pl.BlockSpec((None, t, qk_w), lambda b, c: (b, c, 0)),
                  pl.BlockSpec((None, t, qk_w), lambda b, c: (b, c, 1)),
                  pl.BlockSpec((None, t, W_C), lambda b, c: (b, c, 1)),
                  pl.BlockSpec((None, t, W_C), lambda b, c: (b, c, 2)),
                  pl.BlockSpec((t, DK_C // 2), lambda b, c: (c, 0)),
                  pl.BlockSpec((t, DK_C // 2), lambda b, c: (c, 0)),
                  pl.BlockSpec((H_C, DV_C), lambda b, c: (0, 0)),
                  pl.BlockSpec((W_C, D_MODEL), lambda b, c: (0, 0), pipeline_mode=pl.Buffered(1)),
                  pl.BlockSpec((None, t, D_MODEL), lambda b, c: (b, c, 0)),
                  pl.BlockSpec((1, D_MODEL), lambda b, c: (0, 0))],
        out_specs=[pl.BlockSpec((None, t, D_MODEL), lambda b, c: (b, c, 0)),
                   pl.BlockSpec((None, H_C, DK_C, DV_C), lambda b, c: (b, 0, 0, 0))],
        out_shape=[jax.ShapeDtypeStruct((BATCH, SEQ, D_MODEL), F32),
                   jax.ShapeDtypeStruct((BATCH, H_C, DK_C, DV_C), F32)],
        scratch_shapes=[pltpu.VMEM((H_C, DK_C, DV_C), F32)],
        compiler_params=pltpu.CompilerParams(dimension_semantics=("parallel", "arbitrary"),
                                             vmem_limit_bytes=BIG_VMEM_LIMIT),
        name="ret_prompt",
    )(proj, proj, proj, proj, cos, sin, g_head, wo, h, g_post)


def _rope_sample_kernel(q_ref, k_ref, cos_ref, sin_ref, qo_ref, ko_ref):
    cos, sin = cos_ref[0:1, :], sin_ref[0:1, :]
    for h in range(H_C):
        ks = slice(h * DK_C, (h + 1) * DK_C)
        qo_ref[:, ks] = _rope(q_ref[:, ks], cos, sin)
        ko_ref[:, ks] = _rope(k_ref[:, ks], cos, sin)


def _rope_sample(proj, cos, sin):
    qk_w = H_C * DK_C
    shape = jax.ShapeDtypeStruct((DEC_BATCH, qk_w), F32)
    return pl.pallas_call(
        _rope_sample_kernel,
        grid=(1,),
        in_specs=[pl.BlockSpec((DEC_BATCH, qk_w), lambda i: (0, 0)),
                  pl.BlockSpec((DEC_BATCH, qk_w), lambda i: (0, 1)),
                  pl.BlockSpec((8, DK_C // 2), lambda i: (0, 0)),
                  pl.BlockSpec((8, DK_C // 2), lambda i: (0, 0))],
        out_specs=[pl.BlockSpec((DEC_BATCH, qk_w), lambda i: (0, 0)),
                   pl.BlockSpec((DEC_BATCH, qk_w), lambda i: (0, 0))],
        out_shape=[shape, shape],
        compiler_params=_cparams("arbitrary"),
        name="rope_sample",
    )(proj, proj, cos, sin)


def _ret_sample_body(r0, q_ref, k_ref, v_ref, z_ref, qt_ref, kt_ref, gh_ref, s0_ref, y_ref, s1_ref):
    shift = lax.rem(DEC_BATCH - r0, DEC_BATCH)
    for h in range(H_C):
        gamma = math.exp(LOG_GAMMA[h])
        ks = slice(h * DK_C, (h + 1) * DK_C)
        vs = slice(h * DV_C, (h + 1) * DV_C)
        q_cols = pltpu.roll(qt_ref[h], shift, axis=1)
        k_cols = pltpu.roll(kt_ref[h], shift, axis=1)
        for j in range(s0_ref.shape[0]):
            rows = pl.ds(r0 + j, 1)
            v_row = v_ref[rows, vs]
            s_old = s0_ref[j, h]
            qk = jnp.sum(q_ref[rows, ks] * k_ref[rows, ks], axis=1, keepdims=True) * SCALE_C
            o = qk * v_row + jnp.sum(s_old * q_cols[:, j:j + 1], axis=0, keepdims=True) * gamma
            y_ref[rows, vs] = _rms(o, gh_ref[h:h + 1, :]) * _silu(z_ref[rows, vs])
            s1_ref[j, h] = gamma * s_old + (k_cols[:, j:j + 1] * SCALE_C) * v_row


def _ret_sample_rider(q, k, proj, qt, kt, g_head, s0):
    qk_w = H_C * DK_C

    def build(bmap):
        once = dict(pipeline_mode=pl.Buffered(1))
        fixed = lambda shape, *idx: pl.BlockSpec(shape, lambda j, i: idx or (0,) * len(shape), **once)
        state = pl.BlockSpec((1, H_C, DK_C, DV_C), lambda j, i: (bmap(j, i), 0, 0, 0))
        args = [q, k, proj, proj, qt, kt, g_head, s0]
        in_specs = [fixed((DEC_BATCH, qk_w)), fixed((DEC_BATCH, qk_w)),
                    fixed((DEC_BATCH, W_C), 0, 1), fixed((DEC_BATCH, W_C), 0, 2),
                    fixed((H_C, DK_C, DEC_BATCH)), fixed((H_C, DK_C, DEC_BATCH)),
                    fixed((H_C, DV_C)), state]
        out_shapes = [jax.ShapeDtypeStruct((DEC_BATCH, W_C), F32),
                      jax.ShapeDtypeStruct((DEC_BATCH, H_C, DK_C, DV_C), F32)]
        out_specs = [pl.BlockSpec((DEC_BATCH, W_C), lambda j, i: (0, 0)), state]
        return args, in_specs, out_shapes, out_specs

    return build


def kernel(x_prompt, x_sample, state_mlstm_C, state_mlstm_n, state_mlstm_m, state_s5_re, state_s5_im, state_ret, g_pre, g_post, w_in0, b_gates0, g_head_a, lam_re, lam_im, log_dt, b_re, b_im, c_re, c_im, d_skip, w_glu, b_glu, w_out0, w_in1, g_head_c, w_out1):
    mp = BATCH * SEQ
    xp = x_prompt.reshape(mp, D_MODEL).astype(F32)
    xs = x_sample.reshape(DEC_BATCH, D_MODEL).astype(F32)

    w0t = w_in0.T.astype(F32)
    w0g = jnp.pad(w0t[QKV0:GATE0], ((0, LANES - 2 * H_A), (0, 0)))
    w0b = w0t[GATE0:]
    wg = w_glu.astype(BF16)
    bglu = b_glu.reshape(1, W_B).astype(F32)
    g_pre = g_pre.astype(F32)
    g_post = g_post.astype(F32)
    bg = jnp.pad(b_gates0.astype(F32), (0, LANES - 2 * H_A))[None, :]
    gh_a = g_head_a.astype(F32)
    gh_c = g_head_c.astype(F32)

    dup = lambda a: jnp.concatenate([a, a], axis=-1).astype(F32)
    lamr, lami = dup(lam_re), dup(lam_im)
    b1 = jnp.concatenate([b_re.transpose(0, 2, 1), b_im.transpose(0, 2, 1)], axis=-1)
    cc = jnp.concatenate([c_re.transpose(0, 2, 1), c_im.transpose(0, 2, 1)], axis=1)
    s5m, s5w, s5v, s5p1, s5p2, w0a, wo0 = _s5_build(
        log_dt.reshape(G_B, 1, 1).astype(F32), lamr[:, None, :], lami[:, None, :],
        jnp.tile(b1.astype(F32), (1, S5_T, 1)), jnp.tile(cc.astype(F32), (1, 1, S5_T)),
        jnp.tile(d_skip.astype(F32), (1, S5_T))[:, None, :],
        [(w0t, QKV0), (w_out0.astype(F32), W_A + W_B)])

    a0, gates = _norm_gates(xp, g_pre[0:1], w0g, bg, 512)
    a0s, gates_s = _norm_gates(xs, g_pre[0:1], w0g, bg, DEC_BATCH)
    pas = _proj(a0s, a0s, w0a, QKV0, F32, DEC_BATCH, 2048, True)[0]
    qk_a = H_A * DK_A
    to_cols = lambda a, nh, dk: a.reshape(DEC_BATCH, nh, dk).transpose(1, 2, 0)
    tb_a = DEC_BATCH // ((QKV0 // 2048) * (mp // 512))
    pa, yas, c_s, n_s, m_s = _proj_rider(
        a0, w0a, QKV0, BF16, 512, 2048, True,
        lambda blk, *refs: _mlstm_sample_body(blk * tb_a, *refs), lambda step: step,
        _mlstm_sample_rider(
            gates_s[:, :H_A], gates_s[:, H_A:2 * H_A],
            state_mlstm_m.astype(F32), pas,
            to_cols(pas[:, :qk_a], H_A, DK_A), to_cols(pas[:, qk_a:2 * qk_a], H_A, DK_A),
            state_mlstm_n.reshape(DEC_BATCH, qk_a).astype(F32), gh_a, state_mlstm_C.astype(F32),
            tb_a))
    pb, pbs = _proj(a0, a0s, w0b, 2 * W_B, F32, 1024, 1024, True)
    gates3 = gates.reshape(BATCH, SEQ, LANES)
    gates_row = gates3[:, :, :2 * H_A].transpose(0, 2, 1)
    yb, xf = _s5_prompt(pb, s5m, s5w, s5v, s5p1, s5p2)
    yb = _glu(yb, wg, bglu, pb, 512)
    s5r_p = xf[:, :, :P_B].transpose(1, 0, 2)
    s5i_p = xf[:, :, P_B:].transpose(1, 0, 2)
    h1, a1, c_p, n_p, m_p, w1, wo1 = _mlstm_prompt(
        pa.reshape(BATCH, SEQ, QKV0), gates3, gates_row, gh_a, yb.reshape(BATCH, SEQ, W_B), wo0,
        x_prompt.astype(F32), g_post[0:1], g_pre[1:2], [w_in1.astype(F32), w_out1.astype(F32)])
    h1 = h1.reshape(mp, D_MODEL)
    a1 = a1.reshape(mp, D_MODEL)
    ybs, s5r_s, s5i_s = _s5_sample(
        pbs, state_s5_re.reshape(DEC_BATCH, G_B * P_B).astype(F32),
        state_s5_im.reshape(DEC_BATCH, G_B * P_B).astype(F32), s5m, s5w, s5v, s5p1, s5p2)
    s5r_s = s5r_s.reshape(DEC_BATCH, G_B, P_B)
    s5i_s = s5i_s.reshape(DEC_BATCH, G_B, P_B)
    ybs = _glu(ybs, wg, bglu, pbs, DEC_BATCH)
    h1s, a1s = _outproj(jnp.concatenate([yas, ybs.astype(F32)], axis=-1), wo0, xs, g_post[0:1],
                        g_pre[1:2], 1024)

    cols1 = 2 * H_C * DK_C + 2 * W_C
    p1s = _proj(a1s, a1s, w1, cols1, F32, DEC_BATCH, 2048, False)[0]
    cos_s, sin_s = _rope_table(8, PAST_LEN)
    qs, ks = _rope_sample(p1s, cos_s, sin_s)
    p1, ycs, s_s = _proj_rider(
        a1, w1, cols1, BF16, 512, cols1 // 8, False, _ret_sample_body, lambda step: step,
        _ret_sample_rider(qs, ks, p1s, to_cols(qs, H_C, DK_C), to_cols(ks, H_C, DK_C),
                          gh_c, state_ret.astype(F32)))
    cos_p, sin_p = _rope_table(SEQ, 0)
    y_p, s_p = _ret_prompt(p1.reshape(BATCH, SEQ, -1), cos_p, sin_p, gh_c, wo1,
                           h1.reshape(BATCH, SEQ, D_MODEL), g_post[1:2])
    y_s = _outproj(ycs, wo1, h1s, g_post[1:2], None, 1024)[0]

    return (y_p.reshape(BATCH, SEQ, D_MODEL), y_s.reshape(DEC_BATCH, 1, D_MODEL),
            c_p, n_p.reshape(BATCH, H_A, DK_A), m_p[:, :, 0, 0],
            s5r_p, s5i_p, s_p,
            c_s, n_s.reshape(DEC_BATCH, H_A, DK_A), m_s,
            s5r_s, s5i_s, s_s)
```
